```python
import math
import jax, jax.numpy as jnp
from jax import lax
import numpy as np

D_MODEL = 1024
BATCH = 2
SEQ = 8192
DEPTH = 2

HEAD_DIM = 64
ROPE_THETA = 500000.0
NORM_EPS = 1e-6
Q_BLOCK = 128
A_HEADS = 4
A_Q_DIM = A_HEADS * 2 * HEAD_DIM
A_K_DIM = A_HEADS * 2 * HEAD_DIM
A_V_DIM = A_HEADS * 2 * HEAD_DIM
B_HEADS = 8
B_Q_DIM = B_HEADS * HEAD_DIM
B_KV_DIM = HEAD_DIM
IDX_HEADS = 4
IDX_DIM = 32
IDX_Q_DIM = IDX_HEADS * IDX_DIM
IDX_TOPK_MAX = 256
C_HEADS = 8
C_DIM = C_HEADS * HEAD_DIM
MOBA_BLOCK = 256
MOBA_TOPK = 3
MOBA_Q_CHUNK = 64
N_BRANCH = 3
BRANCH_WIDTH = 512
GATE_DIM = N_BRANCH * D_MODEL
IN_SPLIT = (A_Q_DIM, A_K_DIM, A_V_DIM,
            B_Q_DIM, B_KV_DIM, B_KV_DIM, IDX_Q_DIM, IDX_DIM, IDX_HEADS,
            C_DIM, C_DIM, C_DIM,
            GATE_DIM)
D_IN = (A_Q_DIM + A_K_DIM + A_V_DIM + B_Q_DIM + 2 * B_KV_DIM + IDX_Q_DIM + IDX_DIM
        + IDX_HEADS + 3 * C_DIM + GATE_DIM)
D_FF = -(-8 * D_MODEL // (3 * 256)) * 256

kernel_name = "hybrid_diff_dsa_moba_gated_block"


def rms_norm(x, g):
    xf = x.astype(jnp.float32)
    y = xf * lax.rsqrt(jnp.mean(xf * xf, axis=-1, keepdims=True) + NORM_EPS)
    return (y * g.astype(jnp.float32)).astype(x.dtype)


def partial_rope(x, positions):
    rot = x.shape[-1] // 4
    half = rot // 2
    inv = jnp.power(ROPE_THETA, -jnp.arange(half, dtype=jnp.float32) * 2.0 / rot)
    ang = positions.astype(jnp.float32)[:, :, None] * inv
    cos = jnp.cos(ang)[:, :, None, :].astype(x.dtype)
    sin = jnp.sin(ang)[:, :, None, :].astype(x.dtype)
    x1, x2, rest = x[..., :half], x[..., half:rot], x[..., rot:]
    return jnp.concatenate([x1 * cos - x2 * sin, x2 * cos + x1 * sin, rest], axis=-1)


def diff_attention(q, k, v, lam, subln_g, lam_init):
    B, S, H, _, d = q.shape
    nq = S // Q_BLOCK
    qb = q.reshape(B, nq, Q_BLOCK, H, 2, d).swapaxes(0, 1)
    kpos = jnp.arange(S)
    scale = d ** -0.5

    def one_block(args):
        i, qi = args
        qpos = i * Q_BLOCK + jnp.arange(Q_BLOCK)
        logits = jnp.einsum('bqhcd,bkhcd->bhcqk', qi, k,
                            preferred_element_type=jnp.float32) * scale
        causal = kpos[None, :] <= qpos[:, None]
        logits = jnp.where(causal, logits, -jnp.inf)
        p = jax.nn.softmax(logits, axis=-1)
        attn = p[:, :, 0] - lam * p[:, :, 1]
        return jnp.einsum('bhqk,bkhe->bqhe', attn.astype(v.dtype), v)

    out = lax.map(one_block, (jnp.arange(nq), qb))
    out = out.swapaxes(0, 1).reshape(B, S, H, 2 * d)
    out = rms_norm(out, subln_g) * (1.0 - lam_init)
    return out.reshape(B, S, H * 2 * d)


def dsa_attention(q, k, v, iq, ik, iw, topk):
    B, S, H, d = q.shape
    nq = S // Q_BLOCK
    kpos = jnp.arange(S)
    scale = d ** -0.5
    gather = jax.vmap(lambda t, i: t[i])
    qb = q.reshape(B, nq, Q_BLOCK, H, d).swapaxes(0, 1)
    iqb = iq.reshape(B, nq, Q_BLOCK, IDX_HEADS, IDX_DIM).swapaxes(0, 1)
    iwb = iw.reshape(B, nq, Q_BLOCK, IDX_HEADS).swapaxes(0, 1)

    def one_block(args):
        i, qi, iqi, iwi = args
        qpos = i * Q_BLOCK + jnp.arange(Q_BLOCK)
        rel = jax.nn.relu(jnp.einsum('bqhe,bke->bqhk', iqi, ik,
                                     preferred_element_type=jnp.float32))
        score = jnp.einsum('bqh,bqhk->bqk', iwi.astype(jnp.float32), rel)
        score = jnp.where((kpos[None, :] <= qpos[:, None])[None], score, -jnp.inf)
        _, idx = lax.top_k(score, topk)
        valid = idx <= qpos[None, :, None]
        k_sel = gather(k, idx)
        v_sel = gather(v, idx)
        logits = jnp.einsum('bqhd,bqkd->bhqk', qi, k_sel,
                            preferred_element_type=jnp.float32) * scale
        logits = jnp.where(valid[:, None], logits, -jnp.inf)
        p = jax.nn.softmax(logits, axis=-1)
        return jnp.einsum('bhqk,bqkd->bqhd', p.astype(v.dtype), v_sel)

    out = lax.map(one_block, (jnp.arange(nq), qb, iqb, iwb))
    return out.swapaxes(0, 1).reshape(B, S, H * d)


def moba_attention(q, k, v):
    B, S, H, d = q.shape
    nb = -(-S // MOBA_BLOCK)
    pad = nb * MOBA_BLOCK - S

    def to_blocks(t):
        t = jnp.pad(t, ((0, 0), (0, pad), (0, 0), (0, 0)))
        return t.reshape(B, nb, MOBA_BLOCK, H, d).transpose(0, 3, 1, 2, 4)

    kb, vb = to_blocks(k), to_blocks(v)
    kmean = jnp.mean(kb.astype(jnp.float32), axis=3)
    ntop = min(MOBA_TOPK, nb - 1)
    nc = S // MOBA_Q_CHUNK
    qc_all = q.reshape(B, nc, MOBA_Q_CHUNK, H, d).swapaxes(0, 1)
    scale = d ** -0.5
    blk_ids = jnp.arange(nb)
    gather = jax.vmap(jax.vmap(lambda t, i: t[i]))

    def one_chunk(args):
        c, qc = args
        qpos = c * MOBA_Q_CHUNK + jnp.arange(MOBA_Q_CHUNK)
        own = (c * MOBA_Q_CHUNK) // MOBA_BLOCK
        k_own = lax.dynamic_index_in_dim(kb, own, axis=2, keepdims=False)
        v_own = lax.dynamic_index_in_dim(vb, own, axis=2, keepdims=False)
        kpos_own = own * MOBA_BLOCK + jnp.arange(MOBA_BLOCK)
        lo = jnp.einsum('bqhd,bhkd->bhqk', qc, k_own,
                        preferred_element_type=jnp.float32) * scale
        lo = jnp.where(kpos_own[None, :] <= qpos[:, None], lo, -jnp.inf)
        if ntop == 0:
            p = jax.nn.softmax(lo, axis=-1)
            return jnp.einsum('bhqk,bhkd->bqhd', p.astype(v.dtype), v_own)
        gate = jnp.einsum('bqhd,bhnd->bhqn', qc.astype(jnp.float32), kmean)
        gate = jnp.where(blk_ids < own, gate, -jnp.inf)
        _, idx = lax.top_k(gate, ntop)
        valid = idx < own
        k_sel = gather(kb, idx)
        v_sel = gather(vb, idx)
        lp = jnp.einsum('bqhd,bhqnkd->bhqnk', qc, k_sel,
                        preferred_element_type=jnp.float32) * scale
        lp = jnp.where(valid[..., None], lp, -jnp.inf)
        lp = lp.reshape(B, H, MOBA_Q_CHUNK, ntop * MOBA_BLOCK)
        p = jax.nn.softmax(jnp.concatenate([lo, lp], axis=-1), axis=-1)
        po = p[..., :MOBA_BLOCK].astype(v.dtype)
        pp = p[..., MOBA_BLOCK:].reshape(B, H, MOBA_Q_CHUNK, ntop, MOBA_BLOCK).astype(v.dtype)
        return (jnp.einsum('bhqk,bhkd->bqhd', po, v_own)
                + jnp.einsum('bhqnk,bhqnkd->bqhd', pp, v_sel))

    out = lax.map(one_chunk, (jnp.arange(nc), qc_all))
    return out.swapaxes(0, 1).reshape(B, S, H * d)


def hybrid_mixer(h, positions, w_in, w_branch, w_out, lq1, lk1, lq2, lk2, subln_g, lam_init):
    B, S, _ = h.shape
    proj = jnp.einsum('bsd,de->bse', h, w_in)
    offsets = [int(o) for o in np.cumsum(IN_SPLIT)[:-1]]
    (a_q, a_k, a_v, b_q, b_k, b_v, i_q, i_k, i_w,
     c_q, c_k, c_v, gates) = jnp.split(proj, offsets, axis=-1)

    a_q = partial_rope(a_q.reshape(B, S, 2 * A_HEADS, HEAD_DIM), positions)
    a_k = partial_rope(a_k.reshape(B, S, 2 * A_HEADS, HEAD_DIM), positions)
    a_q = a_q.reshape(B, S, A_HEADS, 2, HEAD_DIM)
    a_k = a_k.reshape(B, S, A_HEADS, 2, HEAD_DIM)
    a_v = a_v.reshape(B, S, A_HEADS, 2 * HEAD_DIM)
    lam = (jnp.exp(jnp.sum(lq1.astype(jnp.float32) * lk1.astype(jnp.float32)))
           - jnp.exp(jnp.sum(lq2.astype(jnp.float32) * lk2.astype(jnp.float32)))
           + lam_init)
    y_a = diff_attention(a_q, a_k, a_v, lam, subln_g, lam_init)

    topk = min(IDX_TOPK_MAX, S // 4)
    b_q = partial_rope(b_q.reshape(B, S, B_HEADS, HEAD_DIM), positions)
    b_k = partial_rope(b_k.reshape(B, S, 1, HEAD_DIM), positions)[:, :, 0]
    i_q = partial_rope(i_q.reshape(B, S, IDX_HEADS, IDX_DIM), positions)
    i_k = partial_rope(i_k.reshape(B, S, 1, IDX_DIM), positions)[:, :, 0]
    y_b = dsa_attention(b_q, b_k, b_v, i_q, i_k, i_w, topk)

    c_q = partial_rope(c_q.reshape(B, S, C_HEADS, HEAD_DIM), positions)
    c_k = partial_rope(c_k.reshape(B, S, C_HEADS, HEAD_DIM), positions)
    c_v = c_v.reshape(B, S, C_HEADS, HEAD_DIM)
    y_c = moba_attention(c_q, c_k, c_v)

    g = jax.nn.sigmoid(gates.reshape(B, S, N_BRANCH, D_MODEL))
    ys = jnp.stack([y_a, y_b, y_c], axis=2)
    branches = jnp.einsum('bsnw,nwd->bsnd', ys, w_branch)
    merged = jnp.sum(g * branches, axis=2)
    return jnp.einsum('bsd,de->bse', merged, w_out)


def swiglu(h, w_ffn_in, w_ffn_out):
    gu = jnp.einsum('bsd,df->bsf', h, w_ffn_in)
    gt, up = jnp.split(gu, 2, axis=-1)
    return jnp.einsum('bsf,fd->bsd', jax.nn.silu(gt) * up, w_ffn_out)


def setup_inputs(seed: int = 0) -> dict:
    key = jax.random.key(seed)
    ks = jax.random.split(key, 14)
    f32 = jnp.float32
    x = jax.random.normal(ks[0], (BATCH, SEQ, D_MODEL), f32)
    offset = jax.random.randint(ks[1], (BATCH, 1), 0, 1024, dtype=jnp.int32)
    positions = (offset + jnp.arange(SEQ, dtype=jnp.int32)[None, :]).astype(jnp.int32)
    w_in = jax.random.normal(ks[2], (DEPTH, D_MODEL, D_IN), f32) * D_MODEL ** -0.5
    w_branch = jax.random.normal(ks[3], (DEPTH, N_BRANCH, BRANCH_WIDTH, D_MODEL), f32) * BRANCH_WIDTH ** -0.5
    w_out = jax.random.normal(ks[4], (DEPTH, D_MODEL, D_MODEL), f32) * D_MODEL ** -0.5
    lambda_q1 = jax.random.normal(ks[5], (DEPTH, HEAD_DIM), f32) * 0.1
    lambda_k1 = jax.random.normal(ks[6], (DEPTH, HEAD_DIM), f32) * 0.1
    lambda_q2 = jax.random.normal(ks[7], (DEPTH, HEAD_DIM), f32) * 0.1
    lambda_k2 = jax.random.normal(ks[8], (DEPTH, HEAD_DIM), f32) * 0.1
    subln_g = 1.0 + 0.02 * jax.random.normal(ks[9], (DEPTH, 2 * HEAD_DIM), f32)
    norm_g = 1.0 + 0.02 * jax.random.normal(ks[10], (DEPTH, 4, D_MODEL), f32)
    w_ffn_in = jax.random.normal(ks[11], (DEPTH, D_MODEL, 2 * D_FF), f32) * D_MODEL ** -0.5
    w_ffn_out = jax.random.normal(ks[12], (DEPTH, D_FF, D_MODEL), f32) * D_FF ** -0.5
    return {"x": x, "positions": positions, "w_in": w_in, "w_branch": w_branch,
            "w_out": w_out, "lambda_q1": lambda_q1, "lambda_k1": lambda_k1,
            "lambda_q2": lambda_q2, "lambda_k2": lambda_k2, "subln_g": subln_g,
            "norm_g": norm_g, "w_ffn_in": w_ffn_in, "w_ffn_out": w_ffn_out}


def reference(x, positions, w_in, w_branch, w_out, lambda_q1, lambda_k1, lambda_q2,
              lambda_k2, subln_g, norm_g, w_ffn_in, w_ffn_out):
    for l in range(DEPTH):
        lam_init = 0.8 - 0.6 * math.exp(-0.3 * l)
        h = rms_norm(x, norm_g[l, 0])
        m = hybrid_mixer(h, positions, w_in[l], w_branch[l], w_out[l], lambda_q1[l],
                         lambda_k1[l], lambda_q2[l], lambda_k2[l], subln_g[l], lam_init)
        x = x + rms_norm(m, norm_g[l, 1])
        h = rms_norm(x, norm_g[l, 2])
        x = x + rms_norm(swiglu(h, w_ffn_in[l], w_ffn_out[l]), norm_g[l, 3])
    return x
```

```python
import functools
import math

import jax
import jax.numpy as jnp
from jax import lax
from jax.experimental import pallas as pl
from jax.experimental.pallas import tpu as pltpu

F32 = jnp.float32
BF16 = jnp.bfloat16

LANES = 128
HEAD_DIM = 64
ROPE_THETA = 500000.0
NORM_EPS = 1e-6
IDX_HEADS = 4
IDX_DIM = 32
IDX_TOPK_MAX = 256
MOBA_BLOCK = 256
MOBA_TOPK = 3
N_BRANCH = 3
BRANCH_WIDTH = 512
NEG_BIG = -1e30
INT_MIN = -2 ** 31
VMEM_LIMIT = 56 * 1024 * 1024

OFF_AQ, OFF_AK, OFF_AV, OFF_BQ = 0, 512, 1024, 1536
OFF_K2, OFF_V2, OFF_IQ, OFF_IK4, OFF_IW = 2048, 2176, 2304, 2432, 2560
OFF_CQ, OFF_CK, OFF_CV, OFF_G = 2688, 3200, 3712, 4224
PACKED_COLS = OFF_G + 3 * 1024


def _nt_dot(a, b):
    return lax.dot_general(a, b, (((1,), (1,)), ((), ())), preferred_element_type=F32)


def _rms(x, g):
    return x * lax.rsqrt(jnp.mean(x * x, axis=-1, keepdims=True) + NORM_EPS) * g


def _cparams(sem):
    return pltpu.CompilerParams(dimension_semantics=sem, vmem_limit_bytes=VMEM_LIMIT)


def _resident(shape, index_map):
    return pl.BlockSpec(shape, index_map, pipeline_mode=pl.Buffered(1))


def _rope_table_kernel(pos_ref, tab_ref):
    pos = pos_ref[0].astype(F32)
    lane = lax.broadcasted_iota(jnp.int32, (1, LANES), 1)

    def tables(head_dim):
        rot = head_dim // 4
        half = rot // 2
        d = lane & (head_dim - 1)
        fi = d & (half - 1)
        inv = jnp.zeros((1, LANES), F32)
        for i in range(half):
            inv = jnp.where(fi == i, F32(ROPE_THETA ** (-(2.0 * i) / rot)), inv)
        ang = pos * inv
        cos = jnp.cos(ang)
        sin = jnp.sin(ang)
        c = jnp.where(d < rot, cos, 1.0)
        s = jnp.where(d < half, -sin, jnp.where(d < rot, sin, 0.0))
        return c, s

    c64, s64 = tables(HEAD_DIM)
    c32, s32 = tables(IDX_DIM)
    tab_ref[0, :, 0 * LANES:1 * LANES] = c64
    tab_ref[0, :, 1 * LANES:2 * LANES] = s64
    tab_ref[0, :, 2 * LANES:3 * LANES] = c32
    tab_ref[0, :, 3 * LANES:4 * LANES] = s32


def _rope_table(positions, tm):
    B, S = positions.shape
    return pl.pallas_call(
        _rope_table_kernel,
        grid=(B, S // tm),
        in_specs=[pl.BlockSpec((1, tm, 1), lambda b, m: (b, m, 0))],
        out_specs=pl.BlockSpec((1, tm, 4 * LANES), lambda b, m: (b, m, 0)),
        out_shape=jax.ShapeDtypeStruct((B, S, 4 * LANES), F32),
        compiler_params=_cparams(("parallel", "parallel")),
        name="rope_table",
    )(positions.reshape(B, S, 1))


def _inproj_kernel(x_ref, g_ref, tab_ref, w_ref,
                   aq_ref, ak_ref, av_ref, bq_ref, k2_ref, v2_ref, iq_ref, ik4_ref, iw_ref,
                   cq_ref, ck_ref, cv_ref, gates_ref, kmean_ref):
    h = _rms(x_ref[0], g_ref[...]).astype(BF16)
    lane = lax.broadcasted_iota(jnp.int32, (1, LANES), 1)

    def proj(off, width):
        return jnp.dot(h, w_ref[:, off:off + width], preferred_element_type=F32)

    def rope_block(y, head_dim, c, s):
        half = head_dim // 8
        up = pltpu.roll(y, LANES - half, 1)
        dn = pltpu.roll(y, half, 1)
        return y * c + jnp.where((lane & (head_dim - 1)) < half, up, dn) * s

    def emit(out_ref, off, width, head_dim=None, dtype=BF16):
        y = proj(off, width)
        if head_dim is None:
            out_ref[0] = y.astype(dtype)
            return None
        t0 = 0 if head_dim == HEAD_DIM else 2 * LANES
        c = tab_ref[0, :, t0:t0 + LANES]
        s = tab_ref[0, :, t0 + LANES:t0 + 2 * LANES]
        blocks = []
        for j in range(width // LANES):
            r = rope_block(y[:, j * LANES:(j + 1) * LANES], head_dim, c, s)
            out_ref[0, :, j * LANES:(j + 1) * LANES] = r.astype(dtype)
            blocks.append(r)
        return blocks

    emit(aq_ref, OFF_AQ, 512, HEAD_DIM)
    emit(ak_ref, OFF_AK, 512, HEAD_DIM)
    emit(av_ref, OFF_AV, 512)
    emit(bq_ref, OFF_BQ, 512, HEAD_DIM)
    emit(k2_ref, OFF_K2, LANES, HEAD_DIM)
    emit(v2_ref, OFF_V2, LANES)
    emit(iq_ref, OFF_IQ, LANES, IDX_DIM)
    emit(ik4_ref, OFF_IK4, LANES, IDX_DIM)
    emit(iw_ref, OFF_IW, LANES, dtype=F32)
    emit(cq_ref, OFF_CQ, 512, HEAD_DIM)
    ck_blocks = emit(ck_ref, OFF_CK, 512, HEAD_DIM)
    emit(cv_ref, OFF_CV, 512)
    emit(gates_ref, OFF_G, 3 * 1024, dtype=F32)

    tm = x_ref.shape[1]
    for blk in range(tm // MOBA_BLOCK):
        for j, r in enumerate(ck_blocks):
            part = r[blk * MOBA_BLOCK:(blk + 1) * MOBA_BLOCK]
            kmean_ref[0, 0, blk:blk + 1, j * LANES:(j + 1) * LANES] = (
                jnp.mean(part, axis=0, keepdims=True))


def _inproj(x, g, tab, w_packed, tm):
    B, S, D = x.shape
    nm = S // tm
    row = lambda width: pl.BlockSpec((1, tm, width), lambda b, m: (b, m, 0))
    shp = lambda width, dt: jax.ShapeDtypeStruct((B, S, width), dt)
    widths = [(512, BF16)] * 4 + [(LANES, BF16)] * 4 + [(LANES, F32)] + [(512, BF16)] * 3 + [(3072, F32)]
    out_specs = [row(w) for w, _ in widths]
    out_shape = [shp(w, dt) for w, dt in widths]
    out_specs.append(pl.BlockSpec((1, 1, tm // MOBA_BLOCK, 512), lambda b, m: (b, m, 0, 0)))
    out_shape.append(jax.ShapeDtypeStruct((B, nm, tm // MOBA_BLOCK, 512), F32))
    return pl.pallas_call(
        _inproj_kernel,
        grid=(B, nm),
        in_specs=[row(D),
                  _resident((1, D), lambda b, m: (0, 0)),
                  row(4 * LANES),
                  _resident((D, PACKED_COLS), lambda b, m: (0, 0))],
        out_specs=out_specs,
        out_shape=out_shape,
        compiler_params=_cparams(("parallel", "parallel")),
        name="inproj",
    )(x, g.reshape(1, D), tab, w_packed)


def _split_heads_rows(q):
    lo = lax.broadcasted_iota(jnp.int32, (1, LANES), 1) < HEAD_DIM
    zero = jnp.zeros_like(q)
    return jnp.concatenate([jnp.where(lo, q, zero), jnp.where(lo, zero, q)], axis=0)


def _online_update(s, v, m_ref, l_ref, acc_ref):
    m_prev = m_ref[...]
    m_new = jnp.maximum(m_prev, jnp.max(s, axis=1, keepdims=True))
    alpha = jnp.exp(m_prev - m_new)
    p = jnp.exp(s - m_new)
    l_ref[...] = alpha * l_ref[...] + jnp.sum(p, axis=1, keepdims=True)
    acc_ref[...] = alpha * acc_ref[...] + jnp.dot(p.astype(BF16), v, preferred_element_type=F32)
    m_ref[...] = m_new


def _diff_attn_kernel(q_ref, k_ref, v_ref, lam_ref, g_ref, o_ref, m_ref, l_ref, acc_ref,
                      *, tq, tk, lam_init):
    qi = pl.program_id(2)
    q2 = _split_heads_rows(q_ref[0])
    m_ref[...] = jnp.full(m_ref.shape, NEG_BIG, F32)
    l_ref[...] = jnp.zeros(l_ref.shape, F32)
    acc_ref[...] = jnp.zeros(acc_ref.shape, F32)

    def chunk(c, masked):
        off = pl.multiple_of(c * tk, tk)
        kc = k_ref[0, pl.ds(off, tk), :]
        vc = v_ref[0, pl.ds(off, tk), :]
        s = _nt_dot(q2, kc)
        if masked:
            r = lax.broadcasted_iota(jnp.int32, (2 * tq, 1), 0)
            qpos = qi * tq + jnp.where(r >= tq, r - tq, r)
            kpos = off + lax.broadcasted_iota(jnp.int32, (1, tk), 1)
            s = jnp.where(kpos <= qpos, s, NEG_BIG)
        _online_update(s, vc, m_ref, l_ref, acc_ref)

    diag = (qi * tq) // tk

    def body(c, carry):
        chunk(c, False)
        return carry

    lax.fori_loop(0, diag, body, 0)
    chunk(diag, True)

    lam_rows = lam_ref[...]
    e1 = jnp.exp(jnp.sum(lam_rows[0:1] * lam_rows[1:2], axis=1, keepdims=True))
    e2 = jnp.exp(jnp.sum(lam_rows[2:3] * lam_rows[3:4], axis=1, keepdims=True))
    lam = e1 - e2 + lam_init
    o = acc_ref[...] / l_ref[...]
    out = o[:tq] - lam * o[tq:]
    o_ref[0] = (_rms(out, g_ref[...]) * (1.0 - lam_init)).astype(o_ref.dtype)


def _diff_attn(aq, ak, av, lam_rows, subln_g, lam_init, tq, tk):
    B, S, _ = aq.shape
    nh = aq.shape[2] // LANES
    kern = functools.partial(_diff_attn_kernel, tq=tq, tk=tk, lam_init=lam_init)
    return pl.pallas_call(
        kern,
        grid=(B, nh, S // tq),
        in_specs=[pl.BlockSpec((1, tq, LANES), lambda b, h, i: (b, i, h)),
                  pl.BlockSpec((1, S, LANES), lambda b, h, i: (b, 0, h)),
                  pl.BlockSpec((1, S, LANES), lambda b, h, i: (b, 0, h)),
                  pl.BlockSpec((4, HEAD_DIM), lambda b, h, i: (0, 0)),
                  pl.BlockSpec((1, LANES), lambda b, h, i: (0, 0))],
        out_specs=pl.BlockSpec((1, tq, LANES), lambda b, h, i: (b, i, h)),
        out_shape=jax.ShapeDtypeStruct(aq.shape, BF16),
        scratch_shapes=[pltpu.VMEM((2 * tq, 1), F32), pltpu.VMEM((2 * tq, 1), F32),
                        pltpu.VMEM((2 * tq, LANES), F32)],
        compiler_params=_cparams(("parallel", "parallel", "arbitrary")),
        name="diff_attn",
    )(aq, ak, av, lam_rows, subln_g.reshape(1, LANES))


def _dsa_kernel(q_ref, iq_ref, iw_ref, ik4_ref, k2_ref, v2_ref, o_ref,
                keys_ref, jcut_ref, m_ref, l_ref, acc_ref, *, tq, tk, topk, idx_bits):
    qi = pl.program_id(1)
    nkv = (qi * tq + tq - 1) // tk + 1
    lane = lax.broadcasted_iota(jnp.int32, (1, LANES), 1)
    qpos = qi * tq + lax.broadcasted_iota(jnp.int32, (tq, 1), 0)
    lane_tk = lax.broadcasted_iota(jnp.int32, (1, tk), 1)
    nsub = tk // LANES

    iq = iq_ref[0]
    zero = jnp.zeros_like(iq)
    iq4 = jnp.concatenate(
        [jnp.where((lane >> 5) == hh, iq, zero) for hh in range(IDX_HEADS)], axis=0)
    iw = iw_ref[0]
    iw_cols = [iw[:, hh:hh + 1] for hh in range(IDX_HEADS)]

    def score_body(c, carry):
        off = pl.multiple_of(c * tk, tk)
        rel = jnp.maximum(_nt_dot(iq4, ik4_ref[0, pl.ds(off, tk), :]), 0.0)
        score = iw_cols[0] * rel[0:tq]
        for hh in range(1, IDX_HEADS):
            score = score + iw_cols[hh] * rel[hh * tq:(hh + 1) * tq]
        score = jnp.where(score == 0.0, 0.0, score)
        bits = lax.bitcast_convert_type(score, jnp.int32)
        key = bits ^ ((bits >> 31) & jnp.int32(0x7FFFFFFF))
        key = jnp.where(off + lane_tk <= qpos, key, jnp.int32(INT_MIN))
        keys_ref[:, pl.ds(off, tk)] = key
        return carry

    lax.fori_loop(0, nkv, score_body, 0)

    def count(pred):
        def body(c, cnt):
            off = pl.multiple_of(c * tk, tk)
            kc = keys_ref[:, pl.ds(off, tk)]
            for j in range(nsub):
                cnt = cnt + pred(kc[:, j * LANES:(j + 1) * LANES],
                                 off + j * LANES + lane).astype(jnp.int32)
            return cnt
        cnt = lax.fori_loop(0, nkv, body, jnp.zeros((tq, LANES), jnp.int32))
        return jnp.sum(cnt.astype(F32), axis=1, keepdims=True)

    def bit_body(i, t):
        cand = t + jnp.left_shift(jnp.int32(1), 31 - i)
        cnt = count(lambda kc, kpos: kc >= cand)
        return jnp.where(cnt >= topk, cand, t)

    thr = lax.fori_loop(0, 32, bit_body, jnp.full((tq, 1), INT_MIN, jnp.int32))
    cnt_ge = count(lambda kc, kpos: kc >= thr)
    need = topk - count(lambda kc, kpos: kc > thr)

    jcut_ref[...] = jnp.full((tq, 1), 2 ** 30, jnp.int32)

    @pl.when(jnp.max(cnt_ge) > topk)
    def _():
        def idx_body(i, j):
            cand = j + jnp.left_shift(jnp.int32(1), idx_bits - 1 - i)
            cnt = count(lambda kc, kpos: (kc == thr) & (kpos < cand))
            return jnp.where(cnt < need, cand, j)
        jcut_ref[...] = lax.fori_loop(0, idx_bits, idx_body, jnp.zeros((tq, 1), jnp.int32))

    jcut = jcut_ref[...]

    q = q_ref[0]
    q8 = jnp.concatenate(
        [_split_heads_rows(q[:, p * LANES:(p + 1) * LANES]) for p in range(4)], axis=0)
    nh = 8
    m_ref[...] = jnp.full(m_ref.shape, NEG_BIG, F32)
    l_ref[...] = jnp.zeros(l_ref.shape, F32)
    acc_ref[...] = jnp.zeros(acc_ref.shape, F32)

    def attn_body(c, carry):
        off = pl.multiple_of(c * tk, tk)
        kc = k2_ref[0, pl.ds(off, tk), :]
        vc = v2_ref[0, pl.ds(off, tk), :]
        keyc = keys_ref[:, pl.ds(off, tk)]
        kpos = off + lane_tk
        sel = ((keyc > thr) | ((keyc == thr) & (kpos <= jcut))) & (kpos <= qpos)
        s = _nt_dot(q8, kc)
        s = jnp.concatenate(
            [jnp.where(sel, s[hh * tq:(hh + 1) * tq], NEG_BIG) for hh in range(nh)], axis=0)
        _online_update(s, vc, m_ref, l_ref, acc_ref)
        return carry

    lax.fori_loop(0, nkv, attn_body, 0)

    o = acc_ref[...] / l_ref[...]
    lo = lane < HEAD_DIM
    for p in range(4):
        o_ref[0, :, p * LANES:(p + 1) * LANES] = jnp.where(
            lo, o[(2 * p) * tq:(2 * p + 1) * tq], o[(2 * p + 1) * tq:(2 * p + 2) * tq]
        ).astype(o_ref.dtype)


def _dsa_attn(bq, iq, iw, ik4, k2, v2, tq, tk):
    B, S, C = bq.shape
    topk = min(IDX_TOPK_MAX, S // 4)
    idx_bits = max(1, (S - 1).bit_length())
    kern = functools.partial(_dsa_kernel, tq=tq, tk=tk, topk=topk, idx_bits=idx_bits)
    row = lambda width: pl.BlockSpec((1, tq, width), lambda b, i: (b, i, 0))
    full = pl.BlockSpec((1, S, LANES), lambda b, i: (b, 0, 0))
    return pl.pallas_call(
        kern,
        grid=(B, S // tq),
        in_specs=[row(C), row(LANES), row(LANES), full, full, full],
        out_specs=row(C),
        out_shape=jax.ShapeDtypeStruct(bq.shape, BF16),
        scratch_shapes=[pltpu.VMEM((tq, S), jnp.int32), pltpu.VMEM((tq, 1), jnp.int32),
                        pltpu.VMEM((8 * tq, 1), F32), pltpu.VMEM((8 * tq, 1), F32),
                        pltpu.VMEM((8 * tq, LANES), F32)],
        compiler_params=_cparams(("parallel", "arbitrary")),
        name="dsa_attn",
    )(bq, iq, iw, ik4, k2, v2)


def _moba_kernel(q_ref, k_ref, v_ref, km_ref, o_ref, m_ref, l_ref, acc_ref, *, nb):
    tq = MOBA_BLOCK
    own = pl.program_id(2)
    lane = lax.broadcasted_iota(jnp.int32, (1, LANES), 1)
    lane_f = lane.astype(F32)
    q2 = _split_heads_rows(q_ref[0])

    km = jnp.concatenate([km_ref[0], jnp.zeros((LANES - nb, LANES), F32)], axis=0)
    km_hi = km.astype(BF16)
    r1 = km - km_hi.astype(F32)
    km_mid = r1.astype(BF16)
    km_lo = (r1 - km_mid.astype(F32)).astype(BF16)
    gate = _nt_dot(q2, km_hi) + _nt_dot(q2, km_mid) + _nt_dot(q2, km_lo)
    past = lane < own
    g = jnp.where(past, gate, -jnp.inf)
    sel = jnp.zeros(g.shape, jnp.bool_)
    for _ in range(min(MOBA_TOPK, nb - 1)):
        mx = jnp.max(g, axis=1, keepdims=True)
        first = jnp.min(jnp.where(g == mx, lane_f, float(LANES)), axis=1, keepdims=True)
        pick = lane_f == first
        sel = sel | pick
        g = jnp.where(pick, -jnp.inf, g)
    sel = sel & past
    bias = jnp.where(sel, 0.0, NEG_BIG).astype(BF16)
    qa = jnp.concatenate([q2, bias], axis=1)

    off = pl.multiple_of(own * tq, tq)
    s = _nt_dot(q2, k_ref[0, pl.ds(off, tq), :])
    r = lax.broadcasted_iota(jnp.int32, (2 * tq, 1), 0)
    qloc = jnp.where(r >= tq, r - tq, r)
    kloc = lax.broadcasted_iota(jnp.int32, (1, tq), 1)
    s = jnp.where(kloc <= qloc, s, NEG_BIG)
    m0 = jnp.max(s, axis=1, keepdims=True)
    p = jnp.exp(s - m0)
    m_ref[...] = m0
    l_ref[...] = jnp.sum(p, axis=1, keepdims=True)
    acc_ref[...] = jnp.dot(p.astype(BF16), v_ref[0, pl.ds(off, tq), :], preferred_element_type=F32)

    def body(j, carry):
        offj = pl.multiple_of(j * tq, tq)
        onehot = jnp.where(lane == j, 1.0, 0.0).astype(BF16)
        ka = jnp.concatenate(
            [k_ref[0, pl.ds(offj, tq), :], jnp.broadcast_to(onehot, (tq, LANES))], axis=1)
        _online_update(_nt_dot(qa, ka), v_ref[0, pl.ds(offj, tq), :], m_ref, l_ref, acc_ref)
        return carry

    lax.fori_loop(0, own, body, 0)

    o = acc_ref[...] / l_ref[...]
    o_ref[0] = jnp.where(lane < HEAD_DIM, o[:tq], o[tq:]).astype(o_ref.dtype)


def _moba_attn(cq, ck, cv, kmean):
    B, S, C = cq.shape
    nb = S // MOBA_BLOCK
    tq = MOBA_BLOCK
    return pl.pallas_call(
        functools.partial(_moba_kernel, nb=nb),
        grid=(B, C // LANES, nb),
        in_specs=[pl.BlockSpec((1, tq, LANES), lambda b, p, i: (b, i, p)),
                  pl.BlockSpec((1, S, LANES), lambda b, p, i: (b, 0, p)),
                  pl.BlockSpec((1, S, LANES), lambda b, p, i: (b, 0, p)),
                  pl.BlockSpec((1, nb, LANES), lambda b, p, i: (b, 0, p))],
        out_specs=pl.BlockSpec((1, tq, LANES), lambda b, p, i: (b, i, p)),
        out_shape=jax.ShapeDtypeStruct(cq.shape, BF16),
        scratch_shapes=[pltpu.VMEM((2 * tq, 1), F32), pltpu.VMEM((2 * tq, 1), F32),
                        pltpu.VMEM((2 * tq, LANES), F32)],
        compiler_params=_cparams(("parallel", "parallel", "arbitrary")),
        name="moba_attn",
    )(cq, ck, cv, kmean)


def _sigmoid(x):
    return 1.0 / (1.0 + jnp.exp(-x))


def _merge_kernel(ya_ref, yb_ref, yc_ref, gates_ref, x_ref, wb_ref, wo_ref, g_ref, o_ref):
    d = x_ref.shape[2]
    merged = None
    for n, y_ref in enumerate((ya_ref, yb_ref, yc_ref)):
        br = jnp.dot(y_ref[0], wb_ref[n], preferred_element_type=F32)
        term = _sigmoid(gates_ref[0, :, n * d:(n + 1) * d]) * br
        merged = term if merged is None else merged + term
    m = jnp.dot(merged.astype(BF16), wo_ref[...], preferred_element_type=F32)
    o_ref[0] = x_ref[0] + _rms(m, g_ref[...])


def _merge(ya, yb, yc, gates, x, wb, wo, g, tm):
    B, S, D = x.shape
    row = lambda width: pl.BlockSpec((1, tm, width), lambda b, m: (b, m, 0))
    return pl.pallas_call(
        _merge_kernel,
        grid=(B, S // tm),
        in_specs=[row(BRANCH_WIDTH), row(BRANCH_WIDTH), row(BRANCH_WIDTH), row(N_BRANCH * D), row(D),
                  _resident(wb.shape, lambda b, m: (0, 0, 0)),
                  _resident(wo.shape, lambda b, m: (0, 0)),
                  _resident((1, D), lambda b, m: (0, 0))],
        out_specs=row(D),
        out_shape=jax.ShapeDtypeStruct(x.shape, F32),
        compiler_params=_cparams(("parallel", "parallel")),
        name="merge_out",
    )(ya, yb, yc, gates, x, wb, wo, g.reshape(1, D))


def _ffn_kernel(x_ref, g_in_ref, wi_ref, wo_ref, g_out_ref, o_ref, acc_ref, *, d_ff, tf):
    x = x_ref[0]
    h = _rms(x, g_in_ref[...]).astype(BF16)
    acc_ref[...] = jnp.zeros(acc_ref.shape, F32)

    def body(c, carry):
        off = pl.multiple_of(c * tf, tf)
        gt = jnp.dot(h, wi_ref[:, pl.ds(off, tf)], preferred_element_type=F32)
        up_off = pl.multiple_of(d_ff + off, LANES)
        up = jnp.dot(h, wi_ref[:, pl.ds(up_off, tf)], preferred_element_type=F32)
        act = (gt * _sigmoid(gt) * up).astype(BF16)
        acc_ref[...] += jnp.dot(act, wo_ref[pl.ds(off, tf), :], preferred_element_type=F32)
        return carry

    lax.fori_loop(0, d_ff // tf, body, 0)
    o_ref[0] = x + _rms(acc_ref[...], g_out_ref[...])


def _ffn(x, g_in, wi, wo, g_out, tm, tf):
    B, S, D = x.shape
    d_ff = wo.shape[0]
    row = pl.BlockSpec((1, tm, D), lambda b, m: (b, m, 0))
    vec = _resident((1, D), lambda b, m: (0, 0))
    return pl.pallas_call(
        functools.partial(_ffn_kernel, d_ff=d_ff, tf=tf),
        grid=(B, S // tm),
        in_specs=[row, vec, _resident(wi.shape, lambda b, m: (0, 0)),
                  _resident(wo.shape, lambda b, m: (0, 0)), vec],
        out_specs=row,
        out_shape=jax.ShapeDtypeStruct(x.shape, F32),
        scratch_shapes=[pltpu.VMEM((tm, D), F32)],
        compiler_params=_cparams(("parallel", "parallel")),
        name="swiglu_ffn",
    )(x, g_in.reshape(1, D), wi, wo, g_out.reshape(1, D))


def _pack_w_in(w):
    d = w.shape[0]
    offs = [0]
    for width in (512, 512, 512, 512, 64, 64, 128, 32, 4, 512, 512, 512, 3072):
        offs.append(offs[-1] + width)
    seg = lambda i: w[:, offs[i]:offs[i + 1]]
    scale = HEAD_DIM ** -0.5
    cols = [seg(0) * scale, seg(1), seg(2), seg(3) * scale,
            seg(4), seg(4), seg(5), seg(5), seg(6),
            seg(7), seg(7), seg(7), seg(7),
            seg(8), jnp.zeros((d, LANES - IDX_HEADS), w.dtype),
            seg(9) * scale, seg(10), seg(11), seg(12)]
    packed = jnp.concatenate(cols, axis=1).astype(BF16)
    assert packed.shape[1] == PACKED_COLS
    return packed


def kernel(x, positions, w_in, w_branch, w_out, lambda_q1, lambda_k1, lambda_q2, lambda_k2,
           subln_g, norm_g, w_ffn_in, w_ffn_out):
    B, S, D = x.shape
    depth = w_in.shape[0]
    tm = min(512, S)
    tab = _rope_table(positions, tm)
    for l in range(depth):
        lam_init = 0.8 - 0.6 * math.exp(-0.3 * l)
        (aq, ak, av, bq, k2, v2, iq, ik4, iw, cq, ck, cv, gates, kmean) = _inproj(
            x, norm_g[l, 0], tab, _pack_w_in(w_in[l]), tm)
        kmean = kmean.reshape(B, S // MOBA_BLOCK, 512)
        lam_rows = jnp.stack([lambda_q1[l], lambda_k1[l], lambda_q2[l], lambda_k2[l]])
        ya = _diff_attn(aq, ak, av, lam_rows, subln_g[l], lam_init, tq=256, tk=min(512, S))
        yb = _dsa_attn(bq, iq, iw, ik4, k2, v2, tq=128, tk=min(512, S))
        yc = _moba_attn(cq, ck, cv, kmean)
        x = _merge(ya, yb, yc, gates, x, w_branch[l].astype(BF16), w_out[l].astype(BF16),
                   norm_g[l, 1], tm)
        x = _ffn(x, norm_g[l, 2], w_ffn_in[l].astype(BF16), w_ffn_out[l].astype(BF16),
                 norm_g[l, 3], tm, tf=256)
    return x
```

```python
import functools
import math

import jax
import jax.numpy as jnp
from jax import lax
from jax.experimental import pallas as pl
from jax.experimental.pallas import tpu as pltpu

F32 = jnp.float32
BF16 = jnp.bfloat16

LANES = 128
HEAD_DIM = 64
ROPE_THETA = 500000.0
NORM_EPS = 1e-6
IDX_HEADS = 4
IDX_DIM = 32
IDX_TOPK_MAX = 256
MOBA_BLOCK = 256
MOBA_TOPK = 3
N_BRANCH = 3
BRANCH_WIDTH = 512
NEG_BIG = -1e30
INT_MIN = -2 ** 31
VMEM_LIMIT = 56 * 1024 * 1024

OFF_AQ, OFF_AK, OFF_AV, OFF_BQ = 0, 512, 1024, 1536
OFF_K2, OFF_V2, OFF_IQ, OFF_IK4, OFF_IW = 2048, 2176, 2304, 2432, 2560
OFF_CQ, OFF_CK, OFF_CV, OFF_G = 2688, 3200, 3712, 4224
PACKED_COLS = OFF_G + 3 * 1024


def _nt_dot(a, b):
    return lax.dot_general(a, b, (((1,), (1,)), ((), ())), preferred_element_type=F32)


def _rms(x, g):
    return x * lax.rsqrt(jnp.mean(x * x, axis=-1, keepdims=True) + NORM_EPS) * g


def _cparams(sem):
    return pltpu.CompilerParams(dimension_semantics=sem, vmem_limit_bytes=VMEM_LIMIT)


def _resident(shape, index_map):
    return pl.BlockSpec(shape, index_map, pipeline_mode=pl.Buffered(1))


def _rope_table_kernel(pos_ref, tab_ref):
    pos = pos_ref[0].astype(F32)
    lane = lax.broadcasted_iota(jnp.int32, (1, LANES), 1)

    def tables(head_dim):
        rot = head_dim // 4
        half = rot // 2
        d = lane & (head_dim - 1)
        fi = d & (half - 1)
        inv = jnp.zeros((1, LANES), F32)
        for i in range(half):
            inv = jnp.where(fi == i, F32(ROPE_THETA ** (-(2.0 * i) / rot)), inv)
        ang = pos * inv
        cos = jnp.cos(ang)
        sin = jnp.sin(ang)
        c = jnp.where(d < rot, cos, 1.0)
        s = jnp.where(d < half, -sin, jnp.where(d < rot, sin, 0.0))
        return c, s

    c64, s64 = tables(HEAD_DIM)
    c32, s32 = tables(IDX_DIM)
    tab_ref[0, :, 0 * LANES:1 * LANES] = c64
    tab_ref[0, :, 1 * LANES:2 * LANES] = s64
    tab_ref[0, :, 2 * LANES:3 * LANES] = c32
    tab_ref[0, :, 3 * LANES:4 * LANES] = s32


def _rope_table(positions, tm):
    B, S = positions.shape
    return pl.pallas_call(
        _rope_table_kernel,
        grid=(B, S // tm),
        in_specs=[pl.BlockSpec((1, tm, 1), lambda b, m: (b, m, 0))],
        out_specs=pl.BlockSpec((1, tm, 4 * LANES), lambda b, m: (b, m, 0)),
        out_shape=jax.ShapeDtypeStruct((B, S, 4 * LANES), F32),
        compiler_params=_cparams(("parallel", "parallel")),
        name="rope_table",
    )(positions.reshape(B, S, 1))


def _inproj_kernel(x_ref, g_ref, tab_ref, w_ref,
                   aq_ref, ak_ref, av_ref, bq_ref, k2_ref, v2_ref, iq_ref, ik4_ref, iw_ref,
                   cq_ref, ck_ref, cv_ref, gates_ref, kmean_ref):
    h = _rms(x_ref[0], g_ref[...]).astype(BF16)
    lane = lax.broadcasted_iota(jnp.int32, (1, LANES), 1)

    def proj(off, width):
        return jnp.dot(h, w_ref[:, off:off + width], preferred_element_type=F32)

    def rope_block(y, head_dim, c, s):
        half = head_dim // 8
        up = pltpu.roll(y, LANES - half, 1)
        dn = pltpu.roll(y, half, 1)
        return y * c + jnp.where((lane & (head_dim - 1)) < half, up, dn) * s

    def emit(out_ref, off, width, head_dim=None, dtype=BF16):
        y = proj(off, width)
        if head_dim is None:
            out_ref[0] = y.astype(dtype)
            return None
        t0 = 0 if head_dim == HEAD_DIM else 2 * LANES
        c = tab_ref[0, :, t0:t0 + LANES]
        s = tab_ref[0, :, t0 + LANES:t0 + 2 * LANES]
        blocks = []
        for j in range(width // LANES):
            r = rope_block(y[:, j * LANES:(j + 1) * LANES], head_dim, c, s)
            out_ref[0, :, j * LANES:(j + 1) * LANES] = r.astype(dtype)
            blocks.append(r)
        return blocks

    emit(aq_ref, OFF_AQ, 512, HEAD_DIM)
    emit(ak_ref, OFF_AK, 512, HEAD_DIM)
    emit(av_ref, OFF_AV, 512)
    emit(bq_ref, OFF_BQ, 512, HEAD_DIM)
    emit(k2_ref, OFF_K2, LANES, HEAD_DIM)
    emit(v2_ref, OFF_V2, LANES)
    emit(iq_ref, OFF_IQ, LANES, IDX_DIM)
    emit(ik4_ref, OFF_IK4, LANES, IDX_DIM)
    emit(iw_ref, OFF_IW, LANES, dtype=F32)
    emit(cq_ref, OFF_CQ, 512, HEAD_DIM)
    ck_blocks = emit(ck_ref, OFF_CK, 512, HEAD_DIM)
    emit(cv_ref, OFF_CV, 512)
    emit(gates_ref, OFF_G, 3 * 1024, dtype=F32)

    tm = x_ref.shape[1]
    for blk in range(tm // MOBA_BLOCK):
        for j, r in enumerate(ck_blocks):
            part = r[blk * MOBA_BLOCK:(blk + 1) * MOBA_BLOCK]
            kmean_ref[0, 0, blk:blk + 1, j * LANES:(j + 1) * LANES] = (
                jnp.mean(part, axis=0, keepdims=True))


def _inproj(x, g, tab, w_packed, tm):
    B, S, D = x.shape
    nm = S // tm
    row = lambda width: pl.BlockSpec((1, tm, width), lambda b, m: (b, m, 0))
    shp = lambda width, dt: jax.ShapeDtypeStruct((B, S, width), dt)
    widths = [(512, BF16)] * 4 + [(LANES, BF16)] * 4 + [(LANES, F32)] + [(512, BF16)] * 3 + [(3072, F32)]
    out_specs = [row(w) for w, _ in widths]
    out_shape = [shp(w, dt) for w, dt in widths]
    out_specs.append(pl.BlockSpec((1, 1, tm // MOBA_BLOCK, 512), lambda b, m: (b, m, 0, 0)))
    out_shape.append(jax.ShapeDtypeStruct((B, nm, tm // MOBA_BLOCK, 512), F32))
    return pl.pallas_call(
        _inproj_kernel,
        grid=(B, nm),
        in_specs=[row(D),
                  _resident((1, D), lambda b, m: (0, 0)),
                  row(4 * LANES),
                  _resident((D, PACKED_COLS), lambda b, m: (0, 0))],
        out_specs=out_specs,
        out_shape=out_shape,
        compiler_params=_cparams(("parallel", "parallel")),
        name="inproj",
    )(x, g.reshape(1, D), tab, w_packed)


def _split_heads_rows(q):
    lo = lax.broadcasted_iota(jnp.int32, (1, LANES), 1) < HEAD_DIM
    zero = jnp.zeros_like(q)
    return jnp.concatenate([jnp.where(lo, q, zero), jnp.where(lo, zero, q)], axis=0)


def _lane_blocks(s):
    return [s[:, j * LANES:(j + 1) * LANES] for j in range(s.shape[1] // LANES)]


def _online_update(s, v, m_ref, l_ref, acc_ref):
    _online_update_cols(_lane_blocks(s), v, m_ref, l_ref, acc_ref)


def _online_update_cols(cols, v, m_ref, l_ref, acc_ref):
    m_prev = m_ref[...]
    m_new = jnp.maximum(m_prev, jnp.max(functools.reduce(jnp.maximum, cols), axis=1, keepdims=True))
    alpha = jnp.exp(m_prev - m_new)
    ps = [jnp.exp(c - m_new) for c in cols]
    l_ref[...] = alpha * l_ref[...] + functools.reduce(jnp.add, ps)
    p = jnp.concatenate([pj.astype(BF16) for pj in ps], axis=1)
    acc_ref[...] = alpha * acc_ref[...] + jnp.dot(p, v, preferred_element_type=F32)
    m_ref[...] = m_new


def _normalised(l_ref, acc_ref):
    return acc_ref[...] / jnp.sum(l_ref[...], axis=1, keepdims=True)


def _diff_attn_kernel(q_ref, k_ref, v_ref, lam_ref, g_ref, o_ref, m_ref, l_ref, acc_ref,
                      *, tq, tk, lam_init):
    qi = pl.program_id(2)
    q2 = _split_heads_rows(q_ref[0])
    m_ref[...] = jnp.full(m_ref.shape, NEG_BIG, F32)
    l_ref[...] = jnp.zeros(l_ref.shape, F32)
    acc_ref[...] = jnp.zeros(acc_ref.shape, F32)

    def chunk(c, masked):
        off = pl.multiple_of(c * tk, tk)
        kc = k_ref[0, pl.ds(off, tk), :]
        vc = v_ref[0, pl.ds(off, tk), :]
        s = _nt_dot(q2, kc)
        if masked:
            r = lax.broadcasted_iota(jnp.int32, (2 * tq, 1), 0)
            qpos = qi * tq + jnp.where(r >= tq, r - tq, r)
            kpos = off + lax.broadcasted_iota(jnp.int32, (1, tk), 1)
            s = jnp.where(kpos <= qpos, s, NEG_BIG)
        _online_update(s, vc, m_ref, l_ref, acc_ref)

    diag = (qi * tq) // tk

    def body(c, carry):
        chunk(c, False)
        return carry

    lax.fori_loop(0, diag, body, 0)
    chunk(diag, True)

    lam_rows = lam_ref[...]
    e1 = jnp.exp(jnp.sum(lam_rows[0:1] * lam_rows[1:2], axis=1, keepdims=True))
    e2 = jnp.exp(jnp.sum(lam_rows[2:3] * lam_rows[3:4], axis=1, keepdims=True))
    lam = e1 - e2 + lam_init
    o = _normalised(l_ref, acc_ref)
    out = o[:tq] - lam * o[tq:]
    o_ref[0] = (_rms(out, g_ref[...]) * (1.0 - lam_init)).astype(o_ref.dtype)


def _diff_attn(aq, ak, av, lam_rows, subln_g, lam_init, tq, tk):
    B, S, _ = aq.shape
    nh = aq.shape[2] // LANES
    kern = functools.partial(_diff_attn_kernel, tq=tq, tk=tk, lam_init=lam_init)
    return pl.pallas_call(
        kern,
        grid=(B, nh, S // tq),
        in_specs=[pl.BlockSpec((1, tq, LANES), lambda b, h, i: (b, i, h)),
                  pl.BlockSpec((1, S, LANES), lambda b, h, i: (b, 0, h)),
                  pl.BlockSpec((1, S, LANES), lambda b, h, i: (b, 0, h)),
                  pl.BlockSpec((4, HEAD_DIM), lambda b, h, i: (0, 0)),
                  pl.BlockSpec((1, LANES), lambda b, h, i: (0, 0))],
        out_specs=pl.BlockSpec((1, tq, LANES), lambda b, h, i: (b, i, h)),
        out_shape=jax.ShapeDtypeStruct(aq.shape, BF16),
        scratch_shapes=[pltpu.VMEM((2 * tq, LANES), F32), pltpu.VMEM((2 * tq, LANES), F32),
                        pltpu.VMEM((2 * tq, LANES), F32)],
        compiler_params=_cparams(("parallel", "parallel", "arbitrary")),
        name="diff_attn",
    )(aq, ak, av, lam_rows, subln_g.reshape(1, LANES))


def _dsa_kernel(q_ref, iq_ref, iw_ref, ik4_ref, k2_ref, v2_ref, o_ref,
                keys_ref, keys_t_ref, jcut_ref, m_ref, l_ref, acc_ref, *, tq, tk, topk, idx_bits):
    qi = pl.program_id(1)
    nkv = (qi * tq + tq - 1) // tk + 1
    lane = lax.broadcasted_iota(jnp.int32, (1, LANES), 1)
    qpos = qi * tq + lax.broadcasted_iota(jnp.int32, (tq, 1), 0)
    lane_tk = lax.broadcasted_iota(jnp.int32, (1, tk), 1)
    nsub = tk // LANES

    iq = iq_ref[0]
    zero = jnp.zeros_like(iq)
    iq4 = jnp.concatenate(
        [jnp.where((lane >> 5) == hh, iq, zero) for hh in range(IDX_HEADS)], axis=0)
    iw = iw_ref[0]
    iw_cols = [iw[:, hh:hh + 1] for hh in range(IDX_HEADS)]

    def score_body(c, carry):
        off = pl.multiple_of(c * tk, tk)
        rel = jnp.maximum(_nt_dot(iq4, ik4_ref[0, pl.ds(off, tk), :]), 0.0)
        score = iw_cols[0] * rel[0:tq]
        for hh in range(1, IDX_HEADS):
            score = score + iw_cols[hh] * rel[hh * tq:(hh + 1) * tq]
        score = jnp.where(score == 0.0, 0.0, score)
        bits = lax.bitcast_convert_type(score, jnp.int32)
        key = bits ^ ((bits >> 31) & jnp.int32(0x7FFFFFFF))
        key = jnp.where(off + lane_tk <= qpos, key, jnp.int32(INT_MIN))
        keys_ref[:, pl.ds(off, tk)] = key
        for j in range(nsub):
            keys_t_ref[pl.ds(pl.multiple_of(off + j * LANES, LANES), LANES), :] = (
                key[:, j * LANES:(j + 1) * LANES].T)
        return carry

    lax.fori_loop(0, nkv, score_body, 0)

    rows_tk = lax.broadcasted_iota(jnp.int32, (tk, 1), 0)
    cnt_rows = 64

    def count(pred):
        def body(c, cnt):
            off = pl.multiple_of(c * tk, tk)
            ind = pred(keys_t_ref[pl.ds(off, tk), :], off + rows_tk).astype(jnp.int32)
            for g in range(tk // cnt_rows):
                cnt = cnt + ind[g * cnt_rows:(g + 1) * cnt_rows]
            return cnt
        cnt = lax.fori_loop(0, nkv, body, jnp.zeros((cnt_rows, tq), jnp.int32))
        return jnp.sum(cnt.astype(F32), axis=0, keepdims=True)

    def bit_body(i, t):
        cand = t + jnp.left_shift(jnp.int32(1), 31 - i)
        cnt = count(lambda kc, kpos: kc >= cand)
        return jnp.where(cnt >= topk, cand, t)

    thr = lax.fori_loop(0, 32, bit_body, jnp.full((1, tq), INT_MIN, jnp.int32))
    cnt_ge = count(lambda kc, kpos: kc >= thr)
    need = topk - count(lambda kc, kpos: kc > thr)

    jcut_ref[...] = jnp.full((1, tq), 2 ** 30, jnp.int32)

    @pl.when(jnp.max(cnt_ge) > topk)
    def _():
        def idx_body(i, j):
            cand = j + jnp.left_shift(jnp.int32(1), idx_bits - 1 - i)
            cnt = count(lambda kc, kpos: (kc == thr) & (kpos < cand))
            return jnp.where(cnt < need, cand, j)
        jcut_ref[...] = lax.fori_loop(0, idx_bits, idx_body, jnp.zeros((1, tq), jnp.int32))

    def as_rows(x):
        return jnp.broadcast_to(x, (LANES, tq)).T

    thr_r = as_rows(thr)
    jcut_r = as_rows(jcut_ref[...])

    q = q_ref[0]
    q8 = jnp.concatenate(
        [_split_heads_rows(q[:, p * LANES:(p + 1) * LANES]) for p in range(4)], axis=0)
    nh = 8
    m_ref[...] = jnp.full(m_ref.shape, NEG_BIG, F32)
    l_ref[...] = jnp.zeros(l_ref.shape, F32)
    acc_ref[...] = jnp.zeros(acc_ref.shape, F32)

    def attn_body(c, carry):
        off = pl.multiple_of(c * tk, tk)
        kc = k2_ref[0, pl.ds(off, tk), :]
        vc = v2_ref[0, pl.ds(off, tk), :]
        s = _nt_dot(q8, kc)
        cols = []
        for j in range(nsub):
            keyj = keys_ref[:, pl.ds(pl.multiple_of(off + j * LANES, LANES), LANES)]
            kpos = off + j * LANES + lane
            sel = ((keyj > thr_r) | ((keyj == thr_r) & (kpos <= jcut_r))) & (kpos <= qpos)
            cols.append(jnp.concatenate(
                [jnp.where(sel, s[hh * tq:(hh + 1) * tq, j * LANES:(j + 1) * LANES], NEG_BIG)
                 for hh in range(nh)], axis=0))
        _online_update_cols(cols, vc, m_ref, l_ref, acc_ref)
        return carry

    lax.fori_loop(0, nkv, attn_body, 0)

    o = _normalised(l_ref, acc_ref)
    lo = lane < HEAD_DIM
    for p in range(4):
        o_ref[0, :, p * LANES:(p + 1) * LANES] = jnp.where(
            lo, o[(2 * p) * tq:(2 * p + 1) * tq], o[(2 * p + 1) * tq:(2 * p + 2) * tq]
        ).astype(o_ref.dtype)


def _dsa_attn(bq, iq, iw, ik4, k2, v2, tq, tk):
    B, S, C = bq.shape
    topk = min(IDX_TOPK_MAX, S // 4)
    idx_bits = max(1, (S - 1).bit_length())
    kern = functools.partial(_dsa_kernel, tq=tq, tk=tk, topk=topk, idx_bits=idx_bits)
    row = lambda width: pl.BlockSpec((1, tq, width), lambda b, i: (b, i, 0))
    full = pl.BlockSpec((1, S, LANES), lambda b, i: (b, 0, 0))
    return pl.pallas_call(
        kern,
        grid=(B, S // tq),
        in_specs=[row(C), row(LANES), row(LANES), full, full, full],
        out_specs=row(C),
        out_shape=jax.ShapeDtypeStruct(bq.shape, BF16),
        scratch_shapes=[pltpu.VMEM((tq, S), jnp.int32), pltpu.VMEM((S, tq), jnp.int32),
                        pltpu.VMEM((1, tq), jnp.int32),
                        pltpu.VMEM((8 * tq, LANES), F32), pltpu.VMEM((8 * tq, LANES), F32),
                        pltpu.VMEM((8 * tq, LANES), F32)],
        compiler_params=_cparams(("parallel", "arbitrary")),
        name="dsa_attn",
    )(bq, iq, iw, ik4, k2, v2)


def _moba_kernel(q_ref, k_ref, v_ref, km_ref, o_ref, m_ref, l_ref, acc_ref, *, nb, kb):
    tq = MOBA_BLOCK
    own = pl.program_id(2)
    lane = lax.broadcasted_iota(jnp.int32, (1, LANES), 1)
    lane_f = lane.astype(F32)
    q2 = _split_heads_rows(q_ref[0])

    km = jnp.concatenate([km_ref[0], jnp.zeros((LANES - nb, LANES), F32)], axis=0)
    km_hi = km.astype(BF16)
    r1 = km - km_hi.astype(F32)
    km_mid = r1.astype(BF16)
    km_lo = (r1 - km_mid.astype(F32)).astype(BF16)
    gate = _nt_dot(q2, km_hi) + _nt_dot(q2, km_mid) + _nt_dot(q2, km_lo)
    past = lane < own
    g = jnp.where(past, gate, -jnp.inf)
    sel = jnp.zeros(g.shape, jnp.bool_)
    for _ in range(min(MOBA_TOPK, nb - 1)):
        mx = jnp.max(g, axis=1, keepdims=True)
        first = jnp.min(jnp.where(g == mx, lane_f, float(LANES)), axis=1, keepdims=True)
        pick = lane_f == first
        sel = sel | pick
        g = jnp.where(pick, -jnp.inf, g)
    sel = sel & past
    bias = jnp.where(sel, 0.0, NEG_BIG).astype(BF16)
    qa = jnp.concatenate([q2, bias], axis=1)

    off = pl.multiple_of(own * tq, tq)
    s = _nt_dot(q2, k_ref[0, pl.ds(off, tq), :])
    r = lax.broadcasted_iota(jnp.int32, (2 * tq, 1), 0)
    qloc = jnp.where(r >= tq, r - tq, r)
    kloc = lax.broadcasted_iota(jnp.int32, (1, tq), 1)
    s = jnp.where(kloc <= qloc, s, NEG_BIG)
    m_ref[...] = jnp.full(m_ref.shape, NEG_BIG, F32)
    l_ref[...] = jnp.zeros(l_ref.shape, F32)
    acc_ref[...] = jnp.zeros(acc_ref.shape, F32)
    _online_update(s, v_ref[0, pl.ds(off, tq), :], m_ref, l_ref, acc_ref)

    tkc = kb * tq
    row_blk = lax.broadcasted_iota(jnp.int32, (tkc, 1), 0) >> (MOBA_BLOCK.bit_length() - 1)

    def body(i, carry):
        offi = pl.multiple_of(i * tkc, tkc)
        onehot = jnp.where(lane == i * kb + row_blk, 1.0, 0.0).astype(BF16)
        ka = jnp.concatenate([k_ref[0, pl.ds(offi, tkc), :], onehot], axis=1)
        _online_update(_nt_dot(qa, ka), v_ref[0, pl.ds(offi, tkc), :], m_ref, l_ref, acc_ref)
        return carry

    lax.fori_loop(0, (own + kb - 1) // kb, body, 0)

    o = _normalised(l_ref, acc_ref)
    o_ref[0] = jnp.where(lane < HEAD_DIM, o[:tq], o[tq:]).astype(o_ref.dtype)


def _moba_attn(cq, ck, cv, kmean):
    B, S, C = cq.shape
    nb = S // MOBA_BLOCK
    tq = MOBA_BLOCK
    return pl.pallas_call(
        functools.partial(_moba_kernel, nb=nb, kb=2 if nb % 2 == 0 else 1),
        grid=(B, C // LANES, nb),
        in_specs=[pl.BlockSpec((1, tq, LANES), lambda b, p, i: (b, i, p)),
                  pl.BlockSpec((1, S, LANES), lambda b, p, i: (b, 0, p)),
                  pl.BlockSpec((1, S, LANES), lambda b, p, i: (b, 0, p)),
                  pl.BlockSpec((1, nb, LANES), lambda b, p, i: (b, 0, p))],
        out_specs=pl.BlockSpec((1, tq, LANES), lambda b, p, i: (b, i, p)),
        out_shape=jax.ShapeDtypeStruct(cq.shape, BF16),
        scratch_shapes=[pltpu.VMEM((2 * tq, LANES), F32), pltpu.VMEM((2 * tq, LANES), F32),
                        pltpu.VMEM((2 * tq, LANES), F32)],
        compiler_params=_cparams(("parallel", "parallel", "arbitrary")),
        name="moba_attn",
    )(cq, ck, cv, kmean)


def _sigmoid(x):
    return 1.0 / (1.0 + jnp.exp(-x))


def _merge_kernel(ya_ref, yb_ref, yc_ref, gates_ref, x_ref, wb_ref, wo_ref, g_ref, o_ref):
    d = x_ref.shape[2]
    merged = None
    for n, y_ref in enumerate((ya_ref, yb_ref, yc_ref)):
        br = jnp.dot(y_ref[0], wb_ref[n], preferred_element_type=F32)
        term = _sigmoid(gates_ref[0, :, n * d:(n + 1) * d]) * br
        merged = term if merged is None else merged + term
    m = jnp.dot(merged.astype(BF16), wo_ref[...], preferred_element_type=F32)
    o_ref[0] = x_ref[0] + _rms(m, g_ref[...])


def _merge(ya, yb, yc, gates, x, wb, wo, g, tm):
    B, S, D = x.shape
    row = lambda width: pl.BlockSpec((1, tm, width), lambda b, m: (b, m, 0))
    return pl.pallas_call(
        _merge_kernel,
        grid=(B, S // tm),
        in_specs=[row(BRANCH_WIDTH), row(BRANCH_WIDTH), row(BRANCH_WIDTH), row(N_BRANCH * D), row(D),
                  _resident(wb.shape, lambda b, m: (0, 0, 0)),
                  _resident(wo.shape, lambda b, m: (0, 0)),
                  _resident((1, D), lambda b, m: (0, 0))],
        out_specs=row(D),
        out_shape=jax.ShapeDtypeStruct(x.shape, F32),
        compiler_params=_cparams(("parallel", "parallel")),
        name="merge_out",
    )(ya, yb, yc, gates, x, wb, wo, g.reshape(1, D))


def _ffn_kernel(x_ref, g_in_ref, wi_ref, wo_ref, g_out_ref, o_ref, acc_ref, *, d_ff, tf):
    x = x_ref[0]
    h = _rms(x, g_in_ref[...]).astype(BF16)
    acc_ref[...] = jnp.zeros(acc_ref.shape, F32)

    def body(c, carry):
        off = pl.multiple_of(c * tf, tf)
        gt = jnp.dot(h, wi_ref[:, pl.ds(off, tf)], preferred_element_type=F32)
        up_off = pl.multiple_of(d_ff + off, LANES)
        up = jnp.dot(h, wi_ref[:, pl.ds(up_off, tf)], preferred_element_type=F32)
        act = (gt * _sigmoid(gt) * up).astype(BF16)
        acc_ref[...] += jnp.dot(act, wo_ref[pl.ds(off, tf), :], preferred_element_type=F32)
        return carry

    lax.fori_loop(0, d_ff // tf, body, 0)
    o_ref[0] = x + _rms(acc_ref[...], g_out_ref[...])


def _ffn(x, g_in, wi, wo, g_out, tm, tf):
    B, S, D = x.shape
    d_ff = wo.shape[0]
    row = pl.BlockSpec((1, tm, D), lambda b, m: (b, m, 0))
    vec = _resident((1, D), lambda b, m: (0, 0))
    return pl.pallas_call(
        functools.partial(_ffn_kernel, d_ff=d_ff, tf=tf),
        grid=(B, S // tm),
        in_specs=[row, vec, _resident(wi.shape, lambda b, m: (0, 0)),
                  _resident(wo.shape, lambda b, m: (0, 0)), vec],
        out_specs=row,
        out_shape=jax.ShapeDtypeStruct(x.shape, F32),
        scratch_shapes=[pltpu.VMEM((tm, D), F32)],
        compiler_params=_cparams(("parallel", "parallel")),
        name="swiglu_ffn",
    )(x, g_in.reshape(1, D), wi, wo, g_out.reshape(1, D))


def _pack_w_in(w):
    d = w.shape[0]
    offs = [0]
    for width in (512, 512, 512, 512, 64, 64, 128, 32, 4, 512, 512, 512, 3072):
        offs.append(offs[-1] + width)
    seg = lambda i: w[:, offs[i]:offs[i + 1]]
    scale = HEAD_DIM ** -0.5
    cols = [seg(0) * scale, seg(1), seg(2), seg(3) * scale,
            seg(4), seg(4), seg(5), seg(5), seg(6),
            seg(7), seg(7), seg(7), seg(7),
            seg(8), jnp.zeros((d, LANES - IDX_HEADS), w.dtype),
            seg(9) * scale, seg(10), seg(11), seg(12)]
    packed = jnp.concatenate(cols, axis=1).astype(BF16)
    assert packed.shape[1] == PACKED_COLS
    return packed


def kernel(x, positions, w_in, w_branch, w_out, lambda_q1, lambda_k1, lambda_q2, lambda_k2,
           subln_g, norm_g, w_ffn_in, w_ffn_out):
    B, S, D = x.shape
    depth = w_in.shape[0]
    tm = min(512, S)
    tab = _rope_table(positions, tm)
    for l in range(depth):
        lam_init = 0.8 - 0.6 * math.exp(-0.3 * l)
        (aq, ak, av, bq, k2, v2, iq, ik4, iw, cq, ck, cv, gates, kmean) = _inproj(
            x, norm_g[l, 0], tab, _pack_w_in(w_in[l]), tm)
        kmean = kmean.reshape(B, S // MOBA_BLOCK, 512)
        lam_rows = jnp.stack([lambda_q1[l], lambda_k1[l], lambda_q2[l], lambda_k2[l]])
        ya = _diff_attn(aq, ak, av, lam_rows, subln_g[l], lam_init, tq=min(512, S), tk=min(512, S))
        yb = _dsa_attn(bq, iq, iw, ik4, k2, v2, tq=128, tk=min(512, S))
        yc = _moba_attn(cq, ck, cv, kmean)
        x = _merge(ya, yb, yc, gates, x, w_branch[l].astype(BF16), w_out[l].astype(BF16),
                   norm_g[l, 1], tm)
        x = _ffn(x, norm_g[l, 2], w_ffn_in[l].astype(BF16), w_ffn_out[l].astype(BF16),
                 norm_g[l, 3], tm, tf=256)
    return x
```

```python
import functools
import math

import jax
import jax.numpy as jnp
from jax import lax
from jax.experimental import pallas as pl
from jax.experimental.pallas import tpu as pltpu

F32 = jnp.float32
BF16 = jnp.bfloat16

LANES = 128
HEAD_DIM = 64
ROPE_THETA = 500000.0
NORM_EPS = 1e-6
IDX_HEADS = 4
IDX_DIM = 32
IDX_TOPK_MAX = 256
MOBA_BLOCK = 256
MOBA_TOPK = 3
N_BRANCH = 3
BRANCH_WIDTH = 512
NEG_BIG = -1e30
LOG2E = math.log2(math.e)
INT_MIN = -2 ** 31
VMEM_LIMIT = 56 * 1024 * 1024

OFF_AQ, OFF_AK, OFF_AV, OFF_BQ = 0, 512, 1024, 1536
OFF_K2, OFF_V2, OFF_IQ, OFF_IK4, OFF_IW = 2048, 2176, 2304, 2432, 2560
OFF_CQ, OFF_CK, OFF_CV, OFF_G = 2688, 3200, 3712, 4224
PACKED_COLS = OFF_G + 3 * 1024


def _nt_dot(a, b):
    return lax.dot_general(a, b, (((1,), (1,)), ((), ())), preferred_element_type=F32)


def _rms(x, g):
    return x * lax.rsqrt(jnp.mean(x * x, axis=-1, keepdims=True) + NORM_EPS) * g


def _cparams(sem):
    return pltpu.CompilerParams(dimension_semantics=sem, vmem_limit_bytes=VMEM_LIMIT)


def _resident(shape, index_map):
    return pl.BlockSpec(shape, index_map, pipeline_mode=pl.Buffered(1))


def _rope_table_kernel(pos_ref, tab_ref):
    pos = pos_ref[0].astype(F32)
    lane = lax.broadcasted_iota(jnp.int32, (1, LANES), 1)

    def tables(head_dim):
        rot = head_dim // 4
        half = rot // 2
        d = lane & (head_dim - 1)
        fi = d & (half - 1)
        inv = jnp.zeros((1, LANES), F32)
        for i in range(half):
            inv = jnp.where(fi == i, F32(ROPE_THETA ** (-(2.0 * i) / rot)), inv)
        ang = pos * inv
        cos = jnp.cos(ang)
        sin = jnp.sin(ang)
        c = jnp.where(d < rot, cos, 1.0)
        s = jnp.where(d < half, -sin, jnp.where(d < rot, sin, 0.0))
        return c, s

    c64, s64 = tables(HEAD_DIM)
    c32, s32 = tables(IDX_DIM)
    tab_ref[0, :, 0 * LANES:1 * LANES] = c64
    tab_ref[0, :, 1 * LANES:2 * LANES] = s64
    tab_ref[0, :, 2 * LANES:3 * LANES] = c32
    tab_ref[0, :, 3 * LANES:4 * LANES] = s32


def _rope_table(positions, tm):
    B, S = positions.shape
    return pl.pallas_call(
        _rope_table_kernel,
        grid=(B, S // tm),
        in_specs=[pl.BlockSpec((1, tm, 1), lambda b, m: (b, m, 0))],
        out_specs=pl.BlockSpec((1, tm, 4 * LANES), lambda b, m: (b, m, 0)),
        out_shape=jax.ShapeDtypeStruct((B, S, 4 * LANES), F32),
        compiler_params=_cparams(("parallel", "parallel")),
        name="rope_table",
    )(positions.reshape(B, S, 1))


def _inproj_kernel(x_ref, g_ref, tab_ref, w_ref,
                   aq_ref, ak_ref, av_ref, bq_ref, k2_ref, v2_ref, iq_ref, ik4_ref, iw_ref,
                   cq_ref, ck_ref, cv_ref, gates_ref, kmean_ref):
    h = _rms(x_ref[0], g_ref[...]).astype(BF16)
    lane = lax.broadcasted_iota(jnp.int32, (1, LANES), 1)

    def proj(off, width):
        return jnp.dot(h, w_ref[:, off:off + width], preferred_element_type=F32)

    def rope_block(y, head_dim, c, s):
        half = head_dim // 8
        up = pltpu.roll(y, LANES - half, 1)
        dn = pltpu.roll(y, half, 1)
        return y * c + jnp.where((lane & (head_dim - 1)) < half, up, dn) * s

    def emit(out_ref, off, width, head_dim=None, dtype=BF16):
        y = proj(off, width)
        if head_dim is None:
            out_ref[0] = y.astype(dtype)
            return None
        t0 = 0 if head_dim == HEAD_DIM else 2 * LANES
        c = tab_ref[0, :, t0:t0 + LANES]
        s = tab_ref[0, :, t0 + LANES:t0 + 2 * LANES]
        blocks = []
        for j in range(width // LANES):
            r = rope_block(y[:, j * LANES:(j + 1) * LANES], head_dim, c, s)
            out_ref[0, :, j * LANES:(j + 1) * LANES] = r.astype(dtype)
            blocks.append(r)
        return blocks

    emit(aq_ref, OFF_AQ, 512, HEAD_DIM)
    emit(ak_ref, OFF_AK, 512, HEAD_DIM)
    emit(av_ref, OFF_AV, 512)
    emit(bq_ref, OFF_BQ, 512, HEAD_DIM)
    emit(k2_ref, OFF_K2, LANES, HEAD_DIM)
    emit(v2_ref, OFF_V2, LANES)
    emit(iq_ref, OFF_IQ, LANES, IDX_DIM)
    emit(ik4_ref, OFF_IK4, LANES, IDX_DIM)
    emit(iw_ref, OFF_IW, LANES, dtype=F32)
    emit(cq_ref, OFF_CQ, 512, HEAD_DIM)
    ck_blocks = emit(ck_ref, OFF_CK, 512, HEAD_DIM)
    emit(cv_ref, OFF_CV, 512)
    emit(gates_ref, OFF_G, 3 * 1024, dtype=F32)

    tm = x_ref.shape[1]
    for blk in range(tm // MOBA_BLOCK):
        for j, r in enumerate(ck_blocks):
            part = r[blk * MOBA_BLOCK:(blk + 1) * MOBA_BLOCK]
            kmean_ref[0, 0, blk:blk + 1, j * LANES:(j + 1) * LANES] = (
                jnp.mean(part, axis=0, keepdims=True))


def _inproj(x, g, tab, w_packed, tm):
    B, S, D = x.shape
    nm = S // tm
    row = lambda width: pl.BlockSpec((1, tm, width), lambda b, m: (b, m, 0))
    shp = lambda width, dt: jax.ShapeDtypeStruct((B, S, width), dt)
    widths = [(512, BF16)] * 4 + [(LANES, BF16)] * 4 + [(LANES, F32)] + [(512, BF16)] * 3 + [(3072, F32)]
    out_specs = [row(w) for w, _ in widths]
    out_shape = [shp(w, dt) for w, dt in widths]
    out_specs.append(pl.BlockSpec((1, 1, tm // MOBA_BLOCK, 512), lambda b, m: (b, m, 0, 0)))
    out_shape.append(jax.ShapeDtypeStruct((B, nm, tm // MOBA_BLOCK, 512), F32))
    return pl.pallas_call(
        _inproj_kernel,
        grid=(B, nm),
        in_specs=[row(D),
                  _resident((1, D), lambda b, m: (0, 0)),
                  row(4 * LANES),
                  _resident((D, PACKED_COLS), lambda b, m: (0, 0))],
        out_specs=out_specs,
        out_shape=out_shape,
        compiler_params=_cparams(("parallel", "parallel")),
        name="inproj",
    )(x, g.reshape(1, D), tab, w_packed)


def _split_heads_rows(q):
    lo = lax.broadcasted_iota(jnp.int32, (1, LANES), 1) < HEAD_DIM
    zero = jnp.zeros_like(q)
    return jnp.concatenate([jnp.where(lo, q, zero), jnp.where(lo, zero, q)], axis=0)


def _lane_blocks(s):
    return [s[:, j * LANES:(j + 1) * LANES] for j in range(s.shape[1] // LANES)]


def _online_update(s, v, m_ref, l_ref, acc_ref):
    _online_update_cols(_lane_blocks(s), v, m_ref, l_ref, acc_ref)


def _online_update_cols(cols, v, m_ref, l_ref, acc_ref):
    m_prev = m_ref[...]
    m_new = jnp.maximum(m_prev, jnp.max(functools.reduce(jnp.maximum, cols), axis=1, keepdims=True))
    alpha = jnp.exp2(m_prev - m_new)
    ps = [jnp.exp2(c - m_new) for c in cols]
    l_ref[...] = alpha * l_ref[...] + functools.reduce(jnp.add, ps)
    p = jnp.concatenate([pj.astype(BF16) for pj in ps], axis=1)
    acc_ref[...] = alpha * acc_ref[...] + jnp.dot(p, v, preferred_element_type=F32)
    m_ref[...] = m_new


def _normalised(l_ref, acc_ref):
    return acc_ref[...] / jnp.sum(l_ref[...], axis=1, keepdims=True)


def _diff_attn_kernel(q_ref, k_ref, v_ref, lam_ref, g_ref, o_ref, m_ref, l_ref, acc_ref,
                      *, tq, tk, lam_init):
    qi = pl.program_id(2)
    q2 = _split_heads_rows(q_ref[0])
    m_ref[...] = jnp.full(m_ref.shape, NEG_BIG, F32)
    l_ref[...] = jnp.zeros(l_ref.shape, F32)
    acc_ref[...] = jnp.zeros(acc_ref.shape, F32)

    def chunk(c, masked):
        off = pl.multiple_of(c * tk, tk)
        kc = k_ref[0, pl.ds(off, tk), :]
        vc = v_ref[0, pl.ds(off, tk), :]
        s = _nt_dot(q2, kc)
        if masked:
            r = lax.broadcasted_iota(jnp.int32, (2 * tq, 1), 0)
            qpos = qi * tq + jnp.where(r >= tq, r - tq, r)
            kpos = off + lax.broadcasted_iota(jnp.int32, (1, tk), 1)
            s = jnp.where(kpos <= qpos, s, NEG_BIG)
        _online_update(s, vc, m_ref, l_ref, acc_ref)

    diag = (qi * tq) // tk

    def body(c, carry):
        chunk(c, False)
        return carry

    lax.fori_loop(0, diag, body, 0)
    chunk(diag, True)

    lam_rows = lam_ref[...]
    e1 = jnp.exp(jnp.sum(lam_rows[0:1] * lam_rows[1:2], axis=1, keepdims=True))
    e2 = jnp.exp(jnp.sum(lam_rows[2:3] * lam_rows[3:4], axis=1, keepdims=True))
    lam = e1 - e2 + lam_init
    o = _normalised(l_ref, acc_ref)
    out = o[:tq] - lam * o[tq:]
    o_ref[0] = (_rms(out, g_ref[...]) * (1.0 - lam_init)).astype(o_ref.dtype)


def _diff_attn(aq, ak, av, lam_rows, subln_g, lam_init, tq, tk):
    B, S, _ = aq.shape
    nh = aq.shape[2] // LANES
    kern = functools.partial(_diff_attn_kernel, tq=tq, tk=tk, lam_init=lam_init)
    return pl.pallas_call(
        kern,
        grid=(B, nh, S // tq),
        in_specs=[pl.BlockSpec((1, tq, LANES), lambda b, h, i: (b, i, h)),
                  pl.BlockSpec((1, S, LANES), lambda b, h, i: (b, 0, h)),
                  pl.BlockSpec((1, S, LANES), lambda b, h, i: (b, 0, h)),
                  pl.BlockSpec((4, HEAD_DIM), lambda b, h, i: (0, 0)),
                  pl.BlockSpec((1, LANES), lambda b, h, i: (0, 0))],
        out_specs=pl.BlockSpec((1, tq, LANES), lambda b, h, i: (b, i, h)),
        out_shape=jax.ShapeDtypeStruct(aq.shape, BF16),
        scratch_shapes=[pltpu.VMEM((2 * tq, LANES), F32), pltpu.VMEM((2 * tq, LANES), F32),
                        pltpu.VMEM((2 * tq, LANES), F32)],
        compiler_params=_cparams(("parallel", "parallel", "arbitrary")),
        name="diff_attn",
    )(aq, ak, av, lam_rows, subln_g.reshape(1, LANES))


def _dsa_kernel(q_ref, iq_ref, iw_ref, ik4_ref, k2_ref, v2_ref, o_ref,
                keys_ref, keys_t_ref, jcut_ref, m_ref, l_ref, acc_ref, *, tq, tk, topk, idx_bits):
    qi = pl.program_id(1)
    nkv = (qi * tq + tq - 1) // tk + 1
    lane = lax.broadcasted_iota(jnp.int32, (1, LANES), 1)
    qpos = qi * tq + lax.broadcasted_iota(jnp.int32, (tq, 1), 0)
    lane_tk = lax.broadcasted_iota(jnp.int32, (1, tk), 1)
    nsub = tk // LANES
    seq = keys_ref.shape[1]

    iq = iq_ref[0]
    zero = jnp.zeros_like(iq)
    iq4 = jnp.concatenate(
        [jnp.where((lane >> 5) == hh, iq, zero) for hh in range(IDX_HEADS)], axis=0)
    iw = iw_ref[0]
    iw_cols = [iw[:, hh:hh + 1] for hh in range(IDX_HEADS)]

    def score_body(c, carry):
        off = pl.multiple_of(c * tk, tk)
        rel = jnp.maximum(_nt_dot(iq4, ik4_ref[0, pl.ds(off, tk), :]), 0.0)
        score = iw_cols[0] * rel[0:tq]
        for hh in range(1, IDX_HEADS):
            score = score + iw_cols[hh] * rel[hh * tq:(hh + 1) * tq]
        score = jnp.where(score == 0.0, 0.0, score)
        bits = lax.bitcast_convert_type(score, jnp.int32)
        key = bits ^ ((bits >> 31) & jnp.int32(0x7FFFFFFF))
        kpos = off + lane_tk
        key = jnp.where(key > 0, key + seq, jnp.where(key == 0, (seq - 1) - kpos, key))
        key = jnp.where(kpos <= qpos, key, jnp.int32(INT_MIN))
        keys_ref[:, pl.ds(off, tk)] = key
        for j in range(nsub):
            keys_t_ref[pl.ds(pl.multiple_of(off + j * LANES, LANES), LANES), :] = (
                key[:, j * LANES:(j + 1) * LANES].T)
        return carry

    lax.fori_loop(0, nkv, score_body, 0)

    rows_tk = lax.broadcasted_iota(jnp.int32, (tk, 1), 0)
    cnt_rows = 64

    def count(pred):
        def body(c, cnt):
            off = pl.multiple_of(c * tk, tk)
            ind = pred(keys_t_ref[pl.ds(off, tk), :], off + rows_tk).astype(jnp.int32)
            for g in range(tk // cnt_rows):
                cnt = cnt + ind[g * cnt_rows:(g + 1) * cnt_rows]
            return cnt
        cnt = lax.fori_loop(0, nkv, body, jnp.zeros((cnt_rows, tq), jnp.int32))
        return jnp.sum(cnt.astype(F32), axis=0, keepdims=True)

    def bit_body(i, t):
        cand = t + jnp.left_shift(jnp.int32(1), 31 - i)
        cnt = count(lambda kc, kpos: kc >= cand)
        return jnp.where(cnt >= topk, cand, t)

    thr = lax.fori_loop(0, 32, bit_body, jnp.full((1, tq), INT_MIN, jnp.int32))
    cnt_ge = count(lambda kc, kpos: kc >= thr)
    need = topk - count(lambda kc, kpos: kc > thr)

    jcut_ref[...] = jnp.full((1, tq), 2 ** 30, jnp.int32)

    @pl.when(jnp.max(cnt_ge) > topk)
    def _():
        def idx_body(i, j):
            cand = j + jnp.left_shift(jnp.int32(1), idx_bits - 1 - i)
            cnt = count(lambda kc, kpos: (kc == thr) & (kpos < cand))
            return jnp.where(cnt < need, cand, j)
        jcut_ref[...] = lax.fori_loop(0, idx_bits, idx_body, jnp.zeros((1, tq), jnp.int32))

    def as_rows(x):
        return jnp.broadcast_to(x, (LANES, tq)).T

    thr_r = as_rows(thr)
    jcut_r = as_rows(jcut_ref[...])

    q = q_ref[0]
    q8 = jnp.concatenate(
        [_split_heads_rows(q[:, p * LANES:(p + 1) * LANES]) for p in range(4)], axis=0)
    nh = 8
    m_ref[...] = jnp.full(m_ref.shape, NEG_BIG, F32)
    l_ref[...] = jnp.zeros(l_ref.shape, F32)
    acc_ref[...] = jnp.zeros(acc_ref.shape, F32)

    def attn_body(c, carry):
        off = pl.multiple_of(c * tk, tk)
        kc = k2_ref[0, pl.ds(off, tk), :]
        vc = v2_ref[0, pl.ds(off, tk), :]
        s = _nt_dot(q8, kc)
        cols = []
        for j in range(nsub):
            keyj = keys_ref[:, pl.ds(pl.multiple_of(off + j * LANES, LANES), LANES)]
            kpos = off + j * LANES + lane
            sel = ((keyj > thr_r) | ((keyj == thr_r) & (kpos <= jcut_r))) & (kpos <= qpos)
            cols.append(jnp.concatenate(
                [jnp.where(sel, s[hh * tq:(hh + 1) * tq, j * LANES:(j + 1) * LANES], NEG_BIG)
                 for hh in range(nh)], axis=0))
        _online_update_cols(cols, vc, m_ref, l_ref, acc_ref)
        return carry

    lax.fori_loop(0, nkv, attn_body, 0)

    o = _normalised(l_ref, acc_ref)
    lo = lane < HEAD_DIM
    for p in range(4):
        o_ref[0, :, p * LANES:(p + 1) * LANES] = jnp.where(
            lo, o[(2 * p) * tq:(2 * p + 1) * tq], o[(2 * p + 1) * tq:(2 * p + 2) * tq]
        ).astype(o_ref.dtype)


def _dsa_attn(bq, iq, iw, ik4, k2, v2, tq, tk):
    B, S, C = bq.shape
    topk = min(IDX_TOPK_MAX, S // 4)
    idx_bits = max(1, (S - 1).bit_length())
    kern = functools.partial(_dsa_kernel, tq=tq, tk=tk, topk=topk, idx_bits=idx_bits)
    row = lambda width: pl.BlockSpec((1, tq, width), lambda b, i: (b, i, 0))
    full = pl.BlockSpec((1, S, LANES), lambda b, i: (b, 0, 0))
    return pl.pallas_call(
        kern,
        grid=(B, S // tq),
        in_specs=[row(C), row(LANES), row(LANES), full, full, full],
        out_specs=row(C),
        out_shape=jax.ShapeDtypeStruct(bq.shape, BF16),
        scratch_shapes=[pltpu.VMEM((tq, S), jnp.int32), pltpu.VMEM((S, tq), jnp.int32),
                        pltpu.VMEM((1, tq), jnp.int32),
                        pltpu.VMEM((8 * tq, LANES), F32), pltpu.VMEM((8 * tq, LANES), F32),
                        pltpu.VMEM((8 * tq, LANES), F32)],
        compiler_params=_cparams(("parallel", "arbitrary")),
        name="dsa_attn",
    )(bq, iq, iw, ik4, k2, v2)


def _moba_kernel(q_ref, k_ref, v_ref, km_ref, o_ref, m_ref, l_ref, acc_ref, *, nb, kb):
    tq = MOBA_BLOCK
    own = pl.program_id(2)
    lane = lax.broadcasted_iota(jnp.int32, (1, LANES), 1)
    lane_f = lane.astype(F32)
    q2 = _split_heads_rows(q_ref[0])

    km = jnp.concatenate([km_ref[0], jnp.zeros((LANES - nb, LANES), F32)], axis=0)
    km_hi = km.astype(BF16)
    r1 = km - km_hi.astype(F32)
    km_mid = r1.astype(BF16)
    km_lo = (r1 - km_mid.astype(F32)).astype(BF16)
    gate = _nt_dot(q2, km_hi) + _nt_dot(q2, km_mid) + _nt_dot(q2, km_lo)
    past = lane < own
    g = jnp.where(past, gate, -jnp.inf)
    sel = jnp.zeros(g.shape, jnp.bool_)
    for _ in range(min(MOBA_TOPK, nb - 1)):
        mx = jnp.max(g, axis=1, keepdims=True)
        first = jnp.min(jnp.where(g == mx, lane_f, float(LANES)), axis=1, keepdims=True)
        pick = lane_f == first
        sel = sel | pick
        g = jnp.where(pick, -jnp.inf, g)
    sel = sel & past
    bias = jnp.where(sel, 0.0, NEG_BIG).astype(BF16)
    qa = jnp.concatenate([q2, bias], axis=1)

    off = pl.multiple_of(own * tq, tq)
    s = _nt_dot(q2, k_ref[0, pl.ds(off, tq), :])
    r = lax.broadcasted_iota(jnp.int32, (2 * tq, 1), 0)
    qloc = jnp.where(r >= tq, r - tq, r)
    kloc = lax.broadcasted_iota(jnp.int32, (1, tq), 1)
    s = jnp.where(kloc <= qloc, s, NEG_BIG)
    m_ref[...] = jnp.full(m_ref.shape, NEG_BIG, F32)
    l_ref[...] = jnp.zeros(l_ref.shape, F32)
    acc_ref[...] = jnp.zeros(acc_ref.shape, F32)
    _online_update(s, v_ref[0, pl.ds(off, tq), :], m_ref, l_ref, acc_ref)

    tkc = kb * tq
    row_blk = lax.broadcasted_iota(jnp.int32, (tkc, 1), 0) >> (MOBA_BLOCK.bit_length() - 1)

    def body(i, carry):
        offi = pl.multiple_of(i * tkc, tkc)
        onehot = jnp.where(lane == i * kb + row_blk, 1.0, 0.0).astype(BF16)
        ka = jnp.concatenate([k_ref[0, pl.ds(offi, tkc), :], onehot], axis=1)
        _online_update(_nt_dot(qa, ka), v_ref[0, pl.ds(offi, tkc), :], m_ref, l_ref, acc_ref)
        return carry

    lax.fori_loop(0, (own + kb - 1) // kb, body, 0)

    o = _normalised(l_ref, acc_ref)
    o_ref[0] = jnp.where(lane < HEAD_DIM, o[:tq], o[tq:]).astype(o_ref.dtype)


def _moba_attn(cq, ck, cv, kmean):
    B, S, C = cq.shape
    nb = S // MOBA_BLOCK
    tq = MOBA_BLOCK
    return pl.pallas_call(
        functools.partial(_moba_kernel, nb=nb, kb=2 if nb % 2 == 0 else 1),
        grid=(B, C // LANES, nb),
        in_specs=[pl.BlockSpec((1, tq, LANES), lambda b, p, i: (b, i, p)),
                  pl.BlockSpec((1, S, LANES), lambda b, p, i: (b, 0, p)),
                  pl.BlockSpec((1, S, LANES), lambda b, p, i: (b, 0, p)),
                  pl.BlockSpec((1, nb, LANES), lambda b, p, i: (b, 0, p))],
        out_specs=pl.BlockSpec((1, tq, LANES), lambda b, p, i: (b, i, p)),
        out_shape=jax.ShapeDtypeStruct(cq.shape, BF16),
        scratch_shapes=[pltpu.VMEM((2 * tq, LANES), F32), pltpu.VMEM((2 * tq, LANES), F32),
                        pltpu.VMEM((2 * tq, LANES), F32)],
        compiler_params=_cparams(("parallel", "parallel", "arbitrary")),
        name="moba_attn",
    )(cq, ck, cv, kmean)


def _sigmoid(x):
    return 1.0 / (1.0 + jnp.exp(-x))


def _merge_kernel(ya_ref, yb_ref, yc_ref, gates_ref, x_ref, wb_ref, wo_ref, g_ref, o_ref):
    d = x_ref.shape[2]
    merged = None
    for n, y_ref in enumerate((ya_ref, yb_ref, yc_ref)):
        br = jnp.dot(y_ref[0], wb_ref[n], preferred_element_type=F32)
        term = _sigmoid(gates_ref[0, :, n * d:(n + 1) * d]) * br
        merged = term if merged is None else merged + term
    m = jnp.dot(merged.astype(BF16), wo_ref[...], preferred_element_type=F32)
    o_ref[0] = x_ref[0] + _rms(m, g_ref[...])


def _merge(ya, yb, yc, gates, x, wb, wo, g, tm):
    B, S, D = x.shape
    row = lambda width: pl.BlockSpec((1, tm, width), lambda b, m: (b, m, 0))
    return pl.pallas_call(
        _merge_kernel,
        grid=(B, S // tm),
        in_specs=[row(BRANCH_WIDTH), row(BRANCH_WIDTH), row(BRANCH_WIDTH), row(N_BRANCH * D), row(D),
                  _resident(wb.shape, lambda b, m: (0, 0, 0)),
                  _resident(wo.shape, lambda b, m: (0, 0)),
                  _resident((1, D), lambda b, m: (0, 0))],
        out_specs=row(D),
        out_shape=jax.ShapeDtypeStruct(x.shape, F32),
        compiler_params=_cparams(("parallel", "parallel")),
        name="merge_out",
    )(ya, yb, yc, gates, x, wb, wo, g.reshape(1, D))


def _ffn_kernel(x_ref, g_in_ref, wi_ref, wo_ref, g_out_ref, o_ref, acc_ref, *, d_ff, tf):
    x = x_ref[0]
    h = _rms(x, g_in_ref[...]).astype(BF16)
    acc_ref[...] = jnp.zeros(acc_ref.shape, F32)

    def body(c, carry):
        off = pl.multiple_of(c * tf, tf)
        gt = jnp.dot(h, wi_ref[:, pl.ds(off, tf)], preferred_element_type=F32)
        up_off = pl.multiple_of(d_ff + off, LANES)
        up = jnp.dot(h, wi_ref[:, pl.ds(up_off, tf)], preferred_element_type=F32)
        act = (gt * _sigmoid(gt) * up).astype(BF16)
        acc_ref[...] += jnp.dot(act, wo_ref[pl.ds(off, tf), :], preferred_element_type=F32)
        return carry

    lax.fori_loop(0, d_ff // tf, body, 0)
    o_ref[0] = x + _rms(acc_ref[...], g_out_ref[...])


def _ffn(x, g_in, wi, wo, g_out, tm, tf):
    B, S, D = x.shape
    d_ff = wo.shape[0]
    row = pl.BlockSpec((1, tm, D), lambda b, m: (b, m, 0))
    vec = _resident((1, D), lambda b, m: (0, 0))
    return pl.pallas_call(
        functools.partial(_ffn_kernel, d_ff=d_ff, tf=tf),
        grid=(B, S // tm),
        in_specs=[row, vec, _resident(wi.shape, lambda b, m: (0, 0)),
                  _resident(wo.shape, lambda b, m: (0, 0)), vec],
        out_specs=row,
        out_shape=jax.ShapeDtypeStruct(x.shape, F32),
        scratch_shapes=[pltpu.VMEM((tm, D), F32)],
        compiler_params=_cparams(("parallel", "parallel")),
        name="swiglu_ffn",
    )(x, g_in.reshape(1, D), wi, wo, g_out.reshape(1, D))


def _pack_w_in(w):
    d = w.shape[0]
    offs = [0]
    for width in (512, 512, 512, 512, 64, 64, 128, 32, 4, 512, 512, 512, 3072):
        offs.append(offs[-1] + width)
    seg = lambda i: w[:, offs[i]:offs[i + 1]]
    scale = HEAD_DIM ** -0.5 * LOG2E
    cols = [seg(0) * scale, seg(1), seg(2), seg(3) * scale,
            seg(4), seg(4), seg(5), seg(5), seg(6),
            seg(7), seg(7), seg(7), seg(7),
            seg(8), jnp.zeros((d, LANES - IDX_HEADS), w.dtype),
            seg(9) * scale, seg(10), seg(11), seg(12)]
    packed = jnp.concatenate(cols, axis=1).astype(BF16)
    assert packed.shape[1] == PACKED_COLS
    return packed


def kernel(x, positions, w_in, w_branch, w_out, lambda_q1, lambda_k1, lambda_q2, lambda_k2,
           subln_g, norm_g, w_ffn_in, w_ffn_out):
    B, S, D = x.shape
    depth = w_in.shape[0]
    tm = min(512, S)
    tab = _rope_table(positions, tm)
    for l in range(depth):
        lam_init = 0.8 - 0.6 * math.exp(-0.3 * l)
        (aq, ak, av, bq, k2, v2, iq, ik4, iw, cq, ck, cv, gates, kmean) = _inproj(
            x, norm_g[l, 0], tab, _pack_w_in(w_in[l]), tm)
        kmean = kmean.reshape(B, S // MOBA_BLOCK, 512)
        lam_rows = jnp.stack([lambda_q1[l], lambda_k1[l], lambda_q2[l], lambda_k2[l]])
        ya = _diff_attn(aq, ak, av, lam_rows, subln_g[l], lam_init, tq=min(512, S), tk=min(512, S))
        yb = _dsa_attn(bq, iq, iw, ik4, k2, v2, tq=128, tk=min(512, S))
        yc = _moba_attn(cq, ck, cv, kmean)
        x = _merge(ya, yb, yc, gates, x, w_branch[l].astype(BF16), w_out[l].astype(BF16),
                   norm_g[l, 1], tm)
        x = _ffn(x, norm_g[l, 2], w_ffn_in[l].astype(BF16), w_ffn_out[l].astype(BF16),
                 norm_g[l, 3], tm, tf=256)
    return x
```

```python
import functools
import math

import jax
import jax.numpy as jnp
from jax import lax
from jax.experimental import pallas as pl
from jax.experimental.pallas import tpu as pltpu

F32 = jnp.float32
BF16 = jnp.bfloat16

LANES = 128
HEAD_DIM = 64
ROPE_THETA = 500000.0
NORM_EPS = 1e-6
IDX_HEADS = 4
IDX_DIM = 32
IDX_TOPK_MAX = 256
MOBA_BLOCK = 256
MOBA_TOPK = 3
N_BRANCH = 3
BRANCH_WIDTH = 512
NEG_BIG = -1e30
LOG2E = math.log2(math.e)
INT_MIN = -2 ** 31
VMEM_LIMIT = 56 * 1024 * 1024

OFF_AQ, OFF_AK, OFF_AV, OFF_BQ = 0, 512, 1024, 1536
OFF_K2, OFF_V2, OFF_IQ, OFF_IK4, OFF_IW = 2048, 2176, 2304, 2432, 2560
OFF_CQ, OFF_CK, OFF_CV, OFF_G = 2688, 3200, 3712, 4224
PACKED_COLS = OFF_G + 3 * 1024


def _nt_dot(a, b):
    return lax.dot_general(a, b, (((1,), (1,)), ((), ())), preferred_element_type=F32)


def _rms(x, g):
    return x * lax.rsqrt(jnp.mean(x * x, axis=-1, keepdims=True) + NORM_EPS) * g


def _cparams(sem):
    return pltpu.CompilerParams(dimension_semantics=sem, vmem_limit_bytes=VMEM_LIMIT)


def _resident(shape, index_map):
    return pl.BlockSpec(shape, index_map, pipeline_mode=pl.Buffered(1))


def _rope_table_kernel(pos_ref, tab_ref):
    pos = pos_ref[0].astype(F32)
    lane = lax.broadcasted_iota(jnp.int32, (1, LANES), 1)

    def tables(head_dim):
        rot = head_dim // 4
        half = rot // 2
        d = lane & (head_dim - 1)
        fi = d & (half - 1)
        inv = jnp.zeros((1, LANES), F32)
        for i in range(half):
            inv = jnp.where(fi == i, F32(ROPE_THETA ** (-(2.0 * i) / rot)), inv)
        ang = pos * inv
        cos = jnp.cos(ang)
        sin = jnp.sin(ang)
        c = jnp.where(d < rot, cos, 1.0)
        s = jnp.where(d < half, -sin, jnp.where(d < rot, sin, 0.0))
        return c, s

    c64, s64 = tables(HEAD_DIM)
    c32, s32 = tables(IDX_DIM)
    tab_ref[0, :, 0 * LANES:1 * LANES] = c64
    tab_ref[0, :, 1 * LANES:2 * LANES] = s64
    tab_ref[0, :, 2 * LANES:3 * LANES] = c32
    tab_ref[0, :, 3 * LANES:4 * LANES] = s32


def _rope_table(positions, tm):
    B, S = positions.shape
    return pl.pallas_call(
        _rope_table_kernel,
        grid=(B, S // tm),
        in_specs=[pl.BlockSpec((1, tm, 1), lambda b, m: (b, m, 0))],
        out_specs=pl.BlockSpec((1, tm, 4 * LANES), lambda b, m: (b, m, 0)),
        out_shape=jax.ShapeDtypeStruct((B, S, 4 * LANES), F32),
        compiler_params=_cparams(("parallel", "parallel")),
        name="rope_table",
    )(positions.reshape(B, S, 1))


def _inproj_kernel(x_ref, g_ref, tab_ref, w_ref,
                   aq_ref, ak_ref, av_ref, bq_ref, k2_ref, v2_ref, iq_ref, ik4_ref, iw_ref,
                   cq_ref, ck_ref, cv_ref, gates_ref, kmean_ref):
    h = _rms(x_ref[0], g_ref[...]).astype(BF16)
    lane = lax.broadcasted_iota(jnp.int32, (1, LANES), 1)

    def proj(off, width):
        return jnp.dot(h, w_ref[:, off:off + width], preferred_element_type=F32)

    def rope_block(y, head_dim, c, s):
        half = head_dim // 8
        up = pltpu.roll(y, LANES - half, 1)
        dn = pltpu.roll(y, half, 1)
        return y * c + jnp.where((lane & (head_dim - 1)) < half, up, dn) * s

    def emit(out_ref, off, width, head_dim=None, dtype=BF16):
        y = proj(off, width)
        if head_dim is None:
            out_ref[0] = y.astype(dtype)
            return None
        t0 = 0 if head_dim == HEAD_DIM else 2 * LANES
        c = tab_ref[0, :, t0:t0 + LANES]
        s = tab_ref[0, :, t0 + LANES:t0 + 2 * LANES]
        blocks = []
        for j in range(width // LANES):
            r = rope_block(y[:, j * LANES:(j + 1) * LANES], head_dim, c, s)
            out_ref[0, :, j * LANES:(j + 1) * LANES] = r.astype(dtype)
            blocks.append(r)
        return blocks

    emit(aq_ref, OFF_AQ, 512, HEAD_DIM)
    emit(ak_ref, OFF_AK, 512, HEAD_DIM)
    emit(av_ref, OFF_AV, 512)
    emit(bq_ref, OFF_BQ, 512, HEAD_DIM)
    emit(k2_ref, OFF_K2, LANES, HEAD_DIM)
    emit(v2_ref, OFF_V2, LANES)
    emit(iq_ref, OFF_IQ, LANES, IDX_DIM)
    emit(ik4_ref, OFF_IK4, LANES, IDX_DIM)
    emit(iw_ref, OFF_IW, LANES, dtype=F32)
    emit(cq_ref, OFF_CQ, 512, HEAD_DIM)
    ck_blocks = emit(ck_ref, OFF_CK, 512, HEAD_DIM)
    emit(cv_ref, OFF_CV, 512)
    emit(gates_ref, OFF_G, 3 * 1024, dtype=F32)

    tm = x_ref.shape[1]
    for blk in range(tm // MOBA_BLOCK):
        for j, r in enumerate(ck_blocks):
            part = r[blk * MOBA_BLOCK:(blk + 1) * MOBA_BLOCK]
            kmean_ref[0, 0, blk:blk + 1, j * LANES:(j + 1) * LANES] = (
                jnp.mean(part, axis=0, keepdims=True))


def _inproj(x, g, tab, w_packed, tm):
    B, S, D = x.shape
    nm = S // tm
    row = lambda width: pl.BlockSpec((1, tm, width), lambda b, m: (b, m, 0))
    shp = lambda width, dt: jax.ShapeDtypeStruct((B, S, width), dt)
    widths = [(512, BF16)] * 4 + [(LANES, BF16)] * 4 + [(LANES, F32)] + [(512, BF16)] * 3 + [(3072, F32)]
    out_specs = [row(w) for w, _ in widths]
    out_shape = [shp(w, dt) for w, dt in widths]
    out_specs.append(pl.BlockSpec((1, 1, tm // MOBA_BLOCK, 512), lambda b, m: (b, m, 0, 0)))
    out_shape.append(jax.ShapeDtypeStruct((B, nm, tm // MOBA_BLOCK, 512), F32))
    return pl.pallas_call(
        _inproj_kernel,
        grid=(B, nm),
        in_specs=[row(D),
                  _resident((1, D), lambda b, m: (0, 0)),
                  row(4 * LANES),
                  _resident((D, PACKED_COLS), lambda b, m: (0, 0))],
        out_specs=out_specs,
        out_shape=out_shape,
        compiler_params=_cparams(("parallel", "parallel")),
        name="inproj",
    )(x, g.reshape(1, D), tab, w_packed)


def _split_heads_rows(q):
    lo = lax.broadcasted_iota(jnp.int32, (1, LANES), 1) < HEAD_DIM
    zero = jnp.zeros_like(q)
    return jnp.concatenate([jnp.where(lo, q, zero), jnp.where(lo, zero, q)], axis=0)


def _lane_blocks(s):
    return [s[:, j * LANES:(j + 1) * LANES] for j in range(s.shape[1] // LANES)]


def _online_update(s, v, m_ref, l_ref, acc_ref):
    _online_update_cols(_lane_blocks(s), v, m_ref, l_ref, acc_ref)


def _online_update_cols(cols, v, m_ref, l_ref, acc_ref):
    m_prev = m_ref[...]
    m_new = jnp.maximum(m_prev, jnp.max(functools.reduce(jnp.maximum, cols), axis=1, keepdims=True))
    alpha = jnp.exp2(m_prev - m_new)
    ps = [jnp.exp2(c - m_new) for c in cols]
    l_ref[...] = alpha * l_ref[...] + functools.reduce(jnp.add, ps)
    p = jnp.concatenate([pj.astype(BF16) for pj in ps], axis=1)
    acc_ref[...] = alpha * acc_ref[...] + jnp.dot(p, v, preferred_element_type=F32)
    m_ref[...] = m_new


def _normalised(l_ref, acc_ref):
    return acc_ref[...] / jnp.sum(l_ref[...], axis=1, keepdims=True)


def _diff_attn_kernel(q_ref, k_ref, v_ref, lam_ref, g_ref, o_ref, m_ref, l_ref, acc_ref,
                      *, tq, tk, lam_init):
    qi = pl.program_id(2)
    q2 = _split_heads_rows(q_ref[0])
    m_ref[...] = jnp.full(m_ref.shape, NEG_BIG, F32)
    l_ref[...] = jnp.zeros(l_ref.shape, F32)
    acc_ref[...] = jnp.zeros(acc_ref.shape, F32)

    def chunk(c, masked):
        off = pl.multiple_of(c * tk, tk)
        kc = k_ref[0, pl.ds(off, tk), :]
        vc = v_ref[0, pl.ds(off, tk), :]
        s = _nt_dot(q2, kc)
        if masked:
            r = lax.broadcasted_iota(jnp.int32, (2 * tq, 1), 0)
            qpos = qi * tq + jnp.where(r >= tq, r - tq, r)
            kpos = off + lax.broadcasted_iota(jnp.int32, (1, tk), 1)
            s = jnp.where(kpos <= qpos, s, NEG_BIG)
        _online_update(s, vc, m_ref, l_ref, acc_ref)

    diag = (qi * tq) // tk

    def body(c, carry):
        chunk(c, False)
        return carry

    lax.fori_loop(0, diag, body, 0)
    chunk(diag, True)

    lam_rows = lam_ref[...]
    e1 = jnp.exp(jnp.sum(lam_rows[0:1] * lam_rows[1:2], axis=1, keepdims=True))
    e2 = jnp.exp(jnp.sum(lam_rows[2:3] * lam_rows[3:4], axis=1, keepdims=True))
    lam = e1 - e2 + lam_init
    o = _normalised(l_ref, acc_ref)
    out = o[:tq] - lam * o[tq:]
    o_ref[0] = (_rms(out, g_ref[...]) * (1.0 - lam_init)).astype(o_ref.dtype)


def _diff_attn(aq, ak, av, lam_rows, subln_g, lam_init, tq, tk):
    B, S, _ = aq.shape
    nh = aq.shape[2] // LANES
    kern = functools.partial(_diff_attn_kernel, tq=tq, tk=tk, lam_init=lam_init)
    return pl.pallas_call(
        kern,
        grid=(B, nh, S // tq),
        in_specs=[pl.BlockSpec((1, tq, LANES), lambda b, h, i: (b, i, h)),
                  pl.BlockSpec((1, S, LANES), lambda b, h, i: (b, 0, h)),
                  pl.BlockSpec((1, S, LANES), lambda b, h, i: (b, 0, h)),
                  pl.BlockSpec((4, HEAD_DIM), lambda b, h, i: (0, 0)),
                  pl.BlockSpec((1, LANES), lambda b, h, i: (0, 0))],
        out_specs=pl.BlockSpec((1, tq, LANES), lambda b, h, i: (b, i, h)),
        out_shape=jax.ShapeDtypeStruct(aq.shape, BF16),
        scratch_shapes=[pltpu.VMEM((2 * tq, LANES), F32), pltpu.VMEM((2 * tq, LANES), F32),
                        pltpu.VMEM((2 * tq, LANES), F32)],
        compiler_params=_cparams(("parallel", "parallel", "arbitrary")),
        name="diff_attn",
    )(aq, ak, av, lam_rows, subln_g.reshape(1, LANES))


def _dsa_kernel(q_ref, iq_ref, iw_ref, ik4_ref, k2_ref, v2_ref, o_ref,
                keys_ref, keys_t_ref, hi_t_ref, lo_t_ref, jcut_ref, m_ref, l_ref, acc_ref,
                *, tq, tk, topk, idx_bits):
    qi = pl.program_id(1)
    nkv = (qi * tq + tq - 1) // tk + 1
    lane = lax.broadcasted_iota(jnp.int32, (1, LANES), 1)
    qpos = qi * tq + lax.broadcasted_iota(jnp.int32, (tq, 1), 0)
    lane_tk = lax.broadcasted_iota(jnp.int32, (1, tk), 1)
    nsub = tk // LANES
    seq = keys_ref.shape[1]

    iq = iq_ref[0]
    zero = jnp.zeros_like(iq)
    iq4 = jnp.concatenate(
        [jnp.where((lane >> 5) == hh, iq, zero) for hh in range(IDX_HEADS)], axis=0)
    iw = iw_ref[0]
    iw_cols = [iw[:, hh:hh + 1] for hh in range(IDX_HEADS)]

    def score_body(c, carry):
        off = pl.multiple_of(c * tk, tk)
        rel = jnp.maximum(_nt_dot(iq4, ik4_ref[0, pl.ds(off, tk), :]), 0.0)
        score = iw_cols[0] * rel[0:tq]
        for hh in range(1, IDX_HEADS):
            score = score + iw_cols[hh] * rel[hh * tq:(hh + 1) * tq]
        score = jnp.where(score == 0.0, 0.0, score)
        bits = lax.bitcast_convert_type(score, jnp.int32)
        key = bits ^ ((bits >> 31) & jnp.int32(0x7FFFFFFF))
        kpos = off + lane_tk
        key = jnp.where(key > 0, key + seq, jnp.where(key == 0, (seq - 1) - kpos, key))
        key = jnp.where(kpos <= qpos, key, jnp.int32(INT_MIN))
        keys_ref[:, pl.ds(off, tk)] = key
        for j in range(nsub):
            rows = pl.ds(pl.multiple_of(off + j * LANES, LANES), LANES)
            kt = key[:, j * LANES:(j + 1) * LANES].T
            keys_t_ref[rows, :] = kt
            hi_t_ref[rows, :] = (kt >> 16).astype(jnp.int16)
        return carry

    lax.fori_loop(0, nkv, score_body, 0)

    rows_tk = lax.broadcasted_iota(jnp.int32, (tk, 1), 0)
    cnt_rows = 64

    def count(pred):
        def body(c, cnt):
            off = pl.multiple_of(c * tk, tk)
            ind = pred(keys_t_ref[pl.ds(off, tk), :], off + rows_tk).astype(jnp.int32)
            for g in range(tk // cnt_rows):
                cnt = cnt + ind[g * cnt_rows:(g + 1) * cnt_rows]
            return cnt
        cnt = lax.fori_loop(0, nkv, body, jnp.zeros((cnt_rows, tq), jnp.int32))
        return jnp.sum(cnt.astype(F32), axis=0, keepdims=True)

    cnt_rows16 = 128

    def count16(ref, cand):
        cand16 = cand.astype(jnp.int16)

        def body(c, cnt):
            off = pl.multiple_of(c * tk, tk)
            ind = jnp.where(ref[pl.ds(off, tk), :] >= cand16, jnp.int16(1), jnp.int16(0))
            for g in range(tk // cnt_rows16):
                cnt = cnt + ind[g * cnt_rows16:(g + 1) * cnt_rows16]
            return cnt
        cnt = lax.fori_loop(0, nkv, body, jnp.zeros((cnt_rows16, tq), jnp.int16))
        return jnp.sum(cnt.astype(jnp.int32).astype(F32), axis=0, keepdims=True)

    def search16(ref, target):
        def bit_body(i, t):
            cand = t + jnp.left_shift(jnp.int32(1), 15 - i)
            return jnp.where(count16(ref, cand) >= target, cand, t)
        return lax.fori_loop(0, 16, bit_body, jnp.full((1, tq), -2 ** 15, jnp.int32))

    thr_hi = search16(hi_t_ref, topk)
    need_lo = topk - count16(hi_t_ref, thr_hi + 1)

    def lo_body(c, carry):
        off = pl.multiple_of(c * tk, tk)
        kt = keys_t_ref[pl.ds(off, tk), :]
        lo = (kt & 0xFFFF) - 2 ** 15
        lo_t_ref[pl.ds(off, tk), :] = jnp.where((kt >> 16) == thr_hi, lo, -2 ** 15).astype(jnp.int16)
        return carry

    lax.fori_loop(0, nkv, lo_body, 0)
    thr = thr_hi * 2 ** 16 + (search16(lo_t_ref, need_lo) + 2 ** 15)
    cnt_ge = count(lambda kc, kpos: kc >= thr)
    need = topk - count(lambda kc, kpos: kc > thr)

    jcut_ref[...] = jnp.full((1, tq), 2 ** 30, jnp.int32)

    @pl.when(jnp.max(cnt_ge) > topk)
    def _():
        def idx_body(i, j):
            cand = j + jnp.left_shift(jnp.int32(1), idx_bits - 1 - i)
            cnt = count(lambda kc, kpos: (kc == thr) & (kpos < cand))
            return jnp.where(cnt < need, cand, j)
        jcut_ref[...] = lax.fori_loop(0, idx_bits, idx_body, jnp.zeros((1, tq), jnp.int32))

    def as_rows(x):
        return jnp.broadcast_to(x, (LANES, tq)).T

    thr_r = as_rows(thr)
    jcut_r = as_rows(jcut_ref[...])

    q = q_ref[0]
    q8 = jnp.concatenate(
        [_split_heads_rows(q[:, p * LANES:(p + 1) * LANES]) for p in range(4)], axis=0)
    nh = 8
    m_ref[...] = jnp.full(m_ref.shape, NEG_BIG, F32)
    l_ref[...] = jnp.zeros(l_ref.shape, F32)
    acc_ref[...] = jnp.zeros(acc_ref.shape, F32)

    def attn_body(c, carry):
        off = pl.multiple_of(c * tk, tk)
        kc = k2_ref[0, pl.ds(off, tk), :]
        vc = v2_ref[0, pl.ds(off, tk), :]
        s = _nt_dot(q8, kc)
        cols = []
        for j in range(nsub):
            keyj = keys_ref[:, pl.ds(pl.multiple_of(off + j * LANES, LANES), LANES)]
            kpos = off + j * LANES + lane
            sel = ((keyj > thr_r) | ((keyj == thr_r) & (kpos <= jcut_r))) & (kpos <= qpos)
            cols.append(jnp.concatenate(
                [jnp.where(sel, s[hh * tq:(hh + 1) * tq, j * LANES:(j + 1) * LANES], NEG_BIG)
                 for hh in range(nh)], axis=0))
        _online_update_cols(cols, vc, m_ref, l_ref, acc_ref)
        return carry

    lax.fori_loop(0, nkv, attn_body, 0)

    o = _normalised(l_ref, acc_ref)
    lo = lane < HEAD_DIM
    for p in range(4):
        o_ref[0, :, p * LANES:(p + 1) * LANES] = jnp.where(
            lo, o[(2 * p) * tq:(2 * p + 1) * tq], o[(2 * p + 1) * tq:(2 * p + 2) * tq]
        ).astype(o_ref.dtype)


def _dsa_attn(bq, iq, iw, ik4, k2, v2, tq, tk):
    B, S, C = bq.shape
    topk = min(IDX_TOPK_MAX, S // 4)
    idx_bits = max(1, (S - 1).bit_length())
    kern = functools.partial(_dsa_kernel, tq=tq, tk=tk, topk=topk, idx_bits=idx_bits)
    row = lambda width: pl.BlockSpec((1, tq, width), lambda b, i: (b, i, 0))
    full = pl.BlockSpec((1, S, LANES), lambda b, i: (b, 0, 0))
    return pl.pallas_call(
        kern,
        grid=(B, S // tq),
        in_specs=[row(C), row(LANES), row(LANES), full, full, full],
        out_specs=row(C),
        out_shape=jax.ShapeDtypeStruct(bq.shape, BF16),
        scratch_shapes=[pltpu.VMEM((tq, S), jnp.int32), pltpu.VMEM((S, tq), jnp.int32),
                        pltpu.VMEM((S, tq), jnp.int16), pltpu.VMEM((S, tq), jnp.int16),
                        pltpu.VMEM((1, tq), jnp.int32),
                        pltpu.VMEM((8 * tq, LANES), F32), pltpu.VMEM((8 * tq, LANES), F32),
                        pltpu.VMEM((8 * tq, LANES), F32)],
        compiler_params=_cparams(("parallel", "arbitrary")),
        name="dsa_attn",
    )(bq, iq, iw, ik4, k2, v2)


def _moba_kernel(q_ref, k_ref, v_ref, km_ref, o_ref, m_ref, l_ref, acc_ref, *, nb, kb):
    tq = MOBA_BLOCK
    own = pl.program_id(2)
    lane = lax.broadcasted_iota(jnp.int32, (1, LANES), 1)
    lane_f = lane.astype(F32)
    q2 = _split_heads_rows(q_ref[0])

    km = jnp.concatenate([km_ref[0], jnp.zeros((LANES - nb, LANES), F32)], axis=0)
    km_hi = km.astype(BF16)
    r1 = km - km_hi.astype(F32)
    km_mid = r1.astype(BF16)
    km_lo = (r1 - km_mid.astype(F32)).astype(BF16)
    gate = _nt_dot(q2, km_hi) + _nt_dot(q2, km_mid) + _nt_dot(q2, km_lo)
    past = lane < own
    g = jnp.where(past, gate, -jnp.inf)
    sel = jnp.zeros(g.shape, jnp.bool_)
    for _ in range(min(MOBA_TOPK, nb - 1)):
        mx = jnp.max(g, axis=1, keepdims=True)
        first = jnp.min(jnp.where(g == mx, lane_f, float(LANES)), axis=1, keepdims=True)
        pick = lane_f == first
        sel = sel | pick
        g = jnp.where(pick, -jnp.inf, g)
    sel = sel & past
    bias = jnp.where(sel, 0.0, NEG_BIG).astype(BF16)
    qa = jnp.concatenate([q2, bias], axis=1)

    off = pl.multiple_of(own * tq, tq)
    s = _nt_dot(q2, k_ref[0, pl.ds(off, tq), :])
    r = lax.broadcasted_iota(jnp.int32, (2 * tq, 1), 0)
    qloc = jnp.where(r >= tq, r - tq, r)
    kloc = lax.broadcasted_iota(jnp.int32, (1, tq), 1)
    s = jnp.where(kloc <= qloc, s, NEG_BIG)
    m_ref[...] = jnp.full(m_ref.shape, NEG_BIG, F32)
    l_ref[...] = jnp.zeros(l_ref.shape, F32)
    acc_ref[...] = jnp.zeros(acc_ref.shape, F32)
    _online_update(s, v_ref[0, pl.ds(off, tq), :], m_ref, l_ref, acc_ref)

    tkc = kb * tq
    row_blk = lax.broadcasted_iota(jnp.int32, (tkc, 1), 0) >> (MOBA_BLOCK.bit_length() - 1)

    def body(i, carry):
        offi = pl.multiple_of(i * tkc, tkc)
        onehot = jnp.where(lane == i * kb + row_blk, 1.0, 0.0).astype(BF16)
        ka = jnp.concatenate([k_ref[0, pl.ds(offi, tkc), :], onehot], axis=1)
        _online_update(_nt_dot(qa, ka), v_ref[0, pl.ds(offi, tkc), :], m_ref, l_ref, acc_ref)
        return carry

    lax.fori_loop(0, (own + kb - 1) // kb, body, 0)

    o = _normalised(l_ref, acc_ref)
    o_ref[0] = jnp.where(lane < HEAD_DIM, o[:tq], o[tq:]).astype(o_ref.dtype)


def _moba_attn(cq, ck, cv, kmean):
    B, S, C = cq.shape
    nb = S // MOBA_BLOCK
    tq = MOBA_BLOCK
    return pl.pallas_call(
        functools.partial(_moba_kernel, nb=nb, kb=2 if nb % 2 == 0 else 1),
        grid=(B, C // LANES, nb),
        in_specs=[pl.BlockSpec((1, tq, LANES), lambda b, p, i: (b, i, p)),
                  pl.BlockSpec((1, S, LANES), lambda b, p, i: (b, 0, p)),
                  pl.BlockSpec((1, S, LANES), lambda b, p, i: (b, 0, p)),
                  pl.BlockSpec((1, nb, LANES), lambda b, p, i: (b, 0, p))],
        out_specs=pl.BlockSpec((1, tq, LANES), lambda b, p, i: (b, i, p)),
        out_shape=jax.ShapeDtypeStruct(cq.shape, BF16),
        scratch_shapes=[pltpu.VMEM((2 * tq, LANES), F32), pltpu.VMEM((2 * tq, LANES), F32),
                        pltpu.VMEM((2 * tq, LANES), F32)],
        compiler_params=_cparams(("parallel", "parallel", "arbitrary")),
        name="moba_attn",
    )(cq, ck, cv, kmean)


def _sigmoid(x):
    return 1.0 / (1.0 + jnp.exp(-x))


def _merge_kernel(ya_ref, yb_ref, yc_ref, gates_ref, x_ref, wb_ref, wo_ref, g_ref, o_ref):
    d = x_ref.shape[2]
    merged = None
    for n, y_ref in enumerate((ya_ref, yb_ref, yc_ref)):
        br = jnp.dot(y_ref[0], wb_ref[n], preferred_element_type=F32)
        term = _sigmoid(gates_ref[0, :, n * d:(n + 1) * d]) * br
        merged = term if merged is None else merged + term
    m = jnp.dot(merged.astype(BF16), wo_ref[...], preferred_element_type=F32)
    o_ref[0] = x_ref[0] + _rms(m, g_ref[...])


def _merge(ya, yb, yc, gates, x, wb, wo, g, tm):
    B, S, D = x.shape
    row = lambda width: pl.BlockSpec((1, tm, width), lambda b, m: (b, m, 0))
    return pl.pallas_call(
        _merge_kernel,
        grid=(B, S // tm),
        in_specs=[row(BRANCH_WIDTH), row(BRANCH_WIDTH), row(BRANCH_WIDTH), row(N_BRANCH * D), row(D),
                  _resident(wb.shape, lambda b, m: (0, 0, 0)),
                  _resident(wo.shape, lambda b, m: (0, 0)),
                  _resident((1, D), lambda b, m: (0, 0))],
        out_specs=row(D),
        out_shape=jax.ShapeDtypeStruct(x.shape, F32),
        compiler_params=_cparams(("parallel", "parallel")),
        name="merge_out",
    )(ya, yb, yc, gates, x, wb, wo, g.reshape(1, D))


def _ffn_kernel(x_ref, g_in_ref, wi_ref, wo_ref, g_out_ref, o_ref, acc_ref, *, d_ff, tf):
    x = x_ref[0]
    h = _rms(x, g_in_ref[...]).astype(BF16)
    acc_ref[...] = jnp.zeros(acc_ref.shape, F32)

    def body(c, carry):
        off = pl.multiple_of(c * tf, tf)
        gt = jnp.dot(h, wi_ref[:, pl.ds(off, tf)], preferred_element_type=F32)
        up_off = pl.multiple_of(d_ff + off, LANES)
        up = jnp.dot(h, wi_ref[:, pl.ds(up_off, tf)], preferred_element_type=F32)
        act = (gt * _sigmoid(gt) * up).astype(BF16)
        acc_ref[...] += jnp.dot(act, wo_ref[pl.ds(off, tf), :], preferred_element_type=F32)
        return carry

    lax.fori_loop(0, d_ff // tf, body, 0)
    o_ref[0] = x + _rms(acc_ref[...], g_out_ref[...])


def _ffn(x, g_in, wi, wo, g_out, tm, tf):
    B, S, D = x.shape
    d_ff = wo.shape[0]
    row = pl.BlockSpec((1, tm, D), lambda b, m: (b, m, 0))
    vec = _resident((1, D), lambda b, m: (0, 0))
    return pl.pallas_call(
        functools.partial(_ffn_kernel, d_ff=d_ff, tf=tf),
        grid=(B, S // tm),
        in_specs=[row, vec, _resident(wi.shape, lambda b, m: (0, 0)),
                  _resident(wo.shape, lambda b, m: (0, 0)), vec],
        out_specs=row,
        out_shape=jax.ShapeDtypeStruct(x.shape, F32),
        scratch_shapes=[pltpu.VMEM((tm, D), F32)],
        compiler_params=_cparams(("parallel", "parallel")),
        name="swiglu_ffn",
    )(x, g_in.reshape(1, D), wi, wo, g_out.reshape(1, D))


def _pack_w_in(w):
    d = w.shape[0]
    offs = [0]
    for width in (512, 512, 512, 512, 64, 64, 128, 32, 4, 512, 512, 512, 3072):
        offs.append(offs[-1] + width)
    seg = lambda i: w[:, offs[i]:offs[i + 1]]
    scale = HEAD_DIM ** -0.5 * LOG2E
    cols = [seg(0) * scale, seg(1), seg(2), seg(3) * scale,
            seg(4), seg(4), seg(5), seg(5), seg(6),
            seg(7), seg(7), seg(7), seg(7),
            seg(8), jnp.zeros((d, LANES - IDX_HEADS), w.dtype),
            seg(9) * scale, seg(10), seg(11), seg(12)]
    packed = jnp.concatenate(cols, axis=1).astype(BF16)
    assert packed.shape[1] == PACKED_COLS
    return packed


def kernel(x, positions, w_in, w_branch, w_out, lambda_q1, lambda_k1, lambda_q2, lambda_k2,
           subln_g, norm_g, w_ffn_in, w_ffn_out):
    B, S, D = x.shape
    depth = w_in.shape[0]
    tm = min(512, S)
    tab = _rope_table(positions, tm)
    for l in range(depth):
        lam_init = 0.8 - 0.6 * math.exp(-0.3 * l)
        (aq, ak, av, bq, k2, v2, iq, ik4, iw, cq, ck, cv, gates, kmean) = _inproj(
            x, norm_g[l, 0], tab, _pack_w_in(w_in[l]), tm)
        kmean = kmean.reshape(B, S // MOBA_BLOCK, 512)
        lam_rows = jnp.stack([lambda_q1[l], lambda_k1[l], lambda_q2[l], lambda_k2[l]])
        ya = _diff_attn(aq, ak, av, lam_rows, subln_g[l], lam_init, tq=min(512, S), tk=min(512, S))
        yb = _dsa_attn(bq, iq, iw, ik4, k2, v2, tq=128, tk=min(512, S))
        yc = _moba_attn(cq, ck, cv, kmean)
        x = _merge(ya, yb, yc, gates, x, w_branch[l].astype(BF16), w_out[l].astype(BF16),
                   norm_g[l, 1], tm)
        x = _ffn(x, norm_g[l, 2], w_ffn_in[l].astype(BF16), w_ffn_out[l].astype(BF16),
                 norm_g[l, 3], tm, tf=256)
    return x
```

```python
import functools
import math

import jax
import jax.numpy as jnp
from jax import lax
from jax.experimental import pallas as pl
from jax.experimental.pallas import tpu as pltpu

F32 = jnp.float32
BF16 = jnp.bfloat16

LANES = 128
HEAD_DIM = 64
ROPE_THETA = 500000.0
NORM_EPS = 1e-6
IDX_HEADS = 4
IDX_DIM = 32
IDX_TOPK_MAX = 256
MOBA_BLOCK = 256
MOBA_TOPK = 3
N_BRANCH = 3
BRANCH_WIDTH = 512
NEG_BIG = -1e30
LOG2E = math.log2(math.e)
ROW_BLOCK = 64
INT_MIN = -2 ** 31
VMEM_LIMIT = 56 * 1024 * 1024

OFF_AQ, OFF_AK, OFF_AV, OFF_BQ = 0, 512, 1024, 1536
OFF_K2, OFF_V2, OFF_IQ, OFF_IK4, OFF_IW = 2048, 2176, 2304, 2432, 2560
OFF_CQ, OFF_CK, OFF_CV, OFF_G = 2688, 3200, 3712, 4224
PACKED_COLS = OFF_G + 3 * 1024


def _nt_dot(a, b):
    return lax.dot_general(a, b, (((1,), (1,)), ((), ())), preferred_element_type=F32)


def _rms(x, g):
    return x * lax.rsqrt(jnp.mean(x * x, axis=-1, keepdims=True) + NORM_EPS) * g


def _cparams(sem):
    return pltpu.CompilerParams(dimension_semantics=sem, vmem_limit_bytes=VMEM_LIMIT)


def _resident(shape, index_map):
    return pl.BlockSpec(shape, index_map, pipeline_mode=pl.Buffered(1))


def _rope_table_kernel(pos_ref, tab_ref):
    pos = pos_ref[0].astype(F32)
    lane = lax.broadcasted_iota(jnp.int32, (1, LANES), 1)

    def tables(head_dim):
        rot = head_dim // 4
        half = rot // 2
        d = lane & (head_dim - 1)
        fi = d & (half - 1)
        inv = jnp.zeros((1, LANES), F32)
        for i in range(half):
            inv = jnp.where(fi == i, F32(ROPE_THETA ** (-(2.0 * i) / rot)), inv)
        ang = pos * inv
        cos = jnp.cos(ang)
        sin = jnp.sin(ang)
        c = jnp.where(d < rot, cos, 1.0)
        s = jnp.where(d < half, -sin, jnp.where(d < rot, sin, 0.0))
        return c, s

    c64, s64 = tables(HEAD_DIM)
    c32, s32 = tables(IDX_DIM)
    tab_ref[0, :, 0 * LANES:1 * LANES] = c64
    tab_ref[0, :, 1 * LANES:2 * LANES] = s64
    tab_ref[0, :, 2 * LANES:3 * LANES] = c32
    tab_ref[0, :, 3 * LANES:4 * LANES] = s32


def _rope_table(positions, tm):
    B, S = positions.shape
    return pl.pallas_call(
        _rope_table_kernel,
        grid=(B, S // tm),
        in_specs=[pl.BlockSpec((1, tm, 1), lambda b, m: (b, m, 0))],
        out_specs=pl.BlockSpec((1, tm, 4 * LANES), lambda b, m: (b, m, 0)),
        out_shape=jax.ShapeDtypeStruct((B, S, 4 * LANES), F32),
        compiler_params=_cparams(("parallel", "parallel")),
        name="rope_table",
    )(positions.reshape(B, S, 1))


def _inproj_kernel(x_ref, g_ref, tab_ref, w_ref,
                   aq_ref, ak_ref, av_ref, bq_ref, k2_ref, v2_ref, iq_ref, ik4_ref, iw_ref,
                   cq_ref, ck_ref, cv_ref, gates_ref, kmean_ref):
    h = _rms(x_ref[0], g_ref[...]).astype(BF16)
    lane = lax.broadcasted_iota(jnp.int32, (1, LANES), 1)

    def proj(off, width):
        return jnp.dot(h, w_ref[:, off:off + width], preferred_element_type=F32)

    def rope_block(y, head_dim, c, s):
        half = head_dim // 8
        up = pltpu.roll(y, LANES - half, 1)
        dn = pltpu.roll(y, half, 1)
        return y * c + jnp.where((lane & (head_dim - 1)) < half, up, dn) * s

    def emit(out_ref, off, width, head_dim=None, dtype=BF16):
        y = proj(off, width)
        if head_dim is None:
            out_ref[0] = y.astype(dtype)
            return None
        t0 = 0 if head_dim == HEAD_DIM else 2 * LANES
        c = tab_ref[0, :, t0:t0 + LANES]
        s = tab_ref[0, :, t0 + LANES:t0 + 2 * LANES]
        blocks = []
        for j in range(width // LANES):
            r = rope_block(y[:, j * LANES:(j + 1) * LANES], head_dim, c, s)
            out_ref[0, :, j * LANES:(j + 1) * LANES] = r.astype(dtype)
            blocks.append(r)
        return blocks

    emit(aq_ref, OFF_AQ, 512, HEAD_DIM)
    emit(ak_ref, OFF_AK, 512, HEAD_DIM)
    emit(av_ref, OFF_AV, 512)
    emit(bq_ref, OFF_BQ, 512, HEAD_DIM)
    emit(k2_ref, OFF_K2, LANES, HEAD_DIM)
    emit(v2_ref, OFF_V2, LANES)
    emit(iq_ref, OFF_IQ, LANES, IDX_DIM)
    emit(ik4_ref, OFF_IK4, LANES, IDX_DIM)
    emit(iw_ref, OFF_IW, LANES, dtype=F32)
    emit(cq_ref, OFF_CQ, 512, HEAD_DIM)
    ck_blocks = emit(ck_ref, OFF_CK, 512, HEAD_DIM)
    emit(cv_ref, OFF_CV, 512)
    emit(gates_ref, OFF_G, 3 * 1024, dtype=F32)

    tm = x_ref.shape[1]
    for blk in range(tm // MOBA_BLOCK):
        for j, r in enumerate(ck_blocks):
            part = r[blk * MOBA_BLOCK:(blk + 1) * MOBA_BLOCK]
            kmean_ref[0, 0, blk:blk + 1, j * LANES:(j + 1) * LANES] = (
                jnp.mean(part, axis=0, keepdims=True))


def _inproj(x, g, tab, w_packed, tm):
    B, S, D = x.shape
    nm = S // tm
    row = lambda width: pl.BlockSpec((1, tm, width), lambda b, m: (b, m, 0))
    shp = lambda width, dt: jax.ShapeDtypeStruct((B, S, width), dt)
    widths = [(512, BF16)] * 4 + [(LANES, BF16)] * 4 + [(LANES, F32)] + [(512, BF16)] * 3 + [(3072, F32)]
    out_specs = [row(w) for w, _ in widths]
    out_shape = [shp(w, dt) for w, dt in widths]
    out_specs.append(pl.BlockSpec((1, 1, tm // MOBA_BLOCK, 512), lambda b, m: (b, m, 0, 0)))
    out_shape.append(jax.ShapeDtypeStruct((B, nm, tm // MOBA_BLOCK, 512), F32))
    return pl.pallas_call(
        _inproj_kernel,
        grid=(B, nm),
        in_specs=[row(D),
                  _resident((1, D), lambda b, m: (0, 0)),
                  row(4 * LANES),
                  _resident((D, PACKED_COLS), lambda b, m: (0, 0))],
        out_specs=out_specs,
        out_shape=out_shape,
        compiler_params=_cparams(("parallel", "parallel")),
        name="inproj",
    )(x, g.reshape(1, D), tab, w_packed)


def _split_heads_rows(q):
    lo = lax.broadcasted_iota(jnp.int32, (1, LANES), 1) < HEAD_DIM
    zero = jnp.zeros_like(q)
    return jnp.concatenate([jnp.where(lo, q, zero), jnp.where(lo, zero, q)], axis=0)


def _lane_blocks(s):
    return [s[:, j * LANES:(j + 1) * LANES] for j in range(s.shape[1] // LANES)]


def _online_update(s, v, m_ref, l_ref, acc_ref, bias=None):
    rows_total = s.shape[0]
    nblk = s.shape[1] // LANES
    p_rows, alphas = [], []
    for r0 in range(0, rows_total, ROW_BLOCK):
        rows = slice(r0, r0 + ROW_BLOCK)
        cols = [s[rows, j * LANES:(j + 1) * LANES] for j in range(nblk)]
        if bias is not None:
            b0 = r0 % bias[0].shape[0]
            cols = [c + bj[b0:b0 + ROW_BLOCK] for c, bj in zip(cols, bias)]
        m_prev = m_ref[rows]
        m_new = jnp.maximum(
            m_prev, jnp.max(functools.reduce(jnp.maximum, cols), axis=1, keepdims=True))
        alpha = jnp.exp2(m_prev - m_new)
        ps = [jnp.exp2(c - m_new) for c in cols]
        l_ref[rows] = alpha * l_ref[rows] + functools.reduce(jnp.add, ps)
        m_ref[rows] = m_new
        p_rows.append(jnp.concatenate([pj.astype(BF16) for pj in ps], axis=1))
        alphas.append(alpha)
    p = jnp.concatenate(p_rows, axis=0)
    alpha = jnp.concatenate(alphas, axis=0)
    acc_ref[...] = alpha * acc_ref[...] + jnp.dot(p, v, preferred_element_type=F32)


def _normalised(l_ref, acc_ref):
    return acc_ref[...] / jnp.sum(l_ref[...], axis=1, keepdims=True)


def _pipelined_chunks(n, produce, consume, sa_ref, sb_ref):
    sa_ref[...] = produce(0)

    def pair(i, carry):
        c = 2 * i
        sb_ref[...] = produce(c + 1)
        consume(sa_ref[...], c)
        sa_ref[...] = produce(jnp.minimum(c + 2, n - 1))
        consume(sb_ref[...], c + 1)
        return carry

    lax.fori_loop(0, n // 2, pair, 0)

    @pl.when(n % 2 == 1)
    def _():
        consume(sa_ref[...], n - 1)


def _diff_attn_kernel(q_ref, k_ref, v_ref, lam_ref, g_ref, o_ref, m_ref, l_ref, acc_ref,
                      *, tq, tk, lam_init):
    qi = pl.program_id(2)
    q2 = _split_heads_rows(q_ref[0])
    m_ref[...] = jnp.full(m_ref.shape, NEG_BIG, F32)
    l_ref[...] = jnp.zeros(l_ref.shape, F32)
    acc_ref[...] = jnp.zeros(acc_ref.shape, F32)

    def rows_of(ref, c):
        return ref[0, pl.ds(pl.multiple_of(c * tk, tk), tk), :]

    def logits(c):
        return _nt_dot(q2, rows_of(k_ref, c))

    def step(s, c):
        _online_update(s, rows_of(v_ref, c), m_ref, l_ref, acc_ref)

    def causal_step(s, c):
        r = lax.broadcasted_iota(jnp.int32, (2 * tq, 1), 0)
        qpos = qi * tq + jnp.where(r >= tq, r - tq, r)
        kpos = c * tk + lax.broadcasted_iota(jnp.int32, (1, tk), 1)
        step(jnp.where(kpos <= qpos, s, NEG_BIG), c)

    diag = (qi * tq) // tk

    def body(c, carry):
        step(logits(c), c)
        return carry

    lax.fori_loop(0, diag, body, 0)
    causal_step(logits(diag), diag)

    lam_rows = lam_ref[...]
    e1 = jnp.exp(jnp.sum(lam_rows[0:1] * lam_rows[1:2], axis=1, keepdims=True))
    e2 = jnp.exp(jnp.sum(lam_rows[2:3] * lam_rows[3:4], axis=1, keepdims=True))
    lam = e1 - e2 + lam_init
    o = _normalised(l_ref, acc_ref)
    out = o[:tq] - lam * o[tq:]
    o_ref[0] = (_rms(out, g_ref[...]) * (1.0 - lam_init)).astype(o_ref.dtype)


def _diff_attn(aq, ak, av, lam_rows, subln_g, lam_init, tq, tk):
    B, S, _ = aq.shape
    nh = aq.shape[2] // LANES
    kern = functools.partial(_diff_attn_kernel, tq=tq, tk=tk, lam_init=lam_init)
    return pl.pallas_call(
        kern,
        grid=(B, nh, S // tq),
        in_specs=[pl.BlockSpec((1, tq, LANES), lambda b, h, i: (b, i, h)),
                  pl.BlockSpec((1, S, LANES), lambda b, h, i: (b, 0, h)),
                  pl.BlockSpec((1, S, LANES), lambda b, h, i: (b, 0, h)),
                  pl.BlockSpec((4, HEAD_DIM), lambda b, h, i: (0, 0)),
                  pl.BlockSpec((1, LANES), lambda b, h, i: (0, 0))],
        out_specs=pl.BlockSpec((1, tq, LANES), lambda b, h, i: (b, i, h)),
        out_shape=jax.ShapeDtypeStruct(aq.shape, BF16),
        scratch_shapes=[pltpu.VMEM((2 * tq, LANES), F32), pltpu.VMEM((2 * tq, LANES), F32),
                        pltpu.VMEM((2 * tq, LANES), F32)],
        compiler_params=_cparams(("parallel", "parallel", "arbitrary")),
        name="diff_attn",
    )(aq, ak, av, lam_rows, subln_g.reshape(1, LANES))


BIT_GROUP = 256


def _bit_transpose32(words):
    w = list(words)
    for j, m in ((16, 0x0000FFFF), (8, 0x00FF00FF), (4, 0x0F0F0F0F), (2, 0x33333333), (1, 0x55555555)):
        for k in range(32):
            if k & j == 0:
                t = (w[k] ^ (w[k + j] >> j)) & m
                w[k] = w[k] ^ t
                w[k + j] = w[k + j] ^ (t << j)
    return w

def _dsa_kernel(q_ref, iq_ref, iw_ref, ik4_ref, k2_ref, v2_ref, o_ref,
                keys_ref, keys_t_ref, planes_ref, act_ref, jcut_ref, m_ref, l_ref, acc_ref,
                *, tq, tk, topk, idx_bits):
    qi = pl.program_id(1)
    nkv = (qi * tq + tq - 1) // tk + 1
    lane = lax.broadcasted_iota(jnp.int32, (1, LANES), 1)
    qpos = qi * tq + lax.broadcasted_iota(jnp.int32, (tq, 1), 0)
    lane_tk = lax.broadcasted_iota(jnp.int32, (1, tk), 1)
    nsub = tk // LANES
    seq = keys_ref.shape[1]

    @pl.when(qi == 0)
    def _():
        planes_ref[...] = jnp.zeros(planes_ref.shape, jnp.int32)

    iq = iq_ref[0]
    zero = jnp.zeros_like(iq)
    iq4 = jnp.concatenate(
        [jnp.where((lane >> (IDX_DIM.bit_length() - 1)) == hh, iq, zero) for hh in range(IDX_HEADS)],
        axis=0)
    iw = iw_ref[0]
    iw_cols = [iw[:, hh:hh + 1] for hh in range(IDX_HEADS)]

    def score_body(c, carry):
        off = pl.multiple_of(c * tk, tk)
        rel = jnp.maximum(_nt_dot(iq4, ik4_ref[0, pl.ds(off, tk), :]), 0.0)
        score = iw_cols[0] * rel[0:tq]
        for hh in range(1, IDX_HEADS):
            score = score + iw_cols[hh] * rel[hh * tq:(hh + 1) * tq]
        score = jnp.where(score == 0.0, 0.0, score)
        bits = lax.bitcast_convert_type(score, jnp.int32)
        key = bits ^ ((bits >> 31) & jnp.int32(0x7FFFFFFF))
        kpos = off + lane_tk
        key = jnp.where(key > 0, key + seq, jnp.where(key == 0, (seq - 1) - kpos, key))
        key = jnp.where(kpos <= qpos, key, jnp.int32(INT_MIN))
        keys_ref[:, pl.ds(off, tk)] = key
        kts = []
        for j in range(nsub):
            kt = key[:, j * LANES:(j + 1) * LANES].T
            keys_t_ref[pl.ds(pl.multiple_of(off + j * LANES, LANES), LANES), :] = kt
            kts.append(kt ^ jnp.int32(INT_MIN))
        for gl in range(tk // BIT_GROUP):
            words = [kts[(gl * BIT_GROUP + 8 * g) // LANES][(8 * g) % LANES:(8 * g) % LANES + 8]
                     for g in range(32)]
            words = _bit_transpose32(words)
            for b in range(32):
                planes_ref[c * (tk // BIT_GROUP) + gl, b] = words[b]
        return carry

    lax.fori_loop(0, nkv, score_body, 0)

    rows_tk = lax.broadcasted_iota(jnp.int32, (tk, 1), 0)
    cnt_rows = 64

    def count(pred):
        def body(c, cnt):
            off = pl.multiple_of(c * tk, tk)
            hit = pred(keys_t_ref[pl.ds(off, tk), :], off + rows_tk)
            for g in range(tk // cnt_rows):
                cnt = jnp.where(hit[g * cnt_rows:(g + 1) * cnt_rows], cnt + 1, cnt)
            return cnt
        cnt = lax.fori_loop(0, nkv, body, jnp.zeros((cnt_rows, tq), jnp.int32))
        return jnp.sum(cnt.astype(F32), axis=0, keepdims=True)

    ngroups = planes_ref.shape[0]
    live = nkv * (tk // BIT_GROUP)
    for g in range(ngroups):
        act_ref[g] = jnp.broadcast_to(jnp.where(g < live, jnp.int32(-1), jnp.int32(0)), (8, tq))

    def sweep(i, keep, first=False):
        parts = [jnp.zeros((8, tq), jnp.int32) for _ in range(4)]
        for g in range(ngroups):
            act = act_ref[g]
            if not first:
                act = act & (planes_ref[g, i - 1] ^ keep)
                act_ref[g] = act
            parts[g % 4] = parts[g % 4] + lax.population_count(act & planes_ref[g, i])
        cnt = (parts[0] + parts[1]) + (parts[2] + parts[3])
        return jnp.sum(cnt.astype(F32), axis=0, keepdims=True)

    def decide(i, ones, thr_u, n_gt):
        take = n_gt + ones >= topk
        thr_u = thr_u | jnp.where(take, jnp.left_shift(jnp.int32(1), 31 - i), 0)
        return thr_u, jnp.where(take, n_gt, n_gt + ones), jnp.where(take, 0, -1).astype(jnp.int32)

    def pass_body(i, carry):
        thr_u, n_gt, keep = carry
        return decide(i, sweep(i, keep), thr_u, n_gt)

    start = decide(0, sweep(0, None, first=True),
                   jnp.zeros((1, tq), jnp.int32), jnp.zeros((1, tq), F32))
    thr_u, n_gt, keep = lax.fori_loop(1, 32, pass_body, start)
    n_eq = jnp.zeros((8, tq), jnp.int32)
    for g in range(ngroups):
        n_eq = n_eq + lax.population_count(act_ref[g] & (planes_ref[g, 31] ^ keep))
    thr = thr_u ^ jnp.int32(INT_MIN)
    cnt_ge = n_gt + jnp.sum(n_eq.astype(F32), axis=0, keepdims=True)
    need = topk - n_gt

    jcut_ref[...] = jnp.full((1, tq), 2 ** 30, jnp.int32)

    @pl.when(jnp.max(cnt_ge) > topk)
    def _():
        def idx_body(i, j):
            cand = j + jnp.left_shift(jnp.int32(1), idx_bits - 1 - i)
            cnt = count(lambda kc, kpos: (kc == thr) & (kpos < cand))
            return jnp.where(cnt < need, cand, j)
        jcut_ref[...] = lax.fori_loop(0, idx_bits, idx_body, jnp.zeros((1, tq), jnp.int32))

    def as_rows(x):
        return jnp.broadcast_to(x, (LANES, tq)).T

    thr_r = as_rows(thr)
    jcut_r = as_rows(jcut_ref[...])

    q = q_ref[0]
    q8 = jnp.concatenate(
        [_split_heads_rows(q[:, p * LANES:(p + 1) * LANES]) for p in range(4)], axis=0)
    nh = 8
    m_ref[...] = jnp.full(m_ref.shape, NEG_BIG, F32)
    l_ref[...] = jnp.zeros(l_ref.shape, F32)
    acc_ref[...] = jnp.zeros(acc_ref.shape, F32)

    def attn_body(c, carry):
        off = pl.multiple_of(c * tk, tk)
        s = _nt_dot(q8, k2_ref[0, pl.ds(off, tk), :])
        bias = []
        for j in range(nsub):
            keyj = keys_ref[:, pl.ds(pl.multiple_of(off + j * LANES, LANES), LANES)]
            kpos = off + j * LANES + lane
            sel = ((keyj > thr_r) | ((keyj == thr_r) & (kpos <= jcut_r))) & (kpos <= qpos)
            bias.append(jnp.where(sel, 0.0, NEG_BIG))
        _online_update(s, v2_ref[0, pl.ds(off, tk), :], m_ref, l_ref, acc_ref, bias=bias)
        return carry

    lax.fori_loop(0, nkv, attn_body, 0)

    o = _normalised(l_ref, acc_ref)
    lo = lane < HEAD_DIM
    for p in range(4):
        o_ref[0, :, p * LANES:(p + 1) * LANES] = jnp.where(
            lo, o[(2 * p) * tq:(2 * p + 1) * tq], o[(2 * p + 1) * tq:(2 * p + 2) * tq]
        ).astype(o_ref.dtype)


def _dsa_attn(bq, iq, iw, ik4, k2, v2, tq, tk):
    B, S, C = bq.shape
    topk = min(IDX_TOPK_MAX, S // 4)
    idx_bits = max(1, (S - 1).bit_length())
    kern = functools.partial(_dsa_kernel, tq=tq, tk=tk, topk=topk, idx_bits=idx_bits)
    row = lambda width: pl.BlockSpec((1, tq, width), lambda b, i: (b, i, 0))
    full = pl.BlockSpec((1, S, LANES), lambda b, i: (b, 0, 0))
    return pl.pallas_call(
        kern,
        grid=(B, S // tq),
        in_specs=[row(C), row(LANES), row(LANES), full, full, full],
        out_specs=row(C),
        out_shape=jax.ShapeDtypeStruct(bq.shape, BF16),
        scratch_shapes=[pltpu.VMEM((tq, S), jnp.int32), pltpu.VMEM((S, tq), jnp.int32),
                        pltpu.VMEM((S // BIT_GROUP, 32, 8, tq), jnp.int32),
                        pltpu.VMEM((S // BIT_GROUP, 8, tq), jnp.int32),
                        pltpu.VMEM((1, tq), jnp.int32),
                        pltpu.VMEM((8 * tq, LANES), F32), pltpu.VMEM((8 * tq, LANES), F32),
                        pltpu.VMEM((8 * tq, LANES), F32)],
        compiler_params=_cparams(("parallel", "arbitrary")),
        name="dsa_attn",
    )(bq, iq, iw, ik4, k2, v2)


def _moba_kernel(q_ref, k_ref, v_ref, km_ref, o_ref, m_ref, l_ref, acc_ref, sa_ref, sb_ref,
                 *, nb, kb):
    tq = MOBA_BLOCK
    own = pl.program_id(2)
    lane = lax.broadcasted_iota(jnp.int32, (1, LANES), 1)
    q2 = _split_heads_rows(q_ref[0])

    km = jnp.concatenate([km_ref[0], jnp.zeros((LANES - nb, LANES), F32)], axis=0)
    km_hi = km.astype(BF16)
    r1 = km - km_hi.astype(F32)
    km_mid = r1.astype(BF16)
    km_lo = (r1 - km_mid.astype(F32)).astype(BF16)
    nb8 = -(-nb // 8) * 8
    gate_t = (_nt_dot(km_hi, q2) + _nt_dot(km_mid, q2) + _nt_dot(km_lo, q2))[:nb8]
    blk = lax.broadcasted_iota(jnp.int32, (nb8, 1), 0)
    blk_f = blk.astype(F32)
    past = blk < own
    g = jnp.where(past, gate_t, -jnp.inf)
    sel = jnp.zeros(g.shape, jnp.bool_)
    for _ in range(min(MOBA_TOPK, nb - 1)):
        mx = jnp.max(g, axis=0, keepdims=True)
        first = jnp.min(jnp.where(g == mx, blk_f, float(LANES)), axis=0, keepdims=True)
        pick = blk_f == first
        sel = sel | pick
        g = jnp.where(pick, -jnp.inf, g)
    bias_t = jnp.concatenate([jnp.where(sel & past, 0.0, NEG_BIG),
                              jnp.full((LANES - nb8, 2 * tq), NEG_BIG, F32)], axis=0)
    qa = jnp.concatenate([q2, bias_t.T.astype(BF16)], axis=1)

    off = pl.multiple_of(own * tq, tq)
    s = _nt_dot(q2, k_ref[0, pl.ds(off, tq), :])
    r = lax.broadcasted_iota(jnp.int32, (2 * tq, 1), 0)
    qloc = jnp.where(r >= tq, r - tq, r)
    kloc = lax.broadcasted_iota(jnp.int32, (1, tq), 1)
    s = jnp.where(kloc <= qloc, s, NEG_BIG)
    m_ref[...] = jnp.full(m_ref.shape, NEG_BIG, F32)
    l_ref[...] = jnp.zeros(l_ref.shape, F32)
    acc_ref[...] = jnp.zeros(acc_ref.shape, F32)
    _online_update(s, v_ref[0, pl.ds(off, tq), :], m_ref, l_ref, acc_ref)

    tkc = kb * tq
    row_blk = lax.broadcasted_iota(jnp.int32, (tkc, 1), 0) >> (MOBA_BLOCK.bit_length() - 1)

    def rows_of(ref, i):
        return ref[0, pl.ds(pl.multiple_of(i * tkc, tkc), tkc), :]

    def logits(i):
        onehot = jnp.where(lane == i * kb + row_blk, 1.0, 0.0).astype(BF16)
        return _nt_dot(qa, jnp.concatenate([rows_of(k_ref, i), onehot], axis=1))

    def step(s, i):
        _online_update(s, rows_of(v_ref, i), m_ref, l_ref, acc_ref)

    _pipelined_chunks((own + kb - 1) // kb, logits, step, sa_ref, sb_ref)

    o = _normalised(l_ref, acc_ref)
    o_ref[0] = jnp.where(lane < HEAD_DIM, o[:tq], o[tq:]).astype(o_ref.dtype)


def _moba_attn(cq, ck, cv, kmean):
    B, S, C = cq.shape
    nb = S // MOBA_BLOCK
    tq = MOBA_BLOCK
    kb = 2 if nb % 2 == 0 else 1
    return pl.pallas_call(
        functools.partial(_moba_kernel, nb=nb, kb=kb),
        grid=(B, C // LANES, nb),
        in_specs=[pl.BlockSpec((1, tq, LANES), lambda b, p, i: (b, i, p)),
                  pl.BlockSpec((1, S, LANES), lambda b, p, i: (b, 0, p)),
                  pl.BlockSpec((1, S, LANES), lambda b, p, i: (b, 0, p)),
                  pl.BlockSpec((1, nb, LANES), lambda b, p, i: (b, 0, p))],
        out_specs=pl.BlockSpec((1, tq, LANES), lambda b, p, i: (b, i, p)),
        out_shape=jax.ShapeDtypeStruct(cq.shape, BF16),
        scratch_shapes=[pltpu.VMEM((2 * tq, LANES), F32), pltpu.VMEM((2 * tq, LANES), F32),
                        pltpu.VMEM((2 * tq, LANES), F32),
                        pltpu.VMEM((2 * tq, kb * tq), F32), pltpu.VMEM((2 * tq, kb * tq), F32)],
        compiler_params=_cparams(("parallel", "parallel", "arbitrary")),
        name="moba_attn",
    )(cq, ck, cv, kmean)


def _sigmoid(x):
    return 1.0 / (1.0 + jnp.exp(-x))


def _merge_kernel(ya_ref, yb_ref, yc_ref, gates_ref, x_ref, wb_ref, wo_ref, g_ref, o_ref):
    d = x_ref.shape[2]
    merged = None
    for n, y_ref in enumerate((ya_ref, yb_ref, yc_ref)):
        br = jnp.dot(y_ref[0], wb_ref[n], preferred_element_type=F32)
        term = _sigmoid(gates_ref[0, :, n * d:(n + 1) * d]) * br
        merged = term if merged is None else merged + term
    m = jnp.dot(merged.astype(BF16), wo_ref[...], preferred_element_type=F32)
    o_ref[0] = x_ref[0] + _rms(m, g_ref[...])


def _merge(ya, yb, yc, gates, x, wb, wo, g, tm):
    B, S, D = x.shape
    row = lambda width: pl.BlockSpec((1, tm, width), lambda b, m: (b, m, 0))
    return pl.pallas_call(
        _merge_kernel,
        grid=(B, S // tm),
        in_specs=[row(BRANCH_WIDTH), row(BRANCH_WIDTH), row(BRANCH_WIDTH), row(N_BRANCH * D), row(D),
                  _resident(wb.shape, lambda b, m: (0, 0, 0)),
                  _resident(wo.shape, lambda b, m: (0, 0)),
                  _resident((1, D), lambda b, m: (0, 0))],
        out_specs=row(D),
        out_shape=jax.ShapeDtypeStruct(x.shape, F32),
        compiler_params=_cparams(("parallel", "parallel")),
        name="merge_out",
    )(ya, yb, yc, gates, x, wb, wo, g.reshape(1, D))


def _ffn_kernel(x_ref, g_in_ref, wi_ref, wo_ref, g_out_ref, o_ref, acc_ref, *, d_ff, tf):
    x = x_ref[0]
    h = _rms(x, g_in_ref[...]).astype(BF16)
    acc_ref[...] = jnp.zeros(acc_ref.shape, F32)

    def body(c, carry):
        off = pl.multiple_of(c * tf, tf)
        gt = jnp.dot(h, wi_ref[:, pl.ds(off, tf)], preferred_element_type=F32)
        up_off = pl.multiple_of(d_ff + off, LANES)
        up = jnp.dot(h, wi_ref[:, pl.ds(up_off, tf)], preferred_element_type=F32)
        act = (gt * _sigmoid(gt) * up).astype(BF16)
        acc_ref[...] += jnp.dot(act, wo_ref[pl.ds(off, tf), :], preferred_element_type=F32)
        return carry

    lax.fori_loop(0, d_ff // tf, body, 0)
    o_ref[0] = x + _rms(acc_ref[...], g_out_ref[...])


def _ffn(x, g_in, wi, wo, g_out, tm, tf):
    B, S, D = x.shape
    d_ff = wo.shape[0]
    row = pl.BlockSpec((1, tm, D), lambda b, m: (b, m, 0))
    vec = _resident((1, D), lambda b, m: (0, 0))
    return pl.pallas_call(
        functools.partial(_ffn_kernel, d_ff=d_ff, tf=tf),
        grid=(B, S // tm),
        in_specs=[row, vec, _resident(wi.shape, lambda b, m: (0, 0)),
                  _resident(wo.shape, lambda b, m: (0, 0)), vec],
        out_specs=row,
        out_shape=jax.ShapeDtypeStruct(x.shape, F32),
        scratch_shapes=[pltpu.VMEM((tm, D), F32)],
        compiler_params=_cparams(("parallel", "parallel")),
        name="swiglu_ffn",
    )(x, g_in.reshape(1, D), wi, wo, g_out.reshape(1, D))


def _pack_w_in(w):
    d = w.shape[0]
    offs = [0]
    for width in (512, 512, 512, 512, 64, 64, 128, 32, 4, 512, 512, 512, 3072):
        offs.append(offs[-1] + width)
    seg = lambda i: w[:, offs[i]:offs[i + 1]]
    scale = HEAD_DIM ** -0.5 * LOG2E
    cols = [seg(0) * scale, seg(1), seg(2), seg(3) * scale,
            seg(4), seg(4), seg(5), seg(5), seg(6),
            seg(7), seg(7), seg(7), seg(7),
            seg(8), jnp.zeros((d, LANES - IDX_HEADS), w.dtype),
            seg(9) * scale, seg(10), seg(11), seg(12)]
    packed = jnp.concatenate(cols, axis=1).astype(BF16)
    assert packed.shape[1] == PACKED_COLS
    return packed


def kernel(x, positions, w_in, w_branch, w_out, lambda_q1, lambda_k1, lambda_q2, lambda_k2,
           subln_g, norm_g, w_ffn_in, w_ffn_out):
    B, S, D = x.shape
    depth = w_in.shape[0]
    tm = min(512, S)
    tab = _rope_table(positions, tm)
    for l in range(depth):
        lam_init = 0.8 - 0.6 * math.exp(-0.3 * l)
        (aq, ak, av, bq, k2, v2, iq, ik4, iw, cq, ck, cv, gates, kmean) = _inproj(
            x, norm_g[l, 0], tab, _pack_w_in(w_in[l]), tm)
        kmean = kmean.reshape(B, S // MOBA_BLOCK, 512)
        lam_rows = jnp.stack([lambda_q1[l], lambda_k1[l], lambda_q2[l], lambda_k2[l]])
        ya = _diff_attn(aq, ak, av, lam_rows, subln_g[l], lam_init, tq=min(512, S), tk=min(512, S))
        yb = _dsa_attn(bq, iq, iw, ik4, k2, v2, tq=128, tk=min(512, S))
        yc = _moba_attn(cq, ck, cv, kmean)
        x = _merge(ya, yb, yc, gates, x, w_branch[l].astype(BF16), w_out[l].astype(BF16),
                   norm_g[l, 1], tm)
        x = _ffn(x, norm_g[l, 2], w_ffn_in[l].astype(BF16), w_ffn_out[l].astype(BF16),
                 norm_g[l, 3], tm, tf=256)
    return x
```

```python
import functools
import math

import jax
import jax.numpy as jnp
from jax import lax
from jax.experimental import pallas as pl
from jax.experimental.pallas import tpu as pltpu

F32 = jnp.float32
BF16 = jnp.bfloat16

LANES = 128
HEAD_DIM = 64
ROPE_THETA = 500000.0
NORM_EPS = 1e-6
IDX_HEADS = 4
IDX_DIM = 32
IDX_TOPK_MAX = 256
MOBA_BLOCK = 256
MOBA_TOPK = 3
N_BRANCH = 3
BRANCH_WIDTH = 512
NEG_BIG = -1e30
LOG2E = math.log2(math.e)
ROW_BLOCK = 64
INT_MIN = -2 ** 31
VMEM_LIMIT = 56 * 1024 * 1024

OFF_AQ, OFF_AK, OFF_AV, OFF_BQ = 0, 512, 1024, 1536
OFF_K2, OFF_V2, OFF_IQ, OFF_IK4, OFF_IW = 2048, 2176, 2304, 2432, 2560
OFF_CQ, OFF_CK, OFF_CV, OFF_G = 2688, 3200, 3712, 4224
PACKED_COLS = OFF_G + 3 * 1024


def _nt_dot(a, b):
    return lax.dot_general(a, b, (((1,), (1,)), ((), ())), preferred_element_type=F32)


def _rms(x, g):
    return x * lax.rsqrt(jnp.mean(x * x, axis=-1, keepdims=True) + NORM_EPS) * g


def _cparams(sem):
    return pltpu.CompilerParams(dimension_semantics=sem, vmem_limit_bytes=VMEM_LIMIT)


def _resident(shape, index_map):
    return pl.BlockSpec(shape, index_map, pipeline_mode=pl.Buffered(1))


def _rope_table_kernel(pos_ref, tab_ref):
    pos = pos_ref[0].astype(F32)
    lane = lax.broadcasted_iota(jnp.int32, (1, LANES), 1)

    def tables(head_dim):
        rot = head_dim // 4
        half = rot // 2
        d = lane & (head_dim - 1)
        fi = d & (half - 1)
        inv = jnp.zeros((1, LANES), F32)
        for i in range(half):
            inv = jnp.where(fi == i, F32(ROPE_THETA ** (-(2.0 * i) / rot)), inv)
        ang = pos * inv
        cos = jnp.cos(ang)
        sin = jnp.sin(ang)
        c = jnp.where(d < rot, cos, 1.0)
        s = jnp.where(d < half, -sin, jnp.where(d < rot, sin, 0.0))
        return c, s

    c64, s64 = tables(HEAD_DIM)
    c32, s32 = tables(IDX_DIM)
    tab_ref[0, :, 0 * LANES:1 * LANES] = c64
    tab_ref[0, :, 1 * LANES:2 * LANES] = s64
    tab_ref[0, :, 2 * LANES:3 * LANES] = c32
    tab_ref[0, :, 3 * LANES:4 * LANES] = s32


def _rope_table(positions, tm):
    B, S = positions.shape
    return pl.pallas_call(
        _rope_table_kernel,
        grid=(B, S // tm),
        in_specs=[pl.BlockSpec((1, tm, 1), lambda b, m: (b, m, 0))],
        out_specs=pl.BlockSpec((1, tm, 4 * LANES), lambda b, m: (b, m, 0)),
        out_shape=jax.ShapeDtypeStruct((B, S, 4 * LANES), F32),
        compiler_params=_cparams(("parallel", "parallel")),
        name="rope_table",
    )(positions.reshape(B, S, 1))


def _inproj_kernel(x_ref, g_ref, tab_ref, w_ref,
                   aq_ref, ak_ref, av_ref, bq_ref, k2_ref, v2_ref, iq_ref, ik4_ref, iw_ref,
                   cq_ref, ck_ref, cv_ref, gates_ref, kmean_ref):
    h = _rms(x_ref[0], g_ref[...]).astype(BF16)
    lane = lax.broadcasted_iota(jnp.int32, (1, LANES), 1)

    def proj(off, width):
        return jnp.dot(h, w_ref[:, off:off + width], preferred_element_type=F32)

    def rope_block(y, head_dim, c, s):
        half = head_dim // 8
        up = pltpu.roll(y, LANES - half, 1)
        dn = pltpu.roll(y, half, 1)
        return y * c + jnp.where((lane & (head_dim - 1)) < half, up, dn) * s

    def emit(out_ref, off, width, head_dim=None, dtype=BF16):
        y = proj(off, width)
        if head_dim is None:
            out_ref[0] = y.astype(dtype)
            return None
        t0 = 0 if head_dim == HEAD_DIM else 2 * LANES
        c = tab_ref[0, :, t0:t0 + LANES]
        s = tab_ref[0, :, t0 + LANES:t0 + 2 * LANES]
        blocks = []
        for j in range(width // LANES):
            r = rope_block(y[:, j * LANES:(j + 1) * LANES], head_dim, c, s)
            out_ref[0, :, j * LANES:(j + 1) * LANES] = r.astype(dtype)
            blocks.append(r)
        return blocks

    emit(aq_ref, OFF_AQ, 512, HEAD_DIM)
    emit(ak_ref, OFF_AK, 512, HEAD_DIM)
    emit(av_ref, OFF_AV, 512)
    emit(bq_ref, OFF_BQ, 512, HEAD_DIM)
    emit(k2_ref, OFF_K2, LANES, HEAD_DIM)
    emit(v2_ref, OFF_V2, LANES)
    emit(iq_ref, OFF_IQ, LANES, IDX_DIM)
    emit(ik4_ref, OFF_IK4, LANES, IDX_DIM)
    emit(iw_ref, OFF_IW, LANES, dtype=F32)
    emit(cq_ref, OFF_CQ, 512, HEAD_DIM)
    ck_blocks = emit(ck_ref, OFF_CK, 512, HEAD_DIM)
    emit(cv_ref, OFF_CV, 512)
    emit(gates_ref, OFF_G, 3 * 1024, dtype=F32)

    tm = x_ref.shape[1]
    for blk in range(tm // MOBA_BLOCK):
        for j, r in enumerate(ck_blocks):
            part = r[blk * MOBA_BLOCK:(blk + 1) * MOBA_BLOCK]
            kmean_ref[0, 0, blk:blk + 1, j * LANES:(j + 1) * LANES] = (
                jnp.mean(part, axis=0, keepdims=True))


def _inproj(x, g, tab, w_packed, tm):
    B, S, D = x.shape
    nm = S // tm
    row = lambda width: pl.BlockSpec((1, tm, width), lambda b, m: (b, m, 0))
    shp = lambda width, dt: jax.ShapeDtypeStruct((B, S, width), dt)
    widths = [(512, BF16)] * 4 + [(LANES, BF16)] * 4 + [(LANES, F32)] + [(512, BF16)] * 3 + [(3072, F32)]
    out_specs = [row(w) for w, _ in widths]
    out_shape = [shp(w, dt) for w, dt in widths]
    out_specs.append(pl.BlockSpec((1, 1, tm // MOBA_BLOCK, 512), lambda b, m: (b, m, 0, 0)))
    out_shape.append(jax.ShapeDtypeStruct((B, nm, tm // MOBA_BLOCK, 512), F32))
    return pl.pallas_call(
        _inproj_kernel,
        grid=(B, nm),
        in_specs=[row(D),
                  _resident((1, D), lambda b, m: (0, 0)),
                  row(4 * LANES),
                  _resident((D, PACKED_COLS), lambda b, m: (0, 0))],
        out_specs=out_specs,
        out_shape=out_shape,
        compiler_params=_cparams(("parallel", "parallel")),
        name="inproj",
    )(x, g.reshape(1, D), tab, w_packed)


def _split_heads_rows(q):
    lo = lax.broadcasted_iota(jnp.int32, (1, LANES), 1) < HEAD_DIM
    zero = jnp.zeros_like(q)
    return jnp.concatenate([jnp.where(lo, q, zero), jnp.where(lo, zero, q)], axis=0)


def _lane_blocks(s):
    return [s[:, j * LANES:(j + 1) * LANES] for j in range(s.shape[1] // LANES)]


def _online_update(s, v, m_ref, l_ref, acc_ref, bias=None, row0=0):
    rows_total = s.shape[0]
    nblk = s.shape[1] // LANES
    p_rows, alphas = [], []
    for r0 in range(0, rows_total, ROW_BLOCK):
        rows = slice(row0 + r0, row0 + r0 + ROW_BLOCK)
        cols = [s[r0:r0 + ROW_BLOCK, j * LANES:(j + 1) * LANES] for j in range(nblk)]
        if bias is not None:
            b0 = r0 % bias[0].shape[0]
            cols = [c + bj[b0:b0 + ROW_BLOCK] for c, bj in zip(cols, bias)]
        m_prev = m_ref[rows]
        m_new = jnp.maximum(
            m_prev, jnp.max(functools.reduce(jnp.maximum, cols), axis=1, keepdims=True))
        alpha = jnp.exp2(m_prev - m_new)
        ps = [jnp.exp2(c - m_new) for c in cols]
        l_ref[rows] = alpha * l_ref[rows] + functools.reduce(jnp.add, ps)
        m_ref[rows] = m_new
        p_rows.append(jnp.concatenate([pj.astype(BF16) for pj in ps], axis=1))
        alphas.append(alpha)
    p = jnp.concatenate(p_rows, axis=0)
    alpha = jnp.concatenate(alphas, axis=0)
    span = slice(row0, row0 + rows_total)
    acc_ref[span] = alpha * acc_ref[span] + jnp.dot(p, v, preferred_element_type=F32)


def _normalised(l_ref, acc_ref):
    return acc_ref[...] / jnp.sum(l_ref[...], axis=1, keepdims=True)


def _pipelined_chunks(n, produce, consume, sa_ref, sb_ref):
    sa_ref[...] = produce(0)

    def pair(i, carry):
        c = 2 * i
        sb_ref[...] = produce(c + 1)
        consume(sa_ref[...], c)
        sa_ref[...] = produce(jnp.minimum(c + 2, n - 1))
        consume(sb_ref[...], c + 1)
        return carry

    lax.fori_loop(0, n // 2, pair, 0)

    @pl.when(n % 2 == 1)
    def _():
        consume(sa_ref[...], n - 1)


def _diff_attn_kernel(q_ref, k_ref, v_ref, lam_ref, g_ref, o_ref, m_ref, l_ref, acc_ref,
                      *, t, lam_init):
    qi = pl.program_id(2)
    half = t // 2
    q = q_ref[0]
    q2 = jnp.concatenate([_split_heads_rows(q[:half]), _split_heads_rows(q[half:])], axis=0)
    m_ref[...] = jnp.full(m_ref.shape, NEG_BIG, F32)
    l_ref[...] = jnp.zeros(l_ref.shape, F32)
    acc_ref[...] = jnp.zeros(acc_ref.shape, F32)

    def body(c, carry):
        off = pl.multiple_of(c * t, t)
        _online_update(_nt_dot(q2, k_ref[0, pl.ds(off, t), :]), v_ref[0, pl.ds(off, t), :],
                       m_ref, l_ref, acc_ref)
        return carry

    lax.fori_loop(0, qi, body, 0)

    r = lax.broadcasted_iota(jnp.int32, (t, 1), 0)
    causal = lax.broadcasted_iota(jnp.int32, (1, half), 1) <= jnp.where(r >= half, r - half, r)
    off_a = pl.multiple_of(qi * t, t)
    s = _nt_dot(q2, k_ref[0, pl.ds(off_a, half), :])
    s = jnp.concatenate([jnp.where(causal, s[:t], NEG_BIG), s[t:]], axis=0)
    _online_update(s, v_ref[0, pl.ds(off_a, half), :], m_ref, l_ref, acc_ref)
    off_b = pl.multiple_of(qi * t + half, half)
    s = jnp.where(causal, _nt_dot(q2[t:], k_ref[0, pl.ds(off_b, half), :]), NEG_BIG)
    _online_update(s, v_ref[0, pl.ds(off_b, half), :], m_ref, l_ref, acc_ref, row0=t)

    lam_rows = lam_ref[...]
    e1 = jnp.exp(jnp.sum(lam_rows[0:1] * lam_rows[1:2], axis=1, keepdims=True))
    e2 = jnp.exp(jnp.sum(lam_rows[2:3] * lam_rows[3:4], axis=1, keepdims=True))
    lam = e1 - e2 + lam_init
    o = _normalised(l_ref, acc_ref)
    o1 = jnp.concatenate([o[:half], o[t:t + half]], axis=0)
    o2 = jnp.concatenate([o[half:t], o[t + half:]], axis=0)
    o_ref[0] = (_rms(o1 - lam * o2, g_ref[...]) * (1.0 - lam_init)).astype(o_ref.dtype)


def _diff_attn(aq, ak, av, lam_rows, subln_g, lam_init, t):
    B, S, _ = aq.shape
    nh = aq.shape[2] // LANES
    kern = functools.partial(_diff_attn_kernel, t=t, lam_init=lam_init)
    return pl.pallas_call(
        kern,
        grid=(B, nh, S // t),
        in_specs=[pl.BlockSpec((1, t, LANES), lambda b, h, i: (b, i, h)),
                  pl.BlockSpec((1, S, LANES), lambda b, h, i: (b, 0, h)),
                  pl.BlockSpec((1, S, LANES), lambda b, h, i: (b, 0, h)),
                  pl.BlockSpec((4, HEAD_DIM), lambda b, h, i: (0, 0)),
                  pl.BlockSpec((1, LANES), lambda b, h, i: (0, 0))],
        out_specs=pl.BlockSpec((1, t, LANES), lambda b, h, i: (b, i, h)),
        out_shape=jax.ShapeDtypeStruct(aq.shape, BF16),
        scratch_shapes=[pltpu.VMEM((2 * t, LANES), F32), pltpu.VMEM((2 * t, LANES), F32),
                        pltpu.VMEM((2 * t, LANES), F32)],
        compiler_params=_cparams(("parallel", "parallel", "arbitrary")),
        name="diff_attn",
    )(aq, ak, av, lam_rows, subln_g.reshape(1, LANES))


BIT_GROUP = 256


def _bit_transpose32(words):
    w = list(words)
    for j, m in ((16, 0x0000FFFF), (8, 0x00FF00FF), (4, 0x0F0F0F0F), (2, 0x33333333), (1, 0x55555555)):
        for k in range(32):
            if k & j == 0:
                t = (w[k] ^ (w[k + j] >> j)) & m
                w[k] = w[k] ^ t
                w[k + j] = w[k + j] ^ (t << j)
    return w

def _dsa_kernel(q_ref, iq_ref, iw_ref, ik4_ref, k2_ref, v2_ref, o_ref,
                keys_ref, keys_t_ref, planes_ref, act_ref, jcut_ref, m_ref, l_ref, acc_ref,
                *, tq, tk, topk, idx_bits):
    qi = pl.program_id(1)
    nkv = (qi * tq + tq - 1) // tk + 1
    lane = lax.broadcasted_iota(jnp.int32, (1, LANES), 1)
    qpos = qi * tq + lax.broadcasted_iota(jnp.int32, (tq, 1), 0)
    lane_tk = lax.broadcasted_iota(jnp.int32, (1, tk), 1)
    nsub = tk // LANES
    seq = keys_ref.shape[1]

    @pl.when(qi == 0)
    def _():
        planes_ref[...] = jnp.zeros(planes_ref.shape, jnp.int32)

    iq = iq_ref[0]
    zero = jnp.zeros_like(iq)
    iq4 = jnp.concatenate(
        [jnp.where((lane >> (IDX_DIM.bit_length() - 1)) == hh, iq, zero) for hh in range(IDX_HEADS)],
        axis=0)
    iw = iw_ref[0]
    iw_cols = [iw[:, hh:hh + 1] for hh in range(IDX_HEADS)]

    def score_body(c, carry):
        off = pl.multiple_of(c * tk, tk)
        rel = jnp.maximum(_nt_dot(iq4, ik4_ref[0, pl.ds(off, tk), :]), 0.0)
        score = iw_cols[0] * rel[0:tq]
        for hh in range(1, IDX_HEADS):
            score = score + iw_cols[hh] * rel[hh * tq:(hh + 1) * tq]
        bits = lax.bitcast_convert_type(score, jnp.int32)
        key = bits ^ ((bits >> 31) & jnp.int32(0x7FFFFFFF))
        kpos = off + lane_tk
        key = jnp.where(score == 0.0, (seq - 1) - kpos, jnp.where(key > 0, key + seq, key))
        key = jnp.where(kpos <= qpos, key, jnp.int32(INT_MIN))
        keys_ref[:, pl.ds(off, tk)] = key
        kts = []
        for j in range(nsub):
            kt = key[:, j * LANES:(j + 1) * LANES].T
            keys_t_ref[pl.ds(pl.multiple_of(off + j * LANES, LANES), LANES), :] = kt
            kts.append(kt)
        for gl in range(tk // BIT_GROUP):
            words = [kts[(gl * BIT_GROUP + 8 * g) // LANES][(8 * g) % LANES:(8 * g) % LANES + 8]
                     for g in range(32)]
            words = _bit_transpose32(words)
            words[0] = ~words[0]
            for b in range(32):
                planes_ref[c * (tk // BIT_GROUP) + gl, b] = words[b]
        return carry

    lax.fori_loop(0, nkv, score_body, 0)

    rows_tk = lax.broadcasted_iota(jnp.int32, (tk, 1), 0)
    cnt_rows = 64

    def count(pred):
        def body(c, cnt):
            off = pl.multiple_of(c * tk, tk)
            hit = pred(keys_t_ref[pl.ds(off, tk), :], off + rows_tk)
            for g in range(tk // cnt_rows):
                cnt = jnp.where(hit[g * cnt_rows:(g + 1) * cnt_rows], cnt + 1, cnt)
            return cnt
        cnt = lax.fori_loop(0, nkv, body, jnp.zeros((cnt_rows, tq), jnp.int32))
        return jnp.sum(cnt.astype(F32), axis=0, keepdims=True)

    ngroups = planes_ref.shape[0]
    live = nkv * (tk // BIT_GROUP)
    for g in range(ngroups):
        act_ref[g] = jnp.broadcast_to(jnp.where(g < live, jnp.int32(-1), jnp.int32(0)), (8, tq))

    def sweep(i, keep, first=False):
        parts = [jnp.zeros((8, tq), jnp.int32) for _ in range(4)]
        for g in range(ngroups):
            act = act_ref[g]
            if not first:
                act = act & (planes_ref[g, i - 1] ^ keep)
                act_ref[g] = act
            parts[g % 4] = parts[g % 4] + lax.population_count(act & planes_ref[g, i])
        cnt = (parts[0] + parts[1]) + (parts[2] + parts[3])
        return jnp.sum(cnt.astype(F32), axis=0, keepdims=True)

    def decide(i, ones, thr_u, n_gt):
        take = n_gt + ones >= topk
        thr_u = thr_u | jnp.where(take, jnp.left_shift(jnp.int32(1), 31 - i), 0)
        return thr_u, jnp.where(take, n_gt, n_gt + ones), jnp.where(take, 0, -1).astype(jnp.int32)

    def pass_body(i, carry):
        thr_u, n_gt, keep = carry
        return decide(i, sweep(i, keep), thr_u, n_gt)

    start = decide(0, sweep(0, None, first=True),
                   jnp.zeros((1, tq), jnp.int32), jnp.zeros((1, tq), F32))
    thr_u, n_gt, keep = lax.fori_loop(1, 32, pass_body, start)
    n_eq = jnp.zeros((8, tq), jnp.int32)
    for g in range(ngroups):
        n_eq = n_eq + lax.population_count(act_ref[g] & (planes_ref[g, 31] ^ keep))
    thr = thr_u ^ jnp.int32(INT_MIN)
    cnt_ge = n_gt + jnp.sum(n_eq.astype(F32), axis=0, keepdims=True)
    need = topk - n_gt

    jcut_ref[...] = jnp.full((1, tq), 2 ** 30, jnp.int32)

    @pl.when(jnp.max(cnt_ge) > topk)
    def _():
        def idx_body(i, j):
            cand = j + jnp.left_shift(jnp.int32(1), idx_bits - 1 - i)
            cnt = count(lambda kc, kpos: (kc == thr) & (kpos < cand))
            return jnp.where(cnt < need, cand, j)
        jcut_ref[...] = lax.fori_loop(0, idx_bits, idx_body, jnp.zeros((1, tq), jnp.int32))

    def as_rows(x):
        return jnp.broadcast_to(x, (LANES, tq)).T

    thr_r = as_rows(thr)
    jcut_r = as_rows(jcut_ref[...])

    q = q_ref[0]
    q8 = jnp.concatenate(
        [_split_heads_rows(q[:, p * LANES:(p + 1) * LANES]) for p in range(4)], axis=0)
    nh = 8
    m_ref[...] = jnp.full(m_ref.shape, NEG_BIG, F32)
    l_ref[...] = jnp.zeros(l_ref.shape, F32)
    acc_ref[...] = jnp.zeros(acc_ref.shape, F32)

    def attn_body(c, carry):
        off = pl.multiple_of(c * tk, tk)
        s = _nt_dot(q8, k2_ref[0, pl.ds(off, tk), :])
        bias = []
        for j in range(nsub):
            keyj = keys_ref[:, pl.ds(pl.multiple_of(off + j * LANES, LANES), LANES)]
            kpos = off + j * LANES + lane
            sel = ((keyj > thr_r) | ((keyj == thr_r) & (kpos <= jcut_r))) & (kpos <= qpos)
            bias.append(jnp.where(sel, 0.0, NEG_BIG))
        _online_update(s, v2_ref[0, pl.ds(off, tk), :], m_ref, l_ref, acc_ref, bias=bias)
        return carry

    lax.fori_loop(0, nkv, attn_body, 0)

    o = _normalised(l_ref, acc_ref)
    lo = lane < HEAD_DIM
    for p in range(4):
        o_ref[0, :, p * LANES:(p + 1) * LANES] = jnp.where(
            lo, o[(2 * p) * tq:(2 * p + 1) * tq], o[(2 * p + 1) * tq:(2 * p + 2) * tq]
        ).astype(o_ref.dtype)


def _dsa_attn(bq, iq, iw, ik4, k2, v2, tq, tk):
    B, S, C = bq.shape
    topk = min(IDX_TOPK_MAX, S // 4)
    idx_bits = max(1, (S - 1).bit_length())
    kern = functools.partial(_dsa_kernel, tq=tq, tk=tk, topk=topk, idx_bits=idx_bits)
    row = lambda width: pl.BlockSpec((1, tq, width), lambda b, i: (b, i, 0))
    full = pl.BlockSpec((1, S, LANES), lambda b, i: (b, 0, 0))
    return pl.pallas_call(
        kern,
        grid=(B, S // tq),
        in_specs=[row(C), row(LANES), row(LANES), full, full, full],
        out_specs=row(C),
        out_shape=jax.ShapeDtypeStruct(bq.shape, BF16),
        scratch_shapes=[pltpu.VMEM((tq, S), jnp.int32), pltpu.VMEM((S, tq), jnp.int32),
                        pltpu.VMEM((S // BIT_GROUP, 32, 8, tq), jnp.int32),
                        pltpu.VMEM((S // BIT_GROUP, 8, tq), jnp.int32),
                        pltpu.VMEM((1, tq), jnp.int32),
                        pltpu.VMEM((8 * tq, LANES), F32), pltpu.VMEM((8 * tq, LANES), F32),
                        pltpu.VMEM((8 * tq, LANES), F32)],
        compiler_params=_cparams(("parallel", "arbitrary")),
        name="dsa_attn",
    )(bq, iq, iw, ik4, k2, v2)


def _moba_kernel(q_ref, k_ref, v_ref, km_ref, o_ref, m_ref, l_ref, acc_ref, sa_ref, sb_ref,
                 *, nb, kb):
    tq = MOBA_BLOCK
    own = pl.program_id(2)
    lane = lax.broadcasted_iota(jnp.int32, (1, LANES), 1)
    q2 = _split_heads_rows(q_ref[0])

    km = jnp.concatenate([km_ref[0], jnp.zeros((LANES - nb, LANES), F32)], axis=0)
    km_hi = km.astype(BF16)
    r1 = km - km_hi.astype(F32)
    km_mid = r1.astype(BF16)
    km_lo = (r1 - km_mid.astype(F32)).astype(BF16)
    nb8 = -(-nb // 8) * 8
    gate_t = (_nt_dot(km_hi, q2) + _nt_dot(km_mid, q2) + _nt_dot(km_lo, q2))[:nb8]
    blk = lax.broadcasted_iota(jnp.int32, (nb8, 1), 0)
    blk_f = blk.astype(F32)
    past = blk < own
    g = jnp.where(past, gate_t, -jnp.inf)
    sel = jnp.zeros(g.shape, jnp.bool_)
    for _ in range(min(MOBA_TOPK, nb - 1)):
        mx = jnp.max(g, axis=0, keepdims=True)
        first = jnp.min(jnp.where(g == mx, blk_f, float(LANES)), axis=0, keepdims=True)
        pick = blk_f == first
        sel = sel | pick
        g = jnp.where(pick, -jnp.inf, g)
    bias_t = jnp.concatenate([jnp.where(sel & past, 0.0, NEG_BIG),
                              jnp.full((LANES - nb8, 2 * tq), NEG_BIG, F32)], axis=0)
    qa = jnp.concatenate([q2, bias_t.T.astype(BF16)], axis=1)

    off = pl.multiple_of(own * tq, tq)
    s = _nt_dot(q2, k_ref[0, pl.ds(off, tq), :])
    r = lax.broadcasted_iota(jnp.int32, (2 * tq, 1), 0)
    qloc = jnp.where(r >= tq, r - tq, r)
    kloc = lax.broadcasted_iota(jnp.int32, (1, tq), 1)
    s = jnp.where(kloc <= qloc, s, NEG_BIG)
    m_ref[...] = jnp.full(m_ref.shape, NEG_BIG, F32)
    l_ref[...] = jnp.zeros(l_ref.shape, F32)
    acc_ref[...] = jnp.zeros(acc_ref.shape, F32)
    _online_update(s, v_ref[0, pl.ds(off, tq), :], m_ref, l_ref, acc_ref)

    tkc = kb * tq
    row_blk = lax.broadcasted_iota(jnp.int32, (tkc, 1), 0) >> (MOBA_BLOCK.bit_length() - 1)

    def rows_of(ref, i):
        return ref[0, pl.ds(pl.multiple_of(i * tkc, tkc), tkc), :]

    def logits(i):
        onehot = jnp.where(lane == i * kb + row_blk, 1.0, 0.0).astype(BF16)
        return _nt_dot(qa, jnp.concatenate([rows_of(k_ref, i), onehot], axis=1))

    def step(s, i):
        _online_update(s, rows_of(v_ref, i), m_ref, l_ref, acc_ref)

    _pipelined_chunks((own + kb - 1) // kb, logits, step, sa_ref, sb_ref)

    o = _normalised(l_ref, acc_ref)
    o_ref[0] = jnp.where(lane < HEAD_DIM, o[:tq], o[tq:]).astype(o_ref.dtype)


def _moba_attn(cq, ck, cv, kmean):
    B, S, C = cq.shape
    nb = S // MOBA_BLOCK
    tq = MOBA_BLOCK
    kb = 2 if nb % 2 == 0 else 1
    return pl.pallas_call(
        functools.partial(_moba_kernel, nb=nb, kb=kb),
        grid=(B, C // LANES, nb),
        in_specs=[pl.BlockSpec((1, tq, LANES), lambda b, p, i: (b, i, p)),
                  pl.BlockSpec((1, S, LANES), lambda b, p, i: (b, 0, p)),
                  pl.BlockSpec((1, S, LANES), lambda b, p, i: (b, 0, p)),
                  pl.BlockSpec((1, nb, LANES), lambda b, p, i: (b, 0, p))],
        out_specs=pl.BlockSpec((1, tq, LANES), lambda b, p, i: (b, i, p)),
        out_shape=jax.ShapeDtypeStruct(cq.shape, BF16),
        scratch_shapes=[pltpu.VMEM((2 * tq, LANES), F32), pltpu.VMEM((2 * tq, LANES), F32),
                        pltpu.VMEM((2 * tq, LANES), F32),
                        pltpu.VMEM((2 * tq, kb * tq), F32), pltpu.VMEM((2 * tq, kb * tq), F32)],
        compiler_params=_cparams(("parallel", "parallel", "arbitrary")),
        name="moba_attn",
    )(cq, ck, cv, kmean)


def _sigmoid(x):
    return 1.0 / (1.0 + jnp.exp(-x))


def _merge_kernel(ya_ref, yb_ref, yc_ref, gates_ref, x_ref, wb_ref, wo_ref, g_ref, o_ref):
    d = x_ref.shape[2]
    merged = None
    for n, y_ref in enumerate((ya_ref, yb_ref, yc_ref)):
        br = jnp.dot(y_ref[0], wb_ref[n], preferred_element_type=F32)
        term = _sigmoid(gates_ref[0, :, n * d:(n + 1) * d]) * br
        merged = term if merged is None else merged + term
    m = jnp.dot(merged.astype(BF16), wo_ref[...], preferred_element_type=F32)
    o_ref[0] = x_ref[0] + _rms(m, g_ref[...])


def _merge(ya, yb, yc, gates, x, wb, wo, g, tm):
    B, S, D = x.shape
    row = lambda width: pl.BlockSpec((1, tm, width), lambda b, m: (b, m, 0))
    return pl.pallas_call(
        _merge_kernel,
        grid=(B, S // tm),
        in_specs=[row(BRANCH_WIDTH), row(BRANCH_WIDTH), row(BRANCH_WIDTH), row(N_BRANCH * D), row(D),
                  _resident(wb.shape, lambda b, m: (0, 0, 0)),
                  _resident(wo.shape, lambda b, m: (0, 0)),
                  _resident((1, D), lambda b, m: (0, 0))],
        out_specs=row(D),
        out_shape=jax.ShapeDtypeStruct(x.shape, F32),
        compiler_params=_cparams(("parallel", "parallel")),
        name="merge_out",
    )(ya, yb, yc, gates, x, wb, wo, g.reshape(1, D))


def _ffn_kernel(x_ref, g_in_ref, wi_ref, wo_ref, g_out_ref, o_ref, acc_ref, *, d_ff, tf):
    x = x_ref[0]
    h = _rms(x, g_in_ref[...]).astype(BF16)
    acc_ref[...] = jnp.zeros(acc_ref.shape, F32)

    def body(c, carry):
        off = pl.multiple_of(c * tf, tf)
        gt = jnp.dot(h, wi_ref[:, pl.ds(off, tf)], preferred_element_type=F32)
        up_off = pl.multiple_of(d_ff + off, LANES)
        up = jnp.dot(h, wi_ref[:, pl.ds(up_off, tf)], preferred_element_type=F32)
        act = (gt * _sigmoid(gt) * up).astype(BF16)
        acc_ref[...] += jnp.dot(act, wo_ref[pl.ds(off, tf), :], preferred_element_type=F32)
        return carry

    lax.fori_loop(0, d_ff // tf, body, 0)
    o_ref[0] = x + _rms(acc_ref[...], g_out_ref[...])


def _ffn(x, g_in, wi, wo, g_out, tm, tf):
    B, S, D = x.shape
    d_ff = wo.shape[0]
    row = pl.BlockSpec((1, tm, D), lambda b, m: (b, m, 0))
    vec = _resident((1, D), lambda b, m: (0, 0))
    return pl.pallas_call(
        functools.partial(_ffn_kernel, d_ff=d_ff, tf=tf),
        grid=(B, S // tm),
        in_specs=[row, vec, _resident(wi.shape, lambda b, m: (0, 0)),
                  _resident(wo.shape, lambda b, m: (0, 0)), vec],
        out_specs=row,
        out_shape=jax.ShapeDtypeStruct(x.shape, F32),
        scratch_shapes=[pltpu.VMEM((tm, D), F32)],
        compiler_params=_cparams(("parallel", "parallel")),
        name="swiglu_ffn",
    )(x, g_in.reshape(1, D), wi, wo, g_out.reshape(1, D))


def _pack_w_in(w):
    d = w.shape[0]
    offs = [0]
    for width in (512, 512, 512, 512, 64, 64, 128, 32, 4, 512, 512, 512, 3072):
        offs.append(offs[-1] + width)
    seg = lambda i: w[:, offs[i]:offs[i + 1]]
    scale = HEAD_DIM ** -0.5 * LOG2E
    cols = [seg(0) * scale, seg(1), seg(2), seg(3) * scale,
            seg(4), seg(4), seg(5), seg(5), seg(6),
            seg(7), seg(7), seg(7), seg(7),
            seg(8), jnp.zeros((d, LANES - IDX_HEADS), w.dtype),
            seg(9) * scale, seg(10), seg(11), seg(12)]
    packed = jnp.concatenate(cols, axis=1).astype(BF16)
    assert packed.shape[1] == PACKED_COLS
    return packed


def kernel(x, positions, w_in, w_branch, w_out, lambda_q1, lambda_k1, lambda_q2, lambda_k2,
           subln_g, norm_g, w_ffn_in, w_ffn_out):
    B, S, D = x.shape
    depth = w_in.shape[0]
    tm = min(512, S)
    tab = _rope_table(positions, tm)
    for l in range(depth):
        lam_init = 0.8 - 0.6 * math.exp(-0.3 * l)
        (aq, ak, av, bq, k2, v2, iq, ik4, iw, cq, ck, cv, gates, kmean) = _inproj(
            x, norm_g[l, 0], tab, _pack_w_in(w_in[l]), tm)
        kmean = kmean.reshape(B, S // MOBA_BLOCK, 512)
        lam_rows = jnp.stack([lambda_q1[l], lambda_k1[l], lambda_q2[l], lambda_k2[l]])
        ya = _diff_attn(aq, ak, av, lam_rows, subln_g[l], lam_init, t=min(1024, S))
        yb = _dsa_attn(bq, iq, iw, ik4, k2, v2, tq=128, tk=min(512, S))
        yc = _moba_attn(cq, ck, cv, kmean)
        x = _merge(ya, yb, yc, gates, x, w_branch[l].astype(BF16), w_out[l].astype(BF16),
                   norm_g[l, 1], tm)
        x = _ffn(x, norm_g[l, 2], w_ffn_in[l].astype(BF16), w_ffn_out[l].astype(BF16),
                 norm_g[l, 3], tm, tf=256)
    return x
```

```python
import functools
import math

import jax
import jax.numpy as jnp
from jax import lax
from jax.experimental import pallas as pl
from jax.experimental.pallas import tpu as pltpu

F32 = jnp.float32
BF16 = jnp.bfloat16

LANES = 128
HEAD_DIM = 64
ROPE_THETA = 500000.0
NORM_EPS = 1e-6
IDX_HEADS = 4
IDX_DIM = 32
IDX_TOPK_MAX = 256
MOBA_BLOCK = 256
MOBA_TOPK = 3
N_BRANCH = 3
BRANCH_WIDTH = 512
NEG_BIG = -1e30
LOG2E = math.log2(math.e)
ROW_BLOCK = 64
INT_MIN = -2 ** 31
VMEM_LIMIT = 56 * 1024 * 1024

OFF_AQ, OFF_AK, OFF_AV, OFF_BQ = 0, 512, 1024, 1536
OFF_K2, OFF_V2, OFF_IQ, OFF_IK4, OFF_IW = 2048, 2176, 2304, 2432, 2560
OFF_CQ, OFF_CK, OFF_CV, OFF_G = 2688, 3200, 3712, 4224
PACKED_COLS = OFF_G + 3 * 1024


def _nt_dot(a, b):
    return lax.dot_general(a, b, (((1,), (1,)), ((), ())), preferred_element_type=F32)


def _rms(x, g):
    return x * lax.rsqrt(jnp.mean(x * x, axis=-1, keepdims=True) + NORM_EPS) * g


def _cparams(sem):
    return pltpu.CompilerParams(dimension_semantics=sem, vmem_limit_bytes=VMEM_LIMIT)


def _resident(shape, index_map):
    return pl.BlockSpec(shape, index_map, pipeline_mode=pl.Buffered(1))


def _rope_table_kernel(pos_ref, tab_ref):
    pos = pos_ref[0].astype(F32)
    lane = lax.broadcasted_iota(jnp.int32, (1, LANES), 1)

    def tables(head_dim):
        rot = head_dim // 4
        half = rot // 2
        d = lane & (head_dim - 1)
        fi = d & (half - 1)
        inv = jnp.zeros((1, LANES), F32)
        for i in range(half):
            inv = jnp.where(fi == i, F32(ROPE_THETA ** (-(2.0 * i) / rot)), inv)
        ang = pos * inv
        cos = jnp.cos(ang)
        sin = jnp.sin(ang)
        c = jnp.where(d < rot, cos, 1.0)
        s = jnp.where(d < half, -sin, jnp.where(d < rot, sin, 0.0))
        return c, s

    c64, s64 = tables(HEAD_DIM)
    c32, s32 = tables(IDX_DIM)
    tab_ref[0, :, 0 * LANES:1 * LANES] = c64
    tab_ref[0, :, 1 * LANES:2 * LANES] = s64
    tab_ref[0, :, 2 * LANES:3 * LANES] = c32
    tab_ref[0, :, 3 * LANES:4 * LANES] = s32


def _rope_table(positions, tm):
    B, S = positions.shape
    return pl.pallas_call(
        _rope_table_kernel,
        grid=(B, S // tm),
        in_specs=[pl.BlockSpec((1, tm, 1), lambda b, m: (b, m, 0))],
        out_specs=pl.BlockSpec((1, tm, 4 * LANES), lambda b, m: (b, m, 0)),
        out_shape=jax.ShapeDtypeStruct((B, S, 4 * LANES), F32),
        compiler_params=_cparams(("parallel", "parallel")),
        name="rope_table",
    )(positions.reshape(B, S, 1))


def _inproj_kernel(x_ref, g_ref, tab_ref, w_ref,
                   aq_ref, ak_ref, av_ref, bq_ref, k2_ref, v2_ref, iq_ref, ik4_ref, iw_ref,
                   cq_ref, ck_ref, cv_ref, gates_ref, kmean_ref):
    h = _rms(x_ref[0], g_ref[...]).astype(BF16)
    lane = lax.broadcasted_iota(jnp.int32, (1, LANES), 1)

    def proj(off, width):
        return jnp.dot(h, w_ref[:, off:off + width], preferred_element_type=F32)

    def rope_block(y, head_dim, c, s):
        half = head_dim // 8
        up = pltpu.roll(y, LANES - half, 1)
        dn = pltpu.roll(y, half, 1)
        return y * c + jnp.where((lane & (head_dim - 1)) < half, up, dn) * s

    def emit(out_ref, off, width, head_dim=None, dtype=BF16):
        y = proj(off, width)
        if head_dim is None:
            out_ref[0] = y.astype(dtype)
            return None
        t0 = 0 if head_dim == HEAD_DIM else 2 * LANES
        c = tab_ref[0, :, t0:t0 + LANES]
        s = tab_ref[0, :, t0 + LANES:t0 + 2 * LANES]
        blocks = []
        for j in range(width // LANES):
            r = rope_block(y[:, j * LANES:(j + 1) * LANES], head_dim, c, s)
            out_ref[0, :, j * LANES:(j + 1) * LANES] = r.astype(dtype)
            blocks.append(r)
        return blocks

    emit(aq_ref, OFF_AQ, 512, HEAD_DIM)
    emit(ak_ref, OFF_AK, 512, HEAD_DIM)
    emit(av_ref, OFF_AV, 512)
    emit(bq_ref, OFF_BQ, 512, HEAD_DIM)
    emit(k2_ref, OFF_K2, LANES, HEAD_DIM)
    emit(v2_ref, OFF_V2, LANES)
    emit(iq_ref, OFF_IQ, LANES, IDX_DIM)
    emit(ik4_ref, OFF_IK4, LANES, IDX_DIM)
    emit(iw_ref, OFF_IW, LANES, dtype=F32)
    emit(cq_ref, OFF_CQ, 512, HEAD_DIM)
    ck_blocks = emit(ck_ref, OFF_CK, 512, HEAD_DIM)
    emit(cv_ref, OFF_CV, 512)
    emit(gates_ref, OFF_G, 3 * 1024, dtype=F32)

    tm = x_ref.shape[1]
    for blk in range(tm // MOBA_BLOCK):
        for j, r in enumerate(ck_blocks):
            part = r[blk * MOBA_BLOCK:(blk + 1) * MOBA_BLOCK]
            kmean_ref[0, 0, blk:blk + 1, j * LANES:(j + 1) * LANES] = (
                jnp.mean(part, axis=0, keepdims=True))


def _inproj(x, g, tab, w_packed, tm):
    B, S, D = x.shape
    nm = S // tm
    row = lambda width: pl.BlockSpec((1, tm, width), lambda b, m: (b, m, 0))
    shp = lambda width, dt: jax.ShapeDtypeStruct((B, S, width), dt)
    widths = [(512, BF16)] * 4 + [(LANES, BF16)] * 4 + [(LANES, F32)] + [(512, BF16)] * 3 + [(3072, F32)]
    out_specs = [row(w) for w, _ in widths]
    out_shape = [shp(w, dt) for w, dt in widths]
    out_specs.append(pl.BlockSpec((1, 1, tm // MOBA_BLOCK, 512), lambda b, m: (b, m, 0, 0)))
    out_shape.append(jax.ShapeDtypeStruct((B, nm, tm // MOBA_BLOCK, 512), F32))
    return pl.pallas_call(
        _inproj_kernel,
        grid=(B, nm),
        in_specs=[row(D),
                  _resident((1, D), lambda b, m: (0, 0)),
                  row(4 * LANES),
                  _resident((D, PACKED_COLS), lambda b, m: (0, 0))],
        out_specs=out_specs,
        out_shape=out_shape,
        compiler_params=_cparams(("parallel", "parallel")),
        name="inproj",
    )(x, g.reshape(1, D), tab, w_packed)


def _split_heads_rows(q):
    lo = lax.broadcasted_iota(jnp.int32, (1, LANES), 1) < HEAD_DIM
    zero = jnp.zeros_like(q)
    return jnp.concatenate([jnp.where(lo, q, zero), jnp.where(lo, zero, q)], axis=0)


def _lane_blocks(s):
    return [s[:, j * LANES:(j + 1) * LANES] for j in range(s.shape[1] // LANES)]


def _online_update(s, v, m_ref, l_ref, acc_ref, bias=None, row0=0):
    rows_total = s.shape[0]
    nblk = s.shape[1] // LANES
    p_rows, alphas = [], []
    for r0 in range(0, rows_total, ROW_BLOCK):
        rows = slice(row0 + r0, row0 + r0 + ROW_BLOCK)
        cols = [s[r0:r0 + ROW_BLOCK, j * LANES:(j + 1) * LANES] for j in range(nblk)]
        if bias is not None:
            b0 = r0 % bias[0].shape[0]
            cols = [c + bj[b0:b0 + ROW_BLOCK] for c, bj in zip(cols, bias)]
        m_prev = m_ref[rows]
        m_new = jnp.maximum(
            m_prev, jnp.max(functools.reduce(jnp.maximum, cols), axis=1, keepdims=True))
        alpha = jnp.exp2(m_prev - m_new)
        ps = [jnp.exp2(c - m_new) for c in cols]
        l_ref[rows] = alpha * l_ref[rows] + functools.reduce(jnp.add, ps)
        m_ref[rows] = m_new
        p_rows.append(jnp.concatenate([pj.astype(BF16) for pj in ps], axis=1))
        alphas.append(alpha)
    p = jnp.concatenate(p_rows, axis=0)
    alpha = jnp.concatenate(alphas, axis=0)
    span = slice(row0, row0 + rows_total)
    acc_ref[span] = alpha * acc_ref[span] + jnp.dot(p, v, preferred_element_type=F32)


def _normalised(l_ref, acc_ref):
    return acc_ref[...] / jnp.sum(l_ref[...], axis=1, keepdims=True)


def _diff_attn_kernel(q_ref, k_ref, v_ref, lam_ref, g_ref, o_ref, m_ref, l_ref, acc_ref,
                      *, t, lam_init):
    qi = pl.program_id(2)
    half = t // 2
    q = q_ref[0]
    q2 = jnp.concatenate([_split_heads_rows(q[:half]), _split_heads_rows(q[half:])], axis=0)
    m_ref[...] = jnp.full(m_ref.shape, NEG_BIG, F32)
    l_ref[...] = jnp.zeros(l_ref.shape, F32)
    acc_ref[...] = jnp.zeros(acc_ref.shape, F32)

    def body(c, carry):
        off = pl.multiple_of(c * t, t)
        _online_update(_nt_dot(q2, k_ref[0, pl.ds(off, t), :]), v_ref[0, pl.ds(off, t), :],
                       m_ref, l_ref, acc_ref)
        return carry

    lax.fori_loop(0, qi, body, 0)

    r = lax.broadcasted_iota(jnp.int32, (t, 1), 0)
    causal = lax.broadcasted_iota(jnp.int32, (1, half), 1) <= jnp.where(r >= half, r - half, r)
    off_a = pl.multiple_of(qi * t, t)
    s = _nt_dot(q2, k_ref[0, pl.ds(off_a, half), :])
    s = jnp.concatenate([jnp.where(causal, s[:t], NEG_BIG), s[t:]], axis=0)
    _online_update(s, v_ref[0, pl.ds(off_a, half), :], m_ref, l_ref, acc_ref)
    off_b = pl.multiple_of(qi * t + half, half)
    s = jnp.where(causal, _nt_dot(q2[t:], k_ref[0, pl.ds(off_b, half), :]), NEG_BIG)
    _online_update(s, v_ref[0, pl.ds(off_b, half), :], m_ref, l_ref, acc_ref, row0=t)

    lam_rows = lam_ref[...]
    e1 = jnp.exp(jnp.sum(lam_rows[0:1] * lam_rows[1:2], axis=1, keepdims=True))
    e2 = jnp.exp(jnp.sum(lam_rows[2:3] * lam_rows[3:4], axis=1, keepdims=True))
    lam = e1 - e2 + lam_init
    o = _normalised(l_ref, acc_ref)
    o1 = jnp.concatenate([o[:half], o[t:t + half]], axis=0)
    o2 = jnp.concatenate([o[half:t], o[t + half:]], axis=0)
    o_ref[0] = (_rms(o1 - lam * o2, g_ref[...]) * (1.0 - lam_init)).astype(o_ref.dtype)


def _diff_attn(aq, ak, av, lam_rows, subln_g, lam_init, t):
    B, S, _ = aq.shape
    nh = aq.shape[2] // LANES
    kern = functools.partial(_diff_attn_kernel, t=t, lam_init=lam_init)
    return pl.pallas_call(
        kern,
        grid=(B, nh, S // t),
        in_specs=[pl.BlockSpec((1, t, LANES), lambda b, h, i: (b, i, h)),
                  pl.BlockSpec((1, S, LANES), lambda b, h, i: (b, 0, h)),
                  pl.BlockSpec((1, S, LANES), lambda b, h, i: (b, 0, h)),
                  pl.BlockSpec((4, HEAD_DIM), lambda b, h, i: (0, 0)),
                  pl.BlockSpec((1, LANES), lambda b, h, i: (0, 0))],
        out_specs=pl.BlockSpec((1, t, LANES), lambda b, h, i: (b, i, h)),
        out_shape=jax.ShapeDtypeStruct(aq.shape, BF16),
        scratch_shapes=[pltpu.VMEM((2 * t, LANES), F32), pltpu.VMEM((2 * t, LANES), F32),
                        pltpu.VMEM((2 * t, LANES), F32)],
        compiler_params=_cparams(("parallel", "parallel", "arbitrary")),
        name="diff_attn",
    )(aq, ak, av, lam_rows, subln_g.reshape(1, LANES))


BIT_GROUP = 256


def _bit_transpose32(words):
    w = list(words)
    for j, m in ((16, 0x0000FFFF), (8, 0x00FF00FF), (4, 0x0F0F0F0F), (2, 0x33333333), (1, 0x55555555)):
        for k in range(32):
            if k & j == 0:
                t = (w[k] ^ (w[k + j] >> j)) & m
                w[k] = w[k] ^ t
                w[k + j] = w[k + j] ^ (t << j)
    return w

def _dsa_kernel(q_ref, iq_ref, iw_ref, ik4_ref, k2_ref, v2_ref, o_ref,
                keys_ref, keys_t_ref, planes_ref, act_ref, jcut_ref, m_ref, l_ref, acc_ref,
                *, tq, tk, topk, idx_bits):
    qi = pl.program_id(1)
    nkv = (qi * tq + tq - 1) // tk + 1
    lane = lax.broadcasted_iota(jnp.int32, (1, LANES), 1)
    qpos = qi * tq + lax.broadcasted_iota(jnp.int32, (tq, 1), 0)
    lane_tk = lax.broadcasted_iota(jnp.int32, (1, tk), 1)
    nsub = tk // LANES
    seq = keys_ref.shape[1]

    @pl.when(qi == 0)
    def _():
        planes_ref[...] = jnp.zeros(planes_ref.shape, jnp.int32)

    iq = iq_ref[0]
    zero = jnp.zeros_like(iq)
    iq4 = jnp.concatenate(
        [jnp.where((lane >> (IDX_DIM.bit_length() - 1)) == hh, iq, zero) for hh in range(IDX_HEADS)],
        axis=0)
    iw = iw_ref[0]
    iw_cols = [iw[:, hh:hh + 1] for hh in range(IDX_HEADS)]

    def score_body(c, carry):
        off = pl.multiple_of(c * tk, tk)
        rel = jnp.maximum(_nt_dot(iq4, ik4_ref[0, pl.ds(off, tk), :]), 0.0)
        score = iw_cols[0] * rel[0:tq]
        for hh in range(1, IDX_HEADS):
            score = score + iw_cols[hh] * rel[hh * tq:(hh + 1) * tq]
        bits = lax.bitcast_convert_type(score, jnp.int32)
        key = bits ^ ((bits >> 31) & jnp.int32(0x7FFFFFFF))
        kpos = off + lane_tk
        key = jnp.where(score == 0.0, (seq - 1) - kpos, jnp.where(key > 0, key + seq, key))
        key = jnp.where(kpos <= qpos, key, jnp.int32(INT_MIN))
        keys_ref[:, pl.ds(off, tk)] = key
        kts = []
        for j in range(nsub):
            kt = key[:, j * LANES:(j + 1) * LANES].T
            keys_t_ref[pl.ds(pl.multiple_of(off + j * LANES, LANES), LANES), :] = kt
            kts.append(kt)
        for gl in range(tk // BIT_GROUP):
            words = [kts[(gl * BIT_GROUP + 8 * g) // LANES][(8 * g) % LANES:(8 * g) % LANES + 8]
                     for g in range(32)]
            words = _bit_transpose32(words)
            words[0] = ~words[0]
            for b in range(32):
                planes_ref[c * (tk // BIT_GROUP) + gl, b] = words[b]
        return carry

    lax.fori_loop(0, nkv, score_body, 0)

    rows_tk = lax.broadcasted_iota(jnp.int32, (tk, 1), 0)
    cnt_rows = 64

    def count(pred):
        def body(c, cnt):
            off = pl.multiple_of(c * tk, tk)
            hit = pred(keys_t_ref[pl.ds(off, tk), :], off + rows_tk)
            for g in range(tk // cnt_rows):
                cnt = jnp.where(hit[g * cnt_rows:(g + 1) * cnt_rows], cnt + 1, cnt)
            return cnt
        cnt = lax.fori_loop(0, nkv, body, jnp.zeros((cnt_rows, tq), jnp.int32))
        return jnp.sum(cnt.astype(F32), axis=0, keepdims=True)

    ngroups = planes_ref.shape[0]
    live = nkv * (tk // BIT_GROUP)
    for g in range(ngroups):
        act_ref[g] = jnp.broadcast_to(jnp.where(g < live, jnp.int32(-1), jnp.int32(0)), (8, tq))

    def sweep(i, keep, first=False):
        parts = [jnp.zeros((8, tq), jnp.int32) for _ in range(4)]
        for g in range(ngroups):
            act = act_ref[g]
            if not first:
                act = act & (planes_ref[g, i - 1] ^ keep)
                act_ref[g] = act
            parts[g % 4] = parts[g % 4] + lax.population_count(act & planes_ref[g, i])
        cnt = (parts[0] + parts[1]) + (parts[2] + parts[3])
        return jnp.sum(cnt.astype(F32), axis=0, keepdims=True)

    def decide(i, ones, thr_u, n_gt):
        take = n_gt + ones >= topk
        thr_u = thr_u | jnp.where(take, jnp.left_shift(jnp.int32(1), 31 - i), 0)
        return thr_u, jnp.where(take, n_gt, n_gt + ones), jnp.where(take, 0, -1).astype(jnp.int32)

    def pass_body(i, carry):
        thr_u, n_gt, keep = carry
        return decide(i, sweep(i, keep), thr_u, n_gt)

    start = decide(0, sweep(0, None, first=True),
                   jnp.zeros((1, tq), jnp.int32), jnp.zeros((1, tq), F32))
    thr_u, n_gt, keep = lax.fori_loop(1, 32, pass_body, start)
    n_eq = jnp.zeros((8, tq), jnp.int32)
    for g in range(ngroups):
        n_eq = n_eq + lax.population_count(act_ref[g] & (planes_ref[g, 31] ^ keep))
    thr = thr_u ^ jnp.int32(INT_MIN)
    cnt_ge = n_gt + jnp.sum(n_eq.astype(F32), axis=0, keepdims=True)
    need = topk - n_gt

    jcut_ref[...] = jnp.full((1, tq), 2 ** 30, jnp.int32)

    @pl.when(jnp.max(cnt_ge) > topk)
    def _():
        def idx_body(i, j):
            cand = j + jnp.left_shift(jnp.int32(1), idx_bits - 1 - i)
            cnt = count(lambda kc, kpos: (kc == thr) & (kpos < cand))
            return jnp.where(cnt < need, cand, j)
        jcut_ref[...] = lax.fori_loop(0, idx_bits, idx_body, jnp.zeros((1, tq), jnp.int32))

    def as_rows(x):
        return jnp.broadcast_to(x, (LANES, tq)).T

    thr_r = as_rows(thr)
    jcut_r = as_rows(jcut_ref[...])

    q = q_ref[0]
    q8 = jnp.concatenate(
        [_split_heads_rows(q[:, p * LANES:(p + 1) * LANES]) for p in range(4)], axis=0)
    nh = 8
    m_ref[...] = jnp.full(m_ref.shape, NEG_BIG, F32)
    l_ref[...] = jnp.zeros(l_ref.shape, F32)
    acc_ref[...] = jnp.zeros(acc_ref.shape, F32)

    def attn_body(c, carry):
        off = pl.multiple_of(c * tk, tk)
        s = _nt_dot(q8, k2_ref[0, pl.ds(off, tk), :])
        bias = []
        for j in range(nsub):
            keyj = keys_ref[:, pl.ds(pl.multiple_of(off + j * LANES, LANES), LANES)]
            kpos = off + j * LANES + lane
            sel = ((keyj > thr_r) | ((keyj == thr_r) & (kpos <= jcut_r))) & (kpos <= qpos)
            bias.append(jnp.where(sel, 0.0, NEG_BIG))
        _online_update(s, v2_ref[0, pl.ds(off, tk), :], m_ref, l_ref, acc_ref, bias=bias)
        return carry

    lax.fori_loop(0, nkv, attn_body, 0)

    o = _normalised(l_ref, acc_ref)
    lo = lane < HEAD_DIM
    for p in range(4):
        o_ref[0, :, p * LANES:(p + 1) * LANES] = jnp.where(
            lo, o[(2 * p) * tq:(2 * p + 1) * tq], o[(2 * p + 1) * tq:(2 * p + 2) * tq]
        ).astype(o_ref.dtype)


def _dsa_attn(bq, iq, iw, ik4, k2, v2, tq, tk):
    B, S, C = bq.shape
    topk = min(IDX_TOPK_MAX, S // 4)
    idx_bits = max(1, (S - 1).bit_length())
    kern = functools.partial(_dsa_kernel, tq=tq, tk=tk, topk=topk, idx_bits=idx_bits)
    row = lambda width: pl.BlockSpec((1, tq, width), lambda b, i: (b, i, 0))
    full = pl.BlockSpec((1, S, LANES), lambda b, i: (b, 0, 0))
    return pl.pallas_call(
        kern,
        grid=(B, S // tq),
        in_specs=[row(C), row(LANES), row(LANES), full, full, full],
        out_specs=row(C),
        out_shape=jax.ShapeDtypeStruct(bq.shape, BF16),
        scratch_shapes=[pltpu.VMEM((tq, S), jnp.int32), pltpu.VMEM((S, tq), jnp.int32),
                        pltpu.VMEM((S // BIT_GROUP, 32, 8, tq), jnp.int32),
                        pltpu.VMEM((S // BIT_GROUP, 8, tq), jnp.int32),
                        pltpu.VMEM((1, tq), jnp.int32),
                        pltpu.VMEM((8 * tq, LANES), F32), pltpu.VMEM((8 * tq, LANES), F32),
                        pltpu.VMEM((8 * tq, LANES), F32)],
        compiler_params=_cparams(("parallel", "arbitrary")),
        name="dsa_attn",
    )(bq, iq, iw, ik4, k2, v2)


def _moba_kernel(q_ref, k_ref, v_ref, km_ref, o_ref, m_ref, l_ref, acc_ref, *, nb):
    blk_rows = MOBA_BLOCK
    rows_ab = 2 * blk_rows
    i = pl.program_id(2)
    lane = lax.broadcasted_iota(jnp.int32, (1, LANES), 1)
    q = q_ref[0]
    q2 = jnp.concatenate([_split_heads_rows(q[:blk_rows]), _split_heads_rows(q[blk_rows:])], axis=0)

    km = jnp.concatenate([km_ref[0], jnp.zeros((LANES - nb, LANES), F32)], axis=0)
    km_hi = km.astype(BF16)
    r1 = km - km_hi.astype(F32)
    km_mid = r1.astype(BF16)
    km_lo = (r1 - km_mid.astype(F32)).astype(BF16)
    nb8 = -(-nb // 8) * 8
    gate_t = (_nt_dot(km_hi, q2) + _nt_dot(km_mid, q2) + _nt_dot(km_lo, q2))[:nb8]
    blk = lax.broadcasted_iota(jnp.int32, (nb8, 1), 0)
    blk_f = blk.astype(F32)
    col = lax.broadcasted_iota(jnp.int32, (1, 2 * rows_ab), 1)
    past = blk < 2 * i + jnp.where(col >= rows_ab, 1, 0)
    g = jnp.where(past, gate_t, -jnp.inf)
    sel = jnp.zeros(g.shape, jnp.bool_)
    for _ in range(min(MOBA_TOPK, nb - 1)):
        mx = jnp.max(g, axis=0, keepdims=True)
        first = jnp.min(jnp.where(g == mx, blk_f, float(LANES)), axis=0, keepdims=True)
        pick = blk_f == first
        sel = sel | pick
        g = jnp.where(pick, -jnp.inf, g)
    bias_t = jnp.concatenate([jnp.where(sel & past, 0.0, NEG_BIG),
                              jnp.full((LANES - nb8, 2 * rows_ab), NEG_BIG, F32)], axis=0)
    qa = jnp.concatenate([q2, bias_t.T.astype(BF16)], axis=1)

    m_ref[...] = jnp.full(m_ref.shape, NEG_BIG, F32)
    l_ref[...] = jnp.zeros(l_ref.shape, F32)
    acc_ref[...] = jnp.zeros(acc_ref.shape, F32)

    def biased_logits(lhs, first_blk, nblk):
        n = nblk * blk_rows
        off = pl.multiple_of(first_blk * blk_rows, blk_rows)
        row_blk = lax.broadcasted_iota(jnp.int32, (n, 1), 0) >> (MOBA_BLOCK.bit_length() - 1)
        onehot = jnp.where(lane == first_blk + row_blk, 1.0, 0.0).astype(BF16)
        ka = jnp.concatenate([k_ref[0, pl.ds(off, n), :], onehot], axis=1)
        return _nt_dot(lhs, ka), v_ref[0, pl.ds(off, n), :]

    def body(c, carry):
        s, v = biased_logits(qa, 4 * c, 4)
        _online_update(s, v, m_ref, l_ref, acc_ref)
        return carry

    lax.fori_loop(0, i // 2, body, 0)

    @pl.when(i % 2 == 1)
    def _():
        s, v = biased_logits(qa, 2 * i - 2, 2)
        _online_update(s, v, m_ref, l_ref, acc_ref)

    r = lax.broadcasted_iota(jnp.int32, (rows_ab, 1), 0)
    causal = (lax.broadcasted_iota(jnp.int32, (1, blk_rows), 1)
              <= jnp.where(r >= blk_rows, r - blk_rows, r))
    off_a = pl.multiple_of(2 * i * blk_rows, blk_rows)
    s_a = jnp.where(causal, _nt_dot(q2[:rows_ab], k_ref[0, pl.ds(off_a, blk_rows), :]), NEG_BIG)
    s_b, v = biased_logits(qa[rows_ab:], 2 * i, 1)
    _online_update(jnp.concatenate([s_a, s_b], axis=0), v, m_ref, l_ref, acc_ref)
    off_b = pl.multiple_of((2 * i + 1) * blk_rows, blk_rows)
    s_b = jnp.where(causal, _nt_dot(q2[rows_ab:], k_ref[0, pl.ds(off_b, blk_rows), :]), NEG_BIG)
    _online_update(s_b, v_ref[0, pl.ds(off_b, blk_rows), :], m_ref, l_ref, acc_ref, row0=rows_ab)

    o = _normalised(l_ref, acc_ref)
    lo = lane < HEAD_DIM
    o_ref[0, :blk_rows] = jnp.where(lo, o[:blk_rows], o[blk_rows:rows_ab]).astype(o_ref.dtype)
    o_ref[0, blk_rows:] = jnp.where(
        lo, o[rows_ab:rows_ab + blk_rows], o[rows_ab + blk_rows:]).astype(o_ref.dtype)


def _moba_attn(cq, ck, cv, kmean):
    B, S, C = cq.shape
    nb = S // MOBA_BLOCK
    tq = 2 * MOBA_BLOCK
    return pl.pallas_call(
        functools.partial(_moba_kernel, nb=nb),
        grid=(B, C // LANES, S // tq),
        in_specs=[pl.BlockSpec((1, tq, LANES), lambda b, p, i: (b, i, p)),
                  pl.BlockSpec((1, S, LANES), lambda b, p, i: (b, 0, p)),
                  pl.BlockSpec((1, S, LANES), lambda b, p, i: (b, 0, p)),
                  pl.BlockSpec((1, nb, LANES), lambda b, p, i: (b, 0, p))],
        out_specs=pl.BlockSpec((1, tq, LANES), lambda b, p, i: (b, i, p)),
        out_shape=jax.ShapeDtypeStruct(cq.shape, BF16),
        scratch_shapes=[pltpu.VMEM((2 * tq, LANES), F32), pltpu.VMEM((2 * tq, LANES), F32),
                        pltpu.VMEM((2 * tq, LANES), F32)],
        compiler_params=_cparams(("parallel", "parallel", "arbitrary")),
        name="moba_attn",
    )(cq, ck, cv, kmean)


def _sigmoid(x):
    return 1.0 / (1.0 + jnp.exp(-x))


def _merge_kernel(ya_ref, yb_ref, yc_ref, gates_ref, x_ref, wb_ref, wo_ref, g_ref, o_ref):
    d = x_ref.shape[2]
    merged = None
    for n, y_ref in enumerate((ya_ref, yb_ref, yc_ref)):
        br = jnp.dot(y_ref[0], wb_ref[n], preferred_element_type=F32)
        term = _sigmoid(gates_ref[0, :, n * d:(n + 1) * d]) * br
        merged = term if merged is None else merged + term
    m = jnp.dot(merged.astype(BF16), wo_ref[...], preferred_element_type=F32)
    o_ref[0] = x_ref[0] + _rms(m, g_ref[...])


def _merge(ya, yb, yc, gates, x, wb, wo, g, tm):
    B, S, D = x.shape
    row = lambda width: pl.BlockSpec((1, tm, width), lambda b, m: (b, m, 0))
    return pl.pallas_call(
        _merge_kernel,
        grid=(B, S // tm),
        in_specs=[row(BRANCH_WIDTH), row(BRANCH_WIDTH), row(BRANCH_WIDTH), row(N_BRANCH * D), row(D),
                  _resident(wb.shape, lambda b, m: (0, 0, 0)),
                  _resident(wo.shape, lambda b, m: (0, 0)),
                  _resident((1, D), lambda b, m: (0, 0))],
        out_specs=row(D),
        out_shape=jax.ShapeDtypeStruct(x.shape, F32),
        compiler_params=_cparams(("parallel", "parallel")),
        name="merge_out",
    )(ya, yb, yc, gates, x, wb, wo, g.reshape(1, D))


def _ffn_kernel(x_ref, g_in_ref, wi_ref, wo_ref, g_out_ref, o_ref, acc_ref, *, d_ff, tf):
    x = x_ref[0]
    h = _rms(x, g_in_ref[...]).astype(BF16)
    acc_ref[...] = jnp.zeros(acc_ref.shape, F32)

    def body(c, carry):
        off = pl.multiple_of(c * tf, tf)
        gt = jnp.dot(h, wi_ref[:, pl.ds(off, tf)], preferred_element_type=F32)
        up_off = pl.multiple_of(d_ff + off, LANES)
        up = jnp.dot(h, wi_ref[:, pl.ds(up_off, tf)], preferred_element_type=F32)
        act = (gt * _sigmoid(gt) * up).astype(BF16)
        acc_ref[...] += jnp.dot(act, wo_ref[pl.ds(off, tf), :], preferred_element_type=F32)
        return carry

    lax.fori_loop(0, d_ff // tf, body, 0)
    o_ref[0] = x + _rms(acc_ref[...], g_out_ref[...])


def _ffn(x, g_in, wi, wo, g_out, tm, tf):
    B, S, D = x.shape
    d_ff = wo.shape[0]
    row = pl.BlockSpec((1, tm, D), lambda b, m: (b, m, 0))
    vec = _resident((1, D), lambda b, m: (0, 0))
    return pl.pallas_call(
        functools.partial(_ffn_kernel, d_ff=d_ff, tf=tf),
        grid=(B, S // tm),
        in_specs=[row, vec, _resident(wi.shape, lambda b, m: (0, 0)),
                  _resident(wo.shape, lambda b, m: (0, 0)), vec],
        out_specs=row,
        out_shape=jax.ShapeDtypeStruct(x.shape, F32),
        scratch_shapes=[pltpu.VMEM((tm, D), F32)],
        compiler_params=_cparams(("parallel", "parallel")),
        name="swiglu_ffn",
    )(x, g_in.reshape(1, D), wi, wo, g_out.reshape(1, D))


def _pack_w_in(w):
    d = w.shape[0]
    offs = [0]
    for width in (512, 512, 512, 512, 64, 64, 128, 32, 4, 512, 512, 512, 3072):
        offs.append(offs[-1] + width)
    seg = lambda i: w[:, offs[i]:offs[i + 1]]
    scale = HEAD_DIM ** -0.5 * LOG2E
    cols = [seg(0) * scale, seg(1), seg(2), seg(3) * scale,
            seg(4), seg(4), seg(5), seg(5), seg(6),
            seg(7), seg(7), seg(7), seg(7),
            seg(8), jnp.zeros((d, LANES - IDX_HEADS), w.dtype),
            seg(9) * scale, seg(10), seg(11), seg(12)]
    packed = jnp.concatenate(cols, axis=1).astype(BF16)
    assert packed.shape[1] == PACKED_COLS
    return packed


def kernel(x, positions, w_in, w_branch, w_out, lambda_q1, lambda_k1, lambda_q2, lambda_k2,
           subln_g, norm_g, w_ffn_in, w_ffn_out):
    B, S, D = x.shape
    depth = w_in.shape[0]
    tm = min(512, S)
    tab = _rope_table(positions, tm)
    for l in range(depth):
        lam_init = 0.8 - 0.6 * math.exp(-0.3 * l)
        (aq, ak, av, bq, k2, v2, iq, ik4, iw, cq, ck, cv, gates, kmean) = _inproj(
            x, norm_g[l, 0], tab, _pack_w_in(w_in[l]), tm)
        kmean = kmean.reshape(B, S // MOBA_BLOCK, 512)
        lam_rows = jnp.stack([lambda_q1[l], lambda_k1[l], lambda_q2[l], lambda_k2[l]])
        ya = _diff_attn(aq, ak, av, lam_rows, subln_g[l], lam_init, t=min(1024, S))
        yb = _dsa_attn(bq, iq, iw, ik4, k2, v2, tq=128, tk=min(512, S))
        yc = _moba_attn(cq, ck, cv, kmean)
        x = _merge(ya, yb, yc, gates, x, w_branch[l].astype(BF16), w_out[l].astype(BF16),
                   norm_g[l, 1], tm)
        x = _ffn(x, norm_g[l, 2], w_ffn_in[l].astype(BF16), w_ffn_out[l].astype(BF16),
                 norm_g[l, 3], tm, tf=256)
    return x
```

```python
import functools
import math

import jax
import jax.numpy as jnp
from jax import lax
from jax.experimental import pallas as pl
from jax.experimental.pallas import tpu as pltpu

F32 = jnp.float32
BF16 = jnp.bfloat16

LANES = 128
HEAD_DIM = 64
ROPE_THETA = 500000.0
NORM_EPS = 1e-6
IDX_HEADS = 4
IDX_DIM = 32
IDX_TOPK_MAX = 256
MOBA_BLOCK = 256
MOBA_TOPK = 3
N_BRANCH = 3
BRANCH_WIDTH = 512
NEG_BIG = -1e30
LOG2E = math.log2(math.e)
ROW_BLOCK = 64
INT_MIN = -2 ** 31
VMEM_LIMIT = 56 * 1024 * 1024

OFF_AQ, OFF_AK, OFF_AV, OFF_BQ = 0, 512, 1024, 1536
OFF_K2, OFF_V2, OFF_IQ, OFF_IK4, OFF_IW = 2048, 2176, 2304, 2432, 2560
OFF_CQ, OFF_CK, OFF_CV, OFF_G = 2688, 3200, 3712, 4224
PACKED_COLS = OFF_G + 3 * 1024


def _nt_dot(a, b):
    return lax.dot_general(a, b, (((1,), (1,)), ((), ())), preferred_element_type=F32)


def _rms(x, g):
    return x * lax.rsqrt(jnp.mean(x * x, axis=-1, keepdims=True) + NORM_EPS) * g


def _cparams(sem):
    return pltpu.CompilerParams(dimension_semantics=sem, vmem_limit_bytes=VMEM_LIMIT)


def _resident(shape, index_map):
    return pl.BlockSpec(shape, index_map, pipeline_mode=pl.Buffered(1))


def _rope_table_kernel(pos_ref, tab_ref):
    pos = pos_ref[0].astype(F32)
    lane = lax.broadcasted_iota(jnp.int32, (1, LANES), 1)

    def tables(head_dim):
        rot = head_dim // 4
        half = rot // 2
        d = lane & (head_dim - 1)
        fi = d & (half - 1)
        inv = jnp.zeros((1, LANES), F32)
        for i in range(half):
            inv = jnp.where(fi == i, F32(ROPE_THETA ** (-(2.0 * i) / rot)), inv)
        ang = pos * inv
        cos = jnp.cos(ang)
        sin = jnp.sin(ang)
        c = jnp.where(d < rot, cos, 1.0)
        s = jnp.where(d < half, -sin, jnp.where(d < rot, sin, 0.0))
        return c, s

    c64, s64 = tables(HEAD_DIM)
    c32, s32 = tables(IDX_DIM)
    tab_ref[0, :, 0 * LANES:1 * LANES] = c64
    tab_ref[0, :, 1 * LANES:2 * LANES] = s64
    tab_ref[0, :, 2 * LANES:3 * LANES] = c32
    tab_ref[0, :, 3 * LANES:4 * LANES] = s32


def _rope_table(positions, tm):
    B, S = positions.shape
    return pl.pallas_call(
        _rope_table_kernel,
        grid=(B, S // tm),
        in_specs=[pl.BlockSpec((1, tm, 1), lambda b, m: (b, m, 0))],
        out_specs=pl.BlockSpec((1, tm, 4 * LANES), lambda b, m: (b, m, 0)),
        out_shape=jax.ShapeDtypeStruct((B, S, 4 * LANES), F32),
        compiler_params=_cparams(("parallel", "parallel")),
        name="rope_table",
    )(positions.reshape(B, S, 1))


def _inproj_kernel(x_ref, g_ref, tab_ref, w_ref,
                   aq_ref, ak_ref, av_ref, bq_ref, k2_ref, v2_ref, iq_ref, ik4_ref, iw_ref,
                   cq_ref, ck_ref, cv_ref, gates_ref, kmean_ref):
    h = _rms(x_ref[0], g_ref[...]).astype(BF16)
    lane = lax.broadcasted_iota(jnp.int32, (1, LANES), 1)

    def proj(off, width):
        return jnp.dot(h, w_ref[:, off:off + width], preferred_element_type=F32)

    def rope_block(y, head_dim, c, s):
        half = head_dim // 8
        up = pltpu.roll(y, LANES - half, 1)
        dn = pltpu.roll(y, half, 1)
        return y * c + jnp.where((lane & (head_dim - 1)) < half, up, dn) * s

    def emit(out_ref, off, width, head_dim=None, dtype=BF16):
        y = proj(off, width)
        if head_dim is None:
            out_ref[0] = y.astype(dtype)
            return None
        t0 = 0 if head_dim == HEAD_DIM else 2 * LANES
        c = tab_ref[0, :, t0:t0 + LANES]
        s = tab_ref[0, :, t0 + LANES:t0 + 2 * LANES]
        blocks = []
        for j in range(width // LANES):
            r = rope_block(y[:, j * LANES:(j + 1) * LANES], head_dim, c, s)
            out_ref[0, :, j * LANES:(j + 1) * LANES] = r.astype(dtype)
            blocks.append(r)
        return blocks

    emit(aq_ref, OFF_AQ, 512, HEAD_DIM)
    emit(ak_ref, OFF_AK, 512, HEAD_DIM)
    emit(av_ref, OFF_AV, 512)
    emit(bq_ref, OFF_BQ, 512, HEAD_DIM)
    emit(k2_ref, OFF_K2, LANES, HEAD_DIM)
    emit(v2_ref, OFF_V2, LANES)
    emit(iq_ref, OFF_IQ, LANES, IDX_DIM)
    emit(ik4_ref, OFF_IK4, LANES, IDX_DIM)
    emit(iw_ref, OFF_IW, LANES, dtype=F32)
    emit(cq_ref, OFF_CQ, 512, HEAD_DIM)
    ck_blocks = emit(ck_ref, OFF_CK, 512, HEAD_DIM)
    emit(cv_ref, OFF_CV, 512)
    emit(gates_ref, OFF_G, 3 * 1024, dtype=F32)

    tm = x_ref.shape[1]
    for blk in range(tm // MOBA_BLOCK):
        for j, r in enumerate(ck_blocks):
            part = r[blk * MOBA_BLOCK:(blk + 1) * MOBA_BLOCK]
            kmean_ref[0, 0, blk:blk + 1, j * LANES:(j + 1) * LANES] = (
                jnp.mean(part, axis=0, keepdims=True))


def _inproj(x, g, tab, w_packed, tm):
    B, S, D = x.shape
    nm = S // tm
    row = lambda width: pl.BlockSpec((1, tm, width), lambda b, m: (b, m, 0))
    shp = lambda width, dt: jax.ShapeDtypeStruct((B, S, width), dt)
    widths = [(512, BF16)] * 4 + [(LANES, BF16)] * 4 + [(LANES, F32)] + [(512, BF16)] * 3 + [(3072, F32)]
    out_specs = [row(w) for w, _ in widths]
    out_shape = [shp(w, dt) for w, dt in widths]
    out_specs.append(pl.BlockSpec((1, 1, tm // MOBA_BLOCK, 512), lambda b, m: (b, m, 0, 0)))
    out_shape.append(jax.ShapeDtypeStruct((B, nm, tm // MOBA_BLOCK, 512), F32))
    return pl.pallas_call(
        _inproj_kernel,
        grid=(B, nm),
        in_specs=[row(D),
                  _resident((1, D), lambda b, m: (0, 0)),
                  row(4 * LANES),
                  _resident((D, PACKED_COLS), lambda b, m: (0, 0))],
        out_specs=out_specs,
        out_shape=out_shape,
        compiler_params=_cparams(("parallel", "parallel")),
        name="inproj",
    )(x, g.reshape(1, D), tab, w_packed)


def _split_heads_rows(q):
    lo = lax.broadcasted_iota(jnp.int32, (1, LANES), 1) < HEAD_DIM
    zero = jnp.zeros_like(q)
    return jnp.concatenate([jnp.where(lo, q, zero), jnp.where(lo, zero, q)], axis=0)


def _lane_blocks(s):
    return [s[:, j * LANES:(j + 1) * LANES] for j in range(s.shape[1] // LANES)]


def _online_update(s, v, m_ref, l_ref, acc_ref, bias=None, row0=0):
    rows_total = s.shape[0]
    nblk = s.shape[1] // LANES
    p_rows, alphas = [], []
    for r0 in range(0, rows_total, ROW_BLOCK):
        rows = slice(row0 + r0, row0 + r0 + ROW_BLOCK)
        cols = [s[r0:r0 + ROW_BLOCK, j * LANES:(j + 1) * LANES] for j in range(nblk)]
        if bias is not None:
            b0 = r0 % bias[0].shape[0]
            cols = [c + bj[b0:b0 + ROW_BLOCK] for c, bj in zip(cols, bias)]
        m_prev = m_ref[rows]
        m_new = jnp.maximum(
            m_prev, jnp.max(functools.reduce(jnp.maximum, cols), axis=1, keepdims=True))
        alpha = jnp.exp2(m_prev - m_new)
        ps = [jnp.exp2(c - m_new) for c in cols]
        l_ref[rows] = alpha * l_ref[rows] + functools.reduce(jnp.add, ps)
        m_ref[rows] = m_new
        p_rows.append(jnp.concatenate([pj.astype(BF16) for pj in ps], axis=1))
        alphas.append(alpha)
    p = jnp.concatenate(p_rows, axis=0)
    alpha = jnp.concatenate(alphas, axis=0)
    span = slice(row0, row0 + rows_total)
    acc_ref[span] = alpha * acc_ref[span] + jnp.dot(p, v, preferred_element_type=F32)


def _normalised(l_ref, acc_ref):
    return acc_ref[...] / jnp.sum(l_ref[...], axis=1, keepdims=True)


def _diff_attn_kernel(q_ref, k_ref, v_ref, lam_ref, g_ref, o_ref, m_ref, l_ref, acc_ref,
                      *, t, lam_init):
    qi = pl.program_id(2)
    half = t // 2
    q = q_ref[0]
    q2 = jnp.concatenate([_split_heads_rows(q[:half]), _split_heads_rows(q[half:])], axis=0)
    m_ref[...] = jnp.full(m_ref.shape, NEG_BIG, F32)
    l_ref[...] = jnp.zeros(l_ref.shape, F32)
    acc_ref[...] = jnp.zeros(acc_ref.shape, F32)

    def body(c, carry):
        off = pl.multiple_of(c * t, t)
        _online_update(_nt_dot(q2, k_ref[0, pl.ds(off, t), :]), v_ref[0, pl.ds(off, t), :],
                       m_ref, l_ref, acc_ref)
        return carry

    lax.fori_loop(0, qi, body, 0)

    r = lax.broadcasted_iota(jnp.int32, (t, 1), 0)
    causal = lax.broadcasted_iota(jnp.int32, (1, half), 1) <= jnp.where(r >= half, r - half, r)
    off_a = pl.multiple_of(qi * t, t)
    s = _nt_dot(q2, k_ref[0, pl.ds(off_a, half), :])
    s = jnp.concatenate([jnp.where(causal, s[:t], NEG_BIG), s[t:]], axis=0)
    _online_update(s, v_ref[0, pl.ds(off_a, half), :], m_ref, l_ref, acc_ref)
    off_b = pl.multiple_of(qi * t + half, half)
    s = jnp.where(causal, _nt_dot(q2[t:], k_ref[0, pl.ds(off_b, half), :]), NEG_BIG)
    _online_update(s, v_ref[0, pl.ds(off_b, half), :], m_ref, l_ref, acc_ref, row0=t)

    lam_rows = lam_ref[...]
    e1 = jnp.exp(jnp.sum(lam_rows[0:1] * lam_rows[1:2], axis=1, keepdims=True))
    e2 = jnp.exp(jnp.sum(lam_rows[2:3] * lam_rows[3:4], axis=1, keepdims=True))
    lam = e1 - e2 + lam_init
    o = _normalised(l_ref, acc_ref)
    o1 = jnp.concatenate([o[:half], o[t:t + half]], axis=0)
    o2 = jnp.concatenate([o[half:t], o[t + half:]], axis=0)
    o_ref[0] = (_rms(o1 - lam * o2, g_ref[...]) * (1.0 - lam_init)).astype(o_ref.dtype)


def _diff_attn(aq, ak, av, lam_rows, subln_g, lam_init, t):
    B, S, _ = aq.shape
    nh = aq.shape[2] // LANES
    kern = functools.partial(_diff_attn_kernel, t=t, lam_init=lam_init)
    return pl.pallas_call(
        kern,
        grid=(B, nh, S // t),
        in_specs=[pl.BlockSpec((1, t, LANES), lambda b, h, i: (b, i, h)),
                  pl.BlockSpec((1, S, LANES), lambda b, h, i: (b, 0, h)),
                  pl.BlockSpec((1, S, LANES), lambda b, h, i: (b, 0, h)),
                  pl.BlockSpec((4, HEAD_DIM), lambda b, h, i: (0, 0)),
                  pl.BlockSpec((1, LANES), lambda b, h, i: (0, 0))],
        out_specs=pl.BlockSpec((1, t, LANES), lambda b, h, i: (b, i, h)),
        out_shape=jax.ShapeDtypeStruct(aq.shape, BF16),
        scratch_shapes=[pltpu.VMEM((2 * t, LANES), F32), pltpu.VMEM((2 * t, LANES), F32),
                        pltpu.VMEM((2 * t, LANES), F32)],
        compiler_params=_cparams(("parallel", "parallel", "arbitrary")),
        name="diff_attn",
    )(aq, ak, av, lam_rows, subln_g.reshape(1, LANES))


BIT_GROUP = 256


def _bit_transpose32(words):
    w = list(words)
    for j, m in ((16, 0x0000FFFF), (8, 0x00FF00FF), (4, 0x0F0F0F0F), (2, 0x33333333), (1, 0x55555555)):
        for k in range(32):
            if k & j == 0:
                t = (w[k] ^ (w[k + j] >> j)) & m
                w[k] = w[k] ^ t
                w[k + j] = w[k + j] ^ (t << j)
    return w

def _dsa_kernel(q_ref, iq_ref, iw_ref, ik4_ref, k2_ref, v2_ref, o_ref,
                keys_ref, planes_ref, act_ref, jcut_ref, m_ref, l_ref, acc_ref,
                *, tq, tk, topk, idx_bits):
    qi = pl.program_id(1)
    nkv = (qi * tq + tq - 1) // tk + 1
    lane = lax.broadcasted_iota(jnp.int32, (1, LANES), 1)
    qpos = qi * tq + lax.broadcasted_iota(jnp.int32, (tq, 1), 0)
    lane_tk = lax.broadcasted_iota(jnp.int32, (1, tk), 1)
    nsub = tk // LANES
    seq = keys_ref.shape[1]

    @pl.when(qi == 0)
    def _():
        planes_ref[...] = jnp.zeros(planes_ref.shape, jnp.int32)

    iq = iq_ref[0]
    zero = jnp.zeros_like(iq)
    iq4 = jnp.concatenate(
        [jnp.where((lane >> (IDX_DIM.bit_length() - 1)) == hh, iq, zero) for hh in range(IDX_HEADS)],
        axis=0)
    iw = iw_ref[0]
    iw_cols = [iw[:, hh:hh + 1] for hh in range(IDX_HEADS)]

    def score_body(c, carry):
        off = pl.multiple_of(c * tk, tk)
        rel = jnp.maximum(_nt_dot(iq4, ik4_ref[0, pl.ds(off, tk), :]), 0.0)
        score = iw_cols[0] * rel[0:tq]
        for hh in range(1, IDX_HEADS):
            score = score + iw_cols[hh] * rel[hh * tq:(hh + 1) * tq]
        bits = lax.bitcast_convert_type(score, jnp.int32)
        key = bits ^ ((bits >> 31) & jnp.int32(0x7FFFFFFF))
        kpos = off + lane_tk
        key = jnp.where(score == 0.0, (seq - 1) - kpos, jnp.where(key > 0, key + seq, key))
        key = jnp.where(kpos <= qpos, key, jnp.int32(INT_MIN))
        keys_ref[:, pl.ds(off, tk)] = key
        kts = [key[:, j * LANES:(j + 1) * LANES].T for j in range(nsub)]
        for gl in range(tk // BIT_GROUP):
            words = [kts[(gl * BIT_GROUP + 8 * g) // LANES][(8 * g) % LANES:(8 * g) % LANES + 8]
                     for g in range(32)]
            words = _bit_transpose32(words)
            words[0] = ~words[0]
            for b in range(32):
                planes_ref[c * (tk // BIT_GROUP) + gl, b] = words[b]
        return carry

    lax.fori_loop(0, nkv, score_body, 0)

    ngroups = planes_ref.shape[0]
    live = nkv * (tk // BIT_GROUP)
    for g in range(ngroups):
        act_ref[g] = jnp.broadcast_to(jnp.where(g < live, jnp.int32(-1), jnp.int32(0)), (8, tq))

    def sweep(i, keep, first=False):
        parts = [jnp.zeros((8, tq), jnp.int32) for _ in range(4)]
        for g in range(ngroups):
            act = act_ref[g]
            if not first:
                act = act & (planes_ref[g, i - 1] ^ keep)
                act_ref[g] = act
            parts[g % 4] = parts[g % 4] + lax.population_count(act & planes_ref[g, i])
        cnt = (parts[0] + parts[1]) + (parts[2] + parts[3])
        return jnp.sum(cnt.astype(F32), axis=0, keepdims=True)

    def decide(i, ones, thr_u, n_gt):
        take = n_gt + ones >= topk
        thr_u = thr_u | jnp.where(take, jnp.left_shift(jnp.int32(1), 31 - i), 0)
        return thr_u, jnp.where(take, n_gt, n_gt + ones), jnp.where(take, 0, -1).astype(jnp.int32)

    def pass_body(i, carry):
        thr_u, n_gt, keep = carry
        return decide(i, sweep(i, keep), thr_u, n_gt)

    start = decide(0, sweep(0, None, first=True),
                   jnp.zeros((1, tq), jnp.int32), jnp.zeros((1, tq), F32))
    thr_u, n_gt, keep = lax.fori_loop(1, 32, pass_body, start)
    n_eq = jnp.zeros((8, tq), jnp.int32)
    for g in range(ngroups):
        n_eq = n_eq + lax.population_count(act_ref[g] & (planes_ref[g, 31] ^ keep))
    thr = thr_u ^ jnp.int32(INT_MIN)
    cnt_ge = n_gt + jnp.sum(n_eq.astype(F32), axis=0, keepdims=True)
    need = topk - n_gt

    def as_rows(x):
        return jnp.broadcast_to(x, (LANES, tq)).T

    thr_r = as_rows(thr)
    need_r = as_rows(need)
    jcut_ref[...] = jnp.full((tq, LANES), 2 ** 30, jnp.int32)

    @pl.when(jnp.max(cnt_ge) > topk)
    def _():
        def ties_before(cand):
            def body(c, cnt):
                off = pl.multiple_of(c * tk, tk)
                for j in range(nsub):
                    keyj = keys_ref[:, pl.ds(pl.multiple_of(off + j * LANES, LANES), LANES)]
                    hit = (keyj == thr_r) & (off + j * LANES + lane < cand)
                    cnt = jnp.where(hit, cnt + 1, cnt)
                return cnt
            cnt = lax.fori_loop(0, nkv, body, jnp.zeros((tq, LANES), jnp.int32))
            return jnp.sum(cnt.astype(F32), axis=1, keepdims=True)

        def idx_body(i, j):
            cand = j + jnp.left_shift(jnp.int32(1), idx_bits - 1 - i)
            return jnp.where(ties_before(cand) < need_r, cand, j)
        jcut_ref[...] = lax.fori_loop(0, idx_bits, idx_body, jnp.zeros((tq, LANES), jnp.int32))

    jcut_r = jcut_ref[...]

    q = q_ref[0]
    q8 = jnp.concatenate(
        [_split_heads_rows(q[:, p * LANES:(p + 1) * LANES]) for p in range(4)], axis=0)
    m_ref[...] = jnp.full(m_ref.shape, NEG_BIG, F32)
    l_ref[...] = jnp.zeros(l_ref.shape, F32)
    acc_ref[...] = jnp.zeros(acc_ref.shape, F32)

    def attn_body(c, carry):
        off = pl.multiple_of(c * tk, tk)
        s = _nt_dot(q8, k2_ref[0, pl.ds(off, tk), :])
        bias = []
        for j in range(nsub):
            keyj = keys_ref[:, pl.ds(pl.multiple_of(off + j * LANES, LANES), LANES)]
            kpos = off + j * LANES + lane
            sel = ((keyj > thr_r) | ((keyj == thr_r) & (kpos <= jcut_r))) & (kpos <= qpos)
            bias.append(jnp.where(sel, 0.0, NEG_BIG))
        _online_update(s, v2_ref[0, pl.ds(off, tk), :], m_ref, l_ref, acc_ref, bias=bias)
        return carry

    lax.fori_loop(0, nkv, attn_body, 0)

    o = _normalised(l_ref, acc_ref)
    lo = lane < HEAD_DIM
    for p in range(4):
        o_ref[0, :, p * LANES:(p + 1) * LANES] = jnp.where(
            lo, o[(2 * p) * tq:(2 * p + 1) * tq], o[(2 * p + 1) * tq:(2 * p + 2) * tq]
        ).astype(o_ref.dtype)


def _dsa_attn(bq, iq, iw, ik4, k2, v2, tq, tk):
    B, S, C = bq.shape
    topk = min(IDX_TOPK_MAX, S // 4)
    idx_bits = max(1, (S - 1).bit_length())
    kern = functools.partial(_dsa_kernel, tq=tq, tk=tk, topk=topk, idx_bits=idx_bits)
    row = lambda width: pl.BlockSpec((1, tq, width), lambda b, i: (b, i, 0))
    full = _resident((1, S, LANES), lambda b, i: (b, 0, 0))
    return pl.pallas_call(
        kern,
        grid=(B, S // tq),
        in_specs=[row(C), row(LANES), row(LANES), full, full, full],
        out_specs=row(C),
        out_shape=jax.ShapeDtypeStruct(bq.shape, BF16),
        scratch_shapes=[pltpu.VMEM((tq, S), jnp.int32),
                        pltpu.VMEM((S // BIT_GROUP, 32, 8, tq), jnp.int32),
                        pltpu.VMEM((S // BIT_GROUP, 8, tq), jnp.int32),
                        pltpu.VMEM((tq, LANES), jnp.int32),
                        pltpu.VMEM((8 * tq, LANES), F32), pltpu.VMEM((8 * tq, LANES), F32),
                        pltpu.VMEM((8 * tq, LANES), F32)],
        compiler_params=_cparams(("parallel", "arbitrary")),
        name="dsa_attn",
    )(bq, iq, iw, ik4, k2, v2)


def _moba_kernel(q_ref, k_ref, v_ref, km_ref, o_ref, m_ref, l_ref, acc_ref, *, nb):
    blk_rows = MOBA_BLOCK
    rows_ab = 2 * blk_rows
    i = pl.program_id(2)
    lane = lax.broadcasted_iota(jnp.int32, (1, LANES), 1)
    q = q_ref[0]
    q2 = jnp.concatenate([_split_heads_rows(q[:blk_rows]), _split_heads_rows(q[blk_rows:])], axis=0)

    km = jnp.concatenate([km_ref[0], jnp.zeros((LANES - nb, LANES), F32)], axis=0)
    km_hi = km.astype(BF16)
    r1 = km - km_hi.astype(F32)
    km_mid = r1.astype(BF16)
    km_lo = (r1 - km_mid.astype(F32)).astype(BF16)
    nb8 = -(-nb // 8) * 8
    gate_t = (_nt_dot(km_hi, q2) + _nt_dot(km_mid, q2) + _nt_dot(km_lo, q2))[:nb8]
    blk = lax.broadcasted_iota(jnp.int32, (nb8, 1), 0)
    blk_f = blk.astype(F32)
    col = lax.broadcasted_iota(jnp.int32, (1, 2 * rows_ab), 1)
    past = blk < 2 * i + jnp.where(col >= rows_ab, 1, 0)
    g = jnp.where(past, gate_t, -jnp.inf)
    sel = jnp.zeros(g.shape, jnp.bool_)
    for _ in range(min(MOBA_TOPK, nb - 1)):
        mx = jnp.max(g, axis=0, keepdims=True)
        first = jnp.min(jnp.where(g == mx, blk_f, float(LANES)), axis=0, keepdims=True)
        pick = blk_f == first
        sel = sel | pick
        g = jnp.where(pick, -jnp.inf, g)
    bias_t = jnp.concatenate([jnp.where(sel & past, 0.0, NEG_BIG),
                              jnp.full((LANES - nb8, 2 * rows_ab), NEG_BIG, F32)], axis=0)
    qa = jnp.concatenate([q2, bias_t.T.astype(BF16)], axis=1)

    m_ref[...] = jnp.full(m_ref.shape, NEG_BIG, F32)
    l_ref[...] = jnp.zeros(l_ref.shape, F32)
    acc_ref[...] = jnp.zeros(acc_ref.shape, F32)

    def biased_logits(lhs, first_blk, nblk):
        n = nblk * blk_rows
        off = pl.multiple_of(first_blk * blk_rows, blk_rows)
        row_blk = lax.broadcasted_iota(jnp.int32, (n, 1), 0) >> (MOBA_BLOCK.bit_length() - 1)
        onehot = jnp.where(lane == first_blk + row_blk, 1.0, 0.0).astype(BF16)
        ka = jnp.concatenate([k_ref[0, pl.ds(off, n), :], onehot], axis=1)
        return _nt_dot(lhs, ka), v_ref[0, pl.ds(off, n), :]

    def body(c, carry):
        s, v = biased_logits(qa, 4 * c, 4)
        _online_update(s, v, m_ref, l_ref, acc_ref)
        return carry

    lax.fori_loop(0, i // 2, body, 0)

    @pl.when(i % 2 == 1)
    def _():
        s, v = biased_logits(qa, 2 * i - 2, 2)
        _online_update(s, v, m_ref, l_ref, acc_ref)

    r = lax.broadcasted_iota(jnp.int32, (rows_ab, 1), 0)
    causal = (lax.broadcasted_iota(jnp.int32, (1, blk_rows), 1)
              <= jnp.where(r >= blk_rows, r - blk_rows, r))
    off_a = pl.multiple_of(2 * i * blk_rows, blk_rows)
    s_a = jnp.where(causal, _nt_dot(q2[:rows_ab], k_ref[0, pl.ds(off_a, blk_rows), :]), NEG_BIG)
    s_b, v = biased_logits(qa[rows_ab:], 2 * i, 1)
    _online_update(jnp.concatenate([s_a, s_b], axis=0), v, m_ref, l_ref, acc_ref)
    off_b = pl.multiple_of((2 * i + 1) * blk_rows, blk_rows)
    s_b = jnp.where(causal, _nt_dot(q2[rows_ab:], k_ref[0, pl.ds(off_b, blk_rows), :]), NEG_BIG)
    _online_update(s_b, v_ref[0, pl.ds(off_b, blk_rows), :], m_ref, l_ref, acc_ref, row0=rows_ab)

    o = _normalised(l_ref, acc_ref)
    lo = lane < HEAD_DIM
    o_ref[0, :blk_rows] = jnp.where(lo, o[:blk_rows], o[blk_rows:rows_ab]).astype(o_ref.dtype)
    o_ref[0, blk_rows:] = jnp.where(
        lo, o[rows_ab:rows_ab + blk_rows], o[rows_ab + blk_rows:]).astype(o_ref.dtype)


def _moba_attn(cq, ck, cv, kmean):
    B, S, C = cq.shape
    nb = S // MOBA_BLOCK
    tq = 2 * MOBA_BLOCK
    return pl.pallas_call(
        functools.partial(_moba_kernel, nb=nb),
        grid=(B, C // LANES, S // tq),
        in_specs=[pl.BlockSpec((1, tq, LANES), lambda b, p, i: (b, i, p)),
                  pl.BlockSpec((1, S, LANES), lambda b, p, i: (b, 0, p)),
                  pl.BlockSpec((1, S, LANES), lambda b, p, i: (b, 0, p)),
                  pl.BlockSpec((1, nb, LANES), lambda b, p, i: (b, 0, p))],
        out_specs=pl.BlockSpec((1, tq, LANES), lambda b, p, i: (b, i, p)),
        out_shape=jax.ShapeDtypeStruct(cq.shape, BF16),
        scratch_shapes=[pltpu.VMEM((2 * tq, LANES), F32), pltpu.VMEM((2 * tq, LANES), F32),
                        pltpu.VMEM((2 * tq, LANES), F32)],
        compiler_params=_cparams(("parallel", "parallel", "arbitrary")),
        name="moba_attn",
    )(cq, ck, cv, kmean)


def _sigmoid(x):
    return 1.0 / (1.0 + jnp.exp(-x))


def _merge_kernel(ya_ref, yb_ref, yc_ref, gates_ref, x_ref, wb_ref, wo_ref, g_ref, o_ref):
    d = x_ref.shape[2]
    merged = None
    for n, y_ref in enumerate((ya_ref, yb_ref, yc_ref)):
        br = jnp.dot(y_ref[0], wb_ref[n], preferred_element_type=F32)
        term = _sigmoid(gates_ref[0, :, n * d:(n + 1) * d]) * br
        merged = term if merged is None else merged + term
    m = jnp.dot(merged.astype(BF16), wo_ref[...], preferred_element_type=F32)
    o_ref[0] = x_ref[0] + _rms(m, g_ref[...])


def _merge(ya, yb, yc, gates, x, wb, wo, g, tm):
    B, S, D = x.shape
    row = lambda width: pl.BlockSpec((1, tm, width), lambda b, m: (b, m, 0))
    return pl.pallas_call(
        _merge_kernel,
        grid=(B, S // tm),
        in_specs=[row(BRANCH_WIDTH), row(BRANCH_WIDTH), row(BRANCH_WIDTH), row(N_BRANCH * D), row(D),
                  _resident(wb.shape, lambda b, m: (0, 0, 0)),
                  _resident(wo.shape, lambda b, m: (0, 0)),
                  _resident((1, D), lambda b, m: (0, 0))],
        out_specs=row(D),
        out_shape=jax.ShapeDtypeStruct(x.shape, F32),
        compiler_params=_cparams(("parallel", "parallel")),
        name="merge_out",
    )(ya, yb, yc, gates, x, wb, wo, g.reshape(1, D))


def _ffn_kernel(x_ref, g_in_ref, wi_ref, wo_ref, g_out_ref, o_ref, acc_ref, *, d_ff, tf):
    x = x_ref[0]
    h = _rms(x, g_in_ref[...]).astype(BF16)
    acc_ref[...] = jnp.zeros(acc_ref.shape, F32)

    def body(c, carry):
        off = pl.multiple_of(c * tf, tf)
        gt = jnp.dot(h, wi_ref[:, pl.ds(off, tf)], preferred_element_type=F32)
        up_off = pl.multiple_of(d_ff + off, LANES)
        up = jnp.dot(h, wi_ref[:, pl.ds(up_off, tf)], preferred_element_type=F32)
        act = (gt * _sigmoid(gt) * up).astype(BF16)
        acc_ref[...] += jnp.dot(act, wo_ref[pl.ds(off, tf), :], preferred_element_type=F32)
        return carry

    lax.fori_loop(0, d_ff // tf, body, 0)
    o_ref[0] = x + _rms(acc_ref[...], g_out_ref[...])


def _ffn(x, g_in, wi, wo, g_out, tm, tf):
    B, S, D = x.shape
    d_ff = wo.shape[0]
    row = pl.BlockSpec((1, tm, D), lambda b, m: (b, m, 0))
    vec = _resident((1, D), lambda b, m: (0, 0))
    return pl.pallas_call(
        functools.partial(_ffn_kernel, d_ff=d_ff, tf=tf),
        grid=(B, S // tm),
        in_specs=[row, vec, _resident(wi.shape, lambda b, m: (0, 0)),
                  _resident(wo.shape, lambda b, m: (0, 0)), vec],
        out_specs=row,
        out_shape=jax.ShapeDtypeStruct(x.shape, F32),
        scratch_shapes=[pltpu.VMEM((tm, D), F32)],
        compiler_params=_cparams(("parallel", "parallel")),
        name="swiglu_ffn",
    )(x, g_in.reshape(1, D), wi, wo, g_out.reshape(1, D))


def _pack_w_in(w):
    d = w.shape[0]
    offs = [0]
    for width in (512, 512, 512, 512, 64, 64, 128, 32, 4, 512, 512, 512, 3072):
        offs.append(offs[-1] + width)
    seg = lambda i: w[:, offs[i]:offs[i + 1]]
    scale = HEAD_DIM ** -0.5 * LOG2E
    cols = [seg(0) * scale, seg(1), seg(2), seg(3) * scale,
            seg(4), seg(4), seg(5), seg(5), seg(6),
            seg(7), seg(7), seg(7), seg(7),
            seg(8), jnp.zeros((d, LANES - IDX_HEADS), w.dtype),
            seg(9) * scale, seg(10), seg(11), seg(12)]
    packed = jnp.concatenate(cols, axis=1).astype(BF16)
    assert packed.shape[1] == PACKED_COLS
    return packed


def kernel(x, positions, w_in, w_branch, w_out, lambda_q1, lambda_k1, lambda_q2, lambda_k2,
           subln_g, norm_g, w_ffn_in, w_ffn_out):
    B, S, D = x.shape
    depth = w_in.shape[0]
    tm = min(512, S)
    tab = _rope_table(positions, tm)
    for l in range(depth):
        lam_init = 0.8 - 0.6 * math.exp(-0.3 * l)
        (aq, ak, av, bq, k2, v2, iq, ik4, iw, cq, ck, cv, gates, kmean) = _inproj(
            x, norm_g[l, 0], tab, _pack_w_in(w_in[l]), tm)
        kmean = kmean.reshape(B, S // MOBA_BLOCK, 512)
        lam_rows = jnp.stack([lambda_q1[l], lambda_k1[l], lambda_q2[l], lambda_k2[l]])
        ya = _diff_attn(aq, ak, av, lam_rows, subln_g[l], lam_init, t=min(1024, S))
        yb = _dsa_attn(bq, iq, iw, ik4, k2, v2, tq=256, tk=min(512, S))
        yc = _moba_attn(cq, ck, cv, kmean)
        x = _merge(ya, yb, yc, gates, x, w_branch[l].astype(BF16), w_out[l].astype(BF16),
                   norm_g[l, 1], tm)
        x = _ffn(x, norm_g[l, 2], w_ffn_in[l].astype(BF16), w_ffn_out[l].astype(BF16),
                 norm_g[l, 3], tm, tf=256)
    return x
```

```python
import functools
import math

import jax
import jax.numpy as jnp
from jax import lax
from jax.experimental import pallas as pl
from jax.experimental.pallas import tpu as pltpu

F32 = jnp.float32
BF16 = jnp.bfloat16

LANES = 128
HEAD_DIM = 64
ROPE_THETA = 500000.0
NORM_EPS = 1e-6
IDX_HEADS = 4
IDX_DIM = 32
IDX_TOPK_MAX = 256
MOBA_BLOCK = 256
MOBA_TOPK = 3
N_BRANCH = 3
BRANCH_WIDTH = 512
NEG_BIG = -1e30
LOG2E = math.log2(math.e)
ROW_BLOCK = 64
INT_MIN = -2 ** 31
VMEM_LIMIT = 56 * 1024 * 1024

OFF_AQ, OFF_AK, OFF_AV, OFF_BQ = 0, 512, 1024, 1536
OFF_K2, OFF_V2, OFF_IQ, OFF_IK4, OFF_IW = 2048, 2176, 2304, 2432, 2560
OFF_CQ, OFF_CK, OFF_CV, OFF_G = 2688, 3200, 3712, 4224
PACKED_COLS = OFF_G + 3 * 1024


def _nt_dot(a, b):
    return lax.dot_general(a, b, (((1,), (1,)), ((), ())), preferred_element_type=F32)


def _rms(x, g):
    return x * lax.rsqrt(jnp.mean(x * x, axis=-1, keepdims=True) + NORM_EPS) * g


def _cparams(sem):
    return pltpu.CompilerParams(dimension_semantics=sem, vmem_limit_bytes=VMEM_LIMIT)


def _resident(shape, index_map):
    return pl.BlockSpec(shape, index_map, pipeline_mode=pl.Buffered(1))


def _rope_table_kernel(pos_ref, tab_ref):
    pos = pos_ref[0].astype(F32)
    lane = lax.broadcasted_iota(jnp.int32, (1, LANES), 1)

    def tables(head_dim):
        rot = head_dim // 4
        half = rot // 2
        d = lane & (head_dim - 1)
        fi = d & (half - 1)
        inv = jnp.zeros((1, LANES), F32)
        for i in range(half):
            inv = jnp.where(fi == i, F32(ROPE_THETA ** (-(2.0 * i) / rot)), inv)
        ang = pos * inv
        cos = jnp.cos(ang)
        sin = jnp.sin(ang)
        c = jnp.where(d < rot, cos, 1.0)
        s = jnp.where(d < half, -sin, jnp.where(d < rot, sin, 0.0))
        return c, s

    c64, s64 = tables(HEAD_DIM)
    c32, s32 = tables(IDX_DIM)
    tab_ref[0, :, 0 * LANES:1 * LANES] = c64
    tab_ref[0, :, 1 * LANES:2 * LANES] = s64
    tab_ref[0, :, 2 * LANES:3 * LANES] = c32
    tab_ref[0, :, 3 * LANES:4 * LANES] = s32


def _rope_table(positions, tm):
    B, S = positions.shape
    return pl.pallas_call(
        _rope_table_kernel,
        grid=(B, S // tm),
        in_specs=[pl.BlockSpec((1, tm, 1), lambda b, m: (b, m, 0))],
        out_specs=pl.BlockSpec((1, tm, 4 * LANES), lambda b, m: (b, m, 0)),
        out_shape=jax.ShapeDtypeStruct((B, S, 4 * LANES), F32),
        compiler_params=_cparams(("parallel", "parallel")),
        name="rope_table",
    )(positions.reshape(B, S, 1))


def _inproj_kernel(x_ref, g_ref, tab_ref, w_ref,
                   aq_ref, ak_ref, av_ref, bq_ref, k2_ref, v2_ref, iq_ref, ik4_ref, iw_ref,
                   cq_ref, ck_ref, cv_ref, gates_ref, kmean_ref):
    h = _rms(x_ref[0], g_ref[...]).astype(BF16)
    lane = lax.broadcasted_iota(jnp.int32, (1, LANES), 1)

    def proj(off, width):
        return jnp.dot(h, w_ref[:, off:off + width], preferred_element_type=F32)

    def rope_block(y, head_dim, c, s):
        half = head_dim // 8
        up = pltpu.roll(y, LANES - half, 1)
        dn = pltpu.roll(y, half, 1)
        return y * c + jnp.where((lane & (head_dim - 1)) < half, up, dn) * s

    def emit(out_ref, off, width, head_dim=None, dtype=BF16):
        y = proj(off, width)
        if head_dim is None:
            out_ref[0] = y.astype(dtype)
            return None
        t0 = 0 if head_dim == HEAD_DIM else 2 * LANES
        c = tab_ref[0, :, t0:t0 + LANES]
        s = tab_ref[0, :, t0 + LANES:t0 + 2 * LANES]
        blocks = []
        for j in range(width // LANES):
            r = rope_block(y[:, j * LANES:(j + 1) * LANES], head_dim, c, s)
            out_ref[0, :, j * LANES:(j + 1) * LANES] = r.astype(dtype)
            blocks.append(r)
        return blocks

    emit(aq_ref, OFF_AQ, 512, HEAD_DIM)
    emit(ak_ref, OFF_AK, 512, HEAD_DIM)
    emit(av_ref, OFF_AV, 512)
    emit(bq_ref, OFF_BQ, 512, HEAD_DIM)
    emit(k2_ref, OFF_K2, LANES, HEAD_DIM)
    emit(v2_ref, OFF_V2, LANES)
    emit(iq_ref, OFF_IQ, LANES, IDX_DIM)
    emit(ik4_ref, OFF_IK4, LANES, IDX_DIM)
    emit(iw_ref, OFF_IW, LANES, dtype=F32)
    emit(cq_ref, OFF_CQ, 512, HEAD_DIM)
    ck_blocks = emit(ck_ref, OFF_CK, 512, HEAD_DIM)
    emit(cv_ref, OFF_CV, 512)
    emit(gates_ref, OFF_G, 3 * 1024)

    tm = x_ref.shape[1]
    for blk in range(tm // MOBA_BLOCK):
        for j, r in enumerate(ck_blocks):
            part = r[blk * MOBA_BLOCK:(blk + 1) * MOBA_BLOCK]
            kmean_ref[0, 0, blk:blk + 1, j * LANES:(j + 1) * LANES] = (
                jnp.mean(part, axis=0, keepdims=True))


def _inproj(x, g, tab, w_packed, layer, tm):
    B, S, D = x.shape
    nm = S // tm
    row = lambda width: pl.BlockSpec((1, tm, width), lambda b, m: (b, m, 0))
    shp = lambda width, dt: jax.ShapeDtypeStruct((B, S, width), dt)
    widths = [(512, BF16)] * 4 + [(LANES, BF16)] * 4 + [(LANES, F32)] + [(512, BF16)] * 3 + [(3072, BF16)]
    out_specs = [row(w) for w, _ in widths]
    out_shape = [shp(w, dt) for w, dt in widths]
    out_specs.append(pl.BlockSpec((1, 1, tm // MOBA_BLOCK, 512), lambda b, m: (b, m, 0, 0)))
    out_shape.append(jax.ShapeDtypeStruct((B, nm, tm // MOBA_BLOCK, 512), F32))
    return pl.pallas_call(
        _inproj_kernel,
        grid=(B, nm),
        in_specs=[row(D),
                  _resident((1, D), lambda b, m: (0, 0)),
                  row(4 * LANES),
                  _resident((None, D, PACKED_COLS), lambda b, m: (layer, 0, 0))],
        out_specs=out_specs,
        out_shape=out_shape,
        compiler_params=_cparams(("parallel", "parallel")),
        name="inproj",
    )(x, g.reshape(1, D), tab, w_packed)


def _split_heads_rows(q):
    lo = lax.broadcasted_iota(jnp.int32, (1, LANES), 1) < HEAD_DIM
    zero = jnp.zeros_like(q)
    return jnp.concatenate([jnp.where(lo, q, zero), jnp.where(lo, zero, q)], axis=0)


def _lane_blocks(s):
    return [s[:, j * LANES:(j + 1) * LANES] for j in range(s.shape[1] // LANES)]


def _online_update(s, v, m_ref, l_ref, acc_ref, bias=None, row0=0):
    rows_total = s.shape[0]
    nblk = s.shape[1] // LANES
    p_rows, alphas = [], []
    for r0 in range(0, rows_total, ROW_BLOCK):
        rows = slice(row0 + r0, row0 + r0 + ROW_BLOCK)
        cols = [s[r0:r0 + ROW_BLOCK, j * LANES:(j + 1) * LANES] for j in range(nblk)]
        if bias is not None:
            b0 = r0 % bias[0].shape[0]
            cols = [c + bj[b0:b0 + ROW_BLOCK] for c, bj in zip(cols, bias)]
        m_prev = m_ref[rows]
        m_new = jnp.maximum(
            m_prev, jnp.max(functools.reduce(jnp.maximum, cols), axis=1, keepdims=True))
        alpha = jnp.exp2(m_prev - m_new)
        ps = [jnp.exp2(c - m_new) for c in cols]
        l_ref[rows] = alpha * l_ref[rows] + functools.reduce(jnp.add, ps)
        m_ref[rows] = m_new
        p_rows.append(jnp.concatenate([pj.astype(BF16) for pj in ps], axis=1))
        alphas.append(alpha)
    p = jnp.concatenate(p_rows, axis=0)
    alpha = jnp.concatenate(alphas, axis=0)
    span = slice(row0, row0 + rows_total)
    acc_ref[span] = alpha * acc_ref[span] + jnp.dot(p, v, preferred_element_type=F32)


def _normalised(l_ref, acc_ref):
    return acc_ref[...] / jnp.sum(l_ref[...], axis=1, keepdims=True)


def _diff_attn_kernel(q_ref, k_ref, v_ref, lam_ref, g_ref, o_ref, m_ref, l_ref, acc_ref,
                      *, t, lam_init):
    qi = pl.program_id(2)
    half = t // 2
    q = q_ref[0]
    q2 = jnp.concatenate([_split_heads_rows(q[:half]), _split_heads_rows(q[half:])], axis=0)
    m_ref[...] = jnp.full(m_ref.shape, NEG_BIG, F32)
    l_ref[...] = jnp.zeros(l_ref.shape, F32)
    acc_ref[...] = jnp.zeros(acc_ref.shape, F32)

    def body(c, carry):
        off = pl.multiple_of(c * t, t)
        _online_update(_nt_dot(q2, k_ref[0, pl.ds(off, t), :]), v_ref[0, pl.ds(off, t), :],
                       m_ref, l_ref, acc_ref)
        return carry

    lax.fori_loop(0, qi, body, 0)

    r = lax.broadcasted_iota(jnp.int32, (t, 1), 0)
    causal = lax.broadcasted_iota(jnp.int32, (1, half), 1) <= jnp.where(r >= half, r - half, r)
    off_a = pl.multiple_of(qi * t, t)
    s = _nt_dot(q2, k_ref[0, pl.ds(off_a, half), :])
    s = jnp.concatenate([jnp.where(causal, s[:t], NEG_BIG), s[t:]], axis=0)
    _online_update(s, v_ref[0, pl.ds(off_a, half), :], m_ref, l_ref, acc_ref)
    off_b = pl.multiple_of(qi * t + half, half)
    s = jnp.where(causal, _nt_dot(q2[t:], k_ref[0, pl.ds(off_b, half), :]), NEG_BIG)
    _online_update(s, v_ref[0, pl.ds(off_b, half), :], m_ref, l_ref, acc_ref, row0=t)

    lam_rows = lam_ref[...]
    e1 = jnp.exp(jnp.sum(lam_rows[0:1] * lam_rows[1:2], axis=1, keepdims=True))
    e2 = jnp.exp(jnp.sum(lam_rows[2:3] * lam_rows[3:4], axis=1, keepdims=True))
    lam = e1 - e2 + lam_init
    o = _normalised(l_ref, acc_ref)
    o1 = jnp.concatenate([o[:half], o[t:t + half]], axis=0)
    o2 = jnp.concatenate([o[half:t], o[t + half:]], axis=0)
    o_ref[0] = (_rms(o1 - lam * o2, g_ref[...]) * (1.0 - lam_init)).astype(o_ref.dtype)


def _diff_attn(aq, ak, av, lam_rows, subln_g, lam_init, t):
    B, S, _ = aq.shape
    nh = aq.shape[2] // LANES
    kern = functools.partial(_diff_attn_kernel, t=t, lam_init=lam_init)
    return pl.pallas_call(
        kern,
        grid=(B, nh, S // t),
        in_specs=[pl.BlockSpec((1, t, LANES), lambda b, h, i: (b, i, h)),
                  pl.BlockSpec((1, S, LANES), lambda b, h, i: (b, 0, h)),
                  pl.BlockSpec((1, S, LANES), lambda b, h, i: (b, 0, h)),
                  pl.BlockSpec((4, HEAD_DIM), lambda b, h, i: (0, 0)),
                  pl.BlockSpec((1, LANES), lambda b, h, i: (0, 0))],
        out_specs=pl.BlockSpec((1, t, LANES), lambda b, h, i: (b, i, h)),
        out_shape=jax.ShapeDtypeStruct(aq.shape, BF16),
        scratch_shapes=[pltpu.VMEM((2 * t, LANES), F32), pltpu.VMEM((2 * t, LANES), F32),
                        pltpu.VMEM((2 * t, LANES), F32)],
        compiler_params=_cparams(("parallel", "parallel", "arbitrary")),
        name="diff_attn",
    )(aq, ak, av, lam_rows, subln_g.reshape(1, LANES))


BIT_GROUP = 256


def _bit_transpose32(words):
    w = list(words)
    for j, m in ((16, 0x0000FFFF), (8, 0x00FF00FF), (4, 0x0F0F0F0F), (2, 0x33333333), (1, 0x55555555)):
        for k in range(32):
            if k & j == 0:
                t = (w[k] ^ (w[k + j] >> j)) & m
                w[k] = w[k] ^ t
                w[k + j] = w[k + j] ^ (t << j)
    return w

def _dsa_kernel(q_ref, iq_ref, iw_ref, ik4_ref, k2_ref, v2_ref, o_ref,
                keys_ref, planes_ref, act_ref, jcut_ref, m_ref, l_ref, acc_ref,
                *, tq, tk, topk, idx_bits):
    qi = pl.program_id(1)
    nkv = (qi * tq + tq - 1) // tk + 1
    lane = lax.broadcasted_iota(jnp.int32, (1, LANES), 1)
    qpos = qi * tq + lax.broadcasted_iota(jnp.int32, (tq, 1), 0)
    lane_tk = lax.broadcasted_iota(jnp.int32, (1, tk), 1)
    nsub = tk // LANES
    seq = keys_ref.shape[1]

    @pl.when(qi == 0)
    def _():
        planes_ref[...] = jnp.zeros(planes_ref.shape, jnp.int32)

    iq = iq_ref[0]
    zero = jnp.zeros_like(iq)
    iq4 = jnp.concatenate(
        [jnp.where((lane >> (IDX_DIM.bit_length() - 1)) == hh, iq, zero) for hh in range(IDX_HEADS)],
        axis=0)
    iw = iw_ref[0]
    iw_cols = [iw[:, hh:hh + 1] for hh in range(IDX_HEADS)]

    def score_body(c, carry):
        off = pl.multiple_of(c * tk, tk)
        rel = jnp.maximum(_nt_dot(iq4, ik4_ref[0, pl.ds(off, tk), :]), 0.0)
        score = iw_cols[0] * rel[0:tq]
        for hh in range(1, IDX_HEADS):
            score = score + iw_cols[hh] * rel[hh * tq:(hh + 1) * tq]
        bits = lax.bitcast_convert_type(score, jnp.int32)
        key = bits ^ ((bits >> 31) & jnp.int32(0x7FFFFFFF))
        kpos = off + lane_tk
        key = jnp.where(score == 0.0, (seq - 1) - kpos, jnp.where(key > 0, key + seq, key))
        key = jnp.where(kpos <= qpos, key, jnp.int32(INT_MIN))
        keys_ref[:, pl.ds(off, tk)] = key
        kts = [key[:, j * LANES:(j + 1) * LANES].T for j in range(nsub)]
        for gl in range(tk // BIT_GROUP):
            words = [kts[(gl * BIT_GROUP + 8 * g) // LANES][(8 * g) % LANES:(8 * g) % LANES + 8]
                     for g in range(32)]
            words = _bit_transpose32(words)
            words[0] = ~words[0]
            for b in range(32):
                planes_ref[c * (tk // BIT_GROUP) + gl, b] = words[b]
        return carry

    lax.fori_loop(0, nkv, score_body, 0)

    ngroups = planes_ref.shape[0]
    live = nkv * (tk // BIT_GROUP)
    for g in range(ngroups):
        act_ref[g] = jnp.broadcast_to(jnp.where(g < live, jnp.int32(-1), jnp.int32(0)), (8, tq))

    def sweep(i, keep, first=False):
        parts = [jnp.zeros((8, tq), jnp.int32) for _ in range(4)]
        for g in range(ngroups):
            act = act_ref[g]
            if not first:
                act = act & (planes_ref[g, i - 1] ^ keep)
                act_ref[g] = act
            parts[g % 4] = parts[g % 4] + lax.population_count(act & planes_ref[g, i])
        cnt = (parts[0] + parts[1]) + (parts[2] + parts[3])
        return jnp.sum(cnt.astype(F32), axis=0, keepdims=True)

    def decide(i, ones, thr_u, n_gt):
        take = n_gt + ones >= topk
        thr_u = thr_u | jnp.where(take, jnp.left_shift(jnp.int32(1), 31 - i), 0)
        return thr_u, jnp.where(take, n_gt, n_gt + ones), jnp.where(take, 0, -1).astype(jnp.int32)

    def pass_body(i, carry):
        thr_u, n_gt, keep = carry
        return decide(i, sweep(i, keep), thr_u, n_gt)

    start = decide(0, sweep(0, None, first=True),
                   jnp.zeros((1, tq), jnp.int32), jnp.zeros((1, tq), F32))
    thr_u, n_gt, keep = lax.fori_loop(1, 32, pass_body, start)
    n_eq = jnp.zeros((8, tq), jnp.int32)
    for g in range(ngroups):
        n_eq = n_eq + lax.population_count(act_ref[g] & (planes_ref[g, 31] ^ keep))
    thr = thr_u ^ jnp.int32(INT_MIN)
    cnt_ge = n_gt + jnp.sum(n_eq.astype(F32), axis=0, keepdims=True)
    need = topk - n_gt

    def as_rows(x):
        return jnp.broadcast_to(x, (LANES, tq)).T

    thr_r = as_rows(thr)
    need_r = as_rows(need)
    jcut_ref[...] = jnp.full((tq, LANES), 2 ** 30, jnp.int32)

    @pl.when(jnp.max(cnt_ge) > topk)
    def _():
        def ties_before(cand):
            def body(c, cnt):
                off = pl.multiple_of(c * tk, tk)
                for j in range(nsub):
                    keyj = keys_ref[:, pl.ds(pl.multiple_of(off + j * LANES, LANES), LANES)]
                    hit = (keyj == thr_r) & (off + j * LANES + lane < cand)
                    cnt = jnp.where(hit, cnt + 1, cnt)
                return cnt
            cnt = lax.fori_loop(0, nkv, body, jnp.zeros((tq, LANES), jnp.int32))
            return jnp.sum(cnt.astype(F32), axis=1, keepdims=True)

        def idx_body(i, j):
            cand = j + jnp.left_shift(jnp.int32(1), idx_bits - 1 - i)
            return jnp.where(ties_before(cand) < need_r, cand, j)
        jcut_ref[...] = lax.fori_loop(0, idx_bits, idx_body, jnp.zeros((tq, LANES), jnp.int32))

    jcut_r = jcut_ref[...]

    q = q_ref[0]
    q8 = jnp.concatenate(
        [_split_heads_rows(q[:, p * LANES:(p + 1) * LANES]) for p in range(4)], axis=0)
    m_ref[...] = jnp.full(m_ref.shape, NEG_BIG, F32)
    l_ref[...] = jnp.zeros(l_ref.shape, F32)
    acc_ref[...] = jnp.zeros(acc_ref.shape, F32)

    def attn_body(c, carry):
        off = pl.multiple_of(c * tk, tk)
        s = _nt_dot(q8, k2_ref[0, pl.ds(off, tk), :])
        bias = []
        for j in range(nsub):
            keyj = keys_ref[:, pl.ds(pl.multiple_of(off + j * LANES, LANES), LANES)]
            kpos = off + j * LANES + lane
            sel = ((keyj > thr_r) | ((keyj == thr_r) & (kpos <= jcut_r))) & (kpos <= qpos)
            bias.append(jnp.where(sel, 0.0, NEG_BIG))
        _online_update(s, v2_ref[0, pl.ds(off, tk), :], m_ref, l_ref, acc_ref, bias=bias)
        return carry

    lax.fori_loop(0, nkv, attn_body, 0)

    o = _normalised(l_ref, acc_ref)
    lo = lane < HEAD_DIM
    for p in range(4):
        o_ref[0, :, p * LANES:(p + 1) * LANES] = jnp.where(
            lo, o[(2 * p) * tq:(2 * p + 1) * tq], o[(2 * p + 1) * tq:(2 * p + 2) * tq]
        ).astype(o_ref.dtype)


def _dsa_attn(bq, iq, iw, ik4, k2, v2, tq, tk):
    B, S, C = bq.shape
    topk = min(IDX_TOPK_MAX, S // 4)
    idx_bits = max(1, (S - 1).bit_length())
    kern = functools.partial(_dsa_kernel, tq=tq, tk=tk, topk=topk, idx_bits=idx_bits)
    row = lambda width: pl.BlockSpec((1, tq, width), lambda b, i: (b, i, 0))
    full = _resident((1, S, LANES), lambda b, i: (b, 0, 0))
    return pl.pallas_call(
        kern,
        grid=(B, S // tq),
        in_specs=[row(C), row(LANES), row(LANES), full, full, full],
        out_specs=row(C),
        out_shape=jax.ShapeDtypeStruct(bq.shape, BF16),
        scratch_shapes=[pltpu.VMEM((tq, S), jnp.int32),
                        pltpu.VMEM((S // BIT_GROUP, 32, 8, tq), jnp.int32),
                        pltpu.VMEM((S // BIT_GROUP, 8, tq), jnp.int32),
                        pltpu.VMEM((tq, LANES), jnp.int32),
                        pltpu.VMEM((8 * tq, LANES), F32), pltpu.VMEM((8 * tq, LANES), F32),
                        pltpu.VMEM((8 * tq, LANES), F32)],
        compiler_params=_cparams(("parallel", "arbitrary")),
        name="dsa_attn",
    )(bq, iq, iw, ik4, k2, v2)


def _moba_kernel(q_ref, k_ref, v_ref, km_ref, o_ref, m_ref, l_ref, acc_ref, *, nb):
    blk_rows = MOBA_BLOCK
    rows_ab = 2 * blk_rows
    i = pl.program_id(2)
    lane = lax.broadcasted_iota(jnp.int32, (1, LANES), 1)
    q = q_ref[0]
    q2 = jnp.concatenate([_split_heads_rows(q[:blk_rows]), _split_heads_rows(q[blk_rows:])], axis=0)

    km = jnp.concatenate([km_ref[0], jnp.zeros((LANES - nb, LANES), F32)], axis=0)
    km_hi = km.astype(BF16)
    r1 = km - km_hi.astype(F32)
    km_mid = r1.astype(BF16)
    km_lo = (r1 - km_mid.astype(F32)).astype(BF16)
    nb8 = -(-nb // 8) * 8
    gate_t = (_nt_dot(km_hi, q2) + _nt_dot(km_mid, q2) + _nt_dot(km_lo, q2))[:nb8]
    blk = lax.broadcasted_iota(jnp.int32, (nb8, 1), 0)
    blk_f = blk.astype(F32)
    col = lax.broadcasted_iota(jnp.int32, (1, 2 * rows_ab), 1)
    past = blk < 2 * i + jnp.where(col >= rows_ab, 1, 0)
    g = jnp.where(past, gate_t, -jnp.inf)
    sel = jnp.zeros(g.shape, jnp.bool_)
    for _ in range(min(MOBA_TOPK, nb - 1)):
        mx = jnp.max(g, axis=0, keepdims=True)
        first = jnp.min(jnp.where(g == mx, blk_f, float(LANES)), axis=0, keepdims=True)
        pick = blk_f == first
        sel = sel | pick
        g = jnp.where(pick, -jnp.inf, g)
    bias_t = jnp.concatenate([jnp.where(sel & past, 0.0, NEG_BIG),
                              jnp.full((LANES - nb8, 2 * rows_ab), NEG_BIG, F32)], axis=0)
    qa = jnp.concatenate([q2, bias_t.T.astype(BF16)], axis=1)

    m_ref[...] = jnp.full(m_ref.shape, NEG_BIG, F32)
    l_ref[...] = jnp.zeros(l_ref.shape, F32)
    acc_ref[...] = jnp.zeros(acc_ref.shape, F32)

    def biased_logits(lhs, first_blk, nblk):
        n = nblk * blk_rows
        off = pl.multiple_of(first_blk * blk_rows, blk_rows)
        row_blk = lax.broadcasted_iota(jnp.int32, (n, 1), 0) >> (MOBA_BLOCK.bit_length() - 1)
        onehot = jnp.where(lane == first_blk + row_blk, 1.0, 0.0).astype(BF16)
        ka = jnp.concatenate([k_ref[0, pl.ds(off, n), :], onehot], axis=1)
        return _nt_dot(lhs, ka), v_ref[0, pl.ds(off, n), :]

    def body(c, carry):
        s, v = biased_logits(qa, 4 * c, 4)
        _online_update(s, v, m_ref, l_ref, acc_ref)
        return carry

    lax.fori_loop(0, i // 2, body, 0)

    @pl.when(i % 2 == 1)
    def _():
        s, v = biased_logits(qa, 2 * i - 2, 2)
        _online_update(s, v, m_ref, l_ref, acc_ref)

    r = lax.broadcasted_iota(jnp.int32, (rows_ab, 1), 0)
    causal = (lax.broadcasted_iota(jnp.int32, (1, blk_rows), 1)
              <= jnp.where(r >= blk_rows, r - blk_rows, r))
    off_a = pl.multiple_of(2 * i * blk_rows, blk_rows)
    s_a = jnp.where(causal, _nt_dot(q2[:rows_ab], k_ref[0, pl.ds(off_a, blk_rows), :]), NEG_BIG)
    s_b, v = biased_logits(qa[rows_ab:], 2 * i, 1)
    _online_update(jnp.concatenate([s_a, s_b], axis=0), v, m_ref, l_ref, acc_ref)
    off_b = pl.multiple_of((2 * i + 1) * blk_rows, blk_rows)
    s_b = jnp.where(causal, _nt_dot(q2[rows_ab:], k_ref[0, pl.ds(off_b, blk_rows), :]), NEG_BIG)
    _online_update(s_b, v_ref[0, pl.ds(off_b, blk_rows), :], m_ref, l_ref, acc_ref, row0=rows_ab)

    o = _normalised(l_ref, acc_ref)
    lo = lane < HEAD_DIM
    o_ref[0, :blk_rows] = jnp.where(lo, o[:blk_rows], o[blk_rows:rows_ab]).astype(o_ref.dtype)
    o_ref[0, blk_rows:] = jnp.where(
        lo, o[rows_ab:rows_ab + blk_rows], o[rows_ab + blk_rows:]).astype(o_ref.dtype)


def _moba_attn(cq, ck, cv, kmean):
    B, S, C = cq.shape
    nb = S // MOBA_BLOCK
    tq = 2 * MOBA_BLOCK
    return pl.pallas_call(
        functools.partial(_moba_kernel, nb=nb),
        grid=(B, C // LANES, S // tq),
        in_specs=[pl.BlockSpec((1, tq, LANES), lambda b, p, i: (b, i, p)),
                  pl.BlockSpec((1, S, LANES), lambda b, p, i: (b, 0, p)),
                  pl.BlockSpec((1, S, LANES), lambda b, p, i: (b, 0, p)),
                  pl.BlockSpec((1, nb, LANES), lambda b, p, i: (b, 0, p))],
        out_specs=pl.BlockSpec((1, tq, LANES), lambda b, p, i: (b, i, p)),
        out_shape=jax.ShapeDtypeStruct(cq.shape, BF16),
        scratch_shapes=[pltpu.VMEM((2 * tq, LANES), F32), pltpu.VMEM((2 * tq, LANES), F32),
                        pltpu.VMEM((2 * tq, LANES), F32)],
        compiler_params=_cparams(("parallel", "parallel", "arbitrary")),
        name="moba_attn",
    )(cq, ck, cv, kmean)


def _sigmoid(x):
    return 1.0 / (1.0 + jnp.exp(-x))


def _merge_kernel(ya_ref, yb_ref, yc_ref, gates_ref, x_ref, wb_ref, wo_ref, g_ref, o_ref):
    d = x_ref.shape[2]
    merged = None
    for n, y_ref in enumerate((ya_ref, yb_ref, yc_ref)):
        br = jnp.dot(y_ref[0], wb_ref[n], preferred_element_type=F32)
        term = _sigmoid(gates_ref[0, :, n * d:(n + 1) * d].astype(F32)) * br
        merged = term if merged is None else merged + term
    m = jnp.dot(merged.astype(BF16), wo_ref[...], preferred_element_type=F32)
    o_ref[0] = x_ref[0] + _rms(m, g_ref[...])


def _merge(ya, yb, yc, gates, x, wb, wo, layer, g, tm):
    B, S, D = x.shape
    row = lambda width: pl.BlockSpec((1, tm, width), lambda b, m: (b, m, 0))
    return pl.pallas_call(
        _merge_kernel,
        grid=(B, S // tm),
        in_specs=[row(BRANCH_WIDTH), row(BRANCH_WIDTH), row(BRANCH_WIDTH), row(N_BRANCH * D), row(D),
                  _resident((None,) + wb.shape[1:], lambda b, m: (layer, 0, 0, 0)),
                  _resident((None,) + wo.shape[1:], lambda b, m: (layer, 0, 0)),
                  _resident((1, D), lambda b, m: (0, 0))],
        out_specs=row(D),
        out_shape=jax.ShapeDtypeStruct(x.shape, F32),
        compiler_params=_cparams(("parallel", "parallel")),
        name="merge_out",
    )(ya, yb, yc, gates, x, wb, wo, g.reshape(1, D))


def _ffn_kernel(x_ref, g_in_ref, wi_ref, wo_ref, g_out_ref, o_ref, acc_ref, *, d_ff, tf):
    x = x_ref[0]
    h = _rms(x, g_in_ref[...]).astype(BF16)
    acc_ref[...] = jnp.zeros(acc_ref.shape, F32)

    def body(c, carry):
        off = pl.multiple_of(c * tf, tf)
        gt = jnp.dot(h, wi_ref[:, pl.ds(off, tf)], preferred_element_type=F32)
        up_off = pl.multiple_of(d_ff + off, LANES)
        up = jnp.dot(h, wi_ref[:, pl.ds(up_off, tf)], preferred_element_type=F32)
        act = (gt * _sigmoid(gt) * up).astype(BF16)
        acc_ref[...] += jnp.dot(act, wo_ref[pl.ds(off, tf), :], preferred_element_type=F32)
        return carry

    lax.fori_loop(0, d_ff // tf, body, 0)
    o_ref[0] = x + _rms(acc_ref[...], g_out_ref[...])


def _ffn(x, g_in, wi, wo, layer, g_out, tm, tf):
    B, S, D = x.shape
    d_ff = wo.shape[1]
    row = pl.BlockSpec((1, tm, D), lambda b, m: (b, m, 0))
    vec = _resident((1, D), lambda b, m: (0, 0))
    return pl.pallas_call(
        functools.partial(_ffn_kernel, d_ff=d_ff, tf=tf),
        grid=(B, S // tm),
        in_specs=[row, vec, _resident((None,) + wi.shape[1:], lambda b, m: (layer, 0, 0)),
                  _resident((None,) + wo.shape[1:], lambda b, m: (layer, 0, 0)), vec],
        out_specs=row,
        out_shape=jax.ShapeDtypeStruct(x.shape, F32),
        scratch_shapes=[pltpu.VMEM((tm, D), F32)],
        compiler_params=_cparams(("parallel", "parallel")),
        name="swiglu_ffn",
    )(x, g_in.reshape(1, D), wi, wo, g_out.reshape(1, D))


def _pack_w_in(w):
    offs = [0]
    for width in (512, 512, 512, 512, 64, 64, 128, 32, 4, 512, 512, 512, 3072):
        offs.append(offs[-1] + width)
    seg = lambda i: w[..., offs[i]:offs[i + 1]]
    scale = HEAD_DIM ** -0.5 * LOG2E
    cols = [seg(0) * scale, seg(1), seg(2), seg(3) * scale,
            seg(4), seg(4), seg(5), seg(5), seg(6),
            seg(7), seg(7), seg(7), seg(7),
            seg(8), jnp.zeros(w.shape[:-1] + (LANES - IDX_HEADS,), w.dtype),
            seg(9) * scale, seg(10), seg(11), seg(12)]
    packed = jnp.concatenate([c.astype(BF16) for c in cols], axis=-1)
    assert packed.shape[-1] == PACKED_COLS
    return packed


def kernel(x, positions, w_in, w_branch, w_out, lambda_q1, lambda_k1, lambda_q2, lambda_k2,
           subln_g, norm_g, w_ffn_in, w_ffn_out):
    B, S, D = x.shape
    depth = w_in.shape[0]
    tm = min(512, S)
    tab = _rope_table(positions, tm)
    w_packed = _pack_w_in(w_in)
    wb, wo = w_branch.astype(BF16), w_out.astype(BF16)
    wi_ffn, wo_ffn = w_ffn_in.astype(BF16), w_ffn_out.astype(BF16)
    for l in range(depth):
        lam_init = 0.8 - 0.6 * math.exp(-0.3 * l)
        (aq, ak, av, bq, k2, v2, iq, ik4, iw, cq, ck, cv, gates, kmean) = _inproj(
            x, norm_g[l, 0], tab, w_packed, l, tm)
        kmean = kmean.reshape(B, S // MOBA_BLOCK, 512)
        lam_rows = jnp.stack([lambda_q1[l], lambda_k1[l], lambda_q2[l], lambda_k2[l]])
        ya = _diff_attn(aq, ak, av, lam_rows, subln_g[l], lam_init, t=min(1024, S))
        yb = _dsa_attn(bq, iq, iw, ik4, k2, v2, tq=256, tk=min(512, S))
        yc = _moba_attn(cq, ck, cv, kmean)
        x = _merge(ya, yb, yc, gates, x, wb, wo, l, norm_g[l, 1], tm)
        x = _ffn(x, norm_g[l, 2], wi_ffn, wo_ffn, l, norm_g[l, 3], tm, tf=256)
    return x
```

```python
import functools
import math

import jax
import jax.numpy as jnp
from jax import lax
from jax.experimental import pallas as pl
from jax.experimental.pallas import tpu as pltpu

F32 = jnp.float32
BF16 = jnp.bfloat16

LANES = 128
HEAD_DIM = 64
ROPE_THETA = 500000.0
NORM_EPS = 1e-6
IDX_HEADS = 4
IDX_DIM = 32
IDX_TOPK_MAX = 256
MOBA_BLOCK = 256
MOBA_TOPK = 3
N_BRANCH = 3
BRANCH_WIDTH = 512
NEG_BIG = -1e30
LOG2E = math.log2(math.e)
ROW_BLOCK = 64
INT_MIN = -2 ** 31
VMEM_LIMIT = 56 * 1024 * 1024

A_AQ, A_AK, A_AV, A_BQ, A_KV, A_IQ, A_IKW, A_COLS = 0, 512, 1024, 1536, 2048, 2176, 2304, 2432
B_START = 2340
B_CQ, B_CK, B_CV, B_G, B_COLS = 0, 512, 1024, 1536, 4608
Q_SCALE = HEAD_DIM ** -0.5 * LOG2E


def _nt_dot(a, b):
    return lax.dot_general(a, b, (((1,), (1,)), ((), ())), preferred_element_type=F32)


def _rms(x, g):
    return x * lax.rsqrt(jnp.mean(x * x, axis=-1, keepdims=True) + NORM_EPS) * g


def _cparams(sem):
    return pltpu.CompilerParams(dimension_semantics=sem, vmem_limit_bytes=VMEM_LIMIT)


def _resident(shape, index_map):
    return pl.BlockSpec(shape, index_map, pipeline_mode=pl.Buffered(1))


def _rope_table_kernel(pos_ref, tab_ref):
    pos = pos_ref[0].astype(F32)
    lane = lax.broadcasted_iota(jnp.int32, (1, LANES), 1)

    def tables(head_dim):
        rot = head_dim // 4
        half = rot // 2
        d = lane & (head_dim - 1)
        fi = d & (half - 1)
        inv = jnp.zeros((1, LANES), F32)
        for i in range(half):
            inv = jnp.where(fi == i, F32(ROPE_THETA ** (-(2.0 * i) / rot)), inv)
        ang = pos * inv
        cos = jnp.cos(ang)
        sin = jnp.sin(ang)
        c = jnp.where(d < rot, cos, 1.0)
        s = jnp.where(d < half, -sin, jnp.where(d < rot, sin, 0.0))
        return c, s

    c64, s64 = tables(HEAD_DIM)
    c32, s32 = tables(IDX_DIM)
    tab_ref[0, :, 0 * LANES:1 * LANES] = c64
    tab_ref[0, :, 1 * LANES:2 * LANES] = s64
    tab_ref[0, :, 2 * LANES:3 * LANES] = c32
    tab_ref[0, :, 3 * LANES:4 * LANES] = s32


def _rope_table(positions, tm):
    B, S = positions.shape
    return pl.pallas_call(
        _rope_table_kernel,
        grid=(B, S // tm),
        in_specs=[pl.BlockSpec((1, tm, 1), lambda b, m: (b, m, 0))],
        out_specs=pl.BlockSpec((1, tm, 4 * LANES), lambda b, m: (b, m, 0)),
        out_shape=jax.ShapeDtypeStruct((B, S, 4 * LANES), F32),
        compiler_params=_cparams(("parallel", "parallel")),
        name="rope_table",
    )(positions.reshape(B, S, 1))


def _inproj_kernel(x_ref, g_ref, tab_ref, wa_ref, wb_ref,
                   aq_ref, ak_ref, av_ref, bq_ref, k2_ref, v2_ref, iq_ref, ik4_ref, iw_ref,
                   cq_ref, ck_ref, cv_ref, gates_ref, kmean_ref):
    h = _rms(x_ref[0], g_ref[...]).astype(BF16)
    lane = lax.broadcasted_iota(jnp.int32, (1, LANES), 1)

    def proj(w_ref, off, width):
        return jnp.dot(h, w_ref[:, off:off + width], preferred_element_type=F32)

    def rope(y, head_dim):
        half = head_dim // 8
        t0 = 0 if head_dim == HEAD_DIM else 2 * LANES
        c = tab_ref[0, :, t0:t0 + LANES]
        s = tab_ref[0, :, t0 + LANES:t0 + 2 * LANES]
        first = (lane & (head_dim - 1)) < half
        blocks = []
        for j in range(y.shape[1] // LANES):
            yj = y[:, j * LANES:(j + 1) * LANES]
            up = pltpu.roll(yj, LANES - half, 1)
            dn = pltpu.roll(yj, half, 1)
            blocks.append(yj * c + jnp.where(first, up, dn) * s)
        return blocks

    def store(out_ref, blocks):
        for j, r in enumerate(blocks):
            out_ref[0, :, j * LANES:(j + 1) * LANES] = r.astype(out_ref.dtype)

    store(aq_ref, rope(proj(wa_ref, A_AQ, 512) * Q_SCALE, HEAD_DIM))
    store(ak_ref, rope(proj(wa_ref, A_AK, 512), HEAD_DIM))
    av_ref[0] = proj(wa_ref, A_AV, 512).astype(av_ref.dtype)
    store(bq_ref, rope(proj(wa_ref, A_BQ, 512) * Q_SCALE, HEAD_DIM))
    kv = proj(wa_ref, A_KV, LANES)
    vk = pltpu.roll(kv, HEAD_DIM, 1)
    lo = lane < HEAD_DIM
    store(k2_ref, rope(jnp.where(lo, kv, vk), HEAD_DIM))
    v2_ref[0] = jnp.where(lo, vk, kv).astype(v2_ref.dtype)
    store(iq_ref, rope(proj(wa_ref, A_IQ, LANES), IDX_DIM))
    ikw = proj(wa_ref, A_IKW, LANES)
    ik = rope(ikw, IDX_DIM)[0]
    ik4 = ik
    for rep in range(1, LANES // IDX_DIM):
        ik4 = jnp.where(lane < rep * IDX_DIM, ik4, pltpu.roll(ik, rep * IDX_DIM, 1))
    ik4_ref[0] = ik4.astype(ik4_ref.dtype)
    iw_ref[0] = jnp.where(lane < IDX_HEADS, pltpu.roll(ikw, LANES - IDX_DIM, 1), 0.0)
    store(cq_ref, rope(proj(wb_ref, B_CQ, 512) * Q_SCALE, HEAD_DIM))
    ck_blocks = rope(proj(wb_ref, B_CK, 512), HEAD_DIM)
    store(ck_ref, ck_blocks)
    cv_ref[0] = proj(wb_ref, B_CV, 512).astype(cv_ref.dtype)
    gates_ref[0] = proj(wb_ref, B_G, 3 * 1024).astype(gates_ref.dtype)

    tm = x_ref.shape[1]
    for blk in range(tm // MOBA_BLOCK):
        for j, r in enumerate(ck_blocks):
            part = r[blk * MOBA_BLOCK:(blk + 1) * MOBA_BLOCK]
            kmean_ref[0, 0, blk:blk + 1, j * LANES:(j + 1) * LANES] = (
                jnp.mean(part, axis=0, keepdims=True))


def _inproj(x, g, tab, w_a, w_b, layer, tm):
    B, S, D = x.shape
    nm = S // tm
    row = lambda width: pl.BlockSpec((1, tm, width), lambda b, m: (b, m, 0))
    shp = lambda width, dt: jax.ShapeDtypeStruct((B, S, width), dt)
    widths = [(512, BF16)] * 4 + [(LANES, BF16)] * 4 + [(LANES, F32)] + [(512, BF16)] * 3 + [(3072, BF16)]
    out_specs = [row(w) for w, _ in widths]
    out_shape = [shp(w, dt) for w, dt in widths]
    out_specs.append(pl.BlockSpec((1, 1, tm // MOBA_BLOCK, 512), lambda b, m: (b, m, 0, 0)))
    out_shape.append(jax.ShapeDtypeStruct((B, nm, tm // MOBA_BLOCK, 512), F32))
    return pl.pallas_call(
        _inproj_kernel,
        grid=(B, nm),
        in_specs=[row(D),
                  _resident((1, D), lambda b, m: (0, 0)),
                  row(4 * LANES),
                  _resident((None, D, A_COLS), lambda b, m: (layer, 0, 0)),
                  _resident((None, D, B_COLS), lambda b, m: (layer, 0, 0))],
        out_specs=out_specs,
        out_shape=out_shape,
        compiler_params=_cparams(("parallel", "parallel")),
        name="inproj",
    )(x, g.reshape(1, D), tab, w_a, w_b)


def _split_heads_rows(q):
    lo = lax.broadcasted_iota(jnp.int32, (1, LANES), 1) < HEAD_DIM
    zero = jnp.zeros_like(q)
    return jnp.concatenate([jnp.where(lo, q, zero), jnp.where(lo, zero, q)], axis=0)


def _lane_blocks(s):
    return [s[:, j * LANES:(j + 1) * LANES] for j in range(s.shape[1] // LANES)]


def _online_update(s, v, m_ref, l_ref, acc_ref, bias=None, row0=0):
    rows_total = s.shape[0]
    nblk = s.shape[1] // LANES
    p_rows, alphas = [], []
    for r0 in range(0, rows_total, ROW_BLOCK):
        rows = slice(row0 + r0, row0 + r0 + ROW_BLOCK)
        cols = [s[r0:r0 + ROW_BLOCK, j * LANES:(j + 1) * LANES] for j in range(nblk)]
        if bias is not None:
            b0 = r0 % bias[0].shape[0]
            cols = [c + bj[b0:b0 + ROW_BLOCK] for c, bj in zip(cols, bias)]
        m_prev = m_ref[rows]
        m_new = jnp.maximum(
            m_prev, jnp.max(functools.reduce(jnp.maximum, cols), axis=1, keepdims=True))
        alpha = jnp.exp2(m_prev - m_new)
        ps = [jnp.exp2(c - m_new) for c in cols]
        l_ref[rows] = alpha * l_ref[rows] + functools.reduce(jnp.add, ps)
        m_ref[rows] = m_new
        p_rows.append(jnp.concatenate([pj.astype(BF16) for pj in ps], axis=1))
        alphas.append(alpha)
    p = jnp.concatenate(p_rows, axis=0)
    alpha = jnp.concatenate(alphas, axis=0)
    span = slice(row0, row0 + rows_total)
    acc_ref[span] = alpha * acc_ref[span] + jnp.dot(p, v, preferred_element_type=F32)


def _normalised(l_ref, acc_ref):
    return acc_ref[...] / jnp.sum(l_ref[...], axis=1, keepdims=True)


def _diff_attn_kernel(q_ref, k_ref, v_ref, lam_ref, g_ref, o_ref, m_ref, l_ref, acc_ref,
                      *, t, lam_init):
    qi = pl.program_id(2)
    half = t // 2
    q = q_ref[0]
    q2 = jnp.concatenate([_split_heads_rows(q[:half]), _split_heads_rows(q[half:])], axis=0)
    m_ref[...] = jnp.full(m_ref.shape, NEG_BIG, F32)
    l_ref[...] = jnp.zeros(l_ref.shape, F32)
    acc_ref[...] = jnp.zeros(acc_ref.shape, F32)

    def body(c, carry):
        off = pl.multiple_of(c * t, t)
        _online_update(_nt_dot(q2, k_ref[0, pl.ds(off, t), :]), v_ref[0, pl.ds(off, t), :],
                       m_ref, l_ref, acc_ref)
        return carry

    lax.fori_loop(0, qi, body, 0)

    r = lax.broadcasted_iota(jnp.int32, (t, 1), 0)
    causal = lax.broadcasted_iota(jnp.int32, (1, half), 1) <= jnp.where(r >= half, r - half, r)
    off_a = pl.multiple_of(qi * t, t)
    s = _nt_dot(q2, k_ref[0, pl.ds(off_a, half), :])
    s = jnp.concatenate([jnp.where(causal, s[:t], NEG_BIG), s[t:]], axis=0)
    _online_update(s, v_ref[0, pl.ds(off_a, half), :], m_ref, l_ref, acc_ref)
    off_b = pl.multiple_of(qi * t + half, half)
    s = jnp.where(causal, _nt_dot(q2[t:], k_ref[0, pl.ds(off_b, half), :]), NEG_BIG)
    _online_update(s, v_ref[0, pl.ds(off_b, half), :], m_ref, l_ref, acc_ref, row0=t)

    lam_rows = lam_ref[...]
    e1 = jnp.exp(jnp.sum(lam_rows[0:1] * lam_rows[1:2], axis=1, keepdims=True))
    e2 = jnp.exp(jnp.sum(lam_rows[2:3] * lam_rows[3:4], axis=1, keepdims=True))
    lam = e1 - e2 + lam_init
    o = _normalised(l_ref, acc_ref)
    o1 = jnp.concatenate([o[:half], o[t:t + half]], axis=0)
    o2 = jnp.concatenate([o[half:t], o[t + half:]], axis=0)
    o_ref[0] = (_rms(o1 - lam * o2, g_ref[...]) * (1.0 - lam_init)).astype(o_ref.dtype)


def _diff_attn(aq, ak, av, lam_rows, subln_g, lam_init, t):
    B, S, _ = aq.shape
    nh = aq.shape[2] // LANES
    kern = functools.partial(_diff_attn_kernel, t=t, lam_init=lam_init)
    return pl.pallas_call(
        kern,
        grid=(B, nh, S // t),
        in_specs=[pl.BlockSpec((1, t, LANES), lambda b, h, i: (b, i, h)),
                  pl.BlockSpec((1, S, LANES), lambda b, h, i: (b, 0, h)),
                  pl.BlockSpec((1, S, LANES), lambda b, h, i: (b, 0, h)),
                  pl.BlockSpec((4, HEAD_DIM), lambda b, h, i: (0, 0)),
                  pl.BlockSpec((1, LANES), lambda b, h, i: (0, 0))],
        out_specs=pl.BlockSpec((1, t, LANES), lambda b, h, i: (b, i, h)),
        out_shape=jax.ShapeDtypeStruct(aq.shape, BF16),
        scratch_shapes=[pltpu.VMEM((2 * t, LANES), F32), pltpu.VMEM((2 * t, LANES), F32),
                        pltpu.VMEM((2 * t, LANES), F32)],
        compiler_params=_cparams(("parallel", "parallel", "arbitrary")),
        name="diff_attn",
    )(aq, ak, av, lam_rows, subln_g.reshape(1, LANES))


BIT_GROUP = 256


def _bit_transpose32(words):
    w = list(words)
    for j, m in ((16, 0x0000FFFF), (8, 0x00FF00FF), (4, 0x0F0F0F0F), (2, 0x33333333), (1, 0x55555555)):
        for k in range(32):
            if k & j == 0:
                t = (w[k] ^ (w[k + j] >> j)) & m
                w[k] = w[k] ^ t
                w[k + j] = w[k + j] ^ (t << j)
    return w

def _dsa_kernel(q_ref, iq_ref, iw_ref, ik4_ref, k2_ref, v2_ref, o_ref,
                keys_ref, planes_ref, act_ref, jcut_ref, m_ref, l_ref, acc_ref,
                *, tq, tk, topk, idx_bits):
    qi = pl.program_id(1)
    nkv = (qi * tq + tq - 1) // tk + 1
    lane = lax.broadcasted_iota(jnp.int32, (1, LANES), 1)
    qpos = qi * tq + lax.broadcasted_iota(jnp.int32, (tq, 1), 0)
    lane_tk = lax.broadcasted_iota(jnp.int32, (1, tk), 1)
    nsub = tk // LANES
    seq = keys_ref.shape[1]

    @pl.when(qi == 0)
    def _():
        planes_ref[...] = jnp.zeros(planes_ref.shape, jnp.int32)

    iq = iq_ref[0]
    zero = jnp.zeros_like(iq)
    iq4 = jnp.concatenate(
        [jnp.where((lane >> (IDX_DIM.bit_length() - 1)) == hh, iq, zero) for hh in range(IDX_HEADS)],
        axis=0)
    iw = iw_ref[0]
    iw_cols = [iw[:, hh:hh + 1] for hh in range(IDX_HEADS)]

    def score_body(c, carry):
        off = pl.multiple_of(c * tk, tk)
        rel = jnp.maximum(_nt_dot(iq4, ik4_ref[0, pl.ds(off, tk), :]), 0.0)
        score = iw_cols[0] * rel[0:tq]
        for hh in range(1, IDX_HEADS):
            score = score + iw_cols[hh] * rel[hh * tq:(hh + 1) * tq]
        bits = lax.bitcast_convert_type(score, jnp.int32)
        key = bits ^ ((bits >> 31) & jnp.int32(0x7FFFFFFF))
        kpos = off + lane_tk
        key = jnp.where(score == 0.0, (seq - 1) - kpos, jnp.where(key > 0, key + seq, key))
        key = jnp.where(kpos <= qpos, key, jnp.int32(INT_MIN))
        keys_ref[:, pl.ds(off, tk)] = key
        kts = [key[:, j * LANES:(j + 1) * LANES].T for j in range(nsub)]
        for gl in range(tk // BIT_GROUP):
            words = [kts[(gl * BIT_GROUP + 8 * g) // LANES][(8 * g) % LANES:(8 * g) % LANES + 8]
                     for g in range(32)]
            words = _bit_transpose32(words)
            words[0] = ~words[0]
            for b in range(32):
                planes_ref[c * (tk // BIT_GROUP) + gl, b] = words[b]
        return carry

    lax.fori_loop(0, nkv, score_body, 0)

    ngroups = planes_ref.shape[0]
    live = nkv * (tk // BIT_GROUP)
    for g in range(ngroups):
        act_ref[g] = jnp.broadcast_to(jnp.where(g < live, jnp.int32(-1), jnp.int32(0)), (8, tq))

    def sweep(i, keep, first=False):
        parts = [jnp.zeros((8, tq), jnp.int32) for _ in range(4)]
        for g in range(ngroups):
            act = act_ref[g]
            if not first:
                act = act & (planes_ref[g, i - 1] ^ keep)
                act_ref[g] = act
            parts[g % 4] = parts[g % 4] + lax.population_count(act & planes_ref[g, i])
        cnt = (parts[0] + parts[1]) + (parts[2] + parts[3])
        return jnp.sum(cnt.astype(F32), axis=0, keepdims=True)

    def decide(i, ones, thr_u, n_gt):
        take = n_gt + ones >= topk
        thr_u = thr_u | jnp.where(take, jnp.left_shift(jnp.int32(1), 31 - i), 0)
        return thr_u, jnp.where(take, n_gt, n_gt + ones), jnp.where(take, 0, -1).astype(jnp.int32)

    def pass_body(i, carry):
        thr_u, n_gt, keep = carry
        return decide(i, sweep(i, keep), thr_u, n_gt)

    start = decide(0, sweep(0, None, first=True),
                   jnp.zeros((1, tq), jnp.int32), jnp.zeros((1, tq), F32))
    thr_u, n_gt, keep = lax.fori_loop(1, 32, pass_body, start)
    n_eq = jnp.zeros((8, tq), jnp.int32)
    for g in range(ngroups):
        n_eq = n_eq + lax.population_count(act_ref[g] & (planes_ref[g, 31] ^ keep))
    thr = thr_u ^ jnp.int32(INT_MIN)
    cnt_ge = n_gt + jnp.sum(n_eq.astype(F32), axis=0, keepdims=True)
    need = topk - n_gt

    def as_rows(x):
        return jnp.broadcast_to(x, (LANES, tq)).T

    thr_r = as_rows(thr)
    need_r = as_rows(need)
    jcut_ref[...] = jnp.full((tq, LANES), 2 ** 30, jnp.int32)

    @pl.when(jnp.max(cnt_ge) > topk)
    def _():
        def ties_before(cand):
            def body(c, cnt):
                off = pl.multiple_of(c * tk, tk)
                for j in range(nsub):
                    keyj = keys_ref[:, pl.ds(pl.multiple_of(off + j * LANES, LANES), LANES)]
                    hit = (keyj == thr_r) & (off + j * LANES + lane < cand)
                    cnt = jnp.where(hit, cnt + 1, cnt)
                return cnt
            cnt = lax.fori_loop(0, nkv, body, jnp.zeros((tq, LANES), jnp.int32))
            return jnp.sum(cnt.astype(F32), axis=1, keepdims=True)

        def idx_body(i, j):
            cand = j + jnp.left_shift(jnp.int32(1), idx_bits - 1 - i)
            return jnp.where(ties_before(cand) < need_r, cand, j)
        jcut_ref[...] = lax.fori_loop(0, idx_bits, idx_body, jnp.zeros((tq, LANES), jnp.int32))

    jcut_r = jcut_ref[...]

    q = q_ref[0]
    q8 = jnp.concatenate(
        [_split_heads_rows(q[:, p * LANES:(p + 1) * LANES]) for p in range(4)], axis=0)
    m_ref[...] = jnp.full(m_ref.shape, NEG_BIG, F32)
    l_ref[...] = jnp.zeros(l_ref.shape, F32)
    acc_ref[...] = jnp.zeros(acc_ref.shape, F32)

    def attn_body(c, carry):
        off = pl.multiple_of(c * tk, tk)
        s = _nt_dot(q8, k2_ref[0, pl.ds(off, tk), :])
        bias = []
        for j in range(nsub):
            keyj = keys_ref[:, pl.ds(pl.multiple_of(off + j * LANES, LANES), LANES)]
            kpos = off + j * LANES + lane
            sel = ((keyj > thr_r) | ((keyj == thr_r) & (kpos <= jcut_r))) & (kpos <= qpos)
            bias.append(jnp.where(sel, 0.0, NEG_BIG))
        _online_update(s, v2_ref[0, pl.ds(off, tk), :], m_ref, l_ref, acc_ref, bias=bias)
        return carry

    lax.fori_loop(0, nkv, attn_body, 0)

    o = _normalised(l_ref, acc_ref)
    lo = lane < HEAD_DIM
    for p in range(4):
        o_ref[0, :, p * LANES:(p + 1) * LANES] = jnp.where(
            lo, o[(2 * p) * tq:(2 * p + 1) * tq], o[(2 * p + 1) * tq:(2 * p + 2) * tq]
        ).astype(o_ref.dtype)


def _dsa_attn(bq, iq, iw, ik4, k2, v2, tq, tk):
    B, S, C = bq.shape
    topk = min(IDX_TOPK_MAX, S // 4)
    idx_bits = max(1, (S - 1).bit_length())
    kern = functools.partial(_dsa_kernel, tq=tq, tk=tk, topk=topk, idx_bits=idx_bits)
    row = lambda width: pl.BlockSpec((1, tq, width), lambda b, i: (b, i, 0))
    full = _resident((1, S, LANES), lambda b, i: (b, 0, 0))
    return pl.pallas_call(
        kern,
        grid=(B, S // tq),
        in_specs=[row(C), row(LANES), row(LANES), full, full, full],
        out_specs=row(C),
        out_shape=jax.ShapeDtypeStruct(bq.shape, BF16),
        scratch_shapes=[pltpu.VMEM((tq, S), jnp.int32),
                        pltpu.VMEM((S // BIT_GROUP, 32, 8, tq), jnp.int32),
                        pltpu.VMEM((S // BIT_GROUP, 8, tq), jnp.int32),
                        pltpu.VMEM((tq, LANES), jnp.int32),
                        pltpu.VMEM((8 * tq, LANES), F32), pltpu.VMEM((8 * tq, LANES), F32),
                        pltpu.VMEM((8 * tq, LANES), F32)],
        compiler_params=_cparams(("parallel", "arbitrary")),
        name="dsa_attn",
    )(bq, iq, iw, ik4, k2, v2)


def _moba_kernel(q_ref, k_ref, v_ref, km_ref, o_ref, m_ref, l_ref, acc_ref, *, nb):
    blk_rows = MOBA_BLOCK
    rows_ab = 2 * blk_rows
    i = pl.program_id(2)
    lane = lax.broadcasted_iota(jnp.int32, (1, LANES), 1)
    q = q_ref[0]
    q2 = jnp.concatenate([_split_heads_rows(q[:blk_rows]), _split_heads_rows(q[blk_rows:])], axis=0)

    km = jnp.concatenate([km_ref[0], jnp.zeros((LANES - nb, LANES), F32)], axis=0)
    km_hi = km.astype(BF16)
    r1 = km - km_hi.astype(F32)
    km_mid = r1.astype(BF16)
    km_lo = (r1 - km_mid.astype(F32)).astype(BF16)
    nb8 = -(-nb // 8) * 8
    gate_t = (_nt_dot(km_hi, q2) + _nt_dot(km_mid, q2) + _nt_dot(km_lo, q2))[:nb8]
    blk = lax.broadcasted_iota(jnp.int32, (nb8, 1), 0)
    blk_f = blk.astype(F32)
    col = lax.broadcasted_iota(jnp.int32, (1, 2 * rows_ab), 1)
    past = blk < 2 * i + jnp.where(col >= rows_ab, 1, 0)
    g = jnp.where(past, gate_t, -jnp.inf)
    sel = jnp.zeros(g.shape, jnp.bool_)
    for _ in range(min(MOBA_TOPK, nb - 1)):
        mx = jnp.max(g, axis=0, keepdims=True)
        first = jnp.min(jnp.where(g == mx, blk_f, float(LANES)), axis=0, keepdims=True)
        pick = blk_f == first
        sel = sel | pick
        g = jnp.where(pick, -jnp.inf, g)
    bias_t = jnp.concatenate([jnp.where(sel & past, 0.0, NEG_BIG),
                              jnp.full((LANES - nb8, 2 * rows_ab), NEG_BIG, F32)], axis=0)
    qa = jnp.concatenate([q2, bias_t.T.astype(BF16)], axis=1)

    m_ref[...] = jnp.full(m_ref.shape, NEG_BIG, F32)
    l_ref[...] = jnp.zeros(l_ref.shape, F32)
    acc_ref[...] = jnp.zeros(acc_ref.shape, F32)

    def biased_logits(lhs, first_blk, nblk):
        n = nblk * blk_rows
        off = pl.multiple_of(first_blk * blk_rows, blk_rows)
        row_blk = lax.broadcasted_iota(jnp.int32, (n, 1), 0) >> (MOBA_BLOCK.bit_length() - 1)
        onehot = jnp.where(lane == first_blk + row_blk, 1.0, 0.0).astype(BF16)
        ka = jnp.concatenate([k_ref[0, pl.ds(off, n), :], onehot], axis=1)
        return _nt_dot(lhs, ka), v_ref[0, pl.ds(off, n), :]

    def body(c, carry):
        s, v = biased_logits(qa, 4 * c, 4)
        _online_update(s, v, m_ref, l_ref, acc_ref)
        return carry

    lax.fori_loop(0, i // 2, body, 0)

    @pl.when(i % 2 == 1)
    def _():
        s, v = biased_logits(qa, 2 * i - 2, 2)
        _online_update(s, v, m_ref, l_ref, acc_ref)

    r = lax.broadcasted_iota(jnp.int32, (rows_ab, 1), 0)
    causal = (lax.broadcasted_iota(jnp.int32, (1, blk_rows), 1)
              <= jnp.where(r >= blk_rows, r - blk_rows, r))
    off_a = pl.multiple_of(2 * i * blk_rows, blk_rows)
    s_a = jnp.where(causal, _nt_dot(q2[:rows_ab], k_ref[0, pl.ds(off_a, blk_rows), :]), NEG_BIG)
    s_b, v = biased_logits(qa[rows_ab:], 2 * i, 1)
    _online_update(jnp.concatenate([s_a, s_b], axis=0), v, m_ref, l_ref, acc_ref)
    off_b = pl.multiple_of((2 * i + 1) * blk_rows, blk_rows)
    s_b = jnp.where(causal, _nt_dot(q2[rows_ab:], k_ref[0, pl.ds(off_b, blk_rows), :]), NEG_BIG)
    _online_update(s_b, v_ref[0, pl.ds(off_b, blk_rows), :], m_ref, l_ref, acc_ref, row0=rows_ab)

    o = _normalised(l_ref, acc_ref)
    lo = lane < HEAD_DIM
    o_ref[0, :blk_rows] = jnp.where(lo, o[:blk_rows], o[blk_rows:rows_ab]).astype(o_ref.dtype)
    o_ref[0, blk_rows:] = jnp.where(
        lo, o[rows_ab:rows_ab + blk_rows], o[rows_ab + blk_rows:]).astype(o_ref.dtype)


def _moba_attn(cq, ck, cv, kmean):
    B, S, C = cq.shape
    nb = S // MOBA_BLOCK
    tq = 2 * MOBA_BLOCK
    return pl.pallas_call(
        functools.partial(_moba_kernel, nb=nb),
        grid=(B, C // LANES, S // tq),
        in_specs=[pl.BlockSpec((1, tq, LANES), lambda b, p, i: (b, i, p)),
                  pl.BlockSpec((1, S, LANES), lambda b, p, i: (b, 0, p)),
                  pl.BlockSpec((1, S, LANES), lambda b, p, i: (b, 0, p)),
                  pl.BlockSpec((1, nb, LANES), lambda b, p, i: (b, 0, p))],
        out_specs=pl.BlockSpec((1, tq, LANES), lambda b, p, i: (b, i, p)),
        out_shape=jax.ShapeDtypeStruct(cq.shape, BF16),
        scratch_shapes=[pltpu.VMEM((2 * tq, LANES), F32), pltpu.VMEM((2 * tq, LANES), F32),
                        pltpu.VMEM((2 * tq, LANES), F32)],
        compiler_params=_cparams(("parallel", "parallel", "arbitrary")),
        name="moba_attn",
    )(cq, ck, cv, kmean)


def _sigmoid(x):
    return 1.0 / (1.0 + jnp.exp(-x))


def _merge_kernel(ya_ref, yb_ref, yc_ref, gates_ref, x_ref, wb_ref, wo_ref, g_ref, o_ref):
    d = x_ref.shape[2]
    merged = None
    for n, y_ref in enumerate((ya_ref, yb_ref, yc_ref)):
        br = jnp.dot(y_ref[0], wb_ref[n], preferred_element_type=F32)
        term = _sigmoid(gates_ref[0, :, n * d:(n + 1) * d].astype(F32)) * br
        merged = term if merged is None else merged + term
    m = jnp.dot(merged.astype(BF16), wo_ref[...], preferred_element_type=F32)
    o_ref[0] = x_ref[0] + _rms(m, g_ref[...])


def _merge(ya, yb, yc, gates, x, wb, wo, layer, g, tm):
    B, S, D = x.shape
    row = lambda width: pl.BlockSpec((1, tm, width), lambda b, m: (b, m, 0))
    return pl.pallas_call(
        _merge_kernel,
        grid=(B, S // tm),
        in_specs=[row(BRANCH_WIDTH), row(BRANCH_WIDTH), row(BRANCH_WIDTH), row(N_BRANCH * D), row(D),
                  _resident((None,) + wb.shape[1:], lambda b, m: (layer, 0, 0, 0)),
                  _resident((None,) + wo.shape[1:], lambda b, m: (layer, 0, 0)),
                  _resident((1, D), lambda b, m: (0, 0))],
        out_specs=row(D),
        out_shape=jax.ShapeDtypeStruct(x.shape, F32),
        compiler_params=_cparams(("parallel", "parallel")),
        name="merge_out",
    )(ya, yb, yc, gates, x, wb, wo, g.reshape(1, D))


def _ffn_kernel(x_ref, g_in_ref, wi_ref, wo_ref, g_out_ref, o_ref, acc_ref, *, d_ff, tf):
    x = x_ref[0]
    h = _rms(x, g_in_ref[...]).astype(BF16)
    acc_ref[...] = jnp.zeros(acc_ref.shape, F32)

    def body(c, carry):
        off = pl.multiple_of(c * tf, tf)
        gt = jnp.dot(h, wi_ref[:, pl.ds(off, tf)], preferred_element_type=F32)
        up_off = pl.multiple_of(d_ff + off, LANES)
        up = jnp.dot(h, wi_ref[:, pl.ds(up_off, tf)], preferred_element_type=F32)
        act = (gt * _sigmoid(gt) * up).astype(BF16)
        acc_ref[...] += jnp.dot(act, wo_ref[pl.ds(off, tf), :], preferred_element_type=F32)
        return carry

    lax.fori_loop(0, d_ff // tf, body, 0)
    o_ref[0] = x + _rms(acc_ref[...], g_out_ref[...])


def _ffn(x, g_in, wi, wo, layer, g_out, tm, tf):
    B, S, D = x.shape
    d_ff = wo.shape[1]
    row = pl.BlockSpec((1, tm, D), lambda b, m: (b, m, 0))
    vec = _resident((1, D), lambda b, m: (0, 0))
    return pl.pallas_call(
        functools.partial(_ffn_kernel, d_ff=d_ff, tf=tf),
        grid=(B, S // tm),
        in_specs=[row, vec, _resident((None,) + wi.shape[1:], lambda b, m: (layer, 0, 0)),
                  _resident((None,) + wo.shape[1:], lambda b, m: (layer, 0, 0)), vec],
        out_specs=row,
        out_shape=jax.ShapeDtypeStruct(x.shape, F32),
        scratch_shapes=[pltpu.VMEM((tm, D), F32)],
        compiler_params=_cparams(("parallel", "parallel")),
        name="swiglu_ffn",
    )(x, g_in.reshape(1, D), wi, wo, g_out.reshape(1, D))


def kernel(x, positions, w_in, w_branch, w_out, lambda_q1, lambda_k1, lambda_q2, lambda_k2,
           subln_g, norm_g, w_ffn_in, w_ffn_out):
    B, S, D = x.shape
    depth = w_in.shape[0]
    tm = min(512, S)
    tab = _rope_table(positions, tm)
    assert w_in.shape[2] == B_START + B_COLS
    w_a, w_b = w_in[..., :A_COLS].astype(BF16), w_in[..., B_START:].astype(BF16)
    wb, wo = w_branch.astype(BF16), w_out.astype(BF16)
    wi_ffn, wo_ffn = w_ffn_in.astype(BF16), w_ffn_out.astype(BF16)
    for l in range(depth):
        lam_init = 0.8 - 0.6 * math.exp(-0.3 * l)
        (aq, ak, av, bq, k2, v2, iq, ik4, iw, cq, ck, cv, gates, kmean) = _inproj(
            x, norm_g[l, 0], tab, w_a, w_b, l, tm)
        kmean = kmean.reshape(B, S // MOBA_BLOCK, 512)
        lam_rows = jnp.stack([lambda_q1[l], lambda_k1[l], lambda_q2[l], lambda_k2[l]])
        ya = _diff_attn(aq, ak, av, lam_rows, subln_g[l], lam_init, t=min(1024, S))
        yb = _dsa_attn(bq, iq, iw, ik4, k2, v2, tq=256, tk=min(512, S))
        yc = _moba_attn(cq, ck, cv, kmean)
        x = _merge(ya, yb, yc, gates, x, wb, wo, l, norm_g[l, 1], tm)
        x = _ffn(x, norm_g[l, 2], wi_ffn, wo_ffn, l, norm_g[l, 3], tm, tf=256)
    return x
```

```python
import functools
import math

import jax
import jax.numpy as jnp
from jax import lax
from jax.experimental import pallas as pl
from jax.experimental.pallas import tpu as pltpu

F32 = jnp.float32
BF16 = jnp.bfloat16

LANES = 128
HEAD_DIM = 64
ROPE_THETA = 500000.0
NORM_EPS = 1e-6
IDX_HEADS = 4
IDX_DIM = 32
IDX_TOPK_MAX = 256
MOBA_BLOCK = 256
MOBA_TOPK = 3
N_BRANCH = 3
BRANCH_WIDTH = 512
NEG_BIG = -1e30
LOG2E = math.log2(math.e)
ROW_BLOCK = 64
INT_MIN = -2 ** 31
VMEM_LIMIT = 56 * 1024 * 1024

A_AQ, A_AK, A_AV, A_BQ, A_KV, A_IQ, A_IKW, A_COLS = 0, 512, 1024, 1536, 2048, 2176, 2304, 2432
B_START = 2340
B_CQ, B_CK, B_CV, B_G, B_COLS = 0, 512, 1024, 1536, 4608
Q_SCALE = HEAD_DIM ** -0.5 * LOG2E


def _nt_dot(a, b):
    return lax.dot_general(a, b, (((1,), (1,)), ((), ())), preferred_element_type=F32)


def _rms(x, g):
    return x * lax.rsqrt(jnp.mean(x * x, axis=-1, keepdims=True) + NORM_EPS) * g


def _cparams(sem):
    return pltpu.CompilerParams(dimension_semantics=sem, vmem_limit_bytes=VMEM_LIMIT)


def _resident(shape, index_map):
    return pl.BlockSpec(shape, index_map, pipeline_mode=pl.Buffered(1))


def _rope_table_kernel(pos_ref, tab_ref):
    pos = pos_ref[0].astype(F32)
    lane = lax.broadcasted_iota(jnp.int32, (1, LANES), 1)

    def tables(head_dim):
        rot = head_dim // 4
        half = rot // 2
        d = lane & (head_dim - 1)
        fi = d & (half - 1)
        inv = jnp.zeros((1, LANES), F32)
        for i in range(half):
            inv = jnp.where(fi == i, F32(ROPE_THETA ** (-(2.0 * i) / rot)), inv)
        ang = pos * inv
        cos = jnp.cos(ang)
        sin = jnp.sin(ang)
        c = jnp.where(d < rot, cos, 1.0)
        s = jnp.where(d < half, -sin, jnp.where(d < rot, sin, 0.0))
        return c, s

    c64, s64 = tables(HEAD_DIM)
    c32, s32 = tables(IDX_DIM)
    tab_ref[0, :, 0 * LANES:1 * LANES] = c64
    tab_ref[0, :, 1 * LANES:2 * LANES] = s64
    tab_ref[0, :, 2 * LANES:3 * LANES] = c32
    tab_ref[0, :, 3 * LANES:4 * LANES] = s32


def _rope_table(positions, tm):
    B, S = positions.shape
    return pl.pallas_call(
        _rope_table_kernel,
        grid=(B, S // tm),
        in_specs=[pl.BlockSpec((1, tm, 1), lambda b, m: (b, m, 0))],
        out_specs=pl.BlockSpec((1, tm, 4 * LANES), lambda b, m: (b, m, 0)),
        out_shape=jax.ShapeDtypeStruct((B, S, 4 * LANES), F32),
        compiler_params=_cparams(("parallel", "parallel")),
        name="rope_table",
    )(positions.reshape(B, S, 1))


def _inproj_kernel(x_ref, g_ref, tab_ref, wa_ref, wb_ref,
                   aq_ref, ak_ref, av_ref, bq_ref, k2_ref, v2_ref, iq_ref, ik4_ref, iw_ref,
                   cq_ref, ck_ref, cv_ref, gates_ref, kmean_ref):
    h = _rms(x_ref[0], g_ref[...]).astype(BF16)
    lane = lax.broadcasted_iota(jnp.int32, (1, LANES), 1)

    def proj(w_ref, off, width):
        return jnp.dot(h, w_ref[:, off:off + width], preferred_element_type=F32)

    def rope(y, head_dim):
        half = head_dim // 8
        t0 = 0 if head_dim == HEAD_DIM else 2 * LANES
        c = tab_ref[0, :, t0:t0 + LANES]
        s = tab_ref[0, :, t0 + LANES:t0 + 2 * LANES]
        first = (lane & (head_dim - 1)) < half
        blocks = []
        for j in range(y.shape[1] // LANES):
            yj = y[:, j * LANES:(j + 1) * LANES]
            up = pltpu.roll(yj, LANES - half, 1)
            dn = pltpu.roll(yj, half, 1)
            blocks.append(yj * c + jnp.where(first, up, dn) * s)
        return blocks

    def store(out_ref, blocks):
        for j, r in enumerate(blocks):
            out_ref[0, :, j * LANES:(j + 1) * LANES] = r.astype(out_ref.dtype)

    store(aq_ref, rope(proj(wa_ref, A_AQ, 512) * Q_SCALE, HEAD_DIM))
    store(ak_ref, rope(proj(wa_ref, A_AK, 512), HEAD_DIM))
    av_ref[0] = proj(wa_ref, A_AV, 512).astype(av_ref.dtype)
    store(bq_ref, rope(proj(wa_ref, A_BQ, 512) * Q_SCALE, HEAD_DIM))
    kv = proj(wa_ref, A_KV, LANES)
    vk = pltpu.roll(kv, HEAD_DIM, 1)
    lo = lane < HEAD_DIM
    store(k2_ref, rope(jnp.where(lo, kv, vk), HEAD_DIM))
    v2_ref[0] = jnp.where(lo, vk, kv).astype(v2_ref.dtype)
    store(iq_ref, rope(proj(wa_ref, A_IQ, LANES), IDX_DIM))
    ikw = proj(wa_ref, A_IKW, LANES)
    ik = rope(ikw, IDX_DIM)[0]
    ik4 = ik
    for rep in range(1, LANES // IDX_DIM):
        ik4 = jnp.where(lane < rep * IDX_DIM, ik4, pltpu.roll(ik, rep * IDX_DIM, 1))
    ik4_ref[0] = ik4.astype(ik4_ref.dtype)
    iw_ref[0] = jnp.where(lane < IDX_HEADS, pltpu.roll(ikw, LANES - IDX_DIM, 1), 0.0)
    store(cq_ref, rope(proj(wb_ref, B_CQ, 512) * Q_SCALE, HEAD_DIM))
    ck_blocks = rope(proj(wb_ref, B_CK, 512), HEAD_DIM)
    store(ck_ref, ck_blocks)
    cv_ref[0] = proj(wb_ref, B_CV, 512).astype(cv_ref.dtype)
    gates_ref[0] = proj(wb_ref, B_G, 3 * 1024).astype(gates_ref.dtype)

    tm = x_ref.shape[1]
    for blk in range(tm // MOBA_BLOCK):
        for j, r in enumerate(ck_blocks):
            part = r[blk * MOBA_BLOCK:(blk + 1) * MOBA_BLOCK]
            kmean_ref[0, 0, blk:blk + 1, j * LANES:(j + 1) * LANES] = (
                jnp.mean(part, axis=0, keepdims=True))


def _inproj(x, g, tab, w_a, w_b, layer, tm):
    B, S, D = x.shape
    nm = S // tm
    row = lambda width: pl.BlockSpec((1, tm, width), lambda b, m: (b, m, 0))
    shp = lambda width, dt: jax.ShapeDtypeStruct((B, S, width), dt)
    widths = [(512, BF16)] * 4 + [(LANES, BF16)] * 4 + [(LANES, F32)] + [(512, BF16)] * 3 + [(3072, BF16)]
    out_specs = [row(w) for w, _ in widths]
    out_shape = [shp(w, dt) for w, dt in widths]
    out_specs.append(pl.BlockSpec((1, 1, tm // MOBA_BLOCK, 512), lambda b, m: (b, m, 0, 0)))
    out_shape.append(jax.ShapeDtypeStruct((B, nm, tm // MOBA_BLOCK, 512), F32))
    return pl.pallas_call(
        _inproj_kernel,
        grid=(B, nm),
        in_specs=[row(D),
                  _resident((1, D), lambda b, m: (0, 0)),
                  row(4 * LANES),
                  _resident((None, D, A_COLS), lambda b, m: (layer, 0, 0)),
                  _resident((None, D, B_COLS), lambda b, m: (layer, 0, 0))],
        out_specs=out_specs,
        out_shape=out_shape,
        compiler_params=_cparams(("parallel", "parallel")),
        name="inproj",
    )(x, g.reshape(1, D), tab, w_a, w_b)


def _split_heads_rows(q):
    lo = lax.broadcasted_iota(jnp.int32, (1, LANES), 1) < HEAD_DIM
    zero = jnp.zeros_like(q)
    return jnp.concatenate([jnp.where(lo, q, zero), jnp.where(lo, zero, q)], axis=0)


def _lane_blocks(s):
    return [s[:, j * LANES:(j + 1) * LANES] for j in range(s.shape[1] // LANES)]


def _online_update(s, v, m_ref, l_ref, acc_ref, bias=None, row0=0):
    rows_total = s.shape[0]
    nblk = s.shape[1] // LANES
    p_rows, alphas = [], []
    for r0 in range(0, rows_total, ROW_BLOCK):
        rows = slice(row0 + r0, row0 + r0 + ROW_BLOCK)
        cols = [s[r0:r0 + ROW_BLOCK, j * LANES:(j + 1) * LANES] for j in range(nblk)]
        if bias is not None:
            b0 = r0 % bias[0].shape[0]
            cols = [c + bj[b0:b0 + ROW_BLOCK] for c, bj in zip(cols, bias)]
        m_prev = m_ref[rows]
        m_new = jnp.maximum(
            m_prev, jnp.max(functools.reduce(jnp.maximum, cols), axis=1, keepdims=True))
        alpha = jnp.exp2(m_prev - m_new)
        ps = [jnp.exp2(c - m_new) for c in cols]
        l_ref[rows] = alpha * l_ref[rows] + functools.reduce(jnp.add, ps)
        m_ref[rows] = m_new
        p_rows.append(jnp.concatenate([pj.astype(BF16) for pj in ps], axis=1))
        alphas.append(alpha)
    p = jnp.concatenate(p_rows, axis=0)
    alpha = jnp.concatenate(alphas, axis=0)
    span = slice(row0, row0 + rows_total)
    acc_ref[span] = alpha * acc_ref[span] + jnp.dot(p, v, preferred_element_type=F32)


def _normalised(l_ref, acc_ref):
    return acc_ref[...] / jnp.sum(l_ref[...], axis=1, keepdims=True)


def _diff_attn_kernel(q_ref, k_ref, v_ref, lam_ref, g_ref, o_ref, m_ref, l_ref, acc_ref,
                      *, t, lam_init):
    qi = pl.program_id(2)
    half = t // 2
    q = q_ref[0]
    q2 = jnp.concatenate([_split_heads_rows(q[:half]), _split_heads_rows(q[half:])], axis=0)
    m_ref[...] = jnp.full(m_ref.shape, NEG_BIG, F32)
    l_ref[...] = jnp.zeros(l_ref.shape, F32)
    acc_ref[...] = jnp.zeros(acc_ref.shape, F32)

    def body(c, carry):
        off = pl.multiple_of(c * t, t)
        _online_update(_nt_dot(q2, k_ref[0, pl.ds(off, t), :]), v_ref[0, pl.ds(off, t), :],
                       m_ref, l_ref, acc_ref)
        return carry

    lax.fori_loop(0, qi, body, 0)

    r = lax.broadcasted_iota(jnp.int32, (t, 1), 0)
    causal = lax.broadcasted_iota(jnp.int32, (1, half), 1) <= jnp.where(r >= half, r - half, r)
    off_a = pl.multiple_of(qi * t, t)
    s = _nt_dot(q2, k_ref[0, pl.ds(off_a, half), :])
    s = jnp.concatenate([jnp.where(causal, s[:t], NEG_BIG), s[t:]], axis=0)
    _online_update(s, v_ref[0, pl.ds(off_a, half), :], m_ref, l_ref, acc_ref)
    off_b = pl.multiple_of(qi * t + half, half)
    s = jnp.where(causal, _nt_dot(q2[t:], k_ref[0, pl.ds(off_b, half), :]), NEG_BIG)
    _online_update(s, v_ref[0, pl.ds(off_b, half), :], m_ref, l_ref, acc_ref, row0=t)

    lam_rows = lam_ref[...]
    e1 = jnp.exp(jnp.sum(lam_rows[0:1] * lam_rows[1:2], axis=1, keepdims=True))
    e2 = jnp.exp(jnp.sum(lam_rows[2:3] * lam_rows[3:4], axis=1, keepdims=True))
    lam = e1 - e2 + lam_init
    o = _normalised(l_ref, acc_ref)
    o1 = jnp.concatenate([o[:half], o[t:t + half]], axis=0)
    o2 = jnp.concatenate([o[half:t], o[t + half:]], axis=0)
    o_ref[0] = (_rms(o1 - lam * o2, g_ref[...]) * (1.0 - lam_init)).astype(o_ref.dtype)


def _diff_attn(aq, ak, av, lam_rows, subln_g, lam_init, t):
    B, S, _ = aq.shape
    nh = aq.shape[2] // LANES
    kern = functools.partial(_diff_attn_kernel, t=t, lam_init=lam_init)
    return pl.pallas_call(
        kern,
        grid=(B, nh, S // t),
        in_specs=[pl.BlockSpec((1, t, LANES), lambda b, h, i: (b, i, h)),
                  pl.BlockSpec((1, S, LANES), lambda b, h, i: (b, 0, h)),
                  pl.BlockSpec((1, S, LANES), lambda b, h, i: (b, 0, h)),
                  pl.BlockSpec((4, HEAD_DIM), lambda b, h, i: (0, 0)),
                  pl.BlockSpec((1, LANES), lambda b, h, i: (0, 0))],
        out_specs=pl.BlockSpec((1, t, LANES), lambda b, h, i: (b, i, h)),
        out_shape=jax.ShapeDtypeStruct(aq.shape, BF16),
        scratch_shapes=[pltpu.VMEM((2 * t, LANES), F32), pltpu.VMEM((2 * t, LANES), F32),
                        pltpu.VMEM((2 * t, LANES), F32)],
        compiler_params=_cparams(("parallel", "parallel", "arbitrary")),
        name="diff_attn",
    )(aq, ak, av, lam_rows, subln_g.reshape(1, LANES))


BIT_GROUP = 256


def _bit_transpose32(words):
    w = list(words)
    for j, m in ((16, 0x0000FFFF), (8, 0x00FF00FF), (4, 0x0F0F0F0F), (2, 0x33333333), (1, 0x55555555)):
        for k in range(32):
            if k & j == 0:
                t = (w[k] ^ (w[k + j] >> j)) & m
                w[k] = w[k] ^ t
                w[k + j] = w[k + j] ^ (t << j)
    return w

def _dsa_kernel(q_ref, iq_ref, iw_ref, ik4_ref, k2_ref, v2_ref, o_ref,
                keys_ref, planes_ref, act_ref, jcut_ref, m_ref, l_ref, acc_ref,
                *, tq, tk, topk, idx_bits):
    qi = pl.program_id(1)
    nkv = (qi * tq + tq - 1) // tk + 1
    lane = lax.broadcasted_iota(jnp.int32, (1, LANES), 1)
    qpos = qi * tq + lax.broadcasted_iota(jnp.int32, (tq, 1), 0)
    lane_tk = lax.broadcasted_iota(jnp.int32, (1, tk), 1)
    nsub = tk // LANES
    seq = keys_ref.shape[1]

    @pl.when(qi == 0)
    def _():
        planes_ref[...] = jnp.zeros(planes_ref.shape, jnp.int32)

    iq = iq_ref[0]
    zero = jnp.zeros_like(iq)
    iq4 = jnp.concatenate(
        [jnp.where((lane >> (IDX_DIM.bit_length() - 1)) == hh, iq, zero) for hh in range(IDX_HEADS)],
        axis=0)
    iw = iw_ref[0]
    iw_cols = [iw[:, hh:hh + 1] for hh in range(IDX_HEADS)]

    def score_body(c, carry):
        off = pl.multiple_of(c * tk, tk)
        rel = jnp.maximum(_nt_dot(iq4, ik4_ref[0, pl.ds(off, tk), :]), 0.0)
        score = iw_cols[0] * rel[0:tq]
        for hh in range(1, IDX_HEADS):
            score = score + iw_cols[hh] * rel[hh * tq:(hh + 1) * tq]
        bits = lax.bitcast_convert_type(score, jnp.int32)
        key = bits ^ ((bits >> 31) & jnp.int32(0x7FFFFFFF))
        kpos = off + lane_tk
        key = jnp.where(score == 0.0, (seq - 1) - kpos, jnp.where(key > 0, key + seq, key))
        key = jnp.where(kpos <= qpos, key, jnp.int32(INT_MIN))
        keys_ref[:, pl.ds(off, tk)] = key
        kts = [key[:, j * LANES:(j + 1) * LANES].T for j in range(nsub)]
        for gl in range(tk // BIT_GROUP):
            words = [kts[(gl * BIT_GROUP + 8 * g) // LANES][(8 * g) % LANES:(8 * g) % LANES + 8]
                     for g in range(32)]
            words = _bit_transpose32(words)
            words[0] = ~words[0]
            for b in range(32):
                planes_ref[c * (tk // BIT_GROUP) + gl, b] = words[b]
        return carry

    lax.fori_loop(0, nkv, score_body, 0)

    ngroups = planes_ref.shape[0]
    live = nkv * (tk // BIT_GROUP)
    for g in range(ngroups):
        act_ref[g] = jnp.broadcast_to(jnp.where(g < live, jnp.int32(-1), jnp.int32(0)), (8, tq))

    def sweep(i, keep, first=False):
        parts = [jnp.zeros((8, tq), jnp.int32) for _ in range(4)]
        for g in range(ngroups):
            act = act_ref[g]
            if not first:
                act = act & (planes_ref[g, i - 1] ^ keep)
                act_ref[g] = act
            parts[g % 4] = parts[g % 4] + lax.population_count(act & planes_ref[g, i])
        cnt = (parts[0] + parts[1]) + (parts[2] + parts[3])
        return jnp.sum(cnt.astype(F32), axis=0, keepdims=True)

    def decide(i, ones, thr_u, n_gt):
        take = n_gt + ones >= topk
        thr_u = thr_u | jnp.where(take, jnp.left_shift(jnp.int32(1), 31 - i), 0)
        return thr_u, jnp.where(take, n_gt, n_gt + ones), jnp.where(take, 0, -1).astype(jnp.int32)

    def pass_body(i, carry):
        thr_u, n_gt, keep = carry
        return decide(i, sweep(i, keep), thr_u, n_gt)

    start = decide(0, sweep(0, None, first=True),
                   jnp.zeros((1, tq), jnp.int32), jnp.zeros((1, tq), F32))
    thr_u, n_gt, keep = lax.fori_loop(1, 32, pass_body, start)
    n_eq = jnp.zeros((8, tq), jnp.int32)
    for g in range(ngroups):
        n_eq = n_eq + lax.population_count(act_ref[g] & (planes_ref[g, 31] ^ keep))
    thr = thr_u ^ jnp.int32(INT_MIN)
    cnt_ge = n_gt + jnp.sum(n_eq.astype(F32), axis=0, keepdims=True)
    need = topk - n_gt

    def as_rows(x):
        return jnp.broadcast_to(x, (LANES, tq)).T

    thr_r = as_rows(thr)
    need_r = as_rows(need)
    jcut_ref[...] = jnp.full((tq, LANES), 2 ** 30, jnp.int32)

    @pl.when(jnp.max(cnt_ge) > topk)
    def _():
        def ties_before(cand):
            def body(c, cnt):
                off = pl.multiple_of(c * tk, tk)
                for j in range(nsub):
                    keyj = keys_ref[:, pl.ds(pl.multiple_of(off + j * LANES, LANES), LANES)]
                    hit = (keyj == thr_r) & (off + j * LANES + lane < cand)
                    cnt = jnp.where(hit, cnt + 1, cnt)
                return cnt
            cnt = lax.fori_loop(0, nkv, body, jnp.zeros((tq, LANES), jnp.int32))
            return jnp.sum(cnt.astype(F32), axis=1, keepdims=True)

        def idx_body(i, j):
            cand = j + jnp.left_shift(jnp.int32(1), idx_bits - 1 - i)
            return jnp.where(ties_before(cand) < need_r, cand, j)
        jcut_ref[...] = lax.fori_loop(0, idx_bits, idx_body, jnp.zeros((tq, LANES), jnp.int32))

    jcut_r = jcut_ref[...]

    q = q_ref[0]
    q8 = jnp.concatenate(
        [_split_heads_rows(q[:, p * LANES:(p + 1) * LANES]) for p in range(4)], axis=0)
    m_ref[...] = jnp.full(m_ref.shape, NEG_BIG, F32)
    l_ref[...] = jnp.zeros(l_ref.shape, F32)
    acc_ref[...] = jnp.zeros(acc_ref.shape, F32)

    def attn_body(c, carry):
        off = pl.multiple_of(c * tk, tk)
        s = _nt_dot(q8, k2_ref[0, pl.ds(off, tk), :])
        bias = []
        for j in range(nsub):
            keyj = keys_ref[:, pl.ds(pl.multiple_of(off + j * LANES, LANES), LANES)]
            kpos = off + j * LANES + lane
            sel = ((keyj > thr_r) | ((keyj == thr_r) & (kpos <= jcut_r))) & (kpos <= qpos)
            bias.append(jnp.where(sel, 0.0, NEG_BIG))
        _online_update(s, v2_ref[0, pl.ds(off, tk), :], m_ref, l_ref, acc_ref, bias=bias)
        return carry

    lax.fori_loop(0, nkv, attn_body, 0)

    o = _normalised(l_ref, acc_ref)
    lo = lane < HEAD_DIM
    for p in range(4):
        o_ref[0, :, p * LANES:(p + 1) * LANES] = jnp.where(
            lo, o[(2 * p) * tq:(2 * p + 1) * tq], o[(2 * p + 1) * tq:(2 * p + 2) * tq]
        ).astype(o_ref.dtype)


def _dsa_attn(bq, iq, iw, ik4, k2, v2, tq, tk):
    B, S, C = bq.shape
    topk = min(IDX_TOPK_MAX, S // 4)
    idx_bits = max(1, (S - 1).bit_length())
    kern = functools.partial(_dsa_kernel, tq=tq, tk=tk, topk=topk, idx_bits=idx_bits)
    row = lambda width: pl.BlockSpec((1, tq, width), lambda b, i: (b, i, 0))
    full = _resident((1, S, LANES), lambda b, i: (b, 0, 0))
    return pl.pallas_call(
        kern,
        grid=(B, S // tq),
        in_specs=[row(C), row(LANES), row(LANES), full, full, full],
        out_specs=row(C),
        out_shape=jax.ShapeDtypeStruct(bq.shape, BF16),
        scratch_shapes=[pltpu.VMEM((tq, S), jnp.int32),
                        pltpu.VMEM((S // BIT_GROUP, 32, 8, tq), jnp.int32),
                        pltpu.VMEM((S // BIT_GROUP, 8, tq), jnp.int32),
                        pltpu.VMEM((tq, LANES), jnp.int32),
                        pltpu.VMEM((8 * tq, LANES), F32), pltpu.VMEM((8 * tq, LANES), F32),
                        pltpu.VMEM((8 * tq, LANES), F32)],
        compiler_params=_cparams(("parallel", "arbitrary")),
        name="dsa_attn",
    )(bq, iq, iw, ik4, k2, v2)


def _moba_kernel(q_ref, k_ref, v_ref, km_ref, o_ref, m_ref, l_ref, acc_ref, *, nb):
    blk_rows = MOBA_BLOCK
    rows_ab = 2 * blk_rows
    i = pl.program_id(2)
    lane = lax.broadcasted_iota(jnp.int32, (1, LANES), 1)
    q = q_ref[0]
    q2 = jnp.concatenate([_split_heads_rows(q[:blk_rows]), _split_heads_rows(q[blk_rows:])], axis=0)

    km = jnp.concatenate([km_ref[0], jnp.zeros((LANES - nb, LANES), F32)], axis=0)
    km_hi = km.astype(BF16)
    r1 = km - km_hi.astype(F32)
    km_mid = r1.astype(BF16)
    km_lo = (r1 - km_mid.astype(F32)).astype(BF16)
    nb8 = -(-nb // 8) * 8
    gate_t = (_nt_dot(km_hi, q2) + _nt_dot(km_mid, q2) + _nt_dot(km_lo, q2))[:nb8]
    blk = lax.broadcasted_iota(jnp.int32, (nb8, 1), 0)
    blk_f = blk.astype(F32)
    col = lax.broadcasted_iota(jnp.int32, (1, 2 * rows_ab), 1)
    past = blk < 2 * i + jnp.where(col >= rows_ab, 1, 0)
    g = jnp.where(past, gate_t, -jnp.inf)
    sel = jnp.zeros(g.shape, jnp.bool_)
    for _ in range(min(MOBA_TOPK, nb - 1)):
        mx = jnp.max(g, axis=0, keepdims=True)
        first = jnp.min(jnp.where(g == mx, blk_f, float(LANES)), axis=0, keepdims=True)
        pick = blk_f == first
        sel = sel | pick
        g = jnp.where(pick, -jnp.inf, g)
    bias_t = jnp.concatenate([jnp.where(sel & past, 0.0, NEG_BIG),
                              jnp.full((LANES - nb8, 2 * rows_ab), NEG_BIG, F32)], axis=0)
    qa = jnp.concatenate([q2, bias_t.T.astype(BF16)], axis=1)

    m_ref[...] = jnp.full(m_ref.shape, NEG_BIG, F32)
    l_ref[...] = jnp.zeros(l_ref.shape, F32)
    acc_ref[...] = jnp.zeros(acc_ref.shape, F32)

    def biased_logits(lhs, first_blk, nblk):
        n = nblk * blk_rows
        off = pl.multiple_of(first_blk * blk_rows, blk_rows)
        row_blk = lax.broadcasted_iota(jnp.int32, (n, 1), 0) >> (MOBA_BLOCK.bit_length() - 1)
        onehot = jnp.where(lane == first_blk + row_blk, 1.0, 0.0).astype(BF16)
        ka = jnp.concatenate([k_ref[0, pl.ds(off, n), :], onehot], axis=1)
        return _nt_dot(lhs, ka), v_ref[0, pl.ds(off, n), :]

    def body(c, carry):
        s, v = biased_logits(qa, 4 * c, 4)
        _online_update(s, v, m_ref, l_ref, acc_ref)
        return carry

    lax.fori_loop(0, i // 2, body, 0)

    @pl.when(i % 2 == 1)
    def _():
        s, v = biased_logits(qa, 2 * i - 2, 2)
        _online_update(s, v, m_ref, l_ref, acc_ref)

    r = lax.broadcasted_iota(jnp.int32, (rows_ab, 1), 0)
    causal = (lax.broadcasted_iota(jnp.int32, (1, blk_rows), 1)
              <= jnp.where(r >= blk_rows, r - blk_rows, r))
    off_a = pl.multiple_of(2 * i * blk_rows, blk_rows)
    s_a = jnp.where(causal, _nt_dot(q2[:rows_ab], k_ref[0, pl.ds(off_a, blk_rows), :]), NEG_BIG)
    s_b, v = biased_logits(qa[rows_ab:], 2 * i, 1)
    _online_update(jnp.concatenate([s_a, s_b], axis=0), v, m_ref, l_ref, acc_ref)
    off_b = pl.multiple_of((2 * i + 1) * blk_rows, blk_rows)
    s_b = jnp.where(causal, _nt_dot(q2[rows_ab:], k_ref[0, pl.ds(off_b, blk_rows), :]), NEG_BIG)
    _online_update(s_b, v_ref[0, pl.ds(off_b, blk_rows), :], m_ref, l_ref, acc_ref, row0=rows_ab)

    o = _normalised(l_ref, acc_ref)
    lo = lane < HEAD_DIM
    o_ref[0, :blk_rows] = jnp.where(lo, o[:blk_rows], o[blk_rows:rows_ab]).astype(o_ref.dtype)
    o_ref[0, blk_rows:] = jnp.where(
        lo, o[rows_ab:rows_ab + blk_rows], o[rows_ab + blk_rows:]).astype(o_ref.dtype)


def _moba_attn(cq, ck, cv, kmean):
    B, S, C = cq.shape
    nb = S // MOBA_BLOCK
    tq = 2 * MOBA_BLOCK
    return pl.pallas_call(
        functools.partial(_moba_kernel, nb=nb),
        grid=(B, C // LANES, S // tq),
        in_specs=[pl.BlockSpec((1, tq, LANES), lambda b, p, i: (b, i, p)),
                  pl.BlockSpec((1, S, LANES), lambda b, p, i: (b, 0, p)),
                  pl.BlockSpec((1, S, LANES), lambda b, p, i: (b, 0, p)),
                  pl.BlockSpec((1, nb, LANES), lambda b, p, i: (b, 0, p))],
        out_specs=pl.BlockSpec((1, tq, LANES), lambda b, p, i: (b, i, p)),
        out_shape=jax.ShapeDtypeStruct(cq.shape, BF16),
        scratch_shapes=[pltpu.VMEM((2 * tq, LANES), F32), pltpu.VMEM((2 * tq, LANES), F32),
                        pltpu.VMEM((2 * tq, LANES), F32)],
        compiler_params=_cparams(("parallel", "parallel", "arbitrary")),
        name="moba_attn",
    )(cq, ck, cv, kmean)


def _sigmoid(x):
    return 1.0 / (1.0 + jnp.exp(-x))


def _merge_kernel(ya_ref, yb_ref, yc_ref, gates_ref, x_ref, wb_ref, wo_ref, g_ref, o_ref):
    d = x_ref.shape[2]
    merged = None
    for n, y_ref in enumerate((ya_ref, yb_ref, yc_ref)):
        br = jnp.dot(y_ref[0], wb_ref[n], preferred_element_type=F32)
        term = _sigmoid(gates_ref[0, :, n * d:(n + 1) * d].astype(F32)) * br
        merged = term if merged is None else merged + term
    m = jnp.dot(merged.astype(BF16), wo_ref[...], preferred_element_type=F32)
    o_ref[0] = x_ref[0] + _rms(m, g_ref[...])


def _merge(ya, yb, yc, gates, x, wb, wo, layer, g, tm):
    B, S, D = x.shape
    row = lambda width: pl.BlockSpec((1, tm, width), lambda b, m: (b, m, 0))
    return pl.pallas_call(
        _merge_kernel,
        grid=(B, S // tm),
        in_specs=[row(BRANCH_WIDTH), row(BRANCH_WIDTH), row(BRANCH_WIDTH), row(N_BRANCH * D), row(D),
                  _resident((None,) + wb.shape[1:], lambda b, m: (layer, 0, 0, 0)),
                  _resident((None,) + wo.shape[1:], lambda b, m: (layer, 0, 0)),
                  _resident((1, D), lambda b, m: (0, 0))],
        out_specs=row(D),
        out_shape=jax.ShapeDtypeStruct(x.shape, F32),
        compiler_params=_cparams(("parallel", "parallel")),
        name="merge_out",
    )(ya, yb, yc, gates, x, wb, wo, g.reshape(1, D))


def _ffn_kernel(x_ref, g_in_ref, wi_ref, wo_ref, g_out_ref, o_ref, acc_ref, *, d_ff, tf):
    x = x_ref[0]
    h = _rms(x, g_in_ref[...]).astype(BF16)
    acc_ref[...] = jnp.zeros(acc_ref.shape, F32)

    def body(c, carry):
        off = pl.multiple_of(c * tf, tf)
        gt = jnp.dot(h, wi_ref[:, pl.ds(off, tf)], preferred_element_type=F32)
        up_off = pl.multiple_of(d_ff + off, LANES)
        up = jnp.dot(h, wi_ref[:, pl.ds(up_off, tf)], preferred_element_type=F32)
        act = (gt * _sigmoid(gt) * up).astype(BF16)
        acc_ref[...] += jnp.dot(act, wo_ref[pl.ds(off, tf), :], preferred_element_type=F32)
        return carry

    lax.fori_loop(0, d_ff // tf, body, 0)
    o_ref[0] = x + _rms(acc_ref[...], g_out_ref[...])


def _ffn(x, g_in, wi, wo, layer, g_out, tm, tf):
    B, S, D = x.shape
    d_ff = wo.shape[1]
    row = pl.BlockSpec((1, tm, D), lambda b, m: (b, m, 0))
    vec = _resident((1, D), lambda b, m: (0, 0))
    return pl.pallas_call(
        functools.partial(_ffn_kernel, d_ff=d_ff, tf=tf),
        grid=(B, S // tm),
        in_specs=[row, vec, _resident((None,) + wi.shape[1:], lambda b, m: (layer, 0, 0)),
                  _resident((None,) + wo.shape[1:], lambda b, m: (layer, 0, 0)), vec],
        out_specs=row,
        out_shape=jax.ShapeDtypeStruct(x.shape, F32),
        scratch_shapes=[pltpu.VMEM((tm, D), F32)],
        compiler_params=_cparams(("parallel", "parallel")),
        name="swiglu_ffn",
    )(x, g_in.reshape(1, D), wi, wo, g_out.reshape(1, D))


def kernel(x, positions, w_in, w_branch, w_out, lambda_q1, lambda_k1, lambda_q2, lambda_k2,
           subln_g, norm_g, w_ffn_in, w_ffn_out):
    B, S, D = x.shape
    depth = w_in.shape[0]
    tm = min(512, S)
    tab = _rope_table(positions, tm)
    assert w_in.shape[2] == B_START + B_COLS
    w_bf = w_in.astype(BF16)
    w_a, w_b = w_bf[..., :A_COLS], w_bf[..., B_START:]
    wb, wo = w_branch.astype(BF16), w_out.astype(BF16)
    wi_ffn, wo_ffn = w_ffn_in.astype(BF16), w_ffn_out.astype(BF16)
    for l in range(depth):
        lam_init = 0.8 - 0.6 * math.exp(-0.3 * l)
        (aq, ak, av, bq, k2, v2, iq, ik4, iw, cq, ck, cv, gates, kmean) = _inproj(
            x, norm_g[l, 0], tab, w_a, w_b, l, tm)
        kmean = kmean.reshape(B, S // MOBA_BLOCK, 512)
        lam_rows = jnp.stack([lambda_q1[l], lambda_k1[l], lambda_q2[l], lambda_k2[l]])
        ya = _diff_attn(aq, ak, av, lam_rows, subln_g[l], lam_init, t=min(1024, S))
        yb = _dsa_attn(bq, iq, iw, ik4, k2, v2, tq=256, tk=min(512, S))
        yc = _moba_attn(cq, ck, cv, kmean)
        x = _merge(ya, yb, yc, gates, x, wb, wo, l, norm_g[l, 1], tm)
        x = _ffn(x, norm_g[l, 2], wi_ffn, wo_ffn, l, norm_g[l, 3], tm, tf=256)
    return x
```

```python
import functools
import math

import jax
import jax.numpy as jnp
from jax import lax
from jax.experimental import pallas as pl
from jax.experimental.pallas import tpu as pltpu

F32 = jnp.float32
BF16 = jnp.bfloat16

LANES = 128
HEAD_DIM = 64
ROPE_THETA = 500000.0
NORM_EPS = 1e-6
IDX_HEADS = 4
IDX_DIM = 32
IDX_TOPK_MAX = 256
MOBA_BLOCK = 256
MOBA_TOPK = 3
N_BRANCH = 3
BRANCH_WIDTH = 512
NEG_BIG = -1e30
LOG2E = math.log2(math.e)
ROW_BLOCK = 64
INT_MIN = -2 ** 31
VMEM_LIMIT = 56 * 1024 * 1024

A_AQ, A_AK, A_AV, A_BQ, A_KV, A_IQ, A_IKW, A_COLS = 0, 512, 1024, 1536, 2048, 2176, 2304, 2432
B_START = 2340
B_CQ, B_CK, B_CV, B_G, B_COLS = 0, 512, 1024, 1536, 4608
Q_SCALE = HEAD_DIM ** -0.5 * LOG2E


def _nt_dot(a, b):
    return lax.dot_general(a, b, (((1,), (1,)), ((), ())), preferred_element_type=F32)


def _rms(x, g):
    return x * lax.rsqrt(jnp.mean(x * x, axis=-1, keepdims=True) + NORM_EPS) * g


def _cparams(sem):
    return pltpu.CompilerParams(dimension_semantics=sem, vmem_limit_bytes=VMEM_LIMIT)


def _resident(shape, index_map):
    return pl.BlockSpec(shape, index_map, pipeline_mode=pl.Buffered(1))


def _rope_table_kernel(pos_ref, tab_ref):
    pos = pos_ref[0].astype(F32)
    lane = lax.broadcasted_iota(jnp.int32, (1, LANES), 1)

    def tables(head_dim):
        rot = head_dim // 4
        half = rot // 2
        d = lane & (head_dim - 1)
        fi = d & (half - 1)
        inv = jnp.zeros((1, LANES), F32)
        for i in range(half):
            inv = jnp.where(fi == i, F32(ROPE_THETA ** (-(2.0 * i) / rot)), inv)
        ang = pos * inv
        cos = jnp.cos(ang)
        sin = jnp.sin(ang)
        c = jnp.where(d < rot, cos, 1.0)
        s = jnp.where(d < half, -sin, jnp.where(d < rot, sin, 0.0))
        return c, s

    c64, s64 = tables(HEAD_DIM)
    c32, s32 = tables(IDX_DIM)
    tab_ref[0, :, 0 * LANES:1 * LANES] = c64
    tab_ref[0, :, 1 * LANES:2 * LANES] = s64
    tab_ref[0, :, 2 * LANES:3 * LANES] = c32
    tab_ref[0, :, 3 * LANES:4 * LANES] = s32


def _rope_table(positions, tm):
    B, S = positions.shape
    return pl.pallas_call(
        _rope_table_kernel,
        grid=(B, S // tm),
        in_specs=[pl.BlockSpec((1, tm, 1), lambda b, m: (b, m, 0))],
        out_specs=pl.BlockSpec((1, tm, 4 * LANES), lambda b, m: (b, m, 0)),
        out_shape=jax.ShapeDtypeStruct((B, S, 4 * LANES), F32),
        compiler_params=_cparams(("parallel", "parallel")),
        name="rope_table",
    )(positions.reshape(B, S, 1))


def _inproj_kernel(x_ref, g_ref, tab_ref, wa_ref, wb_ref,
                   aq_ref, ak_ref, av_ref, bq_ref, k2_ref, v2_ref, iq_ref, ik4_ref, iw_ref,
                   cq_ref, ck_ref, cv_ref, gates_ref, kmean_ref):
    h = _rms(x_ref[0], g_ref[...]).astype(BF16)
    lane = lax.broadcasted_iota(jnp.int32, (1, LANES), 1)

    def proj(w_ref, off, width):
        return jnp.dot(h, w_ref[:, off:off + width], preferred_element_type=F32)

    def rope(y, head_dim):
        half = head_dim // 8
        t0 = 0 if head_dim == HEAD_DIM else 2 * LANES
        c = tab_ref[0, :, t0:t0 + LANES]
        s = tab_ref[0, :, t0 + LANES:t0 + 2 * LANES]
        first = (lane & (head_dim - 1)) < half
        blocks = []
        for j in range(y.shape[1] // LANES):
            yj = y[:, j * LANES:(j + 1) * LANES]
            up = pltpu.roll(yj, LANES - half, 1)
            dn = pltpu.roll(yj, half, 1)
            blocks.append(yj * c + jnp.where(first, up, dn) * s)
        return blocks

    def store(out_ref, blocks):
        for j, r in enumerate(blocks):
            out_ref[0, :, j * LANES:(j + 1) * LANES] = r.astype(out_ref.dtype)

    store(aq_ref, rope(proj(wa_ref, A_AQ, 512) * Q_SCALE, HEAD_DIM))
    store(ak_ref, rope(proj(wa_ref, A_AK, 512), HEAD_DIM))
    av_ref[0] = proj(wa_ref, A_AV, 512).astype(av_ref.dtype)
    store(bq_ref, rope(proj(wa_ref, A_BQ, 512) * Q_SCALE, HEAD_DIM))
    kv = proj(wa_ref, A_KV, LANES)
    vk = pltpu.roll(kv, HEAD_DIM, 1)
    lo = lane < HEAD_DIM
    store(k2_ref, rope(jnp.where(lo, kv, vk), HEAD_DIM))
    v2_ref[0] = jnp.where(lo, vk, kv).astype(v2_ref.dtype)
    store(iq_ref, rope(proj(wa_ref, A_IQ, LANES), IDX_DIM))
    ikw = proj(wa_ref, A_IKW, LANES)
    ik = rope(ikw, IDX_DIM)[0]
    ik4 = ik
    for rep in range(1, LANES // IDX_DIM):
        ik4 = jnp.where(lane < rep * IDX_DIM, ik4, pltpu.roll(ik, rep * IDX_DIM, 1))
    ik4_ref[0] = ik4.astype(ik4_ref.dtype)
    iw_ref[0] = jnp.where(lane < IDX_HEADS, pltpu.roll(ikw, LANES - IDX_DIM, 1), 0.0)
    store(cq_ref, rope(proj(wb_ref, B_CQ, 512) * Q_SCALE, HEAD_DIM))
    ck_blocks = rope(proj(wb_ref, B_CK, 512), HEAD_DIM)
    store(ck_ref, ck_blocks)
    cv_ref[0] = proj(wb_ref, B_CV, 512).astype(cv_ref.dtype)
    gates_ref[0] = proj(wb_ref, B_G, 3 * 1024).astype(gates_ref.dtype)

    tm = x_ref.shape[1]
    for blk in range(tm // MOBA_BLOCK):
        for j, r in enumerate(ck_blocks):
            part = r[blk * MOBA_BLOCK:(blk + 1) * MOBA_BLOCK]
            kmean_ref[0, 0, blk:blk + 1, j * LANES:(j + 1) * LANES] = (
                jnp.mean(part, axis=0, keepdims=True))


def _inproj(x, g, tab, w_a, w_b, layer, tm):
    B, S, D = x.shape
    nm = S // tm
    row = lambda width: pl.BlockSpec((1, tm, width), lambda b, m: (b, m, 0))
    shp = lambda width, dt: jax.ShapeDtypeStruct((B, S, width), dt)
    widths = [(512, BF16)] * 4 + [(LANES, BF16)] * 4 + [(LANES, F32)] + [(512, BF16)] * 3 + [(3072, BF16)]
    out_specs = [row(w) for w, _ in widths]
    out_shape = [shp(w, dt) for w, dt in widths]
    out_specs.append(pl.BlockSpec((1, 1, tm // MOBA_BLOCK, 512), lambda b, m: (b, m, 0, 0)))
    out_shape.append(jax.ShapeDtypeStruct((B, nm, tm // MOBA_BLOCK, 512), F32))
    return pl.pallas_call(
        _inproj_kernel,
        grid=(B, nm),
        in_specs=[row(D),
                  _resident((1, D), lambda b, m: (0, 0)),
                  row(4 * LANES),
                  _resident((None, D, A_COLS), lambda b, m: (layer, 0, 0)),
                  _resident((None, D, B_COLS), lambda b, m: (layer, 0, 0))],
        out_specs=out_specs,
        out_shape=out_shape,
        compiler_params=_cparams(("parallel", "parallel")),
        name="inproj",
    )(x, g.reshape(1, D), tab, w_a, w_b)


def _split_heads_rows(q):
    lo = lax.broadcasted_iota(jnp.int32, (1, LANES), 1) < HEAD_DIM
    zero = jnp.zeros_like(q)
    return jnp.concatenate([jnp.where(lo, q, zero), jnp.where(lo, zero, q)], axis=0)


def _lane_blocks(s):
    return [s[:, j * LANES:(j + 1) * LANES] for j in range(s.shape[1] // LANES)]


def _online_update(s, v, m_ref, l_ref, acc_ref, bias=None, row0=0):
    rows_total = s.shape[0]
    nblk = s.shape[1] // LANES
    p_rows, alphas = [], []
    for r0 in range(0, rows_total, ROW_BLOCK):
        rows = slice(row0 + r0, row0 + r0 + ROW_BLOCK)
        cols = [s[r0:r0 + ROW_BLOCK, j * LANES:(j + 1) * LANES] for j in range(nblk)]
        if bias is not None:
            b0 = r0 % bias[0].shape[0]
            cols = [c + bj[b0:b0 + ROW_BLOCK] for c, bj in zip(cols, bias)]
        m_prev = m_ref[rows]
        m_new = jnp.maximum(
            m_prev, jnp.max(functools.reduce(jnp.maximum, cols), axis=1, keepdims=True))
        alpha = jnp.exp2(m_prev - m_new)
        ps = [jnp.exp2(c - m_new) for c in cols]
        l_ref[rows] = alpha * l_ref[rows] + functools.reduce(jnp.add, ps)
        m_ref[rows] = m_new
        p_rows.append(jnp.concatenate([pj.astype(BF16) for pj in ps], axis=1))
        alphas.append(alpha)
    p = jnp.concatenate(p_rows, axis=0)
    alpha = jnp.concatenate(alphas, axis=0)
    span = slice(row0, row0 + rows_total)
    acc_ref[span] = alpha * acc_ref[span] + jnp.dot(p, v, preferred_element_type=F32)


def _normalised(l_ref, acc_ref):
    return acc_ref[...] / jnp.sum(l_ref[...], axis=1, keepdims=True)


def _diff_attn_kernel(q_ref, k_ref, v_ref, lam_ref, g_ref, o_ref, m_ref, l_ref, acc_ref,
                      *, t, lam_init):
    qi = pl.program_id(2)
    half = t // 2
    q = q_ref[0]
    q2 = jnp.concatenate([_split_heads_rows(q[:half]), _split_heads_rows(q[half:])], axis=0)
    m_ref[...] = jnp.full(m_ref.shape, NEG_BIG, F32)
    l_ref[...] = jnp.zeros(l_ref.shape, F32)
    acc_ref[...] = jnp.zeros(acc_ref.shape, F32)

    def body(c, carry):
        off = pl.multiple_of(c * t, t)
        _online_update(_nt_dot(q2, k_ref[0, pl.ds(off, t), :]), v_ref[0, pl.ds(off, t), :],
                       m_ref, l_ref, acc_ref)
        return carry

    lax.fori_loop(0, qi, body, 0)

    r = lax.broadcasted_iota(jnp.int32, (t, 1), 0)
    causal = lax.broadcasted_iota(jnp.int32, (1, half), 1) <= jnp.where(r >= half, r - half, r)
    off_a = pl.multiple_of(qi * t, t)
    s = _nt_dot(q2, k_ref[0, pl.ds(off_a, half), :])
    s = jnp.concatenate([jnp.where(causal, s[:t], NEG_BIG), s[t:]], axis=0)
    _online_update(s, v_ref[0, pl.ds(off_a, half), :], m_ref, l_ref, acc_ref)
    off_b = pl.multiple_of(qi * t + half, half)
    s = jnp.where(causal, _nt_dot(q2[t:], k_ref[0, pl.ds(off_b, half), :]), NEG_BIG)
    _online_update(s, v_ref[0, pl.ds(off_b, half), :], m_ref, l_ref, acc_ref, row0=t)

    lam_rows = lam_ref[...]
    e1 = jnp.exp(jnp.sum(lam_rows[0:1] * lam_rows[1:2], axis=1, keepdims=True))
    e2 = jnp.exp(jnp.sum(lam_rows[2:3] * lam_rows[3:4], axis=1, keepdims=True))
    lam = e1 - e2 + lam_init
    o = _normalised(l_ref, acc_ref)
    o1 = jnp.concatenate([o[:half], o[t:t + half]], axis=0)
    o2 = jnp.concatenate([o[half:t], o[t + half:]], axis=0)
    o_ref[0] = (_rms(o1 - lam * o2, g_ref[...]) * (1.0 - lam_init)).astype(o_ref.dtype)


def _diff_attn(aq, ak, av, lam_rows, subln_g, lam_init, t):
    B, S, _ = aq.shape
    nh = aq.shape[2] // LANES
    kern = functools.partial(_diff_attn_kernel, t=t, lam_init=lam_init)
    return pl.pallas_call(
        kern,
        grid=(B, nh, S // t),
        in_specs=[pl.BlockSpec((1, t, LANES), lambda b, h, i: (b, i, h)),
                  pl.BlockSpec((1, S, LANES), lambda b, h, i: (b, 0, h)),
                  pl.BlockSpec((1, S, LANES), lambda b, h, i: (b, 0, h)),
                  pl.BlockSpec((4, HEAD_DIM), lambda b, h, i: (0, 0)),
                  pl.BlockSpec((1, LANES), lambda b, h, i: (0, 0))],
        out_specs=pl.BlockSpec((1, t, LANES), lambda b, h, i: (b, i, h)),
        out_shape=jax.ShapeDtypeStruct(aq.shape, BF16),
        scratch_shapes=[pltpu.VMEM((2 * t, LANES), F32), pltpu.VMEM((2 * t, LANES), F32),
                        pltpu.VMEM((2 * t, LANES), F32)],
        compiler_params=_cparams(("parallel", "parallel", "arbitrary")),
        name="diff_attn",
    )(aq, ak, av, lam_rows, subln_g.reshape(1, LANES))


BIT_GROUP = 256


def _bit_transpose32(words):
    w = list(words)
    for j, m in ((16, 0x0000FFFF), (8, 0x00FF00FF), (4, 0x0F0F0F0F), (2, 0x33333333), (1, 0x55555555)):
        for k in range(32):
            if k & j == 0:
                t = (w[k] ^ (w[k + j] >> j)) & m
                w[k] = w[k] ^ t
                w[k + j] = w[k + j] ^ (t << j)
    return w

def _dsa_kernel(q_ref, iq_ref, iw_ref, ik4_ref, k2_ref, v2_ref, o_ref,
                keys_ref, planes_ref, act_ref, jcut_ref, m_ref, l_ref, acc_ref,
                *, tq, tk, topk, idx_bits):
    qi = pl.program_id(1)
    nkv = (qi * tq + tq - 1) // tk + 1
    lane = lax.broadcasted_iota(jnp.int32, (1, LANES), 1)
    qpos = qi * tq + lax.broadcasted_iota(jnp.int32, (tq, 1), 0)
    lane_tk = lax.broadcasted_iota(jnp.int32, (1, tk), 1)
    nsub = tk // LANES
    seq = keys_ref.shape[1]

    @pl.when(qi == 0)
    def _():
        planes_ref[...] = jnp.zeros(planes_ref.shape, jnp.int32)

    iq = iq_ref[0]
    zero = jnp.zeros_like(iq)
    iq4 = jnp.concatenate(
        [jnp.where((lane >> (IDX_DIM.bit_length() - 1)) == hh, iq, zero) for hh in range(IDX_HEADS)],
        axis=0)
    iw = iw_ref[0]
    iw_cols = [iw[:, hh:hh + 1] for hh in range(IDX_HEADS)]

    def score_body(c, carry):
        off = pl.multiple_of(c * tk, tk)
        rel = jnp.maximum(_nt_dot(iq4, ik4_ref[0, pl.ds(off, tk), :]), 0.0)
        score = iw_cols[0] * rel[0:tq]
        for hh in range(1, IDX_HEADS):
            score = score + iw_cols[hh] * rel[hh * tq:(hh + 1) * tq]
        bits = lax.bitcast_convert_type(score, jnp.int32)
        key = bits ^ ((bits >> 31) & jnp.int32(0x7FFFFFFF))
        kpos = off + lane_tk
        key = jnp.where(score == 0.0, (seq - 1) - kpos, jnp.where(key > 0, key + seq, key))
        key = jnp.where(kpos <= qpos, key, jnp.int32(INT_MIN))
        keys_ref[:, pl.ds(off, tk)] = key
        kts = [key[:, j * LANES:(j + 1) * LANES].T for j in range(nsub)]
        for gl in range(tk // BIT_GROUP):
            words = [kts[(gl * BIT_GROUP + 8 * g) // LANES][(8 * g) % LANES:(8 * g) % LANES + 8]
                     for g in range(32)]
            words = _bit_transpose32(words)
            words[0] = ~words[0]
            for b in range(32):
                planes_ref[c * (tk // BIT_GROUP) + gl, b] = words[b]
        return carry

    lax.fori_loop(0, nkv, score_body, 0)

    ngroups = planes_ref.shape[0]
    live = nkv * (tk // BIT_GROUP)
    for g in range(ngroups):
        act_ref[g] = jnp.broadcast_to(jnp.where(g < live, jnp.int32(-1), jnp.int32(0)), (8, tq))

    def sweep(i, keep, first=False):
        parts = [jnp.zeros((8, tq), jnp.int32) for _ in range(4)]
        for g in range(ngroups):
            act = act_ref[g]
            if not first:
                act = act & (planes_ref[g, i - 1] ^ keep)
                act_ref[g] = act
            parts[g % 4] = parts[g % 4] + lax.population_count(act & planes_ref[g, i])
        cnt = (parts[0] + parts[1]) + (parts[2] + parts[3])
        return jnp.sum(cnt.astype(F32), axis=0, keepdims=True)

    def decide(i, ones, thr_u, n_gt):
        take = n_gt + ones >= topk
        thr_u = thr_u | jnp.where(take, jnp.left_shift(jnp.int32(1), 31 - i), 0)
        return thr_u, jnp.where(take, n_gt, n_gt + ones), jnp.where(take, 0, -1).astype(jnp.int32)

    def pass_body(i, carry):
        thr_u, n_gt, keep = carry
        return decide(i, sweep(i, keep), thr_u, n_gt)

    start = decide(0, sweep(0, None, first=True),
                   jnp.zeros((1, tq), jnp.int32), jnp.zeros((1, tq), F32))
    thr_u, n_gt, keep = lax.fori_loop(1, 32, pass_body, start)
    n_eq = jnp.zeros((8, tq), jnp.int32)
    for g in range(ngroups):
        n_eq = n_eq + lax.population_count(act_ref[g] & (planes_ref[g, 31] ^ keep))
    thr = thr_u ^ jnp.int32(INT_MIN)
    cnt_ge = n_gt + jnp.sum(n_eq.astype(F32), axis=0, keepdims=True)
    need = topk - n_gt

    def as_rows(x):
        return jnp.broadcast_to(x, (LANES, tq)).T

    thr_r = as_rows(thr)
    need_r = as_rows(need)
    jcut_ref[...] = jnp.full((tq, LANES), 2 ** 30, jnp.int32)

    @pl.when(jnp.max(cnt_ge) > topk)
    def _():
        def ties_before(cand):
            def body(c, cnt):
                off = pl.multiple_of(c * tk, tk)
                for j in range(nsub):
                    keyj = keys_ref[:, pl.ds(pl.multiple_of(off + j * LANES, LANES), LANES)]
                    hit = (keyj == thr_r) & (off + j * LANES + lane < cand)
                    cnt = jnp.where(hit, cnt + 1, cnt)
                return cnt
            cnt = lax.fori_loop(0, nkv, body, jnp.zeros((tq, LANES), jnp.int32))
            return jnp.sum(cnt.astype(F32), axis=1, keepdims=True)

        def idx_body(i, j):
            cand = j + jnp.left_shift(jnp.int32(1), idx_bits - 1 - i)
            return jnp.where(ties_before(cand) < need_r, cand, j)
        jcut_ref[...] = lax.fori_loop(0, idx_bits, idx_body, jnp.zeros((tq, LANES), jnp.int32))

    jcut_r = jcut_ref[...]

    q = q_ref[0]
    q8 = jnp.concatenate(
        [_split_heads_rows(q[:, p * LANES:(p + 1) * LANES]) for p in range(4)], axis=0)
    m_ref[...] = jnp.full(m_ref.shape, NEG_BIG, F32)
    l_ref[...] = jnp.zeros(l_ref.shape, F32)
    acc_ref[...] = jnp.zeros(acc_ref.shape, F32)

    def attend(off, width):
        s = _nt_dot(q8, k2_ref[0, pl.ds(off, width), :])
        bias = []
        for j in range(width // LANES):
            keyj = keys_ref[:, pl.ds(pl.multiple_of(off + j * LANES, LANES), LANES)]
            kpos = off + j * LANES + lane
            sel = ((keyj > thr_r) | ((keyj == thr_r) & (kpos <= jcut_r))) & (kpos <= qpos)
            bias.append(jnp.where(sel, 0.0, NEG_BIG))
        _online_update(s, v2_ref[0, pl.ds(off, width), :], m_ref, l_ref, acc_ref, bias=bias)

    def attn_body(c, carry):
        attend(pl.multiple_of(c * tk, tk), tk)
        return carry

    lax.fori_loop(0, nkv, attn_body, 0)

    o = _normalised(l_ref, acc_ref)
    lo = lane < HEAD_DIM
    for p in range(4):
        o_ref[0, :, p * LANES:(p + 1) * LANES] = jnp.where(
            lo, o[(2 * p) * tq:(2 * p + 1) * tq], o[(2 * p + 1) * tq:(2 * p + 2) * tq]
        ).astype(o_ref.dtype)


def _dsa_attn(bq, iq, iw, ik4, k2, v2, tq, tk):
    B, S, C = bq.shape
    topk = min(IDX_TOPK_MAX, S // 4)
    idx_bits = max(1, (S - 1).bit_length())
    kern = functools.partial(_dsa_kernel, tq=tq, tk=tk, topk=topk, idx_bits=idx_bits)
    row = lambda width: pl.BlockSpec((1, tq, width), lambda b, i: (b, i, 0))
    full = _resident((1, S, LANES), lambda b, i: (b, 0, 0))
    return pl.pallas_call(
        kern,
        grid=(B, S // tq),
        in_specs=[row(C), row(LANES), row(LANES), full, full, full],
        out_specs=row(C),
        out_shape=jax.ShapeDtypeStruct(bq.shape, BF16),
        scratch_shapes=[pltpu.VMEM((tq, S), jnp.int32),
                        pltpu.VMEM((S // BIT_GROUP, 32, 8, tq), jnp.int32),
                        pltpu.VMEM((S // BIT_GROUP, 8, tq), jnp.int32),
                        pltpu.VMEM((tq, LANES), jnp.int32),
                        pltpu.VMEM((8 * tq, LANES), F32), pltpu.VMEM((8 * tq, LANES), F32),
                        pltpu.VMEM((8 * tq, LANES), F32)],
        compiler_params=_cparams(("parallel", "arbitrary")),
        name="dsa_attn",
    )(bq, iq, iw, ik4, k2, v2)


MOBA_TILE_BLOCKS = 4


def _moba_kernel(q_ref, k_ref, v_ref, km_ref, o_ref, m_ref, l_ref, acc_ref, *, nb):
    nqb = MOBA_TILE_BLOCKS
    blk_rows = MOBA_BLOCK
    grp = 2 * blk_rows
    i = pl.program_id(2)
    first = nqb * i
    lane = lax.broadcasted_iota(jnp.int32, (1, LANES), 1)
    q = q_ref[0]
    q2 = jnp.concatenate(
        [_split_heads_rows(q[j * blk_rows:(j + 1) * blk_rows]) for j in range(nqb)], axis=0)

    km = jnp.concatenate([km_ref[0], jnp.zeros((LANES - nb, LANES), F32)], axis=0)
    km_hi = km.astype(BF16)
    r1 = km - km_hi.astype(F32)
    km_mid = r1.astype(BF16)
    km_lo = (r1 - km_mid.astype(F32)).astype(BF16)
    nb8 = -(-nb // 8) * 8
    gate_t = (_nt_dot(km_hi, q2) + _nt_dot(km_mid, q2) + _nt_dot(km_lo, q2))[:nb8]
    blk = lax.broadcasted_iota(jnp.int32, (nb8, 1), 0)
    blk_f = blk.astype(F32)
    col = lax.broadcasted_iota(jnp.int32, (1, nqb * grp), 1)
    past = blk < first + (col >> (grp.bit_length() - 1))
    g = jnp.where(past, gate_t, -jnp.inf)
    sel = jnp.zeros(g.shape, jnp.bool_)
    for _ in range(min(MOBA_TOPK, nb - 1)):
        mx = jnp.max(g, axis=0, keepdims=True)
        pick = blk_f == jnp.min(jnp.where(g == mx, blk_f, float(LANES)), axis=0, keepdims=True)
        sel = sel | pick
        g = jnp.where(pick, -jnp.inf, g)
    bias_t = jnp.concatenate([jnp.where(sel & past, 0.0, NEG_BIG),
                              jnp.full((LANES - nb8, nqb * grp), NEG_BIG, F32)], axis=0)
    qa = jnp.concatenate([q2, bias_t.T.astype(BF16)], axis=1)

    m_ref[...] = jnp.full(m_ref.shape, NEG_BIG, F32)
    l_ref[...] = jnp.zeros(l_ref.shape, F32)
    acc_ref[...] = jnp.zeros(acc_ref.shape, F32)

    def biased_logits(lhs, first_blk, nblk):
        n = nblk * blk_rows
        off = pl.multiple_of(first_blk * blk_rows, blk_rows)
        row_blk = lax.broadcasted_iota(jnp.int32, (n, 1), 0) >> (MOBA_BLOCK.bit_length() - 1)
        onehot = jnp.where(lane == first_blk + row_blk, 1.0, 0.0).astype(BF16)
        ka = jnp.concatenate([k_ref[0, pl.ds(off, n), :], onehot], axis=1)
        return _nt_dot(lhs, ka), v_ref[0, pl.ds(off, n), :]

    def body(c, carry):
        s, v = biased_logits(qa, nqb * c, nqb)
        _online_update(s, v, m_ref, l_ref, acc_ref)
        return carry

    lax.fori_loop(0, i, body, 0)

    r = lax.broadcasted_iota(jnp.int32, (grp, 1), 0)
    causal = (lax.broadcasted_iota(jnp.int32, (1, blk_rows), 1)
              <= jnp.where(r >= blk_rows, r - blk_rows, r))
    for j in range(nqb):
        off = pl.multiple_of((first + j) * blk_rows, blk_rows)
        s = jnp.where(causal, _nt_dot(q2[j * grp:(j + 1) * grp], k_ref[0, pl.ds(off, blk_rows), :]),
                      NEG_BIG)
        if j + 1 < nqb:
            s_later, _ = biased_logits(qa[(j + 1) * grp:], first + j, 1)
            s = jnp.concatenate([s, s_later], axis=0)
        _online_update(s, v_ref[0, pl.ds(off, blk_rows), :], m_ref, l_ref, acc_ref, row0=j * grp)

    o = _normalised(l_ref, acc_ref)
    lo = lane < HEAD_DIM
    for j in range(nqb):
        o_ref[0, j * blk_rows:(j + 1) * blk_rows] = jnp.where(
            lo, o[j * grp:j * grp + blk_rows], o[j * grp + blk_rows:(j + 1) * grp]).astype(o_ref.dtype)


def _moba_attn(cq, ck, cv, kmean):
    B, S, C = cq.shape
    nb = S // MOBA_BLOCK
    tq = MOBA_TILE_BLOCKS * MOBA_BLOCK
    return pl.pallas_call(
        functools.partial(_moba_kernel, nb=nb),
        grid=(B, C // LANES, S // tq),
        in_specs=[pl.BlockSpec((1, tq, LANES), lambda b, p, i: (b, i, p)),
                  pl.BlockSpec((1, S, LANES), lambda b, p, i: (b, 0, p)),
                  pl.BlockSpec((1, S, LANES), lambda b, p, i: (b, 0, p)),
                  pl.BlockSpec((1, nb, LANES), lambda b, p, i: (b, 0, p))],
        out_specs=pl.BlockSpec((1, tq, LANES), lambda b, p, i: (b, i, p)),
        out_shape=jax.ShapeDtypeStruct(cq.shape, BF16),
        scratch_shapes=[pltpu.VMEM((2 * tq, LANES), F32), pltpu.VMEM((2 * tq, LANES), F32),
                        pltpu.VMEM((2 * tq, LANES), F32)],
        compiler_params=_cparams(("parallel", "parallel", "arbitrary")),
        name="moba_attn",
    )(cq, ck, cv, kmean)


def _sigmoid(x):
    return 1.0 / (1.0 + jnp.exp(-x))


def _merge_kernel(ya_ref, yb_ref, yc_ref, gates_ref, x_ref, wb_ref, wo_ref, g_ref, o_ref):
    d = x_ref.shape[2]
    merged = None
    for n, y_ref in enumerate((ya_ref, yb_ref, yc_ref)):
        br = jnp.dot(y_ref[0], wb_ref[n], preferred_element_type=F32)
        term = _sigmoid(gates_ref[0, :, n * d:(n + 1) * d].astype(F32)) * br
        merged = term if merged is None else merged + term
    m = jnp.dot(merged.astype(BF16), wo_ref[...], preferred_element_type=F32)
    o_ref[0] = x_ref[0] + _rms(m, g_ref[...])


def _merge(ya, yb, yc, gates, x, wb, wo, layer, g, tm):
    B, S, D = x.shape
    row = lambda width: pl.BlockSpec((1, tm, width), lambda b, m: (b, m, 0))
    return pl.pallas_call(
        _merge_kernel,
        grid=(B, S // tm),
        in_specs=[row(BRANCH_WIDTH), row(BRANCH_WIDTH), row(BRANCH_WIDTH), row(N_BRANCH * D), row(D),
                  _resident((None,) + wb.shape[1:], lambda b, m: (layer, 0, 0, 0)),
                  _resident((None,) + wo.shape[1:], lambda b, m: (layer, 0, 0)),
                  _resident((1, D), lambda b, m: (0, 0))],
        out_specs=row(D),
        out_shape=jax.ShapeDtypeStruct(x.shape, F32),
        compiler_params=_cparams(("parallel", "parallel")),
        name="merge_out",
    )(ya, yb, yc, gates, x, wb, wo, g.reshape(1, D))


def _ffn_kernel(x_ref, g_in_ref, wi_ref, wo_ref, g_out_ref, o_ref, acc_ref, *, d_ff, tf):
    x = x_ref[0]
    h = _rms(x, g_in_ref[...]).astype(BF16)
    acc_ref[...] = jnp.zeros(acc_ref.shape, F32)

    def body(c, carry):
        off = pl.multiple_of(c * tf, tf)
        gt = jnp.dot(h, wi_ref[:, pl.ds(off, tf)], preferred_element_type=F32)
        up_off = pl.multiple_of(d_ff + off, LANES)
        up = jnp.dot(h, wi_ref[:, pl.ds(up_off, tf)], preferred_element_type=F32)
        act = (gt * _sigmoid(gt) * up).astype(BF16)
        acc_ref[...] += jnp.dot(act, wo_ref[pl.ds(off, tf), :], preferred_element_type=F32)
        return carry

    lax.fori_loop(0, d_ff // tf, body, 0)
    o_ref[0] = x + _rms(acc_ref[...], g_out_ref[...])


def _ffn(x, g_in, wi, wo, layer, g_out, tm, tf):
    B, S, D = x.shape
    d_ff = wo.shape[1]
    row = pl.BlockSpec((1, tm, D), lambda b, m: (b, m, 0))
    vec = _resident((1, D), lambda b, m: (0, 0))
    return pl.pallas_call(
        functools.partial(_ffn_kernel, d_ff=d_ff, tf=tf),
        grid=(B, S // tm),
        in_specs=[row, vec, _resident((None,) + wi.shape[1:], lambda b, m: (layer, 0, 0)),
                  _resident((None,) + wo.shape[1:], lambda b, m: (layer, 0, 0)), vec],
        out_specs=row,
        out_shape=jax.ShapeDtypeStruct(x.shape, F32),
        scratch_shapes=[pltpu.VMEM((tm, D), F32)],
        compiler_params=_cparams(("parallel", "parallel")),
        name="swiglu_ffn",
    )(x, g_in.reshape(1, D), wi, wo, g_out.reshape(1, D))


def kernel(x, positions, w_in, w_branch, w_out, lambda_q1, lambda_k1, lambda_q2, lambda_k2,
           subln_g, norm_g, w_ffn_in, w_ffn_out):
    B, S, D = x.shape
    depth = w_in.shape[0]
    tm = min(512, S)
    tab = _rope_table(positions, tm)
    assert w_in.shape[2] == B_START + B_COLS
    w_bf = w_in.astype(BF16)
    w_a, w_b = w_bf[..., :A_COLS], w_bf[..., B_START:]
    wb, wo = w_branch.astype(BF16), w_out.astype(BF16)
    wi_ffn, wo_ffn = w_ffn_in.astype(BF16), w_ffn_out.astype(BF16)
    for l in range(depth):
        lam_init = 0.8 - 0.6 * math.exp(-0.3 * l)
        (aq, ak, av, bq, k2, v2, iq, ik4, iw, cq, ck, cv, gates, kmean) = _inproj(
            x, norm_g[l, 0], tab, w_a, w_b, l, tm)
        kmean = kmean.reshape(B, S // MOBA_BLOCK, 512)
        lam_rows = jnp.stack([lambda_q1[l], lambda_k1[l], lambda_q2[l], lambda_k2[l]])
        ya = _diff_attn(aq, ak, av, lam_rows, subln_g[l], lam_init, t=min(1024, S))
        yb = _dsa_attn(bq, iq, iw, ik4, k2, v2, tq=256, tk=min(512, S))
        yc = _moba_attn(cq, ck, cv, kmean)
        x = _merge(ya, yb, yc, gates, x, wb, wo, l, norm_g[l, 1], tm)
        x = _ffn(x, norm_g[l, 2], wi_ffn, wo_ffn, l, norm_g[l, 3], tm, tf=256)
    return x
```

```python
import functools
import math

import jax
import jax.numpy as jnp
from jax import lax
from jax.experimental import pallas as pl
from jax.experimental.pallas import tpu as pltpu

F32 = jnp.float32
BF16 = jnp.bfloat16

LANES = 128
HEAD_DIM = 64
ROPE_THETA = 500000.0
NORM_EPS = 1e-6
IDX_HEADS = 4
IDX_DIM = 32
IDX_TOPK_MAX = 256
MOBA_BLOCK = 256
MOBA_TOPK = 3
N_BRANCH = 3
BRANCH_WIDTH = 512
NEG_BIG = -1e30
LOG2E = math.log2(math.e)
ROW_BLOCK = 64
INT_MIN = -2 ** 31
VMEM_LIMIT = 56 * 1024 * 1024

A_AQ, A_AK, A_AV, A_BQ, A_KV, A_IQ, A_IKW, A_COLS = 0, 512, 1024, 1536, 2048, 2176, 2304, 2432
B_START = 2340
B_CQ, B_CK, B_CV, B_G, B_COLS = 0, 512, 1024, 1536, 4608
Q_SCALE = HEAD_DIM ** -0.5 * LOG2E


def _nt_dot(a, b):
    return lax.dot_general(a, b, (((1,), (1,)), ((), ())), preferred_element_type=F32)


def _rms(x, g):
    return x * lax.rsqrt(jnp.mean(x * x, axis=-1, keepdims=True) + NORM_EPS) * g


def _cparams(sem):
    return pltpu.CompilerParams(dimension_semantics=sem, vmem_limit_bytes=VMEM_LIMIT)


def _resident(shape, index_map):
    return pl.BlockSpec(shape, index_map, pipeline_mode=pl.Buffered(1))


def _rope_table_kernel(pos_ref, tab_ref):
    pos = pos_ref[0].astype(F32)
    lane = lax.broadcasted_iota(jnp.int32, (1, LANES), 1)

    def tables(head_dim):
        rot = head_dim // 4
        half = rot // 2
        d = lane & (head_dim - 1)
        fi = d & (half - 1)
        inv = jnp.zeros((1, LANES), F32)
        for i in range(half):
            inv = jnp.where(fi == i, F32(ROPE_THETA ** (-(2.0 * i) / rot)), inv)
        ang = pos * inv
        cos = jnp.cos(ang)
        sin = jnp.sin(ang)
        c = jnp.where(d < rot, cos, 1.0)
        s = jnp.where(d < half, -sin, jnp.where(d < rot, sin, 0.0))
        return c, s

    c64, s64 = tables(HEAD_DIM)
    c32, s32 = tables(IDX_DIM)
    tab_ref[0, :, 0 * LANES:1 * LANES] = c64
    tab_ref[0, :, 1 * LANES:2 * LANES] = s64
    tab_ref[0, :, 2 * LANES:3 * LANES] = c32
    tab_ref[0, :, 3 * LANES:4 * LANES] = s32


def _rope_table(positions, tm):
    B, S = positions.shape
    return pl.pallas_call(
        _rope_table_kernel,
        grid=(B, S // tm),
        in_specs=[pl.BlockSpec((1, tm, 1), lambda b, m: (b, m, 0))],
        out_specs=pl.BlockSpec((1, tm, 4 * LANES), lambda b, m: (b, m, 0)),
        out_shape=jax.ShapeDtypeStruct((B, S, 4 * LANES), F32),
        compiler_params=_cparams(("parallel", "parallel")),
        name="rope_table",
    )(positions.reshape(B, S, 1))


def _inproj_kernel(x_ref, g_ref, tab_ref, wa_ref, wb_ref,
                   aq_ref, ak_ref, av_ref, bq_ref, k2_ref, v2_ref, iq_ref, ik4_ref, iw_ref,
                   cq_ref, ck_ref, cv_ref, gates_ref, kmean_ref):
    h = _rms(x_ref[0], g_ref[...]).astype(BF16)
    lane = lax.broadcasted_iota(jnp.int32, (1, LANES), 1)

    def proj(w_ref, off, width):
        return jnp.dot(h, w_ref[:, off:off + width], preferred_element_type=F32)

    def rope(y, head_dim):
        half = head_dim // 8
        t0 = 0 if head_dim == HEAD_DIM else 2 * LANES
        c = tab_ref[0, :, t0:t0 + LANES]
        s = tab_ref[0, :, t0 + LANES:t0 + 2 * LANES]
        first = (lane & (head_dim - 1)) < half
        blocks = []
        for j in range(y.shape[1] // LANES):
            yj = y[:, j * LANES:(j + 1) * LANES]
            up = pltpu.roll(yj, LANES - half, 1)
            dn = pltpu.roll(yj, half, 1)
            blocks.append(yj * c + jnp.where(first, up, dn) * s)
        return blocks

    def store(out_ref, blocks):
        for j, r in enumerate(blocks):
            out_ref[0, :, j * LANES:(j + 1) * LANES] = r.astype(out_ref.dtype)

    store(aq_ref, rope(proj(wa_ref, A_AQ, 512) * Q_SCALE, HEAD_DIM))
    store(ak_ref, rope(proj(wa_ref, A_AK, 512), HEAD_DIM))
    av_ref[0] = proj(wa_ref, A_AV, 512).astype(av_ref.dtype)
    store(bq_ref, rope(proj(wa_ref, A_BQ, 512) * Q_SCALE, HEAD_DIM))
    kv = proj(wa_ref, A_KV, LANES)
    vk = pltpu.roll(kv, HEAD_DIM, 1)
    lo = lane < HEAD_DIM
    store(k2_ref, rope(jnp.where(lo, kv, vk), HEAD_DIM))
    v2_ref[0] = jnp.where(lo, vk, kv).astype(v2_ref.dtype)
    store(iq_ref, rope(proj(wa_ref, A_IQ, LANES), IDX_DIM))
    ikw = proj(wa_ref, A_IKW, LANES)
    ik = rope(ikw, IDX_DIM)[0]
    ik4 = ik
    for rep in range(1, LANES // IDX_DIM):
        ik4 = jnp.where(lane < rep * IDX_DIM, ik4, pltpu.roll(ik, rep * IDX_DIM, 1))
    ik4_ref[0] = ik4.astype(ik4_ref.dtype)
    iw_ref[0] = jnp.where(lane < IDX_HEADS, pltpu.roll(ikw, LANES - IDX_DIM, 1), 0.0)
    store(cq_ref, rope(proj(wb_ref, B_CQ, 512) * Q_SCALE, HEAD_DIM))
    ck_blocks = rope(proj(wb_ref, B_CK, 512), HEAD_DIM)
    store(ck_ref, ck_blocks)
    cv_ref[0] = proj(wb_ref, B_CV, 512).astype(cv_ref.dtype)
    gates_ref[0] = proj(wb_ref, B_G, 3 * 1024).astype(gates_ref.dtype)

    tm = x_ref.shape[1]
    for blk in range(tm // MOBA_BLOCK):
        for j, r in enumerate(ck_blocks):
            part = r[blk * MOBA_BLOCK:(blk + 1) * MOBA_BLOCK]
            kmean_ref[0, 0, blk:blk + 1, j * LANES:(j + 1) * LANES] = (
                jnp.mean(part, axis=0, keepdims=True))


def _inproj(x, g, tab, w_a, w_b, layer, tm):
    B, S, D = x.shape
    nm = S // tm
    row = lambda width: pl.BlockSpec((1, tm, width), lambda b, m: (b, m, 0))
    shp = lambda width, dt: jax.ShapeDtypeStruct((B, S, width), dt)
    widths = [(512, BF16)] * 4 + [(LANES, BF16)] * 4 + [(LANES, F32)] + [(512, BF16)] * 3 + [(3072, BF16)]
    out_specs = [row(w) for w, _ in widths]
    out_shape = [shp(w, dt) for w, dt in widths]
    out_specs.append(pl.BlockSpec((1, 1, tm // MOBA_BLOCK, 512), lambda b, m: (b, m, 0, 0)))
    out_shape.append(jax.ShapeDtypeStruct((B, nm, tm // MOBA_BLOCK, 512), F32))
    return pl.pallas_call(
        _inproj_kernel,
        grid=(B, nm),
        in_specs=[row(D),
                  _resident((1, D), lambda b, m: (0, 0)),
                  row(4 * LANES),
                  _resident((None, D, A_COLS), lambda b, m: (layer, 0, 0)),
                  _resident((None, D, B_COLS), lambda b, m: (layer, 0, 0))],
        out_specs=out_specs,
        out_shape=out_shape,
        compiler_params=_cparams(("parallel", "parallel")),
        name="inproj",
    )(x, g.reshape(1, D), tab, w_a, w_b)


def _split_heads_rows(q):
    lo = lax.broadcasted_iota(jnp.int32, (1, LANES), 1) < HEAD_DIM
    zero = jnp.zeros_like(q)
    return jnp.concatenate([jnp.where(lo, q, zero), jnp.where(lo, zero, q)], axis=0)


def _lane_blocks(s):
    return [s[:, j * LANES:(j + 1) * LANES] for j in range(s.shape[1] // LANES)]


def _online_update(s, v, m_ref, l_ref, acc_ref, bias=None, row0=0):
    rows_total = s.shape[0]
    nblk = s.shape[1] // LANES
    p_rows, alphas = [], []
    for r0 in range(0, rows_total, ROW_BLOCK):
        rows = slice(row0 + r0, row0 + r0 + ROW_BLOCK)
        cols = [s[r0:r0 + ROW_BLOCK, j * LANES:(j + 1) * LANES] for j in range(nblk)]
        if bias is not None:
            b0 = r0 % bias[0].shape[0]
            cols = [c + bj[b0:b0 + ROW_BLOCK] for c, bj in zip(cols, bias)]
        m_prev = m_ref[rows]
        m_new = jnp.maximum(
            m_prev, jnp.max(functools.reduce(jnp.maximum, cols), axis=1, keepdims=True))
        alpha = jnp.exp2(m_prev - m_new)
        ps = [jnp.exp2(c - m_new) for c in cols]
        l_ref[rows] = alpha * l_ref[rows] + functools.reduce(jnp.add, ps)
        m_ref[rows] = m_new
        p_rows.append(jnp.concatenate([pj.astype(BF16) for pj in ps], axis=1))
        alphas.append(alpha)
    p = jnp.concatenate(p_rows, axis=0)
    alpha = jnp.concatenate(alphas, axis=0)
    span = slice(row0, row0 + rows_total)
    acc_ref[span] = alpha * acc_ref[span] + jnp.dot(p, v, preferred_element_type=F32)


def _normalised(l_ref, acc_ref):
    return acc_ref[...] / jnp.sum(l_ref[...], axis=1, keepdims=True)


def _diff_attn_kernel(q_ref, k_ref, v_ref, lam_ref, g_ref, o_ref, m_ref, l_ref, acc_ref,
                      *, t, lam_init):
    qi = pl.program_id(2)
    half = t // 2
    q = q_ref[0]
    q2 = jnp.concatenate([_split_heads_rows(q[:half]), _split_heads_rows(q[half:])], axis=0)
    m_ref[...] = jnp.full(m_ref.shape, NEG_BIG, F32)
    l_ref[...] = jnp.zeros(l_ref.shape, F32)
    acc_ref[...] = jnp.zeros(acc_ref.shape, F32)

    def body(c, carry):
        off = pl.multiple_of(c * t, t)
        _online_update(_nt_dot(q2, k_ref[0, pl.ds(off, t), :]), v_ref[0, pl.ds(off, t), :],
                       m_ref, l_ref, acc_ref)
        return carry

    lax.fori_loop(0, qi, body, 0)

    r = lax.broadcasted_iota(jnp.int32, (t, 1), 0)
    causal = lax.broadcasted_iota(jnp.int32, (1, half), 1) <= jnp.where(r >= half, r - half, r)
    off_a = pl.multiple_of(qi * t, t)
    s = _nt_dot(q2, k_ref[0, pl.ds(off_a, half), :])
    s = jnp.concatenate([jnp.where(causal, s[:t], NEG_BIG), s[t:]], axis=0)
    _online_update(s, v_ref[0, pl.ds(off_a, half), :], m_ref, l_ref, acc_ref)
    off_b = pl.multiple_of(qi * t + half, half)
    s = jnp.where(causal, _nt_dot(q2[t:], k_ref[0, pl.ds(off_b, half), :]), NEG_BIG)
    _online_update(s, v_ref[0, pl.ds(off_b, half), :], m_ref, l_ref, acc_ref, row0=t)

    lam_rows = lam_ref[...]
    e1 = jnp.exp(jnp.sum(lam_rows[0:1] * lam_rows[1:2], axis=1, keepdims=True))
    e2 = jnp.exp(jnp.sum(lam_rows[2:3] * lam_rows[3:4], axis=1, keepdims=True))
    lam = e1 - e2 + lam_init
    o = _normalised(l_ref, acc_ref)
    o1 = jnp.concatenate([o[:half], o[t:t + half]], axis=0)
    o2 = jnp.concatenate([o[half:t], o[t + half:]], axis=0)
    o_ref[0] = (_rms(o1 - lam * o2, g_ref[...]) * (1.0 - lam_init)).astype(o_ref.dtype)


def _diff_attn(aq, ak, av, lam_rows, subln_g, lam_init, t):
    B, S, _ = aq.shape
    nh = aq.shape[2] // LANES
    kern = functools.partial(_diff_attn_kernel, t=t, lam_init=lam_init)
    return pl.pallas_call(
        kern,
        grid=(B, nh, S // t),
        in_specs=[pl.BlockSpec((1, t, LANES), lambda b, h, i: (b, i, h)),
                  pl.BlockSpec((1, S, LANES), lambda b, h, i: (b, 0, h)),
                  pl.BlockSpec((1, S, LANES), lambda b, h, i: (b, 0, h)),
                  pl.BlockSpec((4, HEAD_DIM), lambda b, h, i: (0, 0)),
                  pl.BlockSpec((1, LANES), lambda b, h, i: (0, 0))],
        out_specs=pl.BlockSpec((1, t, LANES), lambda b, h, i: (b, i, h)),
        out_shape=jax.ShapeDtypeStruct(aq.shape, BF16),
        scratch_shapes=[pltpu.VMEM((2 * t, LANES), F32), pltpu.VMEM((2 * t, LANES), F32),
                        pltpu.VMEM((2 * t, LANES), F32)],
        compiler_params=_cparams(("parallel", "parallel", "arbitrary")),
        name="diff_attn",
    )(aq, ak, av, lam_rows, subln_g.reshape(1, LANES))


BIT_GROUP = 256


def _bit_transpose32(words):
    w = list(words)
    for j, m in ((16, 0x0000FFFF), (8, 0x00FF00FF), (4, 0x0F0F0F0F), (2, 0x33333333), (1, 0x55555555)):
        for k in range(32):
            if k & j == 0:
                t = (w[k] ^ (w[k + j] >> j)) & m
                w[k] = w[k] ^ t
                w[k + j] = w[k + j] ^ (t << j)
    return w

def _dsa_kernel(q_ref, iq_ref, iw_ref, ik4_ref, k2_ref, v2_ref, o_ref,
                keys_ref, planes_ref, act_ref, jcut_ref, m_ref, l_ref, acc_ref,
                *, tq, tk, topk, idx_bits):
    qi = pl.program_id(1)
    nkv = (qi * tq + tq - 1) // tk + 1
    lane = lax.broadcasted_iota(jnp.int32, (1, LANES), 1)
    qpos = qi * tq + lax.broadcasted_iota(jnp.int32, (tq, 1), 0)
    lane_tk = lax.broadcasted_iota(jnp.int32, (1, tk), 1)
    nsub = tk // LANES
    seq = keys_ref.shape[1]

    @pl.when(qi == 0)
    def _():
        planes_ref[...] = jnp.zeros(planes_ref.shape, jnp.int32)

    iq = iq_ref[0]
    zero = jnp.zeros_like(iq)
    iq4 = jnp.concatenate(
        [jnp.where((lane >> (IDX_DIM.bit_length() - 1)) == hh, iq, zero) for hh in range(IDX_HEADS)],
        axis=0)
    iw = iw_ref[0]
    iw_cols = [iw[:, hh:hh + 1] for hh in range(IDX_HEADS)]

    def score_body(c, carry):
        off = pl.multiple_of(c * tk, tk)
        rel = jnp.maximum(_nt_dot(iq4, ik4_ref[0, pl.ds(off, tk), :]), 0.0)
        score = iw_cols[0] * rel[0:tq]
        for hh in range(1, IDX_HEADS):
            score = score + iw_cols[hh] * rel[hh * tq:(hh + 1) * tq]
        bits = lax.bitcast_convert_type(score, jnp.int32)
        key = bits ^ ((bits >> 31) & jnp.int32(0x7FFFFFFF))
        kpos = off + lane_tk
        key = jnp.where(score == 0.0, (seq - 1) - kpos, jnp.where(key > 0, key + seq, key))
        key = jnp.where(kpos <= qpos, key, jnp.int32(INT_MIN))
        keys_ref[:, pl.ds(off, tk)] = key
        kts = [key[:, j * LANES:(j + 1) * LANES].T for j in range(nsub)]
        for gl in range(tk // BIT_GROUP):
            words = [kts[(gl * BIT_GROUP + 8 * g) // LANES][(8 * g) % LANES:(8 * g) % LANES + 8]
                     for g in range(32)]
            words = _bit_transpose32(words)
            words[0] = ~words[0]
            for b in range(32):
                planes_ref[c * (tk // BIT_GROUP) + gl, b] = words[b]
        return carry

    lax.fori_loop(0, nkv, score_body, 0)

    ngroups = planes_ref.shape[0]
    live = nkv * (tk // BIT_GROUP)
    for g in range(ngroups):
        act_ref[g] = jnp.broadcast_to(jnp.where(g < live, jnp.int32(-1), jnp.int32(0)), (8, tq))

    def sweep(i, keep, first=False):
        parts = [jnp.zeros((8, tq), jnp.int32) for _ in range(4)]
        for g in range(ngroups):
            act = act_ref[g]
            if not first:
                act = act & (planes_ref[g, i - 1] ^ keep)
                act_ref[g] = act
            parts[g % 4] = parts[g % 4] + lax.population_count(act & planes_ref[g, i])
        cnt = (parts[0] + parts[1]) + (parts[2] + parts[3])
        return jnp.sum(cnt.astype(F32), axis=0, keepdims=True)

    def decide(i, ones, thr_u, n_gt):
        take = n_gt + ones >= topk
        thr_u = thr_u | jnp.where(take, jnp.left_shift(jnp.int32(1), 31 - i), 0)
        return thr_u, jnp.where(take, n_gt, n_gt + ones), jnp.where(take, 0, -1).astype(jnp.int32)

    def pass_body(i, carry):
        thr_u, n_gt, keep = carry
        return decide(i, sweep(i, keep), thr_u, n_gt)

    start = decide(0, sweep(0, None, first=True),
                   jnp.zeros((1, tq), jnp.int32), jnp.zeros((1, tq), F32))
    thr_u, n_gt, keep = lax.fori_loop(1, 32, pass_body, start)
    n_eq = jnp.zeros((8, tq), jnp.int32)
    for g in range(ngroups):
        n_eq = n_eq + lax.population_count(act_ref[g] & (planes_ref[g, 31] ^ keep))
    thr = thr_u ^ jnp.int32(INT_MIN)
    cnt_ge = n_gt + jnp.sum(n_eq.astype(F32), axis=0, keepdims=True)
    need = topk - n_gt

    def as_rows(x):
        return jnp.broadcast_to(x, (LANES, tq)).T

    thr_r = as_rows(thr)
    need_r = as_rows(need)
    jcut_ref[...] = jnp.full((tq, LANES), 2 ** 30, jnp.int32)

    @pl.when(jnp.max(cnt_ge) > topk)
    def _():
        def ties_before(cand):
            def body(c, cnt):
                off = pl.multiple_of(c * tk, tk)
                for j in range(nsub):
                    keyj = keys_ref[:, pl.ds(pl.multiple_of(off + j * LANES, LANES), LANES)]
                    hit = (keyj == thr_r) & (off + j * LANES + lane < cand)
                    cnt = jnp.where(hit, cnt + 1, cnt)
                return cnt
            cnt = lax.fori_loop(0, nkv, body, jnp.zeros((tq, LANES), jnp.int32))
            return jnp.sum(cnt.astype(F32), axis=1, keepdims=True)

        def idx_body(i, j):
            cand = j + jnp.left_shift(jnp.int32(1), idx_bits - 1 - i)
            return jnp.where(ties_before(cand) < need_r, cand, j)
        jcut_ref[...] = lax.fori_loop(0, idx_bits, idx_body, jnp.zeros((tq, LANES), jnp.int32))

    jcut_r = jcut_ref[...]

    q = q_ref[0]
    q8 = jnp.concatenate(
        [_split_heads_rows(q[:, p * LANES:(p + 1) * LANES]) for p in range(4)], axis=0)
    m_ref[...] = jnp.full(m_ref.shape, NEG_BIG, F32)
    l_ref[...] = jnp.zeros(l_ref.shape, F32)
    acc_ref[...] = jnp.zeros(acc_ref.shape, F32)

    def attend(off, width):
        s = _nt_dot(q8, k2_ref[0, pl.ds(off, width), :])
        bias = []
        for j in range(width // LANES):
            keyj = keys_ref[:, pl.ds(pl.multiple_of(off + j * LANES, LANES), LANES)]
            kpos = off + j * LANES + lane
            sel = ((keyj > thr_r) | ((keyj == thr_r) & (kpos <= jcut_r))) & (kpos <= qpos)
            bias.append(jnp.where(sel, 0.0, NEG_BIG))
        _online_update(s, v2_ref[0, pl.ds(off, width), :], m_ref, l_ref, acc_ref, bias=bias)

    def attn_body(c, carry):
        attend(pl.multiple_of(c * tk, tk), tk)
        return carry

    lax.fori_loop(0, nkv, attn_body, 0)

    o = _normalised(l_ref, acc_ref)
    lo = lane < HEAD_DIM
    for p in range(4):
        o_ref[0, :, p * LANES:(p + 1) * LANES] = jnp.where(
            lo, o[(2 * p) * tq:(2 * p + 1) * tq], o[(2 * p + 1) * tq:(2 * p + 2) * tq]
        ).astype(o_ref.dtype)


def _dsa_attn(bq, iq, iw, ik4, k2, v2, tq, tk):
    B, S, C = bq.shape
    topk = min(IDX_TOPK_MAX, S // 4)
    idx_bits = max(1, (S - 1).bit_length())
    kern = functools.partial(_dsa_kernel, tq=tq, tk=tk, topk=topk, idx_bits=idx_bits)
    row = lambda width: pl.BlockSpec((1, tq, width), lambda b, i: (b, i, 0))
    full = _resident((1, S, LANES), lambda b, i: (b, 0, 0))
    return pl.pallas_call(
        kern,
        grid=(B, S // tq),
        in_specs=[row(C), row(LANES), row(LANES), full, full, full],
        out_specs=row(C),
        out_shape=jax.ShapeDtypeStruct(bq.shape, BF16),
        scratch_shapes=[pltpu.VMEM((tq, S), jnp.int32),
                        pltpu.VMEM((S // BIT_GROUP, 32, 8, tq), jnp.int32),
                        pltpu.VMEM((S // BIT_GROUP, 8, tq), jnp.int32),
                        pltpu.VMEM((tq, LANES), jnp.int32),
                        pltpu.VMEM((8 * tq, LANES), F32), pltpu.VMEM((8 * tq, LANES), F32),
                        pltpu.VMEM((8 * tq, LANES), F32)],
        compiler_params=_cparams(("parallel", "arbitrary")),
        name="dsa_attn",
    )(bq, iq, iw, ik4, k2, v2)


MOBA_TILE_BLOCKS = 4


def _moba_kernel(q_ref, k_ref, v_ref, km_ref, o_ref, m_ref, l_ref, acc_ref, *, nb):
    nqb = MOBA_TILE_BLOCKS
    blk_rows = MOBA_BLOCK
    grp = 2 * blk_rows
    i = pl.program_id(2)
    first = nqb * i
    lane = lax.broadcasted_iota(jnp.int32, (1, LANES), 1)
    q = q_ref[0]
    q2 = jnp.concatenate(
        [_split_heads_rows(q[j * blk_rows:(j + 1) * blk_rows]) for j in range(nqb)], axis=0)

    km = jnp.concatenate([km_ref[0], jnp.zeros((LANES - nb, LANES), F32)], axis=0)
    km_hi = km.astype(BF16)
    r1 = km - km_hi.astype(F32)
    km_mid = r1.astype(BF16)
    km_lo = (r1 - km_mid.astype(F32)).astype(BF16)
    nb8 = -(-nb // 8) * 8
    gate_t = (_nt_dot(km_hi, q2) + _nt_dot(km_mid, q2) + _nt_dot(km_lo, q2))[:nb8]
    blk = lax.broadcasted_iota(jnp.int32, (nb8, 1), 0)
    blk_f = blk.astype(F32)
    col = lax.broadcasted_iota(jnp.int32, (1, nqb * grp), 1)
    past = blk < first + (col >> (grp.bit_length() - 1))
    g = jnp.where(past, gate_t, -jnp.inf)
    sel = jnp.zeros(g.shape, jnp.bool_)
    for _ in range(min(MOBA_TOPK, nb - 1)):
        mx = jnp.max(g, axis=0, keepdims=True)
        pick = blk_f == jnp.min(jnp.where(g == mx, blk_f, float(LANES)), axis=0, keepdims=True)
        sel = sel | pick
        g = jnp.where(pick, -jnp.inf, g)
    bias_t = jnp.concatenate([jnp.where(sel & past, 0.0, NEG_BIG),
                              jnp.full((LANES - nb8, nqb * grp), NEG_BIG, F32)], axis=0)
    qa = jnp.concatenate([q2, bias_t.T.astype(BF16)], axis=1)

    m_ref[...] = jnp.full(m_ref.shape, NEG_BIG, F32)
    l_ref[...] = jnp.zeros(l_ref.shape, F32)
    acc_ref[...] = jnp.zeros(acc_ref.shape, F32)

    def biased_logits(lhs, first_blk, nblk):
        n = nblk * blk_rows
        off = pl.multiple_of(first_blk * blk_rows, blk_rows)
        row_blk = lax.broadcasted_iota(jnp.int32, (n, 1), 0) >> (MOBA_BLOCK.bit_length() - 1)
        onehot = jnp.where(lane == first_blk + row_blk, 1.0, 0.0).astype(BF16)
        ka = jnp.concatenate([k_ref[0, pl.ds(off, n), :], onehot], axis=1)
        return _nt_dot(lhs, ka), v_ref[0, pl.ds(off, n), :]

    def body(c, carry):
        s, v = biased_logits(qa, nqb * c, nqb)
        _online_update(s, v, m_ref, l_ref, acc_ref)
        return carry

    lax.fori_loop(0, i, body, 0)

    r = lax.broadcasted_iota(jnp.int32, (grp, 1), 0)
    causal = (lax.broadcasted_iota(jnp.int32, (1, blk_rows), 1)
              <= jnp.where(r >= blk_rows, r - blk_rows, r))
    for j in range(nqb):
        off = pl.multiple_of((first + j) * blk_rows, blk_rows)
        s = jnp.where(causal, _nt_dot(q2[j * grp:(j + 1) * grp], k_ref[0, pl.ds(off, blk_rows), :]),
                      NEG_BIG)
        if j + 1 < nqb:
            s_later, _ = biased_logits(qa[(j + 1) * grp:], first + j, 1)
            s = jnp.concatenate([s, s_later], axis=0)
        _online_update(s, v_ref[0, pl.ds(off, blk_rows), :], m_ref, l_ref, acc_ref, row0=j * grp)

    o = _normalised(l_ref, acc_ref)
    lo = lane < HEAD_DIM
    for j in range(nqb):
        o_ref[0, j * blk_rows:(j + 1) * blk_rows] = jnp.where(
            lo, o[j * grp:j * grp + blk_rows], o[j * grp + blk_rows:(j + 1) * grp]).astype(o_ref.dtype)


def _moba_attn(cq, ck, cv, kmean):
    B, S, C = cq.shape
    nb = S // MOBA_BLOCK
    tq = MOBA_TILE_BLOCKS * MOBA_BLOCK
    return pl.pallas_call(
        functools.partial(_moba_kernel, nb=nb),
        grid=(B, C // LANES, S // tq),
        in_specs=[pl.BlockSpec((1, tq, LANES), lambda b, p, i: (b, i, p)),
                  pl.BlockSpec((1, S, LANES), lambda b, p, i: (b, 0, p)),
                  pl.BlockSpec((1, S, LANES), lambda b, p, i: (b, 0, p)),
                  pl.BlockSpec((1, nb, LANES), lambda b, p, i: (b, 0, p))],
        out_specs=pl.BlockSpec((1, tq, LANES), lambda b, p, i: (b, i, p)),
        out_shape=jax.ShapeDtypeStruct(cq.shape, BF16),
        scratch_shapes=[pltpu.VMEM((2 * tq, LANES), F32), pltpu.VMEM((2 * tq, LANES), F32),
                        pltpu.VMEM((2 * tq, LANES), F32)],
        compiler_params=_cparams(("parallel", "parallel", "arbitrary")),
        name="moba_attn",
    )(cq, ck, cv, kmean)


def _sigmoid(x):
    return 1.0 / (1.0 + jnp.exp(-x))


def _merge_kernel(ya_ref, yb_ref, yc_ref, gates_ref, x_ref, wb_ref, wo_ref, g_ref, o_ref):
    d = x_ref.shape[2]
    merged = None
    for n, y_ref in enumerate((ya_ref, yb_ref, yc_ref)):
        br = jnp.dot(y_ref[0], wb_ref[n], preferred_element_type=F32)
        term = _sigmoid(gates_ref[0, :, n * d:(n + 1) * d].astype(F32)) * br
        merged = term if merged is None else merged + term
    m = jnp.dot(merged.astype(BF16), wo_ref[...], preferred_element_type=F32)
    o_ref[0] = x_ref[0] + _rms(m, g_ref[...])


def _merge(ya, yb, yc, gates, x, wb, wo, layer, g, tm):
    B, S, D = x.shape
    row = lambda width: pl.BlockSpec((1, tm, width), lambda b, m: (b, m, 0))
    return pl.pallas_call(
        _merge_kernel,
        grid=(B, S // tm),
        in_specs=[row(BRANCH_WIDTH), row(BRANCH_WIDTH), row(BRANCH_WIDTH), row(N_BRANCH * D), row(D),
                  _resident((None,) + wb.shape[1:], lambda b, m: (layer, 0, 0, 0)),
                  _resident((None,) + wo.shape[1:], lambda b, m: (layer, 0, 0)),
                  _resident((1, D), lambda b, m: (0, 0))],
        out_specs=row(D),
        out_shape=jax.ShapeDtypeStruct(x.shape, F32),
        compiler_params=_cparams(("parallel", "parallel")),
        name="merge_out",
    )(ya, yb, yc, gates, x, wb, wo, g.reshape(1, D))


def _ffn_kernel(x_ref, g_in_ref, wi_ref, wo_ref, g_out_ref, o_ref, acc_ref, *, d_ff, tf):
    x = x_ref[0]
    h = _rms(x, g_in_ref[...]).astype(BF16)
    acc_ref[...] = jnp.zeros(acc_ref.shape, F32)

    def body(c, carry):
        off = pl.multiple_of(c * tf, tf)
        gt = jnp.dot(h, wi_ref[:, pl.ds(off, tf)], preferred_element_type=F32)
        up_off = pl.multiple_of(d_ff + off, LANES)
        up = jnp.dot(h, wi_ref[:, pl.ds(up_off, tf)], preferred_element_type=F32)
        act = (gt * _sigmoid(gt) * up).astype(BF16)
        acc_ref[...] += jnp.dot(act, wo_ref[pl.ds(off, tf), :], preferred_element_type=F32)
        return carry

    lax.fori_loop(0, d_ff // tf, body, 0)
    o_ref[0] = x + _rms(acc_ref[...], g_out_ref[...])


def _ffn(x, g_in, wi, wo, layer, g_out, tm, tf):
    B, S, D = x.shape
    d_ff = wo.shape[1]
    row = pl.BlockSpec((1, tm, D), lambda b, m: (b, m, 0))
    vec = _resident((1, D), lambda b, m: (0, 0))
    return pl.pallas_call(
        functools.partial(_ffn_kernel, d_ff=d_ff, tf=tf),
        grid=(B, S // tm),
        in_specs=[row, vec, _resident((None,) + wi.shape[1:], lambda b, m: (layer, 0, 0)),
                  _resident((None,) + wo.shape[1:], lambda b, m: (layer, 0, 0)), vec],
        out_specs=row,
        out_shape=jax.ShapeDtypeStruct(x.shape, F32),
        scratch_shapes=[pltpu.VMEM((tm, D), F32)],
        compiler_params=_cparams(("parallel", "parallel")),
        name="swiglu_ffn",
    )(x, g_in.reshape(1, D), wi, wo, g_out.reshape(1, D))


def kernel(x, positions, w_in, w_branch, w_out, lambda_q1, lambda_k1, lambda_q2, lambda_k2,
           subln_g, norm_g, w_ffn_in, w_ffn_out):
    B, S, D = x.shape
    depth = w_in.shape[0]
    tm = min(512, S)
    tab = _rope_table(positions, tm)
    assert w_in.shape[2] == B_START + B_COLS
    w_bf = w_in.astype(BF16)
    w_a, w_b = w_bf[..., :A_COLS], w_bf[..., B_START:]
    wb, wo = w_branch.astype(BF16), w_out.astype(BF16)
    wi_ffn, wo_ffn = w_ffn_in.astype(BF16), w_ffn_out.astype(BF16)
    for l in range(depth):
        lam_init = 0.8 - 0.6 * math.exp(-0.3 * l)
        (aq, ak, av, bq, k2, v2, iq, ik4, iw, cq, ck, cv, gates, kmean) = _inproj(
            x, norm_g[l, 0], tab, w_a, w_b, l, tm)
        kmean = kmean.reshape(B, S // MOBA_BLOCK, 512)
        lam_rows = jnp.stack([lambda_q1[l], lambda_k1[l], lambda_q2[l], lambda_k2[l]])
        ya = _diff_attn(aq, ak, av, lam_rows, subln_g[l], lam_init, t=min(1024, S))
        yb = _dsa_attn(bq, iq, iw, ik4, k2, v2, tq=256, tk=min(512, S))
        yc = _moba_attn(cq, ck, cv, kmean)
        x = _merge(ya, yb, yc, gates, x, wb, wo, l, norm_g[l, 1], min(1024, S))
        x = _ffn(x, norm_g[l, 2], wi_ffn, wo_ffn, l, norm_g[l, 3], min(1024, S), tf=256)
    return x
```

```python
import functools
import math
from typing import NamedTuple

import jax
import jax.numpy as jnp
from jax import lax
from jax.experimental import pallas as pl
from jax.experimental.pallas import tpu as pltpu

F32 = jnp.float32
BF16 = jnp.bfloat16

LANES = 128
HEAD_DIM = 64
ROPE_THETA = 500000.0
NORM_EPS = 1e-6
IDX_HEADS = 4
IDX_DIM = 32
IDX_TOPK_MAX = 256
MOBA_BLOCK = 256
MOBA_TOPK = 3
N_BRANCH = 3
BRANCH_WIDTH = 512
NEG_BIG = -1e30
LOG2E = math.log2(math.e)
ROW_BLOCK = 64
INT_MIN = -2 ** 31
VMEM_LIMIT = 56 * 1024 * 1024

A_AQ, A_AK, A_AV, A_BQ, A_KV, A_IQ, A_IKW, A_COLS = 0, 512, 1024, 1536, 2048, 2176, 2304, 2432
B_START = 2340
B_CQ, B_CK, B_CV, B_G, B_COLS = 0, 512, 1024, 1536, 4608
Q_SCALE = HEAD_DIM ** -0.5 * LOG2E


def _nt_dot(a, b):
    return lax.dot_general(a, b, (((1,), (1,)), ((), ())), preferred_element_type=F32)


def _rms(x, g):
    return x * lax.rsqrt(jnp.mean(x * x, axis=-1, keepdims=True) + NORM_EPS) * g


def _cparams(sem):
    return pltpu.CompilerParams(dimension_semantics=sem, vmem_limit_bytes=VMEM_LIMIT)


def _resident(shape, index_map):
    return pl.BlockSpec(shape, index_map, pipeline_mode=pl.Buffered(1))


def _rope_table_kernel(pos_ref, tab_ref):
    pos = pos_ref[0].astype(F32)
    lane = lax.broadcasted_iota(jnp.int32, (1, LANES), 1)

    def tables(head_dim):
        rot = head_dim // 4
        half = rot // 2
        d = lane & (head_dim - 1)
        fi = d & (half - 1)
        inv = jnp.zeros((1, LANES), F32)
        for i in range(half):
            inv = jnp.where(fi == i, F32(ROPE_THETA ** (-(2.0 * i) / rot)), inv)
        ang = pos * inv
        cos = jnp.cos(ang)
        sin = jnp.sin(ang)
        c = jnp.where(d < rot, cos, 1.0)
        s = jnp.where(d < half, -sin, jnp.where(d < rot, sin, 0.0))
        return c, s

    c64, s64 = tables(HEAD_DIM)
    c32, s32 = tables(IDX_DIM)
    tab_ref[0, :, 0 * LANES:1 * LANES] = c64
    tab_ref[0, :, 1 * LANES:2 * LANES] = s64
    tab_ref[0, :, 2 * LANES:3 * LANES] = c32
    tab_ref[0, :, 3 * LANES:4 * LANES] = s32


def _rope_table(positions, tm):
    B, S = positions.shape
    return pl.pallas_call(
        _rope_table_kernel,
        grid=(B, S // tm),
        in_specs=[pl.BlockSpec((1, tm, 1), lambda b, m: (b, m, 0))],
        out_specs=pl.BlockSpec((1, tm, 4 * LANES), lambda b, m: (b, m, 0)),
        out_shape=jax.ShapeDtypeStruct((B, S, 4 * LANES), F32),
        compiler_params=_cparams(("parallel", "parallel")),
        name="rope_table",
    )(positions.reshape(B, S, 1))


def _inproj_kernel(x_ref, g_ref, tab_ref, wa_ref, wb_ref,
                   aq_ref, ak_ref, av_ref, bq_ref, k2_ref, v2_ref, iq_ref, ik4_ref, iw_ref,
                   cq_ref, ck_ref, cv_ref, gates_ref, kmean_ref):
    h = _rms(x_ref[0], g_ref[...]).astype(BF16)
    lane = lax.broadcasted_iota(jnp.int32, (1, LANES), 1)

    def proj(w_ref, off, width):
        return jnp.dot(h, w_ref[:, off:off + width], preferred_element_type=F32)

    def rope(y, head_dim):
        half = head_dim // 8
        t0 = 0 if head_dim == HEAD_DIM else 2 * LANES
        c = tab_ref[0, :, t0:t0 + LANES]
        s = tab_ref[0, :, t0 + LANES:t0 + 2 * LANES]
        first = (lane & (head_dim - 1)) < half
        blocks = []
        for j in range(y.shape[1] // LANES):
            yj = y[:, j * LANES:(j + 1) * LANES]
            up = pltpu.roll(yj, LANES - half, 1)
            dn = pltpu.roll(yj, half, 1)
            blocks.append(yj * c + jnp.where(first, up, dn) * s)
        return blocks

    def store(out_ref, blocks):
        for j, r in enumerate(blocks):
            out_ref[0, :, j * LANES:(j + 1) * LANES] = r.astype(out_ref.dtype)

    store(aq_ref, rope(proj(wa_ref, A_AQ, 512) * Q_SCALE, HEAD_DIM))
    store(ak_ref, rope(proj(wa_ref, A_AK, 512), HEAD_DIM))
    av_ref[0] = proj(wa_ref, A_AV, 512).astype(av_ref.dtype)
    store(bq_ref, rope(proj(wa_ref, A_BQ, 512) * Q_SCALE, HEAD_DIM))
    kv = proj(wa_ref, A_KV, LANES)
    vk = pltpu.roll(kv, HEAD_DIM, 1)
    lo = lane < HEAD_DIM
    store(k2_ref, rope(jnp.where(lo, kv, vk), HEAD_DIM))
    v2_ref[0] = jnp.where(lo, vk, kv).astype(v2_ref.dtype)
    store(iq_ref, rope(proj(wa_ref, A_IQ, LANES), IDX_DIM))
    ikw = proj(wa_ref, A_IKW, LANES)
    ik = rope(ikw, IDX_DIM)[0]
    ik4 = ik
    for rep in range(1, LANES // IDX_DIM):
        ik4 = jnp.where(lane < rep * IDX_DIM, ik4, pltpu.roll(ik, rep * IDX_DIM, 1))
    ik4_ref[0] = ik4.astype(ik4_ref.dtype)
    iw_ref[0] = jnp.where(lane < IDX_HEADS, pltpu.roll(ikw, LANES - IDX_DIM, 1), 0.0)
    store(cq_ref, rope(proj(wb_ref, B_CQ, 512) * Q_SCALE, HEAD_DIM))
    ck_blocks = rope(proj(wb_ref, B_CK, 512), HEAD_DIM)
    store(ck_ref, ck_blocks)
    cv_ref[0] = proj(wb_ref, B_CV, 512).astype(cv_ref.dtype)
    gates_ref[0] = proj(wb_ref, B_G, 3 * 1024).astype(gates_ref.dtype)

    tm = x_ref.shape[1]
    for blk in range(tm // MOBA_BLOCK):
        for j, r in enumerate(ck_blocks):
            part = r[blk * MOBA_BLOCK:(blk + 1) * MOBA_BLOCK]
            kmean_ref[0, 0, blk:blk + 1, j * LANES:(j + 1) * LANES] = (
                jnp.mean(part, axis=0, keepdims=True))


def _inproj(x, g, tab, w_a, w_b, layer, tm):
    B, S, D = x.shape
    nm = S // tm
    row = lambda width: pl.BlockSpec((1, tm, width), lambda b, m: (b, m, 0))
    shp = lambda width, dt: jax.ShapeDtypeStruct((B, S, width), dt)
    widths = [(512, BF16)] * 4 + [(LANES, BF16)] * 4 + [(LANES, F32)] + [(512, BF16)] * 3 + [(3072, BF16)]
    out_specs = [row(w) for w, _ in widths]
    out_shape = [shp(w, dt) for w, dt in widths]
    out_specs.append(pl.BlockSpec((1, 1, tm // MOBA_BLOCK, 512), lambda b, m: (b, m, 0, 0)))
    out_shape.append(jax.ShapeDtypeStruct((B, nm, tm // MOBA_BLOCK, 512), F32))
    return pl.pallas_call(
        _inproj_kernel,
        grid=(B, nm),
        in_specs=[row(D),
                  _resident((1, D), lambda b, m: (0, 0)),
                  row(4 * LANES),
                  _resident((None, D, A_COLS), lambda b, m: (layer, 0, 0)),
                  _resident((None, D, B_COLS), lambda b, m: (layer, 0, 0))],
        out_specs=out_specs,
        out_shape=out_shape,
        compiler_params=_cparams(("parallel", "parallel")),
        name="inproj",
    )(x, g.reshape(1, D), tab, w_a, w_b)


def _split_heads_rows(q):
    lo = lax.broadcasted_iota(jnp.int32, (1, LANES), 1) < HEAD_DIM
    zero = jnp.zeros_like(q)
    return jnp.concatenate([jnp.where(lo, q, zero), jnp.where(lo, zero, q)], axis=0)


def _online_update(s, v, m_ref, l_ref, acc_ref, bias=None, row0=0):
    rows_total = s.shape[0]
    nblk = s.shape[1] // LANES
    p_rows, alphas = [], []
    for r0 in range(0, rows_total, ROW_BLOCK):
        rows = slice(row0 + r0, row0 + r0 + ROW_BLOCK)
        cols = [s[r0:r0 + ROW_BLOCK, j * LANES:(j + 1) * LANES] for j in range(nblk)]
        if bias is not None:
            b0 = r0 % bias[0].shape[0]
            cols = [c + bj[b0:b0 + ROW_BLOCK] for c, bj in zip(cols, bias)]
        m_prev = m_ref[rows]
        m_new = jnp.maximum(
            m_prev, jnp.max(functools.reduce(jnp.maximum, cols), axis=1, keepdims=True))
        alpha = jnp.exp2(m_prev - m_new)
        ps = [jnp.exp2(c - m_new) for c in cols]
        l_ref[rows] = alpha * l_ref[rows] + functools.reduce(jnp.add, ps)
        m_ref[rows] = m_new
        p_rows.append(jnp.concatenate([pj.astype(BF16) for pj in ps], axis=1))
        alphas.append(alpha)
    p = jnp.concatenate(p_rows, axis=0)
    alpha = jnp.concatenate(alphas, axis=0)
    span = slice(row0, row0 + rows_total)
    acc_ref[span] = alpha * acc_ref[span] + jnp.dot(p, v, preferred_element_type=F32)


def _normalised(l_ref, acc_ref):
    return acc_ref[...] / jnp.sum(l_ref[...], axis=1, keepdims=True)


def _diff_attn_kernel(q_ref, k_ref, v_ref, lam_ref, g_ref, o_ref, m_ref, l_ref, acc_ref,
                      *, t, lam_init):
    qi = pl.program_id(2)
    half = t // 2
    q = q_ref[0]
    q2 = jnp.concatenate([_split_heads_rows(q[:half]), _split_heads_rows(q[half:])], axis=0)
    m_ref[...] = jnp.full(m_ref.shape, NEG_BIG, F32)
    l_ref[...] = jnp.zeros(l_ref.shape, F32)
    acc_ref[...] = jnp.zeros(acc_ref.shape, F32)

    def body(c, carry):
        off = pl.multiple_of(c * t, t)
        _online_update(_nt_dot(q2, k_ref[0, pl.ds(off, t), :]), v_ref[0, pl.ds(off, t), :],
                       m_ref, l_ref, acc_ref)
        return carry

    lax.fori_loop(0, qi, body, 0)

    r = lax.broadcasted_iota(jnp.int32, (t, 1), 0)
    causal = lax.broadcasted_iota(jnp.int32, (1, half), 1) <= jnp.where(r >= half, r - half, r)
    off_a = pl.multiple_of(qi * t, t)
    s = _nt_dot(q2, k_ref[0, pl.ds(off_a, half), :])
    s = jnp.concatenate([jnp.where(causal, s[:t], NEG_BIG), s[t:]], axis=0)
    _online_update(s, v_ref[0, pl.ds(off_a, half), :], m_ref, l_ref, acc_ref)
    off_b = pl.multiple_of(qi * t + half, half)
    s = jnp.where(causal, _nt_dot(q2[t:], k_ref[0, pl.ds(off_b, half), :]), NEG_BIG)
    _online_update(s, v_ref[0, pl.ds(off_b, half), :], m_ref, l_ref, acc_ref, row0=t)

    lam_rows = lam_ref[...]
    e1 = jnp.exp(jnp.sum(lam_rows[0:1] * lam_rows[1:2], axis=1, keepdims=True))
    e2 = jnp.exp(jnp.sum(lam_rows[2:3] * lam_rows[3:4], axis=1, keepdims=True))
    lam = e1 - e2 + lam_init
    o = _normalised(l_ref, acc_ref)
    o1 = jnp.concatenate([o[:half], o[t:t + half]], axis=0)
    o2 = jnp.concatenate([o[half:t], o[t + half:]], axis=0)
    o_ref[0] = (_rms(o1 - lam * o2, g_ref[...]) * (1.0 - lam_init)).astype(o_ref.dtype)


def _diff_attn(aq, ak, av, lam_rows, subln_g, lam_init, t):
    B, S, _ = aq.shape
    nh = aq.shape[2] // LANES
    kern = functools.partial(_diff_attn_kernel, t=t, lam_init=lam_init)
    return pl.pallas_call(
        kern,
        grid=(B, nh, S // t),
        in_specs=[pl.BlockSpec((1, t, LANES), lambda b, h, i: (b, i, h)),
                  pl.BlockSpec((1, S, LANES), lambda b, h, i: (b, 0, h)),
                  pl.BlockSpec((1, S, LANES), lambda b, h, i: (b, 0, h)),
                  pl.BlockSpec((4, HEAD_DIM), lambda b, h, i: (0, 0)),
                  pl.BlockSpec((1, LANES), lambda b, h, i: (0, 0))],
        out_specs=pl.BlockSpec((1, t, LANES), lambda b, h, i: (b, i, h)),
        out_shape=jax.ShapeDtypeStruct(aq.shape, BF16),
        scratch_shapes=[pltpu.VMEM((2 * t, LANES), F32), pltpu.VMEM((2 * t, LANES), F32),
                        pltpu.VMEM((2 * t, LANES), F32)],
        compiler_params=_cparams(("parallel", "parallel", "arbitrary")),
        name="diff_attn",
    )(aq, ak, av, lam_rows, subln_g.reshape(1, LANES))


BIT_GROUP = 256


def _bit_transpose32(words):
    w = list(words)
    for j, m in ((16, 0x0000FFFF), (8, 0x00FF00FF), (4, 0x0F0F0F0F), (2, 0x33333333), (1, 0x55555555)):
        for k in range(32):
            if k & j == 0:
                t = (w[k] ^ (w[k + j] >> j)) & m
                w[k] = w[k] ^ t
                w[k + j] = w[k + j] ^ (t << j)
    return w


def _dsa_kernel(q_ref, iq_ref, iw_ref, ik4_ref, k2_ref, v2_ref, o_ref,
                keys_ref, planes_ref, act_ref, jcut_ref, m_ref, l_ref, acc_ref,
                *, tq, tk, topk, idx_bits):
    qi = pl.program_id(1)
    nkv = (qi * tq + tq - 1) // tk + 1
    lane = lax.broadcasted_iota(jnp.int32, (1, LANES), 1)
    qpos = qi * tq + lax.broadcasted_iota(jnp.int32, (tq, 1), 0)
    lane_tk = lax.broadcasted_iota(jnp.int32, (1, tk), 1)
    nsub = tk // LANES
    seq = keys_ref.shape[1]

    @pl.when(qi == 0)
    def _():
        planes_ref[...] = jnp.zeros(planes_ref.shape, jnp.int32)

    iq = iq_ref[0]
    zero = jnp.zeros_like(iq)
    iq4 = jnp.concatenate(
        [jnp.where((lane >> (IDX_DIM.bit_length() - 1)) == hh, iq, zero) for hh in range(IDX_HEADS)],
        axis=0)
    iw = iw_ref[0]
    iw_cols = [iw[:, hh:hh + 1] for hh in range(IDX_HEADS)]

    def score_body(c, carry):
        off = pl.multiple_of(c * tk, tk)
        rel = jnp.maximum(_nt_dot(iq4, ik4_ref[0, pl.ds(off, tk), :]), 0.0)
        score = iw_cols[0] * rel[0:tq]
        for hh in range(1, IDX_HEADS):
            score = score + iw_cols[hh] * rel[hh * tq:(hh + 1) * tq]
        bits = lax.bitcast_convert_type(score, jnp.int32)
        key = bits ^ ((bits >> 31) & jnp.int32(0x7FFFFFFF))
        kpos = off + lane_tk
        key = jnp.where(score == 0.0, (seq - 1) - kpos, jnp.where(key > 0, key + seq, key))
        key = jnp.where(kpos <= qpos, key, jnp.int32(INT_MIN))
        keys_ref[:, pl.ds(off, tk)] = key
        kts = [key[:, j * LANES:(j + 1) * LANES].T for j in range(nsub)]
        for gl in range(tk // BIT_GROUP):
            words = [kts[(gl * BIT_GROUP + 8 * g) // LANES][(8 * g) % LANES:(8 * g) % LANES + 8]
                     for g in range(32)]
            words = _bit_transpose32(words)
            words[0] = ~words[0]
            for b in range(32):
                planes_ref[c * (tk // BIT_GROUP) + gl, b] = words[b]
        return carry

    lax.fori_loop(0, nkv, score_body, 0)

    ngroups = planes_ref.shape[0]
    live = nkv * (tk // BIT_GROUP)
    for g in range(ngroups):
        act_ref[g] = jnp.broadcast_to(jnp.where(g < live, jnp.int32(-1), jnp.int32(0)), (8, tq))

    def sweep(i, keep, first=False):
        parts = [jnp.zeros((8, tq), jnp.int32) for _ in range(4)]
        for g in range(ngroups):
            act = act_ref[g]
            if not first:
                act = act & (planes_ref[g, i - 1] ^ keep)
                act_ref[g] = act
            parts[g % 4] = parts[g % 4] + lax.population_count(act & planes_ref[g, i])
        cnt = (parts[0] + parts[1]) + (parts[2] + parts[3])
        return jnp.sum(cnt.astype(F32), axis=0, keepdims=True)

    def decide(i, ones, thr_u, n_gt):
        take = n_gt + ones >= topk
        thr_u = thr_u | jnp.where(take, jnp.left_shift(jnp.int32(1), 31 - i), 0)
        return thr_u, jnp.where(take, n_gt, n_gt + ones), jnp.where(take, 0, -1).astype(jnp.int32)

    def pass_body(i, carry):
        thr_u, n_gt, keep = carry
        return decide(i, sweep(i, keep), thr_u, n_gt)

    start = decide(0, sweep(0, None, first=True),
                   jnp.zeros((1, tq), jnp.int32), jnp.zeros((1, tq), F32))
    thr_u, n_gt, keep = lax.fori_loop(1, 32, pass_body, start)
    n_eq = jnp.zeros((8, tq), jnp.int32)
    for g in range(ngroups):
        n_eq = n_eq + lax.population_count(act_ref[g] & (planes_ref[g, 31] ^ keep))
    thr = thr_u ^ jnp.int32(INT_MIN)
    cnt_ge = n_gt + jnp.sum(n_eq.astype(F32), axis=0, keepdims=True)
    need = topk - n_gt

    def as_rows(x):
        return jnp.broadcast_to(x, (LANES, tq)).T

    thr_r = as_rows(thr)
    need_r = as_rows(need)
    jcut_ref[...] = jnp.full((tq, LANES), 2 ** 30, jnp.int32)

    @pl.when(jnp.max(cnt_ge) > topk)
    def _():
        def ties_before(cand):
            def body(c, cnt):
                off = pl.multiple_of(c * tk, tk)
                for j in range(nsub):
                    keyj = keys_ref[:, pl.ds(pl.multiple_of(off + j * LANES, LANES), LANES)]
                    hit = (keyj == thr_r) & (off + j * LANES + lane < cand)
                    cnt = jnp.where(hit, cnt + 1, cnt)
                return cnt
            cnt = lax.fori_loop(0, nkv, body, jnp.zeros((tq, LANES), jnp.int32))
            return jnp.sum(cnt.astype(F32), axis=1, keepdims=True)

        def idx_body(i, j):
            cand = j + jnp.left_shift(jnp.int32(1), idx_bits - 1 - i)
            return jnp.where(ties_before(cand) < need_r, cand, j)
        jcut_ref[...] = lax.fori_loop(0, idx_bits, idx_body, jnp.zeros((tq, LANES), jnp.int32))

    jcut_r = jcut_ref[...]

    q = q_ref[0]
    q8 = jnp.concatenate(
        [_split_heads_rows(q[:, p * LANES:(p + 1) * LANES]) for p in range(4)], axis=0)
    m_ref[...] = jnp.full(m_ref.shape, NEG_BIG, F32)
    l_ref[...] = jnp.zeros(l_ref.shape, F32)
    acc_ref[...] = jnp.zeros(acc_ref.shape, F32)

    def attend(off, width):
        s = _nt_dot(q8, k2_ref[0, pl.ds(off, width), :])
        bias = []
        for j in range(width // LANES):
            keyj = keys_ref[:, pl.ds(pl.multiple_of(off + j * LANES, LANES), LANES)]
            kpos = off + j * LANES + lane
            sel = ((keyj > thr_r) | ((keyj == thr_r) & (kpos <= jcut_r))) & (kpos <= qpos)
            bias.append(jnp.where(sel, 0.0, NEG_BIG))
        _online_update(s, v2_ref[0, pl.ds(off, width), :], m_ref, l_ref, acc_ref, bias=bias)

    def attn_body(c, carry):
        attend(pl.multiple_of(c * tk, tk), tk)
        return carry

    lax.fori_loop(0, nkv, attn_body, 0)

    o = _normalised(l_ref, acc_ref)
    lo = lane < HEAD_DIM
    for p in range(4):
        o_ref[0, :, p * LANES:(p + 1) * LANES] = jnp.where(
            lo, o[(2 * p) * tq:(2 * p + 1) * tq], o[(2 * p + 1) * tq:(2 * p + 2) * tq]
        ).astype(o_ref.dtype)


def _dsa_attn(bq, iq, iw, ik4, k2, v2, tq, tk):
    B, S, C = bq.shape
    topk = min(IDX_TOPK_MAX, S // 4)
    idx_bits = max(1, (S - 1).bit_length())
    kern = functools.partial(_dsa_kernel, tq=tq, tk=tk, topk=topk, idx_bits=idx_bits)
    row = lambda width: pl.BlockSpec((1, tq, width), lambda b, i: (b, i, 0))
    full = _resident((1, S, LANES), lambda b, i: (b, 0, 0))
    return pl.pallas_call(
        kern,
        grid=(B, S // tq),
        in_specs=[row(C), row(LANES), row(LANES), full, full, full],
        out_specs=row(C),
        out_shape=jax.ShapeDtypeStruct(bq.shape, BF16),
        scratch_shapes=[pltpu.VMEM((tq, S), jnp.int32),
                        pltpu.VMEM((S // BIT_GROUP, 32, 8, tq), jnp.int32),
                        pltpu.VMEM((S // BIT_GROUP, 8, tq), jnp.int32),
                        pltpu.VMEM((tq, LANES), jnp.int32),
                        pltpu.VMEM((8 * tq, LANES), F32), pltpu.VMEM((8 * tq, LANES), F32),
                        pltpu.VMEM((8 * tq, LANES), F32)],
        compiler_params=_cparams(("parallel", "arbitrary")),
        name="dsa_attn",
    )(bq, iq, iw, ik4, k2, v2)


MOBA_TILE_BLOCKS = 4


def _moba_kernel(q_ref, k_ref, v_ref, km_ref, o_ref, m_ref, l_ref, acc_ref, *, nb):
    nqb = MOBA_TILE_BLOCKS
    blk_rows = MOBA_BLOCK
    grp = 2 * blk_rows
    i = pl.program_id(2)
    first = nqb * i
    lane = lax.broadcasted_iota(jnp.int32, (1, LANES), 1)
    q = q_ref[0]
    q2 = jnp.concatenate(
        [_split_heads_rows(q[j * blk_rows:(j + 1) * blk_rows]) for j in range(nqb)], axis=0)

    km = jnp.concatenate([km_ref[0], jnp.zeros((LANES - nb, LANES), F32)], axis=0)
    km_hi = km.astype(BF16)
    r1 = km - km_hi.astype(F32)
    km_mid = r1.astype(BF16)
    km_lo = (r1 - km_mid.astype(F32)).astype(BF16)
    nb8 = -(-nb // 8) * 8
    gate_t = (_nt_dot(km_hi, q2) + _nt_dot(km_mid, q2) + _nt_dot(km_lo, q2))[:nb8]
    blk = lax.broadcasted_iota(jnp.int32, (nb8, 1), 0)
    blk_f = blk.astype(F32)
    col = lax.broadcasted_iota(jnp.int32, (1, nqb * grp), 1)
    past = blk < first + (col >> (grp.bit_length() - 1))
    g = jnp.where(past, gate_t, -jnp.inf)
    sel = jnp.zeros(g.shape, jnp.bool_)
    for _ in range(min(MOBA_TOPK, nb - 1)):
        mx = jnp.max(g, axis=0, keepdims=True)
        pick = blk_f == jnp.min(jnp.where(g == mx, blk_f, float(LANES)), axis=0, keepdims=True)
        sel = sel | pick
        g = jnp.where(pick, -jnp.inf, g)
    bias_t = jnp.concatenate([jnp.where(sel & past, 0.0, NEG_BIG),
                              jnp.full((LANES - nb8, nqb * grp), NEG_BIG, F32)], axis=0)
    qa = jnp.concatenate([q2, bias_t.T.astype(BF16)], axis=1)

    m_ref[...] = jnp.full(m_ref.shape, NEG_BIG, F32)
    l_ref[...] = jnp.zeros(l_ref.shape, F32)
    acc_ref[...] = jnp.zeros(acc_ref.shape, F32)

    def biased_logits(lhs, first_blk, nblk):
        n = nblk * blk_rows
        off = pl.multiple_of(first_blk * blk_rows, blk_rows)
        row_blk = lax.broadcasted_iota(jnp.int32, (n, 1), 0) >> (MOBA_BLOCK.bit_length() - 1)
        onehot = jnp.where(lane == first_blk + row_blk, 1.0, 0.0).astype(BF16)
        ka = jnp.concatenate([k_ref[0, pl.ds(off, n), :], onehot], axis=1)
        return _nt_dot(lhs, ka), v_ref[0, pl.ds(off, n), :]

    def body(c, carry):
        s, v = biased_logits(qa, nqb * c, nqb)
        _online_update(s, v, m_ref, l_ref, acc_ref)
        return carry

    lax.fori_loop(0, i, body, 0)

    r = lax.broadcasted_iota(jnp.int32, (grp, 1), 0)
    causal = (lax.broadcasted_iota(jnp.int32, (1, blk_rows), 1)
              <= jnp.where(r >= blk_rows, r - blk_rows, r))
    for j in range(nqb):
        off = pl.multiple_of((first + j) * blk_rows, blk_rows)
        s = jnp.where(causal, _nt_dot(q2[j * grp:(j + 1) * grp], k_ref[0, pl.ds(off, blk_rows), :]),
                      NEG_BIG)
        if j + 1 < nqb:
            s_later, _ = biased_logits(qa[(j + 1) * grp:], first + j, 1)
            s = jnp.concatenate([s, s_later], axis=0)
        _online_update(s, v_ref[0, pl.ds(off, blk_rows), :], m_ref, l_ref, acc_ref, row0=j * grp)

    o = _normalised(l_ref, acc_ref)
    lo = lane < HEAD_DIM
    for j in range(nqb):
        o_ref[0, j * blk_rows:(j + 1) * blk_rows] = jnp.where(
            lo, o[j * grp:j * grp + blk_rows], o[j * grp + blk_rows:(j + 1) * grp]).astype(o_ref.dtype)


def _moba_attn(cq, ck, cv, kmean):
    B, S, C = cq.shape
    nb = S // MOBA_BLOCK
    tq = MOBA_TILE_BLOCKS * MOBA_BLOCK
    return pl.pallas_call(
        functools.partial(_moba_kernel, nb=nb),
        grid=(B, C // LANES, S // tq),
        in_specs=[pl.BlockSpec((1, tq, LANES), lambda b, p, i: (b, i, p)),
                  pl.BlockSpec((1, S, LANES), lambda b, p, i: (b, 0, p)),
                  pl.BlockSpec((1, S, LANES), lambda b, p, i: (b, 0, p)),
                  pl.BlockSpec((1, nb, LANES), lambda b, p, i: (b, 0, p))],
        out_specs=pl.BlockSpec((1, tq, LANES), lambda b, p, i: (b, i, p)),
        out_shape=jax.ShapeDtypeStruct(cq.shape, BF16),
        scratch_shapes=[pltpu.VMEM((2 * tq, LANES), F32), pltpu.VMEM((2 * tq, LANES), F32),
                        pltpu.VMEM((2 * tq, LANES), F32)],
        compiler_params=_cparams(("parallel", "parallel", "arbitrary")),
        name="moba_attn",
    )(cq, ck, cv, kmean)


def _sigmoid(x):
    return 1.0 / (1.0 + jnp.exp(-x))


def _merge_kernel(ya_ref, yb_ref, yc_ref, gates_ref, x_ref, wb_ref, wo_ref, g_ref, o_ref):
    d = x_ref.shape[2]
    merged = None
    for n, y_ref in enumerate((ya_ref, yb_ref, yc_ref)):
        br = jnp.dot(y_ref[0], wb_ref[n], preferred_element_type=F32)
        term = _sigmoid(gates_ref[0, :, n * d:(n + 1) * d].astype(F32)) * br
        merged = term if merged is None else merged + term
    m = jnp.dot(merged.astype(BF16), wo_ref[...], preferred_element_type=F32)
    o_ref[0] = x_ref[0] + _rms(m, g_ref[...])


def _merge(ya, yb, yc, gates, x, wb, wo, layer, g, tm):
    B, S, D = x.shape
    row = lambda width: pl.BlockSpec((1, tm, width), lambda b, m: (b, m, 0))
    return pl.pallas_call(
        _merge_kernel,
        grid=(B, S // tm),
        in_specs=[row(BRANCH_WIDTH), row(BRANCH_WIDTH), row(BRANCH_WIDTH), row(N_BRANCH * D), row(D),
                  _resident((None,) + wb.shape[1:], lambda b, m: (layer, 0, 0, 0)),
                  _resident((None,) + wo.shape[1:], lambda b, m: (layer, 0, 0)),
                  _resident((1, D), lambda b, m: (0, 0))],
        out_specs=row(D),
        out_shape=jax.ShapeDtypeStruct(x.shape, F32),
        compiler_params=_cparams(("parallel", "parallel")),
        name="merge_out",
    )(ya, yb, yc, gates, x, wb, wo, g.reshape(1, D))


def _ffn_kernel(x_ref, g_in_ref, wi_ref, wo_ref, g_out_ref, o_ref, acc_ref, *, d_ff, tf):
    x = x_ref[0]
    h = _rms(x, g_in_ref[...]).astype(BF16)
    acc_ref[...] = jnp.zeros(acc_ref.shape, F32)

    def body(c, carry):
        off = pl.multiple_of(c * tf, tf)
        gt = jnp.dot(h, wi_ref[:, pl.ds(off, tf)], preferred_element_type=F32)
        up_off = pl.multiple_of(d_ff + off, LANES)
        up = jnp.dot(h, wi_ref[:, pl.ds(up_off, tf)], preferred_element_type=F32)
        act = (gt * _sigmoid(gt) * up).astype(BF16)
        acc_ref[...] += jnp.dot(act, wo_ref[pl.ds(off, tf), :], preferred_element_type=F32)
        return carry

    lax.fori_loop(0, d_ff // tf, body, 0)
    o_ref[0] = x + _rms(acc_ref[...], g_out_ref[...])


def _ffn(x, g_in, wi, wo, layer, g_out, tm, tf):
    B, S, D = x.shape
    d_ff = wo.shape[1]
    row = pl.BlockSpec((1, tm, D), lambda b, m: (b, m, 0))
    vec = _resident((1, D), lambda b, m: (0, 0))
    return pl.pallas_call(
        functools.partial(_ffn_kernel, d_ff=d_ff, tf=tf),
        grid=(B, S // tm),
        in_specs=[row, vec, _resident((None,) + wi.shape[1:], lambda b, m: (layer, 0, 0)),
                  _resident((None,) + wo.shape[1:], lambda b, m: (layer, 0, 0)), vec],
        out_specs=row,
        out_shape=jax.ShapeDtypeStruct(x.shape, F32),
        scratch_shapes=[pltpu.VMEM((tm, D), F32)],
        compiler_params=_cparams(("parallel", "parallel")),
        name="swiglu_ffn",
    )(x, g_in.reshape(1, D), wi, wo, g_out.reshape(1, D))


class _Tiles(NamedTuple):
    proj_rows: int
    dense_rows: int
    ffn_cols: int
    diff_tile: int
    dsa_queries: int
    dsa_keys: int


def _tiles(seq):
    return _Tiles(proj_rows=min(512, seq), dense_rows=min(1024, seq), ffn_cols=256,
                  diff_tile=min(1024, seq), dsa_queries=min(256, seq), dsa_keys=min(512, seq))


def kernel(x, positions, w_in, w_branch, w_out, lambda_q1, lambda_k1, lambda_q2, lambda_k2,
           subln_g, norm_g, w_ffn_in, w_ffn_out):
    B, S, D = x.shape
    depth = w_in.shape[0]
    tiles = _tiles(S)
    assert S % (MOBA_TILE_BLOCKS * MOBA_BLOCK) == 0 and S % tiles.diff_tile == 0
    tab = _rope_table(positions, tiles.proj_rows)
    assert w_in.shape[2] == B_START + B_COLS
    w_bf = w_in.astype(BF16)
    w_a, w_b = w_bf[..., :A_COLS], w_bf[..., B_START:]
    wb, wo = w_branch.astype(BF16), w_out.astype(BF16)
    wi_ffn, wo_ffn = w_ffn_in.astype(BF16), w_ffn_out.astype(BF16)
    for l in range(depth):
        lam_init = 0.8 - 0.6 * math.exp(-0.3 * l)
        (aq, ak, av, bq, k2, v2, iq, ik4, iw, cq, ck, cv, gates, kmean) = _inproj(
            x, norm_g[l, 0], tab, w_a, w_b, l, tiles.proj_rows)
        kmean = kmean.reshape(B, S // MOBA_BLOCK, 512)
        lam_rows = jnp.stack([lambda_q1[l], lambda_k1[l], lambda_q2[l], lambda_k2[l]])
        ya = _diff_attn(aq, ak, av, lam_rows, subln_g[l], lam_init, t=tiles.diff_tile)
        yb = _dsa_attn(bq, iq, iw, ik4, k2, v2, tq=tiles.dsa_queries, tk=tiles.dsa_keys)
        yc = _moba_attn(cq, ck, cv, kmean)
        x = _merge(ya, yb, yc, gates, x, wb, wo, l, norm_g[l, 1], tiles.dense_rows)
        x = _ffn(x, norm_g[l, 2], wi_ffn, wo_ffn, l, norm_g[l, 3], tiles.dense_rows,
                 tf=tiles.ffn_cols)
    return x
```

```python
import functools
import math
from typing import NamedTuple

import jax
import jax.numpy as jnp
from jax import lax
from jax.experimental import pallas as pl
from jax.experimental.pallas import tpu as pltpu

F32 = jnp.float32
BF16 = jnp.bfloat16

LANES = 128
HEAD_DIM = 64
ROPE_THETA = 500000.0
NORM_EPS = 1e-6
IDX_HEADS = 4
IDX_DIM = 32
IDX_TOPK_MAX = 256
MOBA_BLOCK = 256
MOBA_TOPK = 3
N_BRANCH = 3
BRANCH_WIDTH = 512
NEG_BIG = -1e30
LOG2E = math.log2(math.e)
ROW_BLOCK = 64
INT_MIN = -2 ** 31
VMEM_LIMIT = 56 * 1024 * 1024

A_AQ, A_AK, A_AV, A_BQ, A_KV, A_IQ, A_IKW, A_COLS = 0, 512, 1024, 1536, 2048, 2176, 2304, 2432
B_START = 2340
B_CQ, B_CK, B_CV, B_G, B_COLS = 0, 512, 1024, 1536, 4608
Q_SCALE = HEAD_DIM ** -0.5 * LOG2E


def _nt_dot(a, b):
    return lax.dot_general(a, b, (((1,), (1,)), ((), ())), preferred_element_type=F32)


def _rms(x, g):
    return x * lax.rsqrt(jnp.mean(x * x, axis=-1, keepdims=True) + NORM_EPS) * g


def _cparams(sem):
    return pltpu.CompilerParams(dimension_semantics=sem, vmem_limit_bytes=VMEM_LIMIT)


def _resident(shape, index_map):
    return pl.BlockSpec(shape, index_map, pipeline_mode=pl.Buffered(1))


def _rope_factors(pos, head_dim):
    lane = lax.broadcasted_iota(jnp.int32, (1, LANES), 1)
    rot = head_dim // 4
    half = rot // 2
    d = lane & (head_dim - 1)
    fi = d & (half - 1)
    inv = jnp.zeros((1, LANES), F32)
    for i in range(half):
        inv = jnp.where(fi == i, F32(ROPE_THETA ** (-(2.0 * i) / rot)), inv)
    ang = pos * inv
    c = jnp.where(d < rot, jnp.cos(ang), 1.0)
    s = jnp.where(d < half, -jnp.sin(ang), jnp.where(d < rot, jnp.sin(ang), 0.0))
    return c, s


def _inproj_kernel(x_ref, g_ref, pos_ref, wa_ref, wb_ref,
                   aq_ref, ak_ref, av_ref, bq_ref, k2_ref, v2_ref, iq_ref, ik4_ref, iw_ref,
                   cq_ref, ck_ref, cv_ref, gates_ref, kmean_ref):
    h = _rms(x_ref[0], g_ref[...]).astype(BF16)
    lane = lax.broadcasted_iota(jnp.int32, (1, LANES), 1)

    def proj(w_ref, off, width):
        return jnp.dot(h, w_ref[:, off:off + width], preferred_element_type=F32)

    pos = pos_ref[0].astype(F32)
    factors = {hd: _rope_factors(pos, hd) for hd in (HEAD_DIM, IDX_DIM)}

    def rope(y, head_dim):
        half = head_dim // 8
        c, s = factors[head_dim]
        first = (lane & (head_dim - 1)) < half
        blocks = []
        for j in range(y.shape[1] // LANES):
            yj = y[:, j * LANES:(j + 1) * LANES]
            up = pltpu.roll(yj, LANES - half, 1)
            dn = pltpu.roll(yj, half, 1)
            blocks.append(yj * c + jnp.where(first, up, dn) * s)
        return blocks

    def store(out_ref, blocks):
        for j, r in enumerate(blocks):
            out_ref[0, :, j * LANES:(j + 1) * LANES] = r.astype(out_ref.dtype)

    store(aq_ref, rope(proj(wa_ref, A_AQ, 512) * Q_SCALE, HEAD_DIM))
    store(ak_ref, rope(proj(wa_ref, A_AK, 512), HEAD_DIM))
    av_ref[0] = proj(wa_ref, A_AV, 512).astype(av_ref.dtype)
    store(bq_ref, rope(proj(wa_ref, A_BQ, 512) * Q_SCALE, HEAD_DIM))
    kv = proj(wa_ref, A_KV, LANES)
    vk = pltpu.roll(kv, HEAD_DIM, 1)
    lo = lane < HEAD_DIM
    store(k2_ref, rope(jnp.where(lo, kv, vk), HEAD_DIM))
    v2_ref[0] = jnp.where(lo, vk, kv).astype(v2_ref.dtype)
    store(iq_ref, rope(proj(wa_ref, A_IQ, LANES), IDX_DIM))
    ikw = proj(wa_ref, A_IKW, LANES)
    ik = rope(ikw, IDX_DIM)[0]
    ik4 = ik
    for rep in range(1, LANES // IDX_DIM):
        ik4 = jnp.where(lane < rep * IDX_DIM, ik4, pltpu.roll(ik, rep * IDX_DIM, 1))
    ik4_ref[0] = ik4.astype(ik4_ref.dtype)
    iw_ref[0] = jnp.where(lane < IDX_HEADS, pltpu.roll(ikw, LANES - IDX_DIM, 1), 0.0)
    store(cq_ref, rope(proj(wb_ref, B_CQ, 512) * Q_SCALE, HEAD_DIM))
    ck_blocks = rope(proj(wb_ref, B_CK, 512), HEAD_DIM)
    store(ck_ref, ck_blocks)
    cv_ref[0] = proj(wb_ref, B_CV, 512).astype(cv_ref.dtype)
    gates_ref[0] = proj(wb_ref, B_G, 3 * 1024).astype(gates_ref.dtype)

    tm = x_ref.shape[1]
    for blk in range(tm // MOBA_BLOCK):
        for j, r in enumerate(ck_blocks):
            part = r[blk * MOBA_BLOCK:(blk + 1) * MOBA_BLOCK]
            kmean_ref[0, 0, blk:blk + 1, j * LANES:(j + 1) * LANES] = (
                jnp.mean(part, axis=0, keepdims=True))


def _inproj(x, g, positions, w_a, w_b, layer, tm):
    B, S, D = x.shape
    nm = S // tm
    row = lambda width: pl.BlockSpec((1, tm, width), lambda b, m: (b, m, 0))
    shp = lambda width, dt: jax.ShapeDtypeStruct((B, S, width), dt)
    widths = [(512, BF16)] * 4 + [(LANES, BF16)] * 4 + [(LANES, F32)] + [(512, BF16)] * 3 + [(3072, BF16)]
    out_specs = [row(w) for w, _ in widths]
    out_shape = [shp(w, dt) for w, dt in widths]
    out_specs.append(pl.BlockSpec((1, 1, tm // MOBA_BLOCK, 512), lambda b, m: (b, m, 0, 0)))
    out_shape.append(jax.ShapeDtypeStruct((B, nm, tm // MOBA_BLOCK, 512), F32))
    return pl.pallas_call(
        _inproj_kernel,
        grid=(B, nm),
        in_specs=[row(D),
                  _resident((1, D), lambda b, m: (0, 0)),
                  row(1),
                  _resident((None, D, A_COLS), lambda b, m: (layer, 0, 0)),
                  _resident((None, D, B_COLS), lambda b, m: (layer, 0, 0))],
        out_specs=out_specs,
        out_shape=out_shape,
        compiler_params=_cparams(("parallel", "parallel")),
        name="inproj",
    )(x, g.reshape(1, D), positions.reshape(B, S, 1), w_a, w_b)


def _split_heads_rows(q):
    lo = lax.broadcasted_iota(jnp.int32, (1, LANES), 1) < HEAD_DIM
    zero = jnp.zeros_like(q)
    return jnp.concatenate([jnp.where(lo, q, zero), jnp.where(lo, zero, q)], axis=0)


def _online_update(s, v, m_ref, l_ref, acc_ref, bias=None, row0=0):
    rows_total = s.shape[0]
    nblk = s.shape[1] // LANES
    p_rows, alphas = [], []
    for r0 in range(0, rows_total, ROW_BLOCK):
        rows = slice(row0 + r0, row0 + r0 + ROW_BLOCK)
        cols = [s[r0:r0 + ROW_BLOCK, j * LANES:(j + 1) * LANES] for j in range(nblk)]
        if bias is not None:
            b0 = r0 % bias[0].shape[0]
            cols = [c + bj[b0:b0 + ROW_BLOCK] for c, bj in zip(cols, bias)]
        m_prev = m_ref[rows]
        m_new = jnp.maximum(
            m_prev, jnp.max(functools.reduce(jnp.maximum, cols), axis=1, keepdims=True))
        alpha = jnp.exp2(m_prev - m_new)
        ps = [jnp.exp2(c - m_new) for c in cols]
        l_ref[rows] = alpha * l_ref[rows] + functools.reduce(jnp.add, ps)
        m_ref[rows] = m_new
        p_rows.append(jnp.concatenate([pj.astype(BF16) for pj in ps], axis=1))
        alphas.append(alpha)
    p = jnp.concatenate(p_rows, axis=0)
    alpha = jnp.concatenate(alphas, axis=0)
    span = slice(row0, row0 + rows_total)
    acc_ref[span] = alpha * acc_ref[span] + jnp.dot(p, v, preferred_element_type=F32)


def _normalised(l_ref, acc_ref):
    return acc_ref[...] / jnp.sum(l_ref[...], axis=1, keepdims=True)


def _diff_attn_kernel(q_ref, k_ref, v_ref, lam_ref, g_ref, o_ref, m_ref, l_ref, acc_ref,
                      *, t, lam_init):
    qi = pl.program_id(2)
    half = t // 2
    q = q_ref[0]
    q2 = jnp.concatenate([_split_heads_rows(q[:half]), _split_heads_rows(q[half:])], axis=0)
    m_ref[...] = jnp.full(m_ref.shape, NEG_BIG, F32)
    l_ref[...] = jnp.zeros(l_ref.shape, F32)
    acc_ref[...] = jnp.zeros(acc_ref.shape, F32)

    def body(c, carry):
        off = pl.multiple_of(c * t, t)
        _online_update(_nt_dot(q2, k_ref[0, pl.ds(off, t), :]), v_ref[0, pl.ds(off, t), :],
                       m_ref, l_ref, acc_ref)
        return carry

    lax.fori_loop(0, qi, body, 0)

    r = lax.broadcasted_iota(jnp.int32, (t, 1), 0)
    causal = lax.broadcasted_iota(jnp.int32, (1, half), 1) <= jnp.where(r >= half, r - half, r)
    off_a = pl.multiple_of(qi * t, t)
    s = _nt_dot(q2, k_ref[0, pl.ds(off_a, half), :])
    s = jnp.concatenate([jnp.where(causal, s[:t], NEG_BIG), s[t:]], axis=0)
    _online_update(s, v_ref[0, pl.ds(off_a, half), :], m_ref, l_ref, acc_ref)
    off_b = pl.multiple_of(qi * t + half, half)
    s = jnp.where(causal, _nt_dot(q2[t:], k_ref[0, pl.ds(off_b, half), :]), NEG_BIG)
    _online_update(s, v_ref[0, pl.ds(off_b, half), :], m_ref, l_ref, acc_ref, row0=t)

    lam_rows = lam_ref[...]
    e1 = jnp.exp(jnp.sum(lam_rows[0:1] * lam_rows[1:2], axis=1, keepdims=True))
    e2 = jnp.exp(jnp.sum(lam_rows[2:3] * lam_rows[3:4], axis=1, keepdims=True))
    lam = e1 - e2 + lam_init
    o = _normalised(l_ref, acc_ref)
    o1 = jnp.concatenate([o[:half], o[t:t + half]], axis=0)
    o2 = jnp.concatenate([o[half:t], o[t + half:]], axis=0)
    o_ref[0] = (_rms(o1 - lam * o2, g_ref[...]) * (1.0 - lam_init)).astype(o_ref.dtype)


def _diff_attn(aq, ak, av, lam_rows, subln_g, lam_init, t):
    B, S, _ = aq.shape
    nh = aq.shape[2] // LANES
    kern = functools.partial(_diff_attn_kernel, t=t, lam_init=lam_init)
    return pl.pallas_call(
        kern,
        grid=(B, nh, S // t),
        in_specs=[pl.BlockSpec((1, t, LANES), lambda b, h, i: (b, i, h)),
                  pl.BlockSpec((1, S, LANES), lambda b, h, i: (b, 0, h)),
                  pl.BlockSpec((1, S, LANES), lambda b, h, i: (b, 0, h)),
                  pl.BlockSpec((4, HEAD_DIM), lambda b, h, i: (0, 0)),
                  pl.BlockSpec((1, LANES), lambda b, h, i: (0, 0))],
        out_specs=pl.BlockSpec((1, t, LANES), lambda b, h, i: (b, i, h)),
        out_shape=jax.ShapeDtypeStruct(aq.shape, BF16),
        scratch_shapes=[pltpu.VMEM((2 * t, LANES), F32), pltpu.VMEM((2 * t, LANES), F32),
                        pltpu.VMEM((2 * t, LANES), F32)],
        compiler_params=_cparams(("parallel", "parallel", "arbitrary")),
        name="diff_attn",
    )(aq, ak, av, lam_rows, subln_g.reshape(1, LANES))


BIT_GROUP = 256


def _bit_transpose32(words):
    w = list(words)
    for j, m in ((16, 0x0000FFFF), (8, 0x00FF00FF), (4, 0x0F0F0F0F), (2, 0x33333333), (1, 0x55555555)):
        for k in range(32):
            if k & j == 0:
                t = (w[k] ^ (w[k + j] >> j)) & m
                w[k] = w[k] ^ t
                w[k + j] = w[k + j] ^ (t << j)
    return w


def _dsa_kernel(q_ref, iq_ref, iw_ref, ik4_ref, k2_ref, v2_ref, o_ref,
                keys_ref, planes_ref, act_ref, jcut_ref, m_ref, l_ref, acc_ref,
                *, tq, tk, topk, idx_bits):
    qi = pl.program_id(1)
    nkv = (qi * tq + tq - 1) // tk + 1
    lane = lax.broadcasted_iota(jnp.int32, (1, LANES), 1)
    qpos = qi * tq + lax.broadcasted_iota(jnp.int32, (tq, 1), 0)
    lane_tk = lax.broadcasted_iota(jnp.int32, (1, tk), 1)
    nsub = tk // LANES
    seq = keys_ref.shape[1]

    @pl.when(qi == 0)
    def _():
        planes_ref[...] = jnp.zeros(planes_ref.shape, jnp.int32)

    iq = iq_ref[0]
    zero = jnp.zeros_like(iq)
    iq4 = jnp.concatenate(
        [jnp.where((lane >> (IDX_DIM.bit_length() - 1)) == hh, iq, zero) for hh in range(IDX_HEADS)],
        axis=0)
    iw = iw_ref[0]
    iw_cols = [iw[:, hh:hh + 1] for hh in range(IDX_HEADS)]

    def score_body(c, carry):
        off = pl.multiple_of(c * tk, tk)
        rel = jnp.maximum(_nt_dot(iq4, ik4_ref[0, pl.ds(off, tk), :]), 0.0)
        score = iw_cols[0] * rel[0:tq]
        for hh in range(1, IDX_HEADS):
            score = score + iw_cols[hh] * rel[hh * tq:(hh + 1) * tq]
        bits = lax.bitcast_convert_type(score, jnp.int32)
        key = bits ^ ((bits >> 31) & jnp.int32(0x7FFFFFFF))
        kpos = off + lane_tk
        key = jnp.where(score == 0.0, (seq - 1) - kpos, jnp.where(key > 0, key + seq, key))
        key = jnp.where(kpos <= qpos, key, jnp.int32(INT_MIN))
        keys_ref[:, pl.ds(off, tk)] = key
        kts = [key[:, j * LANES:(j + 1) * LANES].T for j in range(nsub)]
        for gl in range(tk // BIT_GROUP):
            words = [kts[(gl * BIT_GROUP + 8 * g) // LANES][(8 * g) % LANES:(8 * g) % LANES + 8]
                     for g in range(32)]
            words = _bit_transpose32(words)
            words[0] = ~words[0]
            for b in range(32):
                planes_ref[c * (tk // BIT_GROUP) + gl, b] = words[b]
        return carry

    lax.fori_loop(0, nkv, score_body, 0)

    ngroups = planes_ref.shape[0]
    live = nkv * (tk // BIT_GROUP)
    for g in range(ngroups):
        act_ref[g] = jnp.broadcast_to(jnp.where(g < live, jnp.int32(-1), jnp.int32(0)), (8, tq))

    def sweep(i, keep, first=False):
        parts = [jnp.zeros((8, tq), jnp.int32) for _ in range(4)]
        for g in range(ngroups):
            act = act_ref[g]
            if not first:
                act = act & (planes_ref[g, i - 1] ^ keep)
                act_ref[g] = act
            parts[g % 4] = parts[g % 4] + lax.population_count(act & planes_ref[g, i])
        cnt = (parts[0] + parts[1]) + (parts[2] + parts[3])
        return jnp.sum(cnt.astype(F32), axis=0, keepdims=True)

    def decide(i, ones, thr_u, n_gt):
        take = n_gt + ones >= topk
        thr_u = thr_u | jnp.where(take, jnp.left_shift(jnp.int32(1), 31 - i), 0)
        return thr_u, jnp.where(take, n_gt, n_gt + ones), jnp.where(take, 0, -1).astype(jnp.int32)

    def pass_body(i, carry):
        thr_u, n_gt, keep = carry
        return decide(i, sweep(i, keep), thr_u, n_gt)

    start = decide(0, sweep(0, None, first=True),
                   jnp.zeros((1, tq), jnp.int32), jnp.zeros((1, tq), F32))
    thr_u, n_gt, keep = lax.fori_loop(1, 32, pass_body, start)
    n_eq = jnp.zeros((8, tq), jnp.int32)
    for g in range(ngroups):
        n_eq = n_eq + lax.population_count(act_ref[g] & (planes_ref[g, 31] ^ keep))
    thr = thr_u ^ jnp.int32(INT_MIN)
    cnt_ge = n_gt + jnp.sum(n_eq.astype(F32), axis=0, keepdims=True)
    need = topk - n_gt

    def as_rows(x):
        return jnp.broadcast_to(x, (LANES, tq)).T

    thr_r = as_rows(thr)
    need_r = as_rows(need)
    jcut_ref[...] = jnp.full((tq, LANES), 2 ** 30, jnp.int32)

    @pl.when(jnp.max(cnt_ge) > topk)
    def _():
        def ties_before(cand):
            def body(c, cnt):
                off = pl.multiple_of(c * tk, tk)
                for j in range(nsub):
                    keyj = keys_ref[:, pl.ds(pl.multiple_of(off + j * LANES, LANES), LANES)]
                    hit = (keyj == thr_r) & (off + j * LANES + lane < cand)
                    cnt = jnp.where(hit, cnt + 1, cnt)
                return cnt
            cnt = lax.fori_loop(0, nkv, body, jnp.zeros((tq, LANES), jnp.int32))
            return jnp.sum(cnt.astype(F32), axis=1, keepdims=True)

        def idx_body(i, j):
            cand = j + jnp.left_shift(jnp.int32(1), idx_bits - 1 - i)
            return jnp.where(ties_before(cand) < need_r, cand, j)
        jcut_ref[...] = lax.fori_loop(0, idx_bits, idx_body, jnp.zeros((tq, LANES), jnp.int32))

    jcut_r = jcut_ref[...]

    q = q_ref[0]
    q8 = jnp.concatenate(
        [_split_heads_rows(q[:, p * LANES:(p + 1) * LANES]) for p in range(4)], axis=0)
    m_ref[...] = jnp.full(m_ref.shape, NEG_BIG, F32)
    l_ref[...] = jnp.zeros(l_ref.shape, F32)
    acc_ref[...] = jnp.zeros(acc_ref.shape, F32)

    def attend(off, width):
        s = _nt_dot(q8, k2_ref[0, pl.ds(off, width), :])
        bias = []
        for j in range(width // LANES):
            keyj = keys_ref[:, pl.ds(pl.multiple_of(off + j * LANES, LANES), LANES)]
            kpos = off + j * LANES + lane
            sel = ((keyj > thr_r) | ((keyj == thr_r) & (kpos <= jcut_r))) & (kpos <= qpos)
            bias.append(jnp.where(sel, 0.0, NEG_BIG))
        _online_update(s, v2_ref[0, pl.ds(off, width), :], m_ref, l_ref, acc_ref, bias=bias)

    def attn_body(c, carry):
        attend(pl.multiple_of(c * tk, tk), tk)
        return carry

    lax.fori_loop(0, nkv, attn_body, 0)

    o = _normalised(l_ref, acc_ref)
    lo = lane < HEAD_DIM
    for p in range(4):
        o_ref[0, :, p * LANES:(p + 1) * LANES] = jnp.where(
            lo, o[(2 * p) * tq:(2 * p + 1) * tq], o[(2 * p + 1) * tq:(2 * p + 2) * tq]
        ).astype(o_ref.dtype)


def _dsa_attn(bq, iq, iw, ik4, k2, v2, tq, tk):
    B, S, C = bq.shape
    topk = min(IDX_TOPK_MAX, S // 4)
    idx_bits = max(1, (S - 1).bit_length())
    kern = functools.partial(_dsa_kernel, tq=tq, tk=tk, topk=topk, idx_bits=idx_bits)
    row = lambda width: pl.BlockSpec((1, tq, width), lambda b, i: (b, i, 0))
    full = _resident((1, S, LANES), lambda b, i: (b, 0, 0))
    return pl.pallas_call(
        kern,
        grid=(B, S // tq),
        in_specs=[row(C), row(LANES), row(LANES), full, full, full],
        out_specs=row(C),
        out_shape=jax.ShapeDtypeStruct(bq.shape, BF16),
        scratch_shapes=[pltpu.VMEM((tq, S), jnp.int32),
                        pltpu.VMEM((S // BIT_GROUP, 32, 8, tq), jnp.int32),
                        pltpu.VMEM((S // BIT_GROUP, 8, tq), jnp.int32),
                        pltpu.VMEM((tq, LANES), jnp.int32),
                        pltpu.VMEM((8 * tq, LANES), F32), pltpu.VMEM((8 * tq, LANES), F32),
                        pltpu.VMEM((8 * tq, LANES), F32)],
        compiler_params=_cparams(("parallel", "arbitrary")),
        name="dsa_attn",
    )(bq, iq, iw, ik4, k2, v2)


MOBA_TILE_BLOCKS = 4


def _moba_kernel(q_ref, k_ref, v_ref, km_ref, o_ref, m_ref, l_ref, acc_ref, *, nb):
    nqb = MOBA_TILE_BLOCKS
    blk_rows = MOBA_BLOCK
    grp = 2 * blk_rows
    i = pl.program_id(2)
    first = nqb * i
    lane = lax.broadcasted_iota(jnp.int32, (1, LANES), 1)
    q = q_ref[0]
    q2 = jnp.concatenate(
        [_split_heads_rows(q[j * blk_rows:(j + 1) * blk_rows]) for j in range(nqb)], axis=0)

    km = jnp.concatenate([km_ref[0], jnp.zeros((LANES - nb, LANES), F32)], axis=0)
    km_hi = km.astype(BF16)
    r1 = km - km_hi.astype(F32)
    km_mid = r1.astype(BF16)
    km_lo = (r1 - km_mid.astype(F32)).astype(BF16)
    nb8 = -(-nb // 8) * 8
    gate_t = (_nt_dot(km_hi, q2) + _nt_dot(km_mid, q2) + _nt_dot(km_lo, q2))[:nb8]
    blk = lax.broadcasted_iota(jnp.int32, (nb8, 1), 0)
    blk_f = blk.astype(F32)
    col = lax.broadcasted_iota(jnp.int32, (1, nqb * grp), 1)
    past = blk < first + (col >> (grp.bit_length() - 1))
    g = jnp.where(past, gate_t, -jnp.inf)
    sel = jnp.zeros(g.shape, jnp.bool_)
    for _ in range(min(MOBA_TOPK, nb - 1)):
        mx = jnp.max(g, axis=0, keepdims=True)
        pick = blk_f == jnp.min(jnp.where(g == mx, blk_f, float(LANES)), axis=0, keepdims=True)
        sel = sel | pick
        g = jnp.where(pick, -jnp.inf, g)
    bias_t = jnp.concatenate([jnp.where(sel & past, 0.0, NEG_BIG),
                              jnp.full((LANES - nb8, nqb * grp), NEG_BIG, F32)], axis=0)
    qa = jnp.concatenate([q2, bias_t.T.astype(BF16)], axis=1)

    m_ref[...] = jnp.full(m_ref.shape, NEG_BIG, F32)
    l_ref[...] = jnp.zeros(l_ref.shape, F32)
    acc_ref[...] = jnp.zeros(acc_ref.shape, F32)

    def biased_logits(lhs, first_blk, nblk):
        n = nblk * blk_rows
        off = pl.multiple_of(first_blk * blk_rows, blk_rows)
        row_blk = lax.broadcasted_iota(jnp.int32, (n, 1), 0) >> (MOBA_BLOCK.bit_length() - 1)
        onehot = jnp.where(lane == first_blk + row_blk, 1.0, 0.0).astype(BF16)
        ka = jnp.concatenate([k_ref[0, pl.ds(off, n), :], onehot], axis=1)
        return _nt_dot(lhs, ka), v_ref[0, pl.ds(off, n), :]

    def body(c, carry):
        s, v = biased_logits(qa, nqb * c, nqb)
        _online_update(s, v, m_ref, l_ref, acc_ref)
        return carry

    lax.fori_loop(0, i, body, 0)

    r = lax.broadcasted_iota(jnp.int32, (grp, 1), 0)
    causal = (lax.broadcasted_iota(jnp.int32, (1, blk_rows), 1)
              <= jnp.where(r >= blk_rows, r - blk_rows, r))
    for j in range(nqb):
        off = pl.multiple_of((first + j) * blk_rows, blk_rows)
        s = jnp.where(causal, _nt_dot(q2[j * grp:(j + 1) * grp], k_ref[0, pl.ds(off, blk_rows), :]),
                      NEG_BIG)
        if j + 1 < nqb:
            s_later, _ = biased_logits(qa[(j + 1) * grp:], first + j, 1)
            s = jnp.concatenate([s, s_later], axis=0)
        _online_update(s, v_ref[0, pl.ds(off, blk_rows), :], m_ref, l_ref, acc_ref, row0=j * grp)

    o = _normalised(l_ref, acc_ref)
    lo = lane < HEAD_DIM
    for j in range(nqb):
        o_ref[0, j * blk_rows:(j + 1) * blk_rows] = jnp.where(
            lo, o[j * grp:j * grp + blk_rows], o[j * grp + blk_rows:(j + 1) * grp]).astype(o_ref.dtype)


def _moba_attn(cq, ck, cv, kmean):
    B, S, C = cq.shape
    nb = S // MOBA_BLOCK
    tq = MOBA_TILE_BLOCKS * MOBA_BLOCK
    return pl.pallas_call(
        functools.partial(_moba_kernel, nb=nb),
        grid=(B, C // LANES, S // tq),
        in_specs=[pl.BlockSpec((1, tq, LANES), lambda b, p, i: (b, i, p)),
                  pl.BlockSpec((1, S, LANES), lambda b, p, i: (b, 0, p)),
                  pl.BlockSpec((1, S, LANES), lambda b, p, i: (b, 0, p)),
                  pl.BlockSpec((1, nb, LANES), lambda b, p, i: (b, 0, p))],
        out_specs=pl.BlockSpec((1, tq, LANES), lambda b, p, i: (b, i, p)),
        out_shape=jax.ShapeDtypeStruct(cq.shape, BF16),
        scratch_shapes=[pltpu.VMEM((2 * tq, LANES), F32), pltpu.VMEM((2 * tq, LANES), F32),
                        pltpu.VMEM((2 * tq, LANES), F32)],
        compiler_params=_cparams(("parallel", "parallel", "arbitrary")),
        name="moba_attn",
    )(cq, ck, cv, kmean)


def _sigmoid(x):
    return 1.0 / (1.0 + jnp.exp(-x))


def _merge_kernel(ya_ref, yb_ref, yc_ref, gates_ref, x_ref, wb_ref, wo_ref, g_ref, o_ref):
    d = x_ref.shape[2]
    merged = None
    for n, y_ref in enumerate((ya_ref, yb_ref, yc_ref)):
        br = jnp.dot(y_ref[0], wb_ref[n], preferred_element_type=F32)
        term = _sigmoid(gates_ref[0, :, n * d:(n + 1) * d].astype(F32)) * br
        merged = term if merged is None else merged + term
    m = jnp.dot(merged.astype(BF16), wo_ref[...], preferred_element_type=F32)
    o_ref[0] = x_ref[0] + _rms(m, g_ref[...])


def _merge(ya, yb, yc, gates, x, wb, wo, layer, g, tm):
    B, S, D = x.shape
    row = lambda width: pl.BlockSpec((1, tm, width), lambda b, m: (b, m, 0))
    return pl.pallas_call(
        _merge_kernel,
        grid=(B, S // tm),
        in_specs=[row(BRANCH_WIDTH), row(BRANCH_WIDTH), row(BRANCH_WIDTH), row(N_BRANCH * D), row(D),
                  _resident((None,) + wb.shape[1:], lambda b, m: (layer, 0, 0, 0)),
                  _resident((None,) + wo.shape[1:], lambda b, m: (layer, 0, 0)),
                  _resident((1, D), lambda b, m: (0, 0))],
        out_specs=row(D),
        out_shape=jax.ShapeDtypeStruct(x.shape, F32),
        compiler_params=_cparams(("parallel", "parallel")),
        name="merge_out",
    )(ya, yb, yc, gates, x, wb, wo, g.reshape(1, D))


def _ffn_kernel(x_ref, g_in_ref, wi_ref, wo_ref, g_out_ref, o_ref, acc_ref, *, d_ff, tf):
    x = x_ref[0]
    h = _rms(x, g_in_ref[...]).astype(BF16)
    acc_ref[...] = jnp.zeros(acc_ref.shape, F32)

    def body(c, carry):
        off = pl.multiple_of(c * tf, tf)
        gt = jnp.dot(h, wi_ref[:, pl.ds(off, tf)], preferred_element_type=F32)
        up_off = pl.multiple_of(d_ff + off, LANES)
        up = jnp.dot(h, wi_ref[:, pl.ds(up_off, tf)], preferred_element_type=F32)
        act = (gt * _sigmoid(gt) * up).astype(BF16)
        acc_ref[...] += jnp.dot(act, wo_ref[pl.ds(off, tf), :], preferred_element_type=F32)
        return carry

    lax.fori_loop(0, d_ff // tf, body, 0)
    o_ref[0] = x + _rms(acc_ref[...], g_out_ref[...])


def _ffn(x, g_in, wi, wo, layer, g_out, tm, tf):
    B, S, D = x.shape
    d_ff = wo.shape[1]
    row = pl.BlockSpec((1, tm, D), lambda b, m: (b, m, 0))
    vec = _resident((1, D), lambda b, m: (0, 0))
    return pl.pallas_call(
        functools.partial(_ffn_kernel, d_ff=d_ff, tf=tf),
        grid=(B, S // tm),
        in_specs=[row, vec, _resident((None,) + wi.shape[1:], lambda b, m: (layer, 0, 0)),
                  _resident((None,) + wo.shape[1:], lambda b, m: (layer, 0, 0)), vec],
        out_specs=row,
        out_shape=jax.ShapeDtypeStruct(x.shape, F32),
        scratch_shapes=[pltpu.VMEM((tm, D), F32)],
        compiler_params=_cparams(("parallel", "parallel")),
        name="swiglu_ffn",
    )(x, g_in.reshape(1, D), wi, wo, g_out.reshape(1, D))


class _Tiles(NamedTuple):
    proj_rows: int
    dense_rows: int
    ffn_cols: int
    diff_tile: int
    dsa_queries: int
    dsa_keys: int


def _tiles(seq):
    return _Tiles(proj_rows=min(512, seq), dense_rows=min(1024, seq), ffn_cols=256,
                  diff_tile=min(1024, seq), dsa_queries=min(256, seq), dsa_keys=min(512, seq))


def kernel(x, positions, w_in, w_branch, w_out, lambda_q1, lambda_k1, lambda_q2, lambda_k2,
           subln_g, norm_g, w_ffn_in, w_ffn_out):
    B, S, D = x.shape
    depth = w_in.shape[0]
    tiles = _tiles(S)
    assert S % (MOBA_TILE_BLOCKS * MOBA_BLOCK) == 0 and S % tiles.diff_tile == 0
    assert w_in.shape[2] == B_START + B_COLS
    w_bf = w_in.astype(BF16)
    w_a, w_b = w_bf[..., :A_COLS], w_bf[..., B_START:]
    wb, wo = w_branch.astype(BF16), w_out.astype(BF16)
    wi_ffn, wo_ffn = w_ffn_in.astype(BF16), w_ffn_out.astype(BF16)
    for l in range(depth):
        lam_init = 0.8 - 0.6 * math.exp(-0.3 * l)
        (aq, ak, av, bq, k2, v2, iq, ik4, iw, cq, ck, cv, gates, kmean) = _inproj(
            x, norm_g[l, 0], positions, w_a, w_b, l, tiles.proj_rows)
        kmean = kmean.reshape(B, S // MOBA_BLOCK, 512)
        lam_rows = jnp.stack([lambda_q1[l], lambda_k1[l], lambda_q2[l], lambda_k2[l]])
        ya = _diff_attn(aq, ak, av, lam_rows, subln_g[l], lam_init, t=tiles.diff_tile)
        yb = _dsa_attn(bq, iq, iw, ik4, k2, v2, tq=tiles.dsa_queries, tk=tiles.dsa_keys)
        yc = _moba_attn(cq, ck, cv, kmean)
        x = _merge(ya, yb, yc, gates, x, wb, wo, l, norm_g[l, 1], tiles.dense_rows)
        x = _ffn(x, norm_g[l, 2], wi_ffn, wo_ffn, l, norm_g[l, 3], tiles.dense_rows,
                 tf=tiles.ffn_cols)
    return x
```

```python
import functools
import math
from typing import NamedTuple

import jax
import jax.numpy as jnp
from jax import lax
from jax.experimental import pallas as pl
from jax.experimental.pallas import tpu as pltpu

F32 = jnp.float32
BF16 = jnp.bfloat16

LANES = 128
HEAD_DIM = 64
ROPE_THETA = 500000.0
NORM_EPS = 1e-6
IDX_HEADS = 4
IDX_DIM = 32
IDX_TOPK_MAX = 256
MOBA_BLOCK = 256
MOBA_TOPK = 3
N_BRANCH = 3
BRANCH_WIDTH = 512
NEG_BIG = -1e30
LOG2E = math.log2(math.e)
ROW_BLOCK = 64
INT_MIN = -2 ** 31
VMEM_LIMIT = 56 * 1024 * 1024

A_AQ, A_AK, A_AV, A_BQ, A_KV, A_IQ, A_IKW, A_COLS = 0, 512, 1024, 1536, 2048, 2176, 2304, 2432
B_START = 2340
B_CQ, B_CK, B_CV, B_G, B_COLS = 0, 512, 1024, 1536, 4608
Q_SCALE = HEAD_DIM ** -0.5 * LOG2E


def _nt_dot(a, b):
    return lax.dot_general(a, b, (((1,), (1,)), ((), ())), preferred_element_type=F32)


def _rms(x, g):
    return x * lax.rsqrt(jnp.mean(x * x, axis=-1, keepdims=True) + NORM_EPS) * g


def _cparams(sem):
    return pltpu.CompilerParams(dimension_semantics=sem, vmem_limit_bytes=VMEM_LIMIT)


def _resident(shape, index_map):
    return pl.BlockSpec(shape, index_map, pipeline_mode=pl.Buffered(1))


def _rope_table_kernel(pos_ref, tab_ref):
    pos = pos_ref[0].astype(F32)
    lane = lax.broadcasted_iota(jnp.int32, (1, LANES), 1)

    def tables(head_dim):
        rot = head_dim // 4
        half = rot // 2
        d = lane & (head_dim - 1)
        fi = d & (half - 1)
        inv = jnp.zeros((1, LANES), F32)
        for i in range(half):
            inv = jnp.where(fi == i, F32(ROPE_THETA ** (-(2.0 * i) / rot)), inv)
        ang = pos * inv
        cos = jnp.cos(ang)
        sin = jnp.sin(ang)
        c = jnp.where(d < rot, cos, 1.0)
        s = jnp.where(d < half, -sin, jnp.where(d < rot, sin, 0.0))
        return c, s

    c64, s64 = tables(HEAD_DIM)
    c32, s32 = tables(IDX_DIM)
    tab_ref[0, :, 0 * LANES:1 * LANES] = c64
    tab_ref[0, :, 1 * LANES:2 * LANES] = s64
    tab_ref[0, :, 2 * LANES:3 * LANES] = c32
    tab_ref[0, :, 3 * LANES:4 * LANES] = s32


def _rope_table(positions, tm):
    B, S = positions.shape
    return pl.pallas_call(
        _rope_table_kernel,
        grid=(B, S // tm),
        in_specs=[pl.BlockSpec((1, tm, 1), lambda b, m: (b, m, 0))],
        out_specs=pl.BlockSpec((1, tm, 4 * LANES), lambda b, m: (b, m, 0)),
        out_shape=jax.ShapeDtypeStruct((B, S, 4 * LANES), F32),
        compiler_params=_cparams(("parallel", "parallel")),
        name="rope_table",
    )(positions.reshape(B, S, 1))


def _inproj_kernel(x_ref, g_ref, tab_ref, wa_ref, wb_ref,
                   aq_ref, ak_ref, av_ref, bq_ref, k2_ref, v2_ref, iq_ref, ik4_ref, iw_ref,
                   cq_ref, ck_ref, cv_ref, gates_ref, kmean_ref):
    h = _rms(x_ref[0], g_ref[...]).astype(BF16)
    lane = lax.broadcasted_iota(jnp.int32, (1, LANES), 1)

    def proj(w_ref, off, width):
        return jnp.dot(h, w_ref[:, off:off + width], preferred_element_type=F32)

    def rope(y, head_dim):
        half = head_dim // 8
        t0 = 0 if head_dim == HEAD_DIM else 2 * LANES
        c = tab_ref[0, :, t0:t0 + LANES]
        s = tab_ref[0, :, t0 + LANES:t0 + 2 * LANES]
        first = (lane & (head_dim - 1)) < half
        blocks = []
        for j in range(y.shape[1] // LANES):
            yj = y[:, j * LANES:(j + 1) * LANES]
            up = pltpu.roll(yj, LANES - half, 1)
            dn = pltpu.roll(yj, half, 1)
            blocks.append(yj * c + jnp.where(first, up, dn) * s)
        return blocks

    def store(out_ref, blocks):
        for j, r in enumerate(blocks):
            out_ref[0, :, j * LANES:(j + 1) * LANES] = r.astype(out_ref.dtype)

    store(aq_ref, rope(proj(wa_ref, A_AQ, 512) * Q_SCALE, HEAD_DIM))
    store(ak_ref, rope(proj(wa_ref, A_AK, 512), HEAD_DIM))
    av_ref[0] = proj(wa_ref, A_AV, 512).astype(av_ref.dtype)
    store(bq_ref, rope(proj(wa_ref, A_BQ, 512) * Q_SCALE, HEAD_DIM))
    kv = proj(wa_ref, A_KV, LANES)
    vk = pltpu.roll(kv, HEAD_DIM, 1)
    lo = lane < HEAD_DIM
    store(k2_ref, rope(jnp.where(lo, kv, vk), HEAD_DIM))
    v2_ref[0] = jnp.where(lo, vk, kv).astype(v2_ref.dtype)
    store(iq_ref, rope(proj(wa_ref, A_IQ, LANES), IDX_DIM))
    ikw = proj(wa_ref, A_IKW, LANES)
    ik = rope(ikw, IDX_DIM)[0]
    ik4 = ik
    for rep in range(1, LANES // IDX_DIM):
        ik4 = jnp.where(lane < rep * IDX_DIM, ik4, pltpu.roll(ik, rep * IDX_DIM, 1))
    ik4_ref[0] = ik4.astype(ik4_ref.dtype)
    iw_ref[0] = jnp.where(lane < IDX_HEADS, pltpu.roll(ikw, LANES - IDX_DIM, 1), 0.0)
    store(cq_ref, rope(proj(wb_ref, B_CQ, 512) * Q_SCALE, HEAD_DIM))
    ck_blocks = rope(proj(wb_ref, B_CK, 512), HEAD_DIM)
    store(ck_ref, ck_blocks)
    cv_ref[0] = proj(wb_ref, B_CV, 512).astype(cv_ref.dtype)
    gates_ref[0] = proj(wb_ref, B_G, 3 * 1024).astype(gates_ref.dtype)

    tm = x_ref.shape[1]
    for blk in range(tm // MOBA_BLOCK):
        for j, r in enumerate(ck_blocks):
            part = r[blk * MOBA_BLOCK:(blk + 1) * MOBA_BLOCK]
            kmean_ref[0, 0, blk:blk + 1, j * LANES:(j + 1) * LANES] = (
                jnp.mean(part, axis=0, keepdims=True))


def _inproj(x, g, tab, w_a, w_b, layer, tm):
    B, S, D = x.shape
    nm = S // tm
    row = lambda width: pl.BlockSpec((1, tm, width), lambda b, m: (b, m, 0))
    shp = lambda width, dt: jax.ShapeDtypeStruct((B, S, width), dt)
    widths = [(512, BF16)] * 4 + [(LANES, BF16)] * 4 + [(LANES, F32)] + [(512, BF16)] * 3 + [(3072, BF16)]
    out_specs = [row(w) for w, _ in widths]
    out_shape = [shp(w, dt) for w, dt in widths]
    out_specs.append(pl.BlockSpec((1, 1, tm // MOBA_BLOCK, 512), lambda b, m: (b, m, 0, 0)))
    out_shape.append(jax.ShapeDtypeStruct((B, nm, tm // MOBA_BLOCK, 512), F32))
    return pl.pallas_call(
        _inproj_kernel,
        grid=(B, nm),
        in_specs=[row(D),
                  _resident((1, D), lambda b, m: (0, 0)),
                  row(4 * LANES),
                  _resident((None, D, A_COLS), lambda b, m: (layer, 0, 0)),
                  _resident((None, D, B_COLS), lambda b, m: (layer, 0, 0))],
        out_specs=out_specs,
        out_shape=out_shape,
        compiler_params=_cparams(("parallel", "parallel")),
        name="inproj",
    )(x, g.reshape(1, D), tab, w_a, w_b)


def _split_heads_rows(q):
    lo = lax.broadcasted_iota(jnp.int32, (1, LANES), 1) < HEAD_DIM
    zero = jnp.zeros_like(q)
    return jnp.concatenate([jnp.where(lo, q, zero), jnp.where(lo, zero, q)], axis=0)


def _online_update(s, v, m_ref, l_ref, acc_ref, bias=None, row0=0):
    rows_total = s.shape[0]
    nblk = s.shape[1] // LANES
    p_rows, alphas = [], []
    for r0 in range(0, rows_total, ROW_BLOCK):
        rows = slice(row0 + r0, row0 + r0 + ROW_BLOCK)
        cols = [s[r0:r0 + ROW_BLOCK, j * LANES:(j + 1) * LANES] for j in range(nblk)]
        if bias is not None:
            b0 = r0 % bias[0].shape[0]
            cols = [c + bj[b0:b0 + ROW_BLOCK] for c, bj in zip(cols, bias)]
        m_prev = m_ref[rows]
        m_new = jnp.maximum(
            m_prev, jnp.max(functools.reduce(jnp.maximum, cols), axis=1, keepdims=True))
        alpha = jnp.exp2(m_prev - m_new)
        ps = [jnp.exp2(c - m_new) for c in cols]
        l_ref[rows] = alpha * l_ref[rows] + functools.reduce(jnp.add, ps)
        m_ref[rows] = m_new
        p_rows.append(jnp.concatenate([pj.astype(BF16) for pj in ps], axis=1))
        alphas.append(alpha)
    p = jnp.concatenate(p_rows, axis=0)
    alpha = jnp.concatenate(alphas, axis=0)
    span = slice(row0, row0 + rows_total)
    acc_ref[span] = alpha * acc_ref[span] + jnp.dot(p, v, preferred_element_type=F32)


def _normalised(l_ref, acc_ref):
    return acc_ref[...] / jnp.sum(l_ref[...], axis=1, keepdims=True)


def _diff_attn_kernel(q_ref, k_ref, v_ref, lam_ref, g_ref, o_ref, m_ref, l_ref, acc_ref,
                      *, t, lam_init):
    qi = pl.program_id(2)
    half = t // 2
    q = q_ref[0]
    q2 = jnp.concatenate([_split_heads_rows(q[:half]), _split_heads_rows(q[half:])], axis=0)
    m_ref[...] = jnp.full(m_ref.shape, NEG_BIG, F32)
    l_ref[...] = jnp.zeros(l_ref.shape, F32)
    acc_ref[...] = jnp.zeros(acc_ref.shape, F32)

    def body(c, carry):
        off = pl.multiple_of(c * t, t)
        _online_update(_nt_dot(q2, k_ref[0, pl.ds(off, t), :]), v_ref[0, pl.ds(off, t), :],
                       m_ref, l_ref, acc_ref)
        return carry

    lax.fori_loop(0, qi, body, 0)

    r = lax.broadcasted_iota(jnp.int32, (t, 1), 0)
    causal = lax.broadcasted_iota(jnp.int32, (1, half), 1) <= jnp.where(r >= half, r - half, r)
    off_a = pl.multiple_of(qi * t, t)
    s = _nt_dot(q2, k_ref[0, pl.ds(off_a, half), :])
    s = jnp.concatenate([jnp.where(causal, s[:t], NEG_BIG), s[t:]], axis=0)
    _online_update(s, v_ref[0, pl.ds(off_a, half), :], m_ref, l_ref, acc_ref)
    off_b = pl.multiple_of(qi * t + half, half)
    s = jnp.where(causal, _nt_dot(q2[t:], k_ref[0, pl.ds(off_b, half), :]), NEG_BIG)
    _online_update(s, v_ref[0, pl.ds(off_b, half), :], m_ref, l_ref, acc_ref, row0=t)

    lam_rows = lam_ref[...]
    e1 = jnp.exp(jnp.sum(lam_rows[0:1] * lam_rows[1:2], axis=1, keepdims=True))
    e2 = jnp.exp(jnp.sum(lam_rows[2:3] * lam_rows[3:4], axis=1, keepdims=True))
    lam = e1 - e2 + lam_init
    o = _normalised(l_ref, acc_ref)
    o1 = jnp.concatenate([o[:half], o[t:t + half]], axis=0)
    o2 = jnp.concatenate([o[half:t], o[t + half:]], axis=0)
    o_ref[0] = (_rms(o1 - lam * o2, g_ref[...]) * (1.0 - lam_init)).astype(o_ref.dtype)


def _diff_attn(aq, ak, av, lam_rows, subln_g, lam_init, t):
    B, S, _ = aq.shape
    nh = aq.shape[2] // LANES
    kern = functools.partial(_diff_attn_kernel, t=t, lam_init=lam_init)
    return pl.pallas_call(
        kern,
        grid=(B, nh, S // t),
        in_specs=[pl.BlockSpec((1, t, LANES), lambda b, h, i: (b, i, h)),
                  pl.BlockSpec((1, S, LANES), lambda b, h, i: (b, 0, h)),
                  pl.BlockSpec((1, S, LANES), lambda b, h, i: (b, 0, h)),
                  pl.BlockSpec((4, HEAD_DIM), lambda b, h, i: (0, 0)),
                  pl.BlockSpec((1, LANES), lambda b, h, i: (0, 0))],
        out_specs=pl.BlockSpec((1, t, LANES), lambda b, h, i: (b, i, h)),
        out_shape=jax.ShapeDtypeStruct(aq.shape, BF16),
        scratch_shapes=[pltpu.VMEM((2 * t, LANES), F32), pltpu.VMEM((2 * t, LANES), F32),
                        pltpu.VMEM((2 * t, LANES), F32)],
        compiler_params=_cparams(("parallel", "parallel", "arbitrary")),
        name="diff_attn",
    )(aq, ak, av, lam_rows, subln_g.reshape(1, LANES))


BIT_GROUP = 256


def _bit_transpose32(words):
    w = list(words)
    for j, m in ((16, 0x0000FFFF), (8, 0x00FF00FF), (4, 0x0F0F0F0F), (2, 0x33333333), (1, 0x55555555)):
        for k in range(32):
            if k & j == 0:
                t = (w[k] ^ (w[k + j] >> j)) & m
                w[k] = w[k] ^ t
                w[k + j] = w[k + j] ^ (t << j)
    return w


def _dsa_kernel(q_ref, iq_ref, iw_ref, ik4_ref, k2_ref, v2_ref, o_ref,
                keys_ref, planes_ref, act_ref, jcut_ref, m_ref, l_ref, acc_ref,
                *, tq, tk, topk, idx_bits):
    qi = pl.program_id(1)
    nkv = (qi * tq + tq - 1) // tk + 1
    lane = lax.broadcasted_iota(jnp.int32, (1, LANES), 1)
    qpos = qi * tq + lax.broadcasted_iota(jnp.int32, (tq, 1), 0)
    lane_tk = lax.broadcasted_iota(jnp.int32, (1, tk), 1)
    nsub = tk // LANES
    seq = keys_ref.shape[1]

    @pl.when(qi == 0)
    def _():
        planes_ref[...] = jnp.zeros(planes_ref.shape, jnp.int32)

    iq = iq_ref[0]
    zero = jnp.zeros_like(iq)
    iq4 = jnp.concatenate(
        [jnp.where((lane >> (IDX_DIM.bit_length() - 1)) == hh, iq, zero) for hh in range(IDX_HEADS)],
        axis=0)
    iw = iw_ref[0]
    iw_cols = [iw[:, hh:hh + 1] for hh in range(IDX_HEADS)]

    def score_body(c, carry):
        off = pl.multiple_of(c * tk, tk)
        rel = jnp.maximum(_nt_dot(iq4, ik4_ref[0, pl.ds(off, tk), :]), 0.0)
        score = iw_cols[0] * rel[0:tq]
        for hh in range(1, IDX_HEADS):
            score = score + iw_cols[hh] * rel[hh * tq:(hh + 1) * tq]
        bits = lax.bitcast_convert_type(score, jnp.int32)
        key = bits ^ ((bits >> 31) & jnp.int32(0x7FFFFFFF))
        kpos = off + lane_tk
        key = jnp.where(score == 0.0, (seq - 1) - kpos, jnp.where(key > 0, key + seq, key))
        key = jnp.where(kpos <= qpos, key, jnp.int32(INT_MIN))
        keys_ref[:, pl.ds(off, tk)] = key
        kts = [key[:, j * LANES:(j + 1) * LANES].T for j in range(nsub)]
        for gl in range(tk // BIT_GROUP):
            words = [kts[(gl * BIT_GROUP + 8 * g) // LANES][(8 * g) % LANES:(8 * g) % LANES + 8]
                     for g in range(32)]
            words = _bit_transpose32(words)
            words[0] = ~words[0]
            for b in range(32):
                planes_ref[c * (tk // BIT_GROUP) + gl, b] = words[b]
        return carry

    lax.fori_loop(0, nkv, score_body, 0)

    ngroups = planes_ref.shape[0]
    live = nkv * (tk // BIT_GROUP)
    for g in range(ngroups):
        act_ref[g] = jnp.broadcast_to(jnp.where(g < live, jnp.int32(-1), jnp.int32(0)), (8, tq))

    def sweep(i, keep, first=False):
        parts = [jnp.zeros((8, tq), jnp.int32) for _ in range(4)]
        for g in range(ngroups):
            act = act_ref[g]
            if not first:
                act = act & (planes_ref[g, i - 1] ^ keep)
                act_ref[g] = act
            parts[g % 4] = parts[g % 4] + lax.population_count(act & planes_ref[g, i])
        cnt = (parts[0] + parts[1]) + (parts[2] + parts[3])
        return jnp.sum(cnt.astype(F32), axis=0, keepdims=True)

    def decide(i, ones, thr_u, n_gt):
        take = n_gt + ones >= topk
        thr_u = thr_u | jnp.where(take, jnp.left_shift(jnp.int32(1), 31 - i), 0)
        return thr_u, jnp.where(take, n_gt, n_gt + ones), jnp.where(take, 0, -1).astype(jnp.int32)

    def pass_body(i, carry):
        thr_u, n_gt, keep = carry
        return decide(i, sweep(i, keep), thr_u, n_gt)

    start = decide(0, sweep(0, None, first=True),
                   jnp.zeros((1, tq), jnp.int32), jnp.zeros((1, tq), F32))
    thr_u, n_gt, keep = lax.fori_loop(1, 32, pass_body, start)
    n_eq = jnp.zeros((8, tq), jnp.int32)
    for g in range(ngroups):
        n_eq = n_eq + lax.population_count(act_ref[g] & (planes_ref[g, 31] ^ keep))
    thr = thr_u ^ jnp.int32(INT_MIN)
    cnt_ge = n_gt + jnp.sum(n_eq.astype(F32), axis=0, keepdims=True)
    need = topk - n_gt

    def as_rows(x):
        return jnp.broadcast_to(x, (LANES, tq)).T

    thr_r = as_rows(thr)
    need_r = as_rows(need)
    jcut_ref[...] = jnp.full((tq, LANES), 2 ** 30, jnp.int32)

    @pl.when(jnp.max(cnt_ge) > topk)
    def _():
        def ties_before(cand):
            def body(c, cnt):
                off = pl.multiple_of(c * tk, tk)
                for j in range(nsub):
                    keyj = keys_ref[:, pl.ds(pl.multiple_of(off + j * LANES, LANES), LANES)]
                    hit = (keyj == thr_r) & (off + j * LANES + lane < cand)
                    cnt = jnp.where(hit, cnt + 1, cnt)
                return cnt
            cnt = lax.fori_loop(0, nkv, body, jnp.zeros((tq, LANES), jnp.int32))
            return jnp.sum(cnt.astype(F32), axis=1, keepdims=True)

        def idx_body(i, j):
            cand = j + jnp.left_shift(jnp.int32(1), idx_bits - 1 - i)
            return jnp.where(ties_before(cand) < need_r, cand, j)
        jcut_ref[...] = lax.fori_loop(0, idx_bits, idx_body, jnp.zeros((tq, LANES), jnp.int32))

    jcut_r = jcut_ref[...]

    q = q_ref[0]
    q8 = jnp.concatenate(
        [_split_heads_rows(q[:, p * LANES:(p + 1) * LANES]) for p in range(4)], axis=0)
    m_ref[...] = jnp.full(m_ref.shape, NEG_BIG, F32)
    l_ref[...] = jnp.zeros(l_ref.shape, F32)
    acc_ref[...] = jnp.zeros(acc_ref.shape, F32)

    def attend(off, width):
        s = _nt_dot(q8, k2_ref[0, pl.ds(off, width), :])
        bias = []
        for j in range(width // LANES):
            keyj = keys_ref[:, pl.ds(pl.multiple_of(off + j * LANES, LANES), LANES)]
            kpos = off + j * LANES + lane
            sel = ((keyj > thr_r) | ((keyj == thr_r) & (kpos <= jcut_r))) & (kpos <= qpos)
            bias.append(jnp.where(sel, 0.0, NEG_BIG))
        _online_update(s, v2_ref[0, pl.ds(off, width), :], m_ref, l_ref, acc_ref, bias=bias)

    def attn_body(c, carry):
        attend(pl.multiple_of(c * tk, tk), tk)
        return carry

    lax.fori_loop(0, nkv, attn_body, 0)

    o = _normalised(l_ref, acc_ref)
    lo = lane < HEAD_DIM
    for p in range(4):
        o_ref[0, :, p * LANES:(p + 1) * LANES] = jnp.where(
            lo, o[(2 * p) * tq:(2 * p + 1) * tq], o[(2 * p + 1) * tq:(2 * p + 2) * tq]
        ).astype(o_ref.dtype)


def _dsa_attn(bq, iq, iw, ik4, k2, v2, tq, tk):
    B, S, C = bq.shape
    topk = min(IDX_TOPK_MAX, S // 4)
    idx_bits = max(1, (S - 1).bit_length())
    kern = functools.partial(_dsa_kernel, tq=tq, tk=tk, topk=topk, idx_bits=idx_bits)
    row = lambda width: pl.BlockSpec((1, tq, width), lambda b, i: (b, i, 0))
    full = _resident((1, S, LANES), lambda b, i: (b, 0, 0))
    return pl.pallas_call(
        kern,
        grid=(B, S // tq),
        in_specs=[row(C), row(LANES), row(LANES), full, full, full],
        out_specs=row(C),
        out_shape=jax.ShapeDtypeStruct(bq.shape, BF16),
        scratch_shapes=[pltpu.VMEM((tq, S), jnp.int32),
                        pltpu.VMEM((S // BIT_GROUP, 32, 8, tq), jnp.int32),
                        pltpu.VMEM((S // BIT_GROUP, 8, tq), jnp.int32),
                        pltpu.VMEM((tq, LANES), jnp.int32),
                        pltpu.VMEM((8 * tq, LANES), F32), pltpu.VMEM((8 * tq, LANES), F32),
                        pltpu.VMEM((8 * tq, LANES), F32)],
        compiler_params=_cparams(("parallel", "arbitrary")),
        name="dsa_attn",
    )(bq, iq, iw, ik4, k2, v2)


MOBA_TILE_BLOCKS = 8
MOBA_BULK_BLOCKS = 4


def _moba_kernel(q_ref, k_ref, v_ref, km_ref, o_ref, m_ref, l_ref, acc_ref, *, nb):
    nqb = MOBA_TILE_BLOCKS
    blk_rows = MOBA_BLOCK
    grp = 2 * blk_rows
    i = pl.program_id(2)
    first = nqb * i
    lane = lax.broadcasted_iota(jnp.int32, (1, LANES), 1)
    q = q_ref[0]
    q2 = jnp.concatenate(
        [_split_heads_rows(q[j * blk_rows:(j + 1) * blk_rows]) for j in range(nqb)], axis=0)

    km = jnp.concatenate([km_ref[0], jnp.zeros((LANES - nb, LANES), F32)], axis=0)
    km_hi = km.astype(BF16)
    r1 = km - km_hi.astype(F32)
    km_mid = r1.astype(BF16)
    km_lo = (r1 - km_mid.astype(F32)).astype(BF16)
    nb8 = -(-nb // 8) * 8
    gate_t = (_nt_dot(km_hi, q2) + _nt_dot(km_mid, q2) + _nt_dot(km_lo, q2))[:nb8]
    blk = lax.broadcasted_iota(jnp.int32, (nb8, 1), 0)
    blk_f = blk.astype(F32)
    col = lax.broadcasted_iota(jnp.int32, (1, nqb * grp), 1)
    past = blk < first + (col >> (grp.bit_length() - 1))
    g = jnp.where(past, gate_t, -jnp.inf)
    sel = jnp.zeros(g.shape, jnp.bool_)
    for _ in range(min(MOBA_TOPK, nb - 1)):
        mx = jnp.max(g, axis=0, keepdims=True)
        pick = blk_f == jnp.min(jnp.where(g == mx, blk_f, float(LANES)), axis=0, keepdims=True)
        sel = sel | pick
        g = jnp.where(pick, -jnp.inf, g)
    bias_t = jnp.concatenate([jnp.where(sel & past, 0.0, NEG_BIG),
                              jnp.full((LANES - nb8, nqb * grp), NEG_BIG, F32)], axis=0)
    qa = jnp.concatenate([q2, bias_t.T.astype(BF16)], axis=1)

    m_ref[...] = jnp.full(m_ref.shape, NEG_BIG, F32)
    l_ref[...] = jnp.zeros(l_ref.shape, F32)
    acc_ref[...] = jnp.zeros(acc_ref.shape, F32)

    def biased_logits(lhs, first_blk, nblk):
        n = nblk * blk_rows
        off = pl.multiple_of(first_blk * blk_rows, blk_rows)
        row_blk = lax.broadcasted_iota(jnp.int32, (n, 1), 0) >> (MOBA_BLOCK.bit_length() - 1)
        onehot = jnp.where(lane == first_blk + row_blk, 1.0, 0.0).astype(BF16)
        ka = jnp.concatenate([k_ref[0, pl.ds(off, n), :], onehot], axis=1)
        return _nt_dot(lhs, ka), v_ref[0, pl.ds(off, n), :]

    def body(c, carry):
        s, v = biased_logits(qa, MOBA_BULK_BLOCKS * c, MOBA_BULK_BLOCKS)
        _online_update(s, v, m_ref, l_ref, acc_ref)
        return carry

    lax.fori_loop(0, first // MOBA_BULK_BLOCKS, body, 0)

    r = lax.broadcasted_iota(jnp.int32, (grp, 1), 0)
    causal = (lax.broadcasted_iota(jnp.int32, (1, blk_rows), 1)
              <= jnp.where(r >= blk_rows, r - blk_rows, r))
    for j in range(nqb):
        off = pl.multiple_of((first + j) * blk_rows, blk_rows)
        s = jnp.where(causal, _nt_dot(q2[j * grp:(j + 1) * grp], k_ref[0, pl.ds(off, blk_rows), :]),
                      NEG_BIG)
        if j + 1 < nqb:
            s_later, _ = biased_logits(qa[(j + 1) * grp:], first + j, 1)
            s = jnp.concatenate([s, s_later], axis=0)
        _online_update(s, v_ref[0, pl.ds(off, blk_rows), :], m_ref, l_ref, acc_ref, row0=j * grp)

    o = _normalised(l_ref, acc_ref)
    lo = lane < HEAD_DIM
    for j in range(nqb):
        o_ref[0, j * blk_rows:(j + 1) * blk_rows] = jnp.where(
            lo, o[j * grp:j * grp + blk_rows], o[j * grp + blk_rows:(j + 1) * grp]).astype(o_ref.dtype)


def _moba_attn(cq, ck, cv, kmean):
    B, S, C = cq.shape
    nb = S // MOBA_BLOCK
    tq = MOBA_TILE_BLOCKS * MOBA_BLOCK
    return pl.pallas_call(
        functools.partial(_moba_kernel, nb=nb),
        grid=(B, C // LANES, S // tq),
        in_specs=[pl.BlockSpec((1, tq, LANES), lambda b, p, i: (b, i, p)),
                  pl.BlockSpec((1, S, LANES), lambda b, p, i: (b, 0, p)),
                  pl.BlockSpec((1, S, LANES), lambda b, p, i: (b, 0, p)),
                  pl.BlockSpec((1, nb, LANES), lambda b, p, i: (b, 0, p))],
        out_specs=pl.BlockSpec((1, tq, LANES), lambda b, p, i: (b, i, p)),
        out_shape=jax.ShapeDtypeStruct(cq.shape, BF16),
        scratch_shapes=[pltpu.VMEM((2 * tq, LANES), F32), pltpu.VMEM((2 * tq, LANES), F32),
                        pltpu.VMEM((2 * tq, LANES), F32)],
        compiler_params=_cparams(("parallel", "parallel", "arbitrary")),
        name="moba_attn",
    )(cq, ck, cv, kmean)


def _sigmoid(x):
    return 1.0 / (1.0 + jnp.exp(-x))


def _merge_kernel(ya_ref, yb_ref, yc_ref, gates_ref, x_ref, wb_ref, wo_ref, g_ref, o_ref):
    d = x_ref.shape[2]
    merged = None
    for n, y_ref in enumerate((ya_ref, yb_ref, yc_ref)):
        br = jnp.dot(y_ref[0], wb_ref[n], preferred_element_type=F32)
        term = _sigmoid(gates_ref[0, :, n * d:(n + 1) * d].astype(F32)) * br
        merged = term if merged is None else merged + term
    m = jnp.dot(merged.astype(BF16), wo_ref[...], preferred_element_type=F32)
    o_ref[0] = x_ref[0] + _rms(m, g_ref[...])


def _merge(ya, yb, yc, gates, x, wb, wo, layer, g, tm):
    B, S, D = x.shape
    row = lambda width: pl.BlockSpec((1, tm, width), lambda b, m: (b, m, 0))
    return pl.pallas_call(
        _merge_kernel,
        grid=(B, S // tm),
        in_specs=[row(BRANCH_WIDTH), row(BRANCH_WIDTH), row(BRANCH_WIDTH), row(N_BRANCH * D), row(D),
                  _resident((None,) + wb.shape[1:], lambda b, m: (layer, 0, 0, 0)),
                  _resident((None,) + wo.shape[1:], lambda b, m: (layer, 0, 0)),
                  _resident((1, D), lambda b, m: (0, 0))],
        out_specs=row(D),
        out_shape=jax.ShapeDtypeStruct(x.shape, F32),
        compiler_params=_cparams(("parallel", "parallel")),
        name="merge_out",
    )(ya, yb, yc, gates, x, wb, wo, g.reshape(1, D))


def _ffn_kernel(x_ref, g_in_ref, wi_ref, wo_ref, g_out_ref, o_ref, acc_ref, *, d_ff, tf):
    x = x_ref[0]
    h = _rms(x, g_in_ref[...]).astype(BF16)
    acc_ref[...] = jnp.zeros(acc_ref.shape, F32)

    def body(c, carry):
        off = pl.multiple_of(c * tf, tf)
        gt = jnp.dot(h, wi_ref[:, pl.ds(off, tf)], preferred_element_type=F32)
        up_off = pl.multiple_of(d_ff + off, LANES)
        up = jnp.dot(h, wi_ref[:, pl.ds(up_off, tf)], preferred_element_type=F32)
        act = (gt * _sigmoid(gt) * up).astype(BF16)
        acc_ref[...] += jnp.dot(act, wo_ref[pl.ds(off, tf), :], preferred_element_type=F32)
        return carry

    lax.fori_loop(0, d_ff // tf, body, 0)
    o_ref[0] = x + _rms(acc_ref[...], g_out_ref[...])


def _ffn(x, g_in, wi, wo, layer, g_out, tm, tf):
    B, S, D = x.shape
    d_ff = wo.shape[1]
    row = pl.BlockSpec((1, tm, D), lambda b, m: (b, m, 0))
    vec = _resident((1, D), lambda b, m: (0, 0))
    return pl.pallas_call(
        functools.partial(_ffn_kernel, d_ff=d_ff, tf=tf),
        grid=(B, S // tm),
        in_specs=[row, vec, _resident((None,) + wi.shape[1:], lambda b, m: (layer, 0, 0)),
                  _resident((None,) + wo.shape[1:], lambda b, m: (layer, 0, 0)), vec],
        out_specs=row,
        out_shape=jax.ShapeDtypeStruct(x.shape, F32),
        scratch_shapes=[pltpu.VMEM((tm, D), F32)],
        compiler_params=_cparams(("parallel", "parallel")),
        name="swiglu_ffn",
    )(x, g_in.reshape(1, D), wi, wo, g_out.reshape(1, D))


class _Tiles(NamedTuple):
    proj_rows: int
    dense_rows: int
    ffn_cols: int
    diff_tile: int
    dsa_queries: int
    dsa_keys: int


def _tiles(seq):
    return _Tiles(proj_rows=min(512, seq), dense_rows=min(1024, seq), ffn_cols=256,
                  diff_tile=min(1024, seq), dsa_queries=min(256, seq), dsa_keys=min(512, seq))


def kernel(x, positions, w_in, w_branch, w_out, lambda_q1, lambda_k1, lambda_q2, lambda_k2,
           subln_g, norm_g, w_ffn_in, w_ffn_out):
    B, S, D = x.shape
    depth = w_in.shape[0]
    tiles = _tiles(S)
    assert S % (MOBA_TILE_BLOCKS * MOBA_BLOCK) == 0 and S % tiles.diff_tile == 0
    tab = _rope_table(positions, tiles.proj_rows)
    assert w_in.shape[2] == B_START + B_COLS
    w_bf = w_in.astype(BF16)
    w_a, w_b = w_bf[..., :A_COLS], w_bf[..., B_START:]
    wb, wo = w_branch.astype(BF16), w_out.astype(BF16)
    wi_ffn, wo_ffn = w_ffn_in.astype(BF16), w_ffn_out.astype(BF16)
    for l in range(depth):
        lam_init = 0.8 - 0.6 * math.exp(-0.3 * l)
        (aq, ak, av, bq, k2, v2, iq, ik4, iw, cq, ck, cv, gates, kmean) = _inproj(
            x, norm_g[l, 0], tab, w_a, w_b, l, tiles.proj_rows)
        kmean = kmean.reshape(B, S // MOBA_BLOCK, 512)
        lam_rows = jnp.stack([lambda_q1[l], lambda_k1[l], lambda_q2[l], lambda_k2[l]])
        ya = _diff_attn(aq, ak, av, lam_rows, subln_g[l], lam_init, t=tiles.diff_tile)
        yb = _dsa_attn(bq, iq, iw, ik4, k2, v2, tq=tiles.dsa_queries, tk=tiles.dsa_keys)
        yc = _moba_attn(cq, ck, cv, kmean)
        x = _merge(ya, yb, yc, gates, x, wb, wo, l, norm_g[l, 1], tiles.dense_rows)
        x = _ffn(x, norm_g[l, 2], wi_ffn, wo_ffn, l, norm_g[l, 3], tiles.dense_rows,
                 tf=tiles.ffn_cols)
    return x
```

```python
import functools
import math
from typing import NamedTuple

import jax
import jax.numpy as jnp
from jax import lax
from jax.experimental import pallas as pl
from jax.experimental.pallas import tpu as pltpu

F32 = jnp.float32
BF16 = jnp.bfloat16

LANES = 128
HEAD_DIM = 64
ROPE_THETA = 500000.0
NORM_EPS = 1e-6
IDX_HEADS = 4
IDX_DIM = 32
IDX_TOPK_MAX = 256
MOBA_BLOCK = 256
MOBA_TOPK = 3
N_BRANCH = 3
BRANCH_WIDTH = 512
NEG_BIG = -1e30
LOG2E = math.log2(math.e)
ROW_BLOCK = 64
INT_MIN = -2 ** 31
VMEM_LIMIT = 56 * 1024 * 1024

A_AQ, A_AK, A_AV, A_BQ, A_KV, A_IQ, A_IKW, A_COLS = 0, 512, 1024, 1536, 2048, 2176, 2304, 2432
B_START = 2340
B_CQ, B_CK, B_CV, B_G, B_COLS = 0, 512, 1024, 1536, 4608
Q_SCALE = HEAD_DIM ** -0.5 * LOG2E


def _nt_dot(a, b):
    return lax.dot_general(a, b, (((1,), (1,)), ((), ())), preferred_element_type=F32)


def _rms(x, g):
    return x * lax.rsqrt(jnp.mean(x * x, axis=-1, keepdims=True) + NORM_EPS) * g


def _cparams(sem):
    return pltpu.CompilerParams(dimension_semantics=sem, vmem_limit_bytes=VMEM_LIMIT)


def _resident(shape, index_map):
    return pl.BlockSpec(shape, index_map, pipeline_mode=pl.Buffered(1))


def _rope_table_kernel(pos_ref, tab_ref):
    pos = pos_ref[0].astype(F32)
    lane = lax.broadcasted_iota(jnp.int32, (1, LANES), 1)

    def tables(head_dim):
        rot = head_dim // 4
        half = rot // 2
        d = lane & (head_dim - 1)
        fi = d & (half - 1)
        inv = jnp.zeros((1, LANES), F32)
        for i in range(half):
            inv = jnp.where(fi == i, F32(ROPE_THETA ** (-(2.0 * i) / rot)), inv)
        ang = pos * inv
        cos = jnp.cos(ang)
        sin = jnp.sin(ang)
        c = jnp.where(d < rot, cos, 1.0)
        s = jnp.where(d < half, -sin, jnp.where(d < rot, sin, 0.0))
        return c, s

    c64, s64 = tables(HEAD_DIM)
    c32, s32 = tables(IDX_DIM)
    tab_ref[0, :, 0 * LANES:1 * LANES] = c64
    tab_ref[0, :, 1 * LANES:2 * LANES] = s64
    tab_ref[0, :, 2 * LANES:3 * LANES] = c32
    tab_ref[0, :, 3 * LANES:4 * LANES] = s32


def _rope_table(positions, tm):
    B, S = positions.shape
    return pl.pallas_call(
        _rope_table_kernel,
        grid=(B, S // tm),
        in_specs=[pl.BlockSpec((1, tm, 1), lambda b, m: (b, m, 0))],
        out_specs=pl.BlockSpec((1, tm, 4 * LANES), lambda b, m: (b, m, 0)),
        out_shape=jax.ShapeDtypeStruct((B, S, 4 * LANES), F32),
        compiler_params=_cparams(("parallel", "parallel")),
        name="rope_table",
    )(positions.reshape(B, S, 1))


def _inproj_kernel(x_ref, g_ref, tab_ref, wa_ref, wb_ref,
                   aq_ref, ak_ref, av_ref, bq_ref, k2_ref, v2_ref, iq_ref, ik4_ref, iw_ref,
                   cq_ref, ck_ref, cv_ref, gates_ref, kmean_ref):
    h = _rms(x_ref[0], g_ref[...]).astype(BF16)
    lane = lax.broadcasted_iota(jnp.int32, (1, LANES), 1)

    def proj(w_ref, off, width):
        return jnp.dot(h, w_ref[:, off:off + width], preferred_element_type=F32)

    def rope(y, head_dim):
        half = head_dim // 8
        t0 = 0 if head_dim == HEAD_DIM else 2 * LANES
        c = tab_ref[0, :, t0:t0 + LANES]
        s = tab_ref[0, :, t0 + LANES:t0 + 2 * LANES]
        first = (lane & (head_dim - 1)) < half
        blocks = []
        for j in range(y.shape[1] // LANES):
            yj = y[:, j * LANES:(j + 1) * LANES]
            up = pltpu.roll(yj, LANES - half, 1)
            dn = pltpu.roll(yj, half, 1)
            blocks.append(yj * c + jnp.where(first, up, dn) * s)
        return blocks

    def store(out_ref, blocks):
        for j, r in enumerate(blocks):
            out_ref[0, :, j * LANES:(j + 1) * LANES] = r.astype(out_ref.dtype)

    store(aq_ref, rope(proj(wa_ref, A_AQ, 512) * Q_SCALE, HEAD_DIM))
    store(ak_ref, rope(proj(wa_ref, A_AK, 512), HEAD_DIM))
    av_ref[0] = proj(wa_ref, A_AV, 512).astype(av_ref.dtype)
    store(bq_ref, rope(proj(wa_ref, A_BQ, 512) * Q_SCALE, HEAD_DIM))
    kv = proj(wa_ref, A_KV, LANES)
    vk = pltpu.roll(kv, HEAD_DIM, 1)
    lo = lane < HEAD_DIM
    store(k2_ref, rope(jnp.where(lo, kv, vk), HEAD_DIM))
    v2_ref[0] = jnp.where(lo, vk, kv).astype(v2_ref.dtype)
    store(iq_ref, rope(proj(wa_ref, A_IQ, LANES), IDX_DIM))
    ikw = proj(wa_ref, A_IKW, LANES)
    ik = rope(ikw, IDX_DIM)[0]
    ik4 = ik
    for rep in range(1, LANES // IDX_DIM):
        ik4 = jnp.where(lane < rep * IDX_DIM, ik4, pltpu.roll(ik, rep * IDX_DIM, 1))
    ik4_ref[0] = ik4.astype(ik4_ref.dtype)
    iw_ref[0] = jnp.where(lane < IDX_HEADS, pltpu.roll(ikw, LANES - IDX_DIM, 1), 0.0)
    store(cq_ref, rope(proj(wb_ref, B_CQ, 512) * Q_SCALE, HEAD_DIM))
    ck_blocks = rope(proj(wb_ref, B_CK, 512), HEAD_DIM)
    store(ck_ref, ck_blocks)
    cv_ref[0] = proj(wb_ref, B_CV, 512).astype(cv_ref.dtype)
    gates_ref[0] = proj(wb_ref, B_G, 3 * 1024).astype(gates_ref.dtype)

    tm = x_ref.shape[1]
    for blk in range(tm // MOBA_BLOCK):
        for j, r in enumerate(ck_blocks):
            part = r[blk * MOBA_BLOCK:(blk + 1) * MOBA_BLOCK]
            kmean_ref[0, 0, blk:blk + 1, j * LANES:(j + 1) * LANES] = (
                jnp.mean(part, axis=0, keepdims=True))


def _inproj(x, g, tab, w_a, w_b, layer, tm):
    B, S, D = x.shape
    nm = S // tm
    row = lambda width: pl.BlockSpec((1, tm, width), lambda b, m: (b, m, 0))
    shp = lambda width, dt: jax.ShapeDtypeStruct((B, S, width), dt)
    widths = [(512, BF16)] * 4 + [(LANES, BF16)] * 4 + [(LANES, F32)] + [(512, BF16)] * 3 + [(3072, BF16)]
    out_specs = [row(w) for w, _ in widths]
    out_shape = [shp(w, dt) for w, dt in widths]
    out_specs.append(pl.BlockSpec((1, 1, tm // MOBA_BLOCK, 512), lambda b, m: (b, m, 0, 0)))
    out_shape.append(jax.ShapeDtypeStruct((B, nm, tm // MOBA_BLOCK, 512), F32))
    return pl.pallas_call(
        _inproj_kernel,
        grid=(B, nm),
        in_specs=[row(D),
                  _resident((1, D), lambda b, m: (0, 0)),
                  row(4 * LANES),
                  _resident((None, D, A_COLS), lambda b, m: (layer, 0, 0)),
                  _resident((None, D, B_COLS), lambda b, m: (layer, 0, 0))],
        out_specs=out_specs,
        out_shape=out_shape,
        compiler_params=_cparams(("parallel", "parallel")),
        name="inproj",
    )(x, g.reshape(1, D), tab, w_a, w_b)


def _split_heads_rows(q):
    lo = lax.broadcasted_iota(jnp.int32, (1, LANES), 1) < HEAD_DIM
    zero = jnp.zeros_like(q)
    return jnp.concatenate([jnp.where(lo, q, zero), jnp.where(lo, zero, q)], axis=0)


def _online_update(s, v, m_ref, l_ref, acc_ref, bias=None, row0=0):
    rows_total = s.shape[0]
    nblk = s.shape[1] // LANES
    p_rows, alphas = [], []
    for r0 in range(0, rows_total, ROW_BLOCK):
        rows = slice(row0 + r0, row0 + r0 + ROW_BLOCK)
        cols = [s[r0:r0 + ROW_BLOCK, j * LANES:(j + 1) * LANES] for j in range(nblk)]
        if bias is not None:
            b0 = r0 % bias[0].shape[0]
            cols = [c + bj[b0:b0 + ROW_BLOCK] for c, bj in zip(cols, bias)]
        m_prev = m_ref[rows]
        m_new = jnp.maximum(
            m_prev, jnp.max(functools.reduce(jnp.maximum, cols), axis=1, keepdims=True))
        alpha = jnp.exp2(m_prev - m_new)
        ps = [jnp.exp2(c - m_new) for c in cols]
        l_ref[rows] = alpha * l_ref[rows] + functools.reduce(jnp.add, ps)
        m_ref[rows] = m_new
        p_rows.append(jnp.concatenate([pj.astype(BF16) for pj in ps], axis=1))
        alphas.append(alpha)
    p = jnp.concatenate(p_rows, axis=0)
    alpha = jnp.concatenate(alphas, axis=0)
    span = slice(row0, row0 + rows_total)
    acc_ref[span] = alpha * acc_ref[span] + jnp.dot(p, v, preferred_element_type=F32)


def _normalised(l_ref, acc_ref):
    return acc_ref[...] / jnp.sum(l_ref[...], axis=1, keepdims=True)


def _diff_attn_kernel(q_ref, k_ref, v_ref, lam_ref, g_ref, o_ref, m_ref, l_ref, acc_ref,
                      *, t, lam_init):
    qi = pl.program_id(2)
    half = t // 2
    q = q_ref[0]
    q2 = jnp.concatenate([_split_heads_rows(q[:half]), _split_heads_rows(q[half:])], axis=0)
    m_ref[...] = jnp.full(m_ref.shape, NEG_BIG, F32)
    l_ref[...] = jnp.zeros(l_ref.shape, F32)
    acc_ref[...] = jnp.zeros(acc_ref.shape, F32)

    def body(c, carry):
        off = pl.multiple_of(c * t, t)
        _online_update(_nt_dot(q2, k_ref[0, pl.ds(off, t), :]), v_ref[0, pl.ds(off, t), :],
                       m_ref, l_ref, acc_ref)
        return carry

    lax.fori_loop(0, qi, body, 0)

    r = lax.broadcasted_iota(jnp.int32, (t, 1), 0)
    causal = lax.broadcasted_iota(jnp.int32, (1, half), 1) <= jnp.where(r >= half, r - half, r)
    off_a = pl.multiple_of(qi * t, t)
    s = _nt_dot(q2, k_ref[0, pl.ds(off_a, half), :])
    s = jnp.concatenate([jnp.where(causal, s[:t], NEG_BIG), s[t:]], axis=0)
    _online_update(s, v_ref[0, pl.ds(off_a, half), :], m_ref, l_ref, acc_ref)
    off_b = pl.multiple_of(qi * t + half, half)
    s = jnp.where(causal, _nt_dot(q2[t:], k_ref[0, pl.ds(off_b, half), :]), NEG_BIG)
    _online_update(s, v_ref[0, pl.ds(off_b, half), :], m_ref, l_ref, acc_ref, row0=t)

    lam_rows = lam_ref[...]
    e1 = jnp.exp(jnp.sum(lam_rows[0:1] * lam_rows[1:2], axis=1, keepdims=True))
    e2 = jnp.exp(jnp.sum(lam_rows[2:3] * lam_rows[3:4], axis=1, keepdims=True))
    lam = e1 - e2 + lam_init
    o = _normalised(l_ref, acc_ref)
    o1 = jnp.concatenate([o[:half], o[t:t + half]], axis=0)
    o2 = jnp.concatenate([o[half:t], o[t + half:]], axis=0)
    o_ref[0] = (_rms(o1 - lam * o2, g_ref[...]) * (1.0 - lam_init)).astype(o_ref.dtype)


def _diff_attn(aq, ak, av, lam_rows, subln_g, lam_init, t):
    B, S, _ = aq.shape
    nh = aq.shape[2] // LANES
    kern = functools.partial(_diff_attn_kernel, t=t, lam_init=lam_init)
    return pl.pallas_call(
        kern,
        grid=(B, nh, S // t),
        in_specs=[pl.BlockSpec((1, t, LANES), lambda b, h, i: (b, i, h)),
                  pl.BlockSpec((1, S, LANES), lambda b, h, i: (b, 0, h)),
                  pl.BlockSpec((1, S, LANES), lambda b, h, i: (b, 0, h)),
                  pl.BlockSpec((4, HEAD_DIM), lambda b, h, i: (0, 0)),
                  pl.BlockSpec((1, LANES), lambda b, h, i: (0, 0))],
        out_specs=pl.BlockSpec((1, t, LANES), lambda b, h, i: (b, i, h)),
        out_shape=jax.ShapeDtypeStruct(aq.shape, BF16),
        scratch_shapes=[pltpu.VMEM((2 * t, LANES), F32), pltpu.VMEM((2 * t, LANES), F32),
                        pltpu.VMEM((2 * t, LANES), F32)],
        compiler_params=_cparams(("parallel", "parallel", "arbitrary")),
        name="diff_attn",
    )(aq, ak, av, lam_rows, subln_g.reshape(1, LANES))


BIT_GROUP = 256


def _bit_transpose32(words):
    w = list(words)
    for j, m in ((16, 0x0000FFFF), (8, 0x00FF00FF), (4, 0x0F0F0F0F), (2, 0x33333333), (1, 0x55555555)):
        for k in range(32):
            if k & j == 0:
                t = (w[k] ^ (w[k + j] >> j)) & m
                w[k] = w[k] ^ t
                w[k + j] = w[k + j] ^ (t << j)
    return w


def _dsa_kernel(q_ref, iq_ref, iw_ref, ik4_ref, k2_ref, v2_ref, o_ref,
                keys_ref, planes_ref, act_ref, jcut_ref, m_ref, l_ref, acc_ref,
                *, tq, tk, topk, idx_bits):
    qi = pl.program_id(1)
    nkv = (qi * tq + tq - 1) // tk + 1
    lane = lax.broadcasted_iota(jnp.int32, (1, LANES), 1)
    qpos = qi * tq + lax.broadcasted_iota(jnp.int32, (tq, 1), 0)
    lane_tk = lax.broadcasted_iota(jnp.int32, (1, tk), 1)
    nsub = tk // LANES
    seq = keys_ref.shape[1]

    @pl.when(qi == 0)
    def _():
        planes_ref[...] = jnp.zeros(planes_ref.shape, jnp.int32)

    iq = iq_ref[0]
    zero = jnp.zeros_like(iq)
    iq4 = jnp.concatenate(
        [jnp.where((lane >> (IDX_DIM.bit_length() - 1)) == hh, iq, zero) for hh in range(IDX_HEADS)],
        axis=0)
    iw = iw_ref[0]
    iw_cols = [iw[:, hh:hh + 1] for hh in range(IDX_HEADS)]

    def score_chunk(c, has_future_keys):
        off = pl.multiple_of(c * tk, tk)
        rel = jnp.maximum(_nt_dot(iq4, ik4_ref[0, pl.ds(off, tk), :]), 0.0)
        score = iw_cols[0] * rel[0:tq]
        for hh in range(1, IDX_HEADS):
            score = score + iw_cols[hh] * rel[hh * tq:(hh + 1) * tq]
        bits = lax.bitcast_convert_type(score, jnp.int32)
        key = bits ^ ((bits >> 31) & jnp.int32(0x7FFFFFFF))
        kpos = off + lane_tk
        key = jnp.where(score == 0.0, (seq - 1) - kpos, jnp.where(key > 0, key + seq, key))
        if has_future_keys:
            key = jnp.where(kpos <= qpos, key, jnp.int32(INT_MIN))
        keys_ref[:, pl.ds(off, tk)] = key
        kts = [key[:, j * LANES:(j + 1) * LANES].T for j in range(nsub)]
        for gl in range(tk // BIT_GROUP):
            words = [kts[(gl * BIT_GROUP + 8 * g) // LANES][(8 * g) % LANES:(8 * g) % LANES + 8]
                     for g in range(32)]
            words = _bit_transpose32(words)
            words[0] = ~words[0]
            for b in range(32):
                planes_ref[c * (tk // BIT_GROUP) + gl, b] = words[b]

    def score_body(c, carry):
        score_chunk(c, False)
        return carry

    lax.fori_loop(0, nkv - 1, score_body, 0)
    score_chunk(nkv - 1, True)

    ngroups = planes_ref.shape[0]
    live = nkv * (tk // BIT_GROUP)
    for g in range(ngroups):
        act_ref[g] = jnp.broadcast_to(jnp.where(g < live, jnp.int32(-1), jnp.int32(0)), (8, tq))

    def sweep(i, keep, first=False):
        parts = [jnp.zeros((8, tq), jnp.int32) for _ in range(4)]
        for g in range(ngroups):
            act = act_ref[g]
            if not first:
                act = act & (planes_ref[g, i - 1] ^ keep)
                act_ref[g] = act
            parts[g % 4] = parts[g % 4] + lax.population_count(act & planes_ref[g, i])
        cnt = (parts[0] + parts[1]) + (parts[2] + parts[3])
        return jnp.sum(cnt.astype(F32), axis=0, keepdims=True)

    def decide(i, ones, thr_u, n_gt):
        take = n_gt + ones >= topk
        thr_u = thr_u | jnp.where(take, jnp.left_shift(jnp.int32(1), 31 - i), 0)
        return thr_u, jnp.where(take, n_gt, n_gt + ones), jnp.where(take, 0, -1).astype(jnp.int32)

    def pass_body(i, carry):
        thr_u, n_gt, keep = carry
        return decide(i, sweep(i, keep), thr_u, n_gt)

    start = decide(0, sweep(0, None, first=True),
                   jnp.zeros((1, tq), jnp.int32), jnp.zeros((1, tq), F32))
    thr_u, n_gt, keep = lax.fori_loop(1, 32, pass_body, start)
    n_eq = jnp.zeros((8, tq), jnp.int32)
    for g in range(ngroups):
        n_eq = n_eq + lax.population_count(act_ref[g] & (planes_ref[g, 31] ^ keep))
    thr = thr_u ^ jnp.int32(INT_MIN)
    cnt_ge = n_gt + jnp.sum(n_eq.astype(F32), axis=0, keepdims=True)
    need = topk - n_gt

    def as_rows(x):
        return jnp.broadcast_to(x, (LANES, tq)).T

    thr_r = as_rows(thr)
    need_r = as_rows(need)
    jcut_ref[...] = jnp.full((tq, LANES), 2 ** 30, jnp.int32)

    @pl.when(jnp.max(cnt_ge) > topk)
    def _():
        def ties_before(cand):
            def body(c, cnt):
                off = pl.multiple_of(c * tk, tk)
                for j in range(nsub):
                    keyj = keys_ref[:, pl.ds(pl.multiple_of(off + j * LANES, LANES), LANES)]
                    hit = (keyj == thr_r) & (off + j * LANES + lane < cand)
                    cnt = jnp.where(hit, cnt + 1, cnt)
                return cnt
            cnt = lax.fori_loop(0, nkv, body, jnp.zeros((tq, LANES), jnp.int32))
            return jnp.sum(cnt.astype(F32), axis=1, keepdims=True)

        def idx_body(i, j):
            cand = j + jnp.left_shift(jnp.int32(1), idx_bits - 1 - i)
            return jnp.where(ties_before(cand) < need_r, cand, j)
        jcut_ref[...] = lax.fori_loop(0, idx_bits, idx_body, jnp.zeros((tq, LANES), jnp.int32))

    jcut_r = jcut_ref[...]

    q = q_ref[0]
    q8 = jnp.concatenate(
        [_split_heads_rows(q[:, p * LANES:(p + 1) * LANES]) for p in range(4)], axis=0)
    m_ref[...] = jnp.full(m_ref.shape, NEG_BIG, F32)
    l_ref[...] = jnp.zeros(l_ref.shape, F32)
    acc_ref[...] = jnp.zeros(acc_ref.shape, F32)

    def attend(off, width):
        s = _nt_dot(q8, k2_ref[0, pl.ds(off, width), :])
        bias = []
        for j in range(width // LANES):
            keyj = keys_ref[:, pl.ds(pl.multiple_of(off + j * LANES, LANES), LANES)]
            kpos = off + j * LANES + lane
            sel = ((keyj > thr_r) | ((keyj == thr_r) & (kpos <= jcut_r))) & (kpos <= qpos)
            bias.append(jnp.where(sel, 0.0, NEG_BIG))
        _online_update(s, v2_ref[0, pl.ds(off, width), :], m_ref, l_ref, acc_ref, bias=bias)

    def attn_body(c, carry):
        attend(pl.multiple_of(c * 2 * tk, 2 * tk), 2 * tk)
        return carry

    lax.fori_loop(0, nkv // 2, attn_body, 0)

    @pl.when(nkv % 2 == 1)
    def _():
        attend(pl.multiple_of((nkv - 1) * tk, tk), tk)

    o = _normalised(l_ref, acc_ref)
    lo = lane < HEAD_DIM
    for p in range(4):
        o_ref[0, :, p * LANES:(p + 1) * LANES] = jnp.where(
            lo, o[(2 * p) * tq:(2 * p + 1) * tq], o[(2 * p + 1) * tq:(2 * p + 2) * tq]
        ).astype(o_ref.dtype)


def _dsa_attn(bq, iq, iw, ik4, k2, v2, tq, tk):
    B, S, C = bq.shape
    assert tk % tq == 0 and tk % BIT_GROUP == 0 and S % tk == 0
    topk = min(IDX_TOPK_MAX, S // 4)
    idx_bits = max(1, (S - 1).bit_length())
    kern = functools.partial(_dsa_kernel, tq=tq, tk=tk, topk=topk, idx_bits=idx_bits)
    row = lambda width: pl.BlockSpec((1, tq, width), lambda b, i: (b, i, 0))
    full = _resident((1, S, LANES), lambda b, i: (b, 0, 0))
    return pl.pallas_call(
        kern,
        grid=(B, S // tq),
        in_specs=[row(C), row(LANES), row(LANES), full, full, full],
        out_specs=row(C),
        out_shape=jax.ShapeDtypeStruct(bq.shape, BF16),
        scratch_shapes=[pltpu.VMEM((tq, S), jnp.int32),
                        pltpu.VMEM((S // BIT_GROUP, 32, 8, tq), jnp.int32),
                        pltpu.VMEM((S // BIT_GROUP, 8, tq), jnp.int32),
                        pltpu.VMEM((tq, LANES), jnp.int32),
                        pltpu.VMEM((8 * tq, LANES), F32), pltpu.VMEM((8 * tq, LANES), F32),
                        pltpu.VMEM((8 * tq, LANES), F32)],
        compiler_params=_cparams(("parallel", "arbitrary")),
        name="dsa_attn",
    )(bq, iq, iw, ik4, k2, v2)


MOBA_TILE_BLOCKS = 8
MOBA_BULK_BLOCKS = 4


def _moba_kernel(q_ref, k_ref, v_ref, km_ref, o_ref, m_ref, l_ref, acc_ref, *, nb):
    nqb = MOBA_TILE_BLOCKS
    blk_rows = MOBA_BLOCK
    grp = 2 * blk_rows
    i = pl.program_id(2)
    first = nqb * i
    lane = lax.broadcasted_iota(jnp.int32, (1, LANES), 1)
    q = q_ref[0]
    q2 = jnp.concatenate(
        [_split_heads_rows(q[j * blk_rows:(j + 1) * blk_rows]) for j in range(nqb)], axis=0)

    km = jnp.concatenate([km_ref[0], jnp.zeros((LANES - nb, LANES), F32)], axis=0)
    km_hi = km.astype(BF16)
    r1 = km - km_hi.astype(F32)
    km_mid = r1.astype(BF16)
    km_lo = (r1 - km_mid.astype(F32)).astype(BF16)
    nb8 = -(-nb // 8) * 8
    gate_t = (_nt_dot(km_hi, q2) + _nt_dot(km_mid, q2) + _nt_dot(km_lo, q2))[:nb8]
    blk = lax.broadcasted_iota(jnp.int32, (nb8, 1), 0)
    blk_f = blk.astype(F32)
    col = lax.broadcasted_iota(jnp.int32, (1, nqb * grp), 1)
    past = blk < first + (col >> (grp.bit_length() - 1))
    g = jnp.where(past, gate_t, -jnp.inf)
    sel = jnp.zeros(g.shape, jnp.bool_)
    for _ in range(min(MOBA_TOPK, nb - 1)):
        mx = jnp.max(g, axis=0, keepdims=True)
        pick = blk_f == jnp.min(jnp.where(g == mx, blk_f, float(LANES)), axis=0, keepdims=True)
        sel = sel | pick
        g = jnp.where(pick, -jnp.inf, g)
    bias_t = jnp.concatenate([jnp.where(sel & past, 0.0, NEG_BIG),
                              jnp.full((LANES - nb8, nqb * grp), NEG_BIG, F32)], axis=0)
    qa = jnp.concatenate([q2, bias_t.T.astype(BF16)], axis=1)

    m_ref[...] = jnp.full(m_ref.shape, NEG_BIG, F32)
    l_ref[...] = jnp.zeros(l_ref.shape, F32)
    acc_ref[...] = jnp.zeros(acc_ref.shape, F32)

    def biased_logits(lhs, first_blk, nblk):
        n = nblk * blk_rows
        off = pl.multiple_of(first_blk * blk_rows, blk_rows)
        row_blk = lax.broadcasted_iota(jnp.int32, (n, 1), 0) >> (MOBA_BLOCK.bit_length() - 1)
        onehot = jnp.where(lane == first_blk + row_blk, 1.0, 0.0).astype(BF16)
        ka = jnp.concatenate([k_ref[0, pl.ds(off, n), :], onehot], axis=1)
        return _nt_dot(lhs, ka), v_ref[0, pl.ds(off, n), :]

    def body(c, carry):
        s, v = biased_logits(qa, MOBA_BULK_BLOCKS * c, MOBA_BULK_BLOCKS)
        _online_update(s, v, m_ref, l_ref, acc_ref)
        return carry

    lax.fori_loop(0, first // MOBA_BULK_BLOCKS, body, 0)

    r = lax.broadcasted_iota(jnp.int32, (grp, 1), 0)
    causal = (lax.broadcasted_iota(jnp.int32, (1, blk_rows), 1)
              <= jnp.where(r >= blk_rows, r - blk_rows, r))
    for j in range(nqb):
        off = pl.multiple_of((first + j) * blk_rows, blk_rows)
        s = jnp.where(causal, _nt_dot(q2[j * grp:(j + 1) * grp], k_ref[0, pl.ds(off, blk_rows), :]),
                      NEG_BIG)
        if j + 1 < nqb:
            s_later, _ = biased_logits(qa[(j + 1) * grp:], first + j, 1)
            s = jnp.concatenate([s, s_later], axis=0)
        _online_update(s, v_ref[0, pl.ds(off, blk_rows), :], m_ref, l_ref, acc_ref, row0=j * grp)

    o = _normalised(l_ref, acc_ref)
    lo = lane < HEAD_DIM
    for j in range(nqb):
        o_ref[0, j * blk_rows:(j + 1) * blk_rows] = jnp.where(
            lo, o[j * grp:j * grp + blk_rows], o[j * grp + blk_rows:(j + 1) * grp]).astype(o_ref.dtype)


def _moba_attn(cq, ck, cv, kmean):
    B, S, C = cq.shape
    nb = S // MOBA_BLOCK
    tq = MOBA_TILE_BLOCKS * MOBA_BLOCK
    return pl.pallas_call(
        functools.partial(_moba_kernel, nb=nb),
        grid=(B, C // LANES, S // tq),
        in_specs=[pl.BlockSpec((1, tq, LANES), lambda b, p, i: (b, i, p)),
                  pl.BlockSpec((1, S, LANES), lambda b, p, i: (b, 0, p)),
                  pl.BlockSpec((1, S, LANES), lambda b, p, i: (b, 0, p)),
                  pl.BlockSpec((1, nb, LANES), lambda b, p, i: (b, 0, p))],
        out_specs=pl.BlockSpec((1, tq, LANES), lambda b, p, i: (b, i, p)),
        out_shape=jax.ShapeDtypeStruct(cq.shape, BF16),
        scratch_shapes=[pltpu.VMEM((2 * tq, LANES), F32), pltpu.VMEM((2 * tq, LANES), F32),
                        pltpu.VMEM((2 * tq, LANES), F32)],
        compiler_params=_cparams(("parallel", "parallel", "arbitrary")),
        name="moba_attn",
    )(cq, ck, cv, kmean)


def _sigmoid(x):
    return 1.0 / (1.0 + jnp.exp(-x))


def _merge_kernel(ya_ref, yb_ref, yc_ref, gates_ref, x_ref, wb_ref, wo_ref, g_ref, o_ref):
    d = x_ref.shape[2]
    merged = None
    for n, y_ref in enumerate((ya_ref, yb_ref, yc_ref)):
        br = jnp.dot(y_ref[0], wb_ref[n], preferred_element_type=F32)
        term = _sigmoid(gates_ref[0, :, n * d:(n + 1) * d].astype(F32)) * br
        merged = term if merged is None else merged + term
    m = jnp.dot(merged.astype(BF16), wo_ref[...], preferred_element_type=F32)
    o_ref[0] = x_ref[0] + _rms(m, g_ref[...])


def _merge(ya, yb, yc, gates, x, wb, wo, layer, g, tm):
    B, S, D = x.shape
    row = lambda width: pl.BlockSpec((1, tm, width), lambda b, m: (b, m, 0))
    return pl.pallas_call(
        _merge_kernel,
        grid=(B, S // tm),
        in_specs=[row(BRANCH_WIDTH), row(BRANCH_WIDTH), row(BRANCH_WIDTH), row(N_BRANCH * D), row(D),
                  _resident((None,) + wb.shape[1:], lambda b, m: (layer, 0, 0, 0)),
                  _resident((None,) + wo.shape[1:], lambda b, m: (layer, 0, 0)),
                  _resident((1, D), lambda b, m: (0, 0))],
        out_specs=row(D),
        out_shape=jax.ShapeDtypeStruct(x.shape, F32),
        compiler_params=_cparams(("parallel", "parallel")),
        name="merge_out",
    )(ya, yb, yc, gates, x, wb, wo, g.reshape(1, D))


def _ffn_kernel(x_ref, g_in_ref, wi_ref, wo_ref, g_out_ref, o_ref, acc_ref, *, d_ff, tf):
    x = x_ref[0]
    h = _rms(x, g_in_ref[...]).astype(BF16)
    acc_ref[...] = jnp.zeros(acc_ref.shape, F32)

    def body(c, carry):
        off = pl.multiple_of(c * tf, tf)
        gt = jnp.dot(h, wi_ref[:, pl.ds(off, tf)], preferred_element_type=F32)
        up_off = pl.multiple_of(d_ff + off, LANES)
        up = jnp.dot(h, wi_ref[:, pl.ds(up_off, tf)], preferred_element_type=F32)
        act = (gt * _sigmoid(gt) * up).astype(BF16)
        acc_ref[...] += jnp.dot(act, wo_ref[pl.ds(off, tf), :], preferred_element_type=F32)
        return carry

    lax.fori_loop(0, d_ff // tf, body, 0)
    o_ref[0] = x + _rms(acc_ref[...], g_out_ref[...])


def _ffn(x, g_in, wi, wo, layer, g_out, tm, tf):
    B, S, D = x.shape
    d_ff = wo.shape[1]
    row = pl.BlockSpec((1, tm, D), lambda b, m: (b, m, 0))
    vec = _resident((1, D), lambda b, m: (0, 0))
    return pl.pallas_call(
        functools.partial(_ffn_kernel, d_ff=d_ff, tf=tf),
        grid=(B, S // tm),
        in_specs=[row, vec, _resident((None,) + wi.shape[1:], lambda b, m: (layer, 0, 0)),
                  _resident((None,) + wo.shape[1:], lambda b, m: (layer, 0, 0)), vec],
        out_specs=row,
        out_shape=jax.ShapeDtypeStruct(x.shape, F32),
        scratch_shapes=[pltpu.VMEM((tm, D), F32)],
        compiler_params=_cparams(("parallel", "parallel")),
        name="swiglu_ffn",
    )(x, g_in.reshape(1, D), wi, wo, g_out.reshape(1, D))


class _Tiles(NamedTuple):
    proj_rows: int
    dense_rows: int
    ffn_cols: int
    diff_tile: int
    dsa_queries: int
    dsa_keys: int


def _tiles(seq):
    return _Tiles(proj_rows=min(512, seq), dense_rows=min(1024, seq), ffn_cols=256,
                  diff_tile=min(1024, seq), dsa_queries=min(256, seq), dsa_keys=min(512, seq))


def kernel(x, positions, w_in, w_branch, w_out, lambda_q1, lambda_k1, lambda_q2, lambda_k2,
           subln_g, norm_g, w_ffn_in, w_ffn_out):
    B, S, D = x.shape
    depth = w_in.shape[0]
    tiles = _tiles(S)
    assert S % (MOBA_TILE_BLOCKS * MOBA_BLOCK) == 0 and S % tiles.diff_tile == 0
    tab = _rope_table(positions, tiles.proj_rows)
    assert w_in.shape[2] == B_START + B_COLS
    w_bf = w_in.astype(BF16)
    w_a, w_b = w_bf[..., :A_COLS], w_bf[..., B_START:]
    wb, wo = w_branch.astype(BF16), w_out.astype(BF16)
    wi_ffn, wo_ffn = w_ffn_in.astype(BF16), w_ffn_out.astype(BF16)
    for l in range(depth):
        lam_init = 0.8 - 0.6 * math.exp(-0.3 * l)
        (aq, ak, av, bq, k2, v2, iq, ik4, iw, cq, ck, cv, gates, kmean) = _inproj(
            x, norm_g[l, 0], tab, w_a, w_b, l, tiles.proj_rows)
        kmean = kmean.reshape(B, S // MOBA_BLOCK, 512)
        lam_rows = jnp.stack([lambda_q1[l], lambda_k1[l], lambda_q2[l], lambda_k2[l]])
        ya = _diff_attn(aq, ak, av, lam_rows, subln_g[l], lam_init, t=tiles.diff_tile)
        yb = _dsa_attn(bq, iq, iw, ik4, k2, v2, tq=tiles.dsa_queries, tk=tiles.dsa_keys)
        yc = _moba_attn(cq, ck, cv, kmean)
        x = _merge(ya, yb, yc, gates, x, wb, wo, l, norm_g[l, 1], tiles.dense_rows)
        x = _ffn(x, norm_g[l, 2], wi_ffn, wo_ffn, l, norm_g[l, 3], tiles.dense_rows,
                 tf=tiles.ffn_cols)
    return x
```

```python
import functools
import math
from typing import NamedTuple

import jax
import jax.numpy as jnp
from jax import lax
from jax.experimental import pallas as pl
from jax.experimental.pallas import tpu as pltpu

F32 = jnp.float32
BF16 = jnp.bfloat16

LANES = 128
HEAD_DIM = 64
ROPE_THETA = 500000.0
NORM_EPS = 1e-6
IDX_HEADS = 4
IDX_DIM = 32
IDX_TOPK_MAX = 256
MOBA_BLOCK = 256
MOBA_TOPK = 3
N_BRANCH = 3
BRANCH_WIDTH = 512
NEG_BIG = -1e30
LOG2E = math.log2(math.e)
ROW_BLOCK = 64
INT_MIN = -2 ** 31
VMEM_LIMIT = 56 * 1024 * 1024

A_AQ, A_AK, A_AV, A_BQ, A_KV, A_IQ, A_IKW, A_COLS = 0, 512, 1024, 1536, 2048, 2176, 2304, 2432
B_START = 2340
B_CQ, B_CK, B_CV, B_G, B_COLS = 0, 512, 1024, 1536, 4608
Q_SCALE = HEAD_DIM ** -0.5 * LOG2E


def _nt_dot(a, b):
    return lax.dot_general(a, b, (((1,), (1,)), ((), ())), preferred_element_type=F32)


def _rms(x, g):
    return x * lax.rsqrt(jnp.mean(x * x, axis=-1, keepdims=True) + NORM_EPS) * g


def _cparams(sem):
    return pltpu.CompilerParams(dimension_semantics=sem, vmem_limit_bytes=VMEM_LIMIT)


def _resident(shape, index_map):
    return pl.BlockSpec(shape, index_map, pipeline_mode=pl.Buffered(1))


def _rope_table_kernel(pos_ref, tab_ref):
    pos = pos_ref[0].astype(F32)
    lane = lax.broadcasted_iota(jnp.int32, (1, LANES), 1)

    def tables(head_dim):
        rot = head_dim // 4
        half = rot // 2
        d = lane & (head_dim - 1)
        fi = d & (half - 1)
        inv = jnp.zeros((1, LANES), F32)
        for i in range(half):
            inv = jnp.where(fi == i, F32(ROPE_THETA ** (-(2.0 * i) / rot)), inv)
        ang = pos * inv
        cos = jnp.cos(ang)
        sin = jnp.sin(ang)
        c = jnp.where(d < rot, cos, 1.0)
        s = jnp.where(d < half, -sin, jnp.where(d < rot, sin, 0.0))
        return c, s

    c64, s64 = tables(HEAD_DIM)
    c32, s32 = tables(IDX_DIM)
    tab_ref[0, :, 0 * LANES:1 * LANES] = c64
    tab_ref[0, :, 1 * LANES:2 * LANES] = s64
    tab_ref[0, :, 2 * LANES:3 * LANES] = c32
    tab_ref[0, :, 3 * LANES:4 * LANES] = s32


def _rope_table(positions, tm):
    B, S = positions.shape
    return pl.pallas_call(
        _rope_table_kernel,
        grid=(B, S // tm),
        in_specs=[pl.BlockSpec((1, tm, 1), lambda b, m: (b, m, 0))],
        out_specs=pl.BlockSpec((1, tm, 4 * LANES), lambda b, m: (b, m, 0)),
        out_shape=jax.ShapeDtypeStruct((B, S, 4 * LANES), F32),
        compiler_params=_cparams(("parallel", "parallel")),
        name="rope_table",
    )(positions.reshape(B, S, 1))


def _inproj_kernel(x_ref, g_ref, tab_ref, wa_ref, wb_ref,
                   aq_ref, ak_ref, av_ref, bq_ref, k2_ref, v2_ref, iq_ref, ik4_ref, iw_ref,
                   cq_ref, ck_ref, cv_ref, gates_ref, kmean_ref):
    h = _rms(x_ref[0], g_ref[...]).astype(BF16)
    lane = lax.broadcasted_iota(jnp.int32, (1, LANES), 1)

    def proj(w_ref, off, width):
        return jnp.dot(h, w_ref[:, off:off + width], preferred_element_type=F32)

    def rope(y, head_dim):
        half = head_dim // 8
        t0 = 0 if head_dim == HEAD_DIM else 2 * LANES
        c = tab_ref[0, :, t0:t0 + LANES]
        s = tab_ref[0, :, t0 + LANES:t0 + 2 * LANES]
        first = (lane & (head_dim - 1)) < half
        blocks = []
        for j in range(y.shape[1] // LANES):
            yj = y[:, j * LANES:(j + 1) * LANES]
            up = pltpu.roll(yj, LANES - half, 1)
            dn = pltpu.roll(yj, half, 1)
            blocks.append(yj * c + jnp.where(first, up, dn) * s)
        return blocks

    def store(out_ref, blocks):
        for j, r in enumerate(blocks):
            out_ref[0, :, j * LANES:(j + 1) * LANES] = r.astype(out_ref.dtype)

    store(aq_ref, rope(proj(wa_ref, A_AQ, 512) * Q_SCALE, HEAD_DIM))
    store(ak_ref, rope(proj(wa_ref, A_AK, 512), HEAD_DIM))
    av_ref[0] = proj(wa_ref, A_AV, 512).astype(av_ref.dtype)
    store(bq_ref, rope(proj(wa_ref, A_BQ, 512) * Q_SCALE, HEAD_DIM))
    kv = proj(wa_ref, A_KV, LANES)
    vk = pltpu.roll(kv, HEAD_DIM, 1)
    lo = lane < HEAD_DIM
    store(k2_ref, rope(jnp.where(lo, kv, vk), HEAD_DIM))
    v2_ref[0] = jnp.where(lo, vk, kv).astype(v2_ref.dtype)
    store(iq_ref, rope(proj(wa_ref, A_IQ, LANES), IDX_DIM))
    ikw = proj(wa_ref, A_IKW, LANES)
    ik = rope(ikw, IDX_DIM)[0]
    ik4 = ik
    for rep in range(1, LANES // IDX_DIM):
        ik4 = jnp.where(lane < rep * IDX_DIM, ik4, pltpu.roll(ik, rep * IDX_DIM, 1))
    ik4_ref[0] = ik4.astype(ik4_ref.dtype)
    iw_ref[0] = jnp.where(lane < IDX_HEADS, pltpu.roll(ikw, LANES - IDX_DIM, 1), 0.0)
    store(cq_ref, rope(proj(wb_ref, B_CQ, 512) * Q_SCALE, HEAD_DIM))
    ck_blocks = rope(proj(wb_ref, B_CK, 512), HEAD_DIM)
    store(ck_ref, ck_blocks)
    cv_ref[0] = proj(wb_ref, B_CV, 512).astype(cv_ref.dtype)
    gates_ref[0] = proj(wb_ref, B_G, 3 * 1024).astype(gates_ref.dtype)

    tm = x_ref.shape[1]
    for blk in range(tm // MOBA_BLOCK):
        for j, r in enumerate(ck_blocks):
            part = r[blk * MOBA_BLOCK:(blk + 1) * MOBA_BLOCK]
            kmean_ref[0, 0, blk:blk + 1, j * LANES:(j + 1) * LANES] = (
                jnp.mean(part, axis=0, keepdims=True))


def _inproj(x, g, tab, w_a, w_b, layer, tm):
    B, S, D = x.shape
    nm = S // tm
    row = lambda width: pl.BlockSpec((1, tm, width), lambda b, m: (b, m, 0))
    shp = lambda width, dt: jax.ShapeDtypeStruct((B, S, width), dt)
    widths = [(512, BF16)] * 4 + [(LANES, BF16)] * 4 + [(LANES, F32)] + [(512, BF16)] * 3 + [(3072, BF16)]
    out_specs = [row(w) for w, _ in widths]
    out_shape = [shp(w, dt) for w, dt in widths]
    out_specs.append(pl.BlockSpec((1, 1, tm // MOBA_BLOCK, 512), lambda b, m: (b, m, 0, 0)))
    out_shape.append(jax.ShapeDtypeStruct((B, nm, tm // MOBA_BLOCK, 512), F32))
    return pl.pallas_call(
        _inproj_kernel,
        grid=(B, nm),
        in_specs=[row(D),
                  _resident((1, D), lambda b, m: (0, 0)),
                  row(4 * LANES),
                  _resident((None, D, A_COLS), lambda b, m: (layer, 0, 0)),
                  _resident((None, D, B_COLS), lambda b, m: (layer, 0, 0))],
        out_specs=out_specs,
        out_shape=out_shape,
        compiler_params=_cparams(("parallel", "parallel")),
        name="inproj",
    )(x, g.reshape(1, D), tab, w_a, w_b)


def _split_heads_rows(q):
    lo = lax.broadcasted_iota(jnp.int32, (1, LANES), 1) < HEAD_DIM
    zero = jnp.zeros_like(q)
    return jnp.concatenate([jnp.where(lo, q, zero), jnp.where(lo, zero, q)], axis=0)


def _online_update(s, v, m_ref, l_ref, acc_ref, bias=None, row0=0):
    rows_total = s.shape[0]
    nblk = s.shape[1] // LANES
    p_rows, alphas = [], []
    for r0 in range(0, rows_total, ROW_BLOCK):
        rows = slice(row0 + r0, row0 + r0 + ROW_BLOCK)
        cols = [s[r0:r0 + ROW_BLOCK, j * LANES:(j + 1) * LANES] for j in range(nblk)]
        if bias is not None:
            b0 = r0 % bias[0].shape[0]
            cols = [c + bj[b0:b0 + ROW_BLOCK] for c, bj in zip(cols, bias)]
        m_prev = m_ref[rows]
        m_new = jnp.maximum(
            m_prev, jnp.max(functools.reduce(jnp.maximum, cols), axis=1, keepdims=True))
        alpha = jnp.exp2(m_prev - m_new)
        ps = [jnp.exp2(c - m_new) for c in cols]
        l_ref[rows] = alpha * l_ref[rows] + functools.reduce(jnp.add, ps)
        m_ref[rows] = m_new
        p_rows.append(jnp.concatenate([pj.astype(BF16) for pj in ps], axis=1))
        alphas.append(alpha)
    p = jnp.concatenate(p_rows, axis=0)
    alpha = jnp.concatenate(alphas, axis=0)
    span = slice(row0, row0 + rows_total)
    acc_ref[span] = alpha * acc_ref[span] + jnp.dot(p, v, preferred_element_type=F32)


def _normalised(l_ref, acc_ref):
    return acc_ref[...] / jnp.sum(l_ref[...], axis=1, keepdims=True)


DIFF_SUB = 512
DIFF_BULK_KEYS = 1024


def _diff_attn_kernel(q_ref, k_ref, v_ref, lam_ref, g_ref, o_ref, m_ref, l_ref, acc_ref,
                      *, t, lam_init):
    qi = pl.program_id(2)
    sub = DIFF_SUB
    nsb = t // sub
    grp = 2 * sub
    q = q_ref[0]
    q2 = jnp.concatenate(
        [_split_heads_rows(q[j * sub:(j + 1) * sub]) for j in range(nsb)], axis=0)
    m_ref[...] = jnp.full(m_ref.shape, NEG_BIG, F32)
    l_ref[...] = jnp.zeros(l_ref.shape, F32)
    acc_ref[...] = jnp.zeros(acc_ref.shape, F32)

    def body(c, carry):
        off = pl.multiple_of(c * DIFF_BULK_KEYS, DIFF_BULK_KEYS)
        _online_update(_nt_dot(q2, k_ref[0, pl.ds(off, DIFF_BULK_KEYS), :]),
                       v_ref[0, pl.ds(off, DIFF_BULK_KEYS), :], m_ref, l_ref, acc_ref)
        return carry

    lax.fori_loop(0, qi * (t // DIFF_BULK_KEYS), body, 0)

    r = lax.broadcasted_iota(jnp.int32, (grp, 1), 0)
    causal = lax.broadcasted_iota(jnp.int32, (1, sub), 1) <= jnp.where(r >= sub, r - sub, r)
    for j in range(nsb):
        off = pl.multiple_of(qi * t + j * sub, sub)
        s = _nt_dot(q2[j * grp:], k_ref[0, pl.ds(off, sub), :])
        own = jnp.where(causal, s[:grp], NEG_BIG)
        s = own if j + 1 == nsb else jnp.concatenate([own, s[grp:]], axis=0)
        _online_update(s, v_ref[0, pl.ds(off, sub), :], m_ref, l_ref, acc_ref, row0=j * grp)

    lam_rows = lam_ref[...]
    e1 = jnp.exp(jnp.sum(lam_rows[0:1] * lam_rows[1:2], axis=1, keepdims=True))
    e2 = jnp.exp(jnp.sum(lam_rows[2:3] * lam_rows[3:4], axis=1, keepdims=True))
    lam = e1 - e2 + lam_init
    o = _normalised(l_ref, acc_ref)
    for j in range(nsb):
        out = o[j * grp:j * grp + sub] - lam * o[j * grp + sub:(j + 1) * grp]
        o_ref[0, j * sub:(j + 1) * sub] = (
            _rms(out, g_ref[...]) * (1.0 - lam_init)).astype(o_ref.dtype)


def _diff_attn(aq, ak, av, lam_rows, subln_g, lam_init, t):
    B, S, _ = aq.shape
    nh = aq.shape[2] // LANES
    kern = functools.partial(_diff_attn_kernel, t=t, lam_init=lam_init)
    return pl.pallas_call(
        kern,
        grid=(B, nh, S // t),
        in_specs=[pl.BlockSpec((1, t, LANES), lambda b, h, i: (b, i, h)),
                  pl.BlockSpec((1, S, LANES), lambda b, h, i: (b, 0, h)),
                  pl.BlockSpec((1, S, LANES), lambda b, h, i: (b, 0, h)),
                  pl.BlockSpec((4, HEAD_DIM), lambda b, h, i: (0, 0)),
                  pl.BlockSpec((1, LANES), lambda b, h, i: (0, 0))],
        out_specs=pl.BlockSpec((1, t, LANES), lambda b, h, i: (b, i, h)),
        out_shape=jax.ShapeDtypeStruct(aq.shape, BF16),
        scratch_shapes=[pltpu.VMEM((2 * t, LANES), F32), pltpu.VMEM((2 * t, LANES), F32),
                        pltpu.VMEM((2 * t, LANES), F32)],
        compiler_params=_cparams(("parallel", "parallel", "arbitrary")),
        name="diff_attn",
    )(aq, ak, av, lam_rows, subln_g.reshape(1, LANES))


BIT_GROUP = 256


def _bit_transpose32(words):
    w = list(words)
    for j, m in ((16, 0x0000FFFF), (8, 0x00FF00FF), (4, 0x0F0F0F0F), (2, 0x33333333), (1, 0x55555555)):
        for k in range(32):
            if k & j == 0:
                t = (w[k] ^ (w[k + j] >> j)) & m
                w[k] = w[k] ^ t
                w[k + j] = w[k + j] ^ (t << j)
    return w


def _dsa_kernel(q_ref, iq_ref, iw_ref, ik4_ref, k2_ref, v2_ref, o_ref,
                keys_ref, planes_ref, act_ref, jcut_ref, m_ref, l_ref, acc_ref,
                *, tq, tk, topk, idx_bits):
    qi = pl.program_id(1)
    nkv = (qi * tq + tq - 1) // tk + 1
    lane = lax.broadcasted_iota(jnp.int32, (1, LANES), 1)
    qpos = qi * tq + lax.broadcasted_iota(jnp.int32, (tq, 1), 0)
    lane_tk = lax.broadcasted_iota(jnp.int32, (1, tk), 1)
    nsub = tk // LANES
    seq = keys_ref.shape[1]

    @pl.when(qi == 0)
    def _():
        planes_ref[...] = jnp.zeros(planes_ref.shape, jnp.int32)

    iq = iq_ref[0]
    zero = jnp.zeros_like(iq)
    iq4 = jnp.concatenate(
        [jnp.where((lane >> (IDX_DIM.bit_length() - 1)) == hh, iq, zero) for hh in range(IDX_HEADS)],
        axis=0)
    iw = iw_ref[0]
    iw_cols = [iw[:, hh:hh + 1] for hh in range(IDX_HEADS)]

    def score_chunk(c, has_future_keys):
        off = pl.multiple_of(c * tk, tk)
        rel = jnp.maximum(_nt_dot(iq4, ik4_ref[0, pl.ds(off, tk), :]), 0.0)
        score = iw_cols[0] * rel[0:tq]
        for hh in range(1, IDX_HEADS):
            score = score + iw_cols[hh] * rel[hh * tq:(hh + 1) * tq]
        bits = lax.bitcast_convert_type(score, jnp.int32)
        key = bits ^ ((bits >> 31) & jnp.int32(0x7FFFFFFF))
        kpos = off + lane_tk
        key = jnp.where(score == 0.0, (seq - 1) - kpos, jnp.where(key > 0, key + seq, key))
        if has_future_keys:
            key = jnp.where(kpos <= qpos, key, jnp.int32(INT_MIN))
        keys_ref[:, pl.ds(off, tk)] = key
        kts = [key[:, j * LANES:(j + 1) * LANES].T for j in range(nsub)]
        for gl in range(tk // BIT_GROUP):
            words = [kts[(gl * BIT_GROUP + 8 * g) // LANES][(8 * g) % LANES:(8 * g) % LANES + 8]
                     for g in range(32)]
            words = _bit_transpose32(words)
            words[0] = ~words[0]
            for b in range(32):
                planes_ref[c * (tk // BIT_GROUP) + gl, b] = words[b]

    def score_body(c, carry):
        score_chunk(c, False)
        return carry

    lax.fori_loop(0, nkv - 1, score_body, 0)
    score_chunk(nkv - 1, True)

    ngroups = planes_ref.shape[0]
    live = nkv * (tk // BIT_GROUP)
    for g in range(ngroups):
        act_ref[g] = jnp.broadcast_to(jnp.where(g < live, jnp.int32(-1), jnp.int32(0)), (8, tq))

    def sweep(i, keep, first=False):
        parts = [jnp.zeros((8, tq), jnp.int32) for _ in range(4)]
        for g in range(ngroups):
            act = act_ref[g]
            if not first:
                act = act & (planes_ref[g, i - 1] ^ keep)
                act_ref[g] = act
            parts[g % 4] = parts[g % 4] + lax.population_count(act & planes_ref[g, i])
        cnt = (parts[0] + parts[1]) + (parts[2] + parts[3])
        return jnp.sum(cnt.astype(F32), axis=0, keepdims=True)

    def decide(i, ones, thr_u, n_gt):
        take = n_gt + ones >= topk
        thr_u = thr_u | jnp.where(take, jnp.left_shift(jnp.int32(1), 31 - i), 0)
        return thr_u, jnp.where(take, n_gt, n_gt + ones), jnp.where(take, 0, -1).astype(jnp.int32)

    def pass_body(i, carry):
        thr_u, n_gt, keep = carry
        return decide(i, sweep(i, keep), thr_u, n_gt)

    start = decide(0, sweep(0, None, first=True),
                   jnp.zeros((1, tq), jnp.int32), jnp.zeros((1, tq), F32))
    thr_u, n_gt, keep = lax.fori_loop(1, 32, pass_body, start)
    n_eq = jnp.zeros((8, tq), jnp.int32)
    for g in range(ngroups):
        n_eq = n_eq + lax.population_count(act_ref[g] & (planes_ref[g, 31] ^ keep))
    thr = thr_u ^ jnp.int32(INT_MIN)
    cnt_ge = n_gt + jnp.sum(n_eq.astype(F32), axis=0, keepdims=True)
    need = topk - n_gt

    def as_rows(x):
        return jnp.broadcast_to(x, (LANES, tq)).T

    thr_r = as_rows(thr)
    need_r = as_rows(need)
    jcut_ref[...] = jnp.full((tq, LANES), 2 ** 30, jnp.int32)

    @pl.when(jnp.max(cnt_ge) > topk)
    def _():
        def ties_before(cand):
            def body(c, cnt):
                off = pl.multiple_of(c * tk, tk)
                for j in range(nsub):
                    keyj = keys_ref[:, pl.ds(pl.multiple_of(off + j * LANES, LANES), LANES)]
                    hit = (keyj == thr_r) & (off + j * LANES + lane < cand)
                    cnt = jnp.where(hit, cnt + 1, cnt)
                return cnt
            cnt = lax.fori_loop(0, nkv, body, jnp.zeros((tq, LANES), jnp.int32))
            return jnp.sum(cnt.astype(F32), axis=1, keepdims=True)

        def idx_body(i, j):
            cand = j + jnp.left_shift(jnp.int32(1), idx_bits - 1 - i)
            return jnp.where(ties_before(cand) < need_r, cand, j)
        jcut_ref[...] = lax.fori_loop(0, idx_bits, idx_body, jnp.zeros((tq, LANES), jnp.int32))

    jcut_r = jcut_ref[...]

    q = q_ref[0]
    q8 = jnp.concatenate(
        [_split_heads_rows(q[:, p * LANES:(p + 1) * LANES]) for p in range(4)], axis=0)
    m_ref[...] = jnp.full(m_ref.shape, NEG_BIG, F32)
    l_ref[...] = jnp.zeros(l_ref.shape, F32)
    acc_ref[...] = jnp.zeros(acc_ref.shape, F32)

    def attend(off, width):
        s = _nt_dot(q8, k2_ref[0, pl.ds(off, width), :])
        bias = []
        for j in range(width // LANES):
            keyj = keys_ref[:, pl.ds(pl.multiple_of(off + j * LANES, LANES), LANES)]
            kpos = off + j * LANES + lane
            sel = ((keyj > thr_r) | ((keyj == thr_r) & (kpos <= jcut_r))) & (kpos <= qpos)
            bias.append(jnp.where(sel, 0.0, NEG_BIG))
        _online_update(s, v2_ref[0, pl.ds(off, width), :], m_ref, l_ref, acc_ref, bias=bias)

    def attn_body(c, carry):
        attend(pl.multiple_of(c * 2 * tk, 2 * tk), 2 * tk)
        return carry

    lax.fori_loop(0, nkv // 2, attn_body, 0)

    @pl.when(nkv % 2 == 1)
    def _():
        attend(pl.multiple_of((nkv - 1) * tk, tk), tk)

    o = _normalised(l_ref, acc_ref)
    lo = lane < HEAD_DIM
    for p in range(4):
        o_ref[0, :, p * LANES:(p + 1) * LANES] = jnp.where(
            lo, o[(2 * p) * tq:(2 * p + 1) * tq], o[(2 * p + 1) * tq:(2 * p + 2) * tq]
        ).astype(o_ref.dtype)


def _dsa_attn(bq, iq, iw, ik4, k2, v2, tq, tk):
    B, S, C = bq.shape
    assert tk % tq == 0 and tk % BIT_GROUP == 0 and S % tk == 0
    topk = min(IDX_TOPK_MAX, S // 4)
    idx_bits = max(1, (S - 1).bit_length())
    kern = functools.partial(_dsa_kernel, tq=tq, tk=tk, topk=topk, idx_bits=idx_bits)
    row = lambda width: pl.BlockSpec((1, tq, width), lambda b, i: (b, i, 0))
    full = _resident((1, S, LANES), lambda b, i: (b, 0, 0))
    return pl.pallas_call(
        kern,
        grid=(B, S // tq),
        in_specs=[row(C), row(LANES), row(LANES), full, full, full],
        out_specs=row(C),
        out_shape=jax.ShapeDtypeStruct(bq.shape, BF16),
        scratch_shapes=[pltpu.VMEM((tq, S), jnp.int32),
                        pltpu.VMEM((S // BIT_GROUP, 32, 8, tq), jnp.int32),
                        pltpu.VMEM((S // BIT_GROUP, 8, tq), jnp.int32),
                        pltpu.VMEM((tq, LANES), jnp.int32),
                        pltpu.VMEM((8 * tq, LANES), F32), pltpu.VMEM((8 * tq, LANES), F32),
                        pltpu.VMEM((8 * tq, LANES), F32)],
        compiler_params=_cparams(("parallel", "arbitrary")),
        name="dsa_attn",
    )(bq, iq, iw, ik4, k2, v2)


MOBA_TILE_BLOCKS = 8
MOBA_BULK_BLOCKS = 4


def _moba_kernel(q_ref, k_ref, v_ref, km_ref, o_ref, m_ref, l_ref, acc_ref, *, nb):
    nqb = MOBA_TILE_BLOCKS
    blk_rows = MOBA_BLOCK
    grp = 2 * blk_rows
    i = pl.program_id(2)
    first = nqb * i
    lane = lax.broadcasted_iota(jnp.int32, (1, LANES), 1)
    q = q_ref[0]
    q2 = jnp.concatenate(
        [_split_heads_rows(q[j * blk_rows:(j + 1) * blk_rows]) for j in range(nqb)], axis=0)

    km = jnp.concatenate([km_ref[0], jnp.zeros((LANES - nb, LANES), F32)], axis=0)
    km_hi = km.astype(BF16)
    r1 = km - km_hi.astype(F32)
    km_mid = r1.astype(BF16)
    km_lo = (r1 - km_mid.astype(F32)).astype(BF16)
    nb8 = -(-nb // 8) * 8
    gate_t = (_nt_dot(km_hi, q2) + _nt_dot(km_mid, q2) + _nt_dot(km_lo, q2))[:nb8]
    blk = lax.broadcasted_iota(jnp.int32, (nb8, 1), 0)
    blk_f = blk.astype(F32)
    col = lax.broadcasted_iota(jnp.int32, (1, nqb * grp), 1)
    past = blk < first + (col >> (grp.bit_length() - 1))
    g = jnp.where(past, gate_t, -jnp.inf)
    sel = jnp.zeros(g.shape, jnp.bool_)
    for _ in range(min(MOBA_TOPK, nb - 1)):
        mx = jnp.max(g, axis=0, keepdims=True)
        pick = blk_f == jnp.min(jnp.where(g == mx, blk_f, float(LANES)), axis=0, keepdims=True)
        sel = sel | pick
        g = jnp.where(pick, -jnp.inf, g)
    bias_t = jnp.concatenate([jnp.where(sel & past, 0.0, NEG_BIG),
                              jnp.full((LANES - nb8, nqb * grp), NEG_BIG, F32)], axis=0)
    qa = jnp.concatenate([q2, bias_t.T.astype(BF16)], axis=1)

    m_ref[...] = jnp.full(m_ref.shape, NEG_BIG, F32)
    l_ref[...] = jnp.zeros(l_ref.shape, F32)
    acc_ref[...] = jnp.zeros(acc_ref.shape, F32)

    def biased_logits(lhs, first_blk, nblk):
        n = nblk * blk_rows
        off = pl.multiple_of(first_blk * blk_rows, blk_rows)
        row_blk = lax.broadcasted_iota(jnp.int32, (n, 1), 0) >> (MOBA_BLOCK.bit_length() - 1)
        onehot = jnp.where(lane == first_blk + row_blk, 1.0, 0.0).astype(BF16)
        ka = jnp.concatenate([k_ref[0, pl.ds(off, n), :], onehot], axis=1)
        return _nt_dot(lhs, ka), v_ref[0, pl.ds(off, n), :]

    def body(c, carry):
        s, v = biased_logits(qa, MOBA_BULK_BLOCKS * c, MOBA_BULK_BLOCKS)
        _online_update(s, v, m_ref, l_ref, acc_ref)
        return carry

    lax.fori_loop(0, first // MOBA_BULK_BLOCKS, body, 0)

    r = lax.broadcasted_iota(jnp.int32, (grp, 1), 0)
    causal = (lax.broadcasted_iota(jnp.int32, (1, blk_rows), 1)
              <= jnp.where(r >= blk_rows, r - blk_rows, r))
    for j in range(nqb):
        off = pl.multiple_of((first + j) * blk_rows, blk_rows)
        s = jnp.where(causal, _nt_dot(q2[j * grp:(j + 1) * grp], k_ref[0, pl.ds(off, blk_rows), :]),
                      NEG_BIG)
        if j + 1 < nqb:
            s_later, _ = biased_logits(qa[(j + 1) * grp:], first + j, 1)
            s = jnp.concatenate([s, s_later], axis=0)
        _online_update(s, v_ref[0, pl.ds(off, blk_rows), :], m_ref, l_ref, acc_ref, row0=j * grp)

    o = _normalised(l_ref, acc_ref)
    lo = lane < HEAD_DIM
    for j in range(nqb):
        o_ref[0, j * blk_rows:(j + 1) * blk_rows] = jnp.where(
            lo, o[j * grp:j * grp + blk_rows], o[j * grp + blk_rows:(j + 1) * grp]).astype(o_ref.dtype)


def _moba_attn(cq, ck, cv, kmean):
    B, S, C = cq.shape
    nb = S // MOBA_BLOCK
    tq = MOBA_TILE_BLOCKS * MOBA_BLOCK
    return pl.pallas_call(
        functools.partial(_moba_kernel, nb=nb),
        grid=(B, C // LANES, S // tq),
        in_specs=[pl.BlockSpec((1, tq, LANES), lambda b, p, i: (b, i, p)),
                  pl.BlockSpec((1, S, LANES), lambda b, p, i: (b, 0, p)),
                  pl.BlockSpec((1, S, LANES), lambda b, p, i: (b, 0, p)),
                  pl.BlockSpec((1, nb, LANES), lambda b, p, i: (b, 0, p))],
        out_specs=pl.BlockSpec((1, tq, LANES), lambda b, p, i: (b, i, p)),
        out_shape=jax.ShapeDtypeStruct(cq.shape, BF16),
        scratch_shapes=[pltpu.VMEM((2 * tq, LANES), F32), pltpu.VMEM((2 * tq, LANES), F32),
                        pltpu.VMEM((2 * tq, LANES), F32)],
        compiler_params=_cparams(("parallel", "parallel", "arbitrary")),
        name="moba_attn",
    )(cq, ck, cv, kmean)


def _sigmoid(x):
    return 1.0 / (1.0 + jnp.exp(-x))


def _merge_kernel(ya_ref, yb_ref, yc_ref, gates_ref, x_ref, wb_ref, wo_ref, g_ref, o_ref):
    d = x_ref.shape[2]
    merged = None
    for n, y_ref in enumerate((ya_ref, yb_ref, yc_ref)):
        br = jnp.dot(y_ref[0], wb_ref[n], preferred_element_type=F32)
        term = _sigmoid(gates_ref[0, :, n * d:(n + 1) * d].astype(F32)) * br
        merged = term if merged is None else merged + term
    m = jnp.dot(merged.astype(BF16), wo_ref[...], preferred_element_type=F32)
    o_ref[0] = x_ref[0] + _rms(m, g_ref[...])


def _merge(ya, yb, yc, gates, x, wb, wo, layer, g, tm):
    B, S, D = x.shape
    row = lambda width: pl.BlockSpec((1, tm, width), lambda b, m: (b, m, 0))
    return pl.pallas_call(
        _merge_kernel,
        grid=(B, S // tm),
        in_specs=[row(BRANCH_WIDTH), row(BRANCH_WIDTH), row(BRANCH_WIDTH), row(N_BRANCH * D), row(D),
                  _resident((None,) + wb.shape[1:], lambda b, m: (layer, 0, 0, 0)),
                  _resident((None,) + wo.shape[1:], lambda b, m: (layer, 0, 0)),
                  _resident((1, D), lambda b, m: (0, 0))],
        out_specs=row(D),
        out_shape=jax.ShapeDtypeStruct(x.shape, F32),
        compiler_params=_cparams(("parallel", "parallel")),
        name="merge_out",
    )(ya, yb, yc, gates, x, wb, wo, g.reshape(1, D))


def _ffn_kernel(x_ref, g_in_ref, wi_ref, wo_ref, g_out_ref, o_ref, acc_ref, *, d_ff, tf):
    x = x_ref[0]
    h = _rms(x, g_in_ref[...]).astype(BF16)
    acc_ref[...] = jnp.zeros(acc_ref.shape, F32)

    def body(c, carry):
        off = pl.multiple_of(c * tf, tf)
        gt = jnp.dot(h, wi_ref[:, pl.ds(off, tf)], preferred_element_type=F32)
        up_off = pl.multiple_of(d_ff + off, LANES)
        up = jnp.dot(h, wi_ref[:, pl.ds(up_off, tf)], preferred_element_type=F32)
        act = (gt * _sigmoid(gt) * up).astype(BF16)
        acc_ref[...] += jnp.dot(act, wo_ref[pl.ds(off, tf), :], preferred_element_type=F32)
        return carry

    lax.fori_loop(0, d_ff // tf, body, 0)
    o_ref[0] = x + _rms(acc_ref[...], g_out_ref[...])


def _ffn(x, g_in, wi, wo, layer, g_out, tm, tf):
    B, S, D = x.shape
    d_ff = wo.shape[1]
    row = pl.BlockSpec((1, tm, D), lambda b, m: (b, m, 0))
    vec = _resident((1, D), lambda b, m: (0, 0))
    return pl.pallas_call(
        functools.partial(_ffn_kernel, d_ff=d_ff, tf=tf),
        grid=(B, S // tm),
        in_specs=[row, vec, _resident((None,) + wi.shape[1:], lambda b, m: (layer, 0, 0)),
                  _resident((None,) + wo.shape[1:], lambda b, m: (layer, 0, 0)), vec],
        out_specs=row,
        out_shape=jax.ShapeDtypeStruct(x.shape, F32),
        scratch_shapes=[pltpu.VMEM((tm, D), F32)],
        compiler_params=_cparams(("parallel", "parallel")),
        name="swiglu_ffn",
    )(x, g_in.reshape(1, D), wi, wo, g_out.reshape(1, D))


class _Tiles(NamedTuple):
    proj_rows: int
    dense_rows: int
    ffn_cols: int
    diff_tile: int
    dsa_queries: int
    dsa_keys: int


def _tiles(seq):
    return _Tiles(proj_rows=min(512, seq), dense_rows=min(1024, seq), ffn_cols=256,
                  diff_tile=min(2048, seq), dsa_queries=min(256, seq), dsa_keys=min(512, seq))


def kernel(x, positions, w_in, w_branch, w_out, lambda_q1, lambda_k1, lambda_q2, lambda_k2,
           subln_g, norm_g, w_ffn_in, w_ffn_out):
    B, S, D = x.shape
    depth = w_in.shape[0]
    tiles = _tiles(S)
    assert S % (MOBA_TILE_BLOCKS * MOBA_BLOCK) == 0 and S % tiles.diff_tile == 0
    tab = _rope_table(positions, tiles.proj_rows)
    assert w_in.shape[2] == B_START + B_COLS
    w_bf = w_in.astype(BF16)
    w_a, w_b = w_bf[..., :A_COLS], w_bf[..., B_START:]
    wb, wo = w_branch.astype(BF16), w_out.astype(BF16)
    wi_ffn, wo_ffn = w_ffn_in.astype(BF16), w_ffn_out.astype(BF16)
    for l in range(depth):
        lam_init = 0.8 - 0.6 * math.exp(-0.3 * l)
        (aq, ak, av, bq, k2, v2, iq, ik4, iw, cq, ck, cv, gates, kmean) = _inproj(
            x, norm_g[l, 0], tab, w_a, w_b, l, tiles.proj_rows)
        kmean = kmean.reshape(B, S // MOBA_BLOCK, 512)
        lam_rows = jnp.stack([lambda_q1[l], lambda_k1[l], lambda_q2[l], lambda_k2[l]])
        ya = _diff_attn(aq, ak, av, lam_rows, subln_g[l], lam_init, t=tiles.diff_tile)
        yb = _dsa_attn(bq, iq, iw, ik4, k2, v2, tq=tiles.dsa_queries, tk=tiles.dsa_keys)
        yc = _moba_attn(cq, ck, cv, kmean)
        x = _merge(ya, yb, yc, gates, x, wb, wo, l, norm_g[l, 1], tiles.dense_rows)
        x = _ffn(x, norm_g[l, 2], wi_ffn, wo_ffn, l, norm_g[l, 3], tiles.dense_rows,
                 tf=tiles.ffn_cols)
    return x
```

```python
import functools
import math
from typing import NamedTuple

import jax
import jax.numpy as jnp
from jax import lax
from jax.experimental import pallas as pl
from jax.experimental.pallas import tpu as pltpu

F32 = jnp.float32
BF16 = jnp.bfloat16

LANES = 128
HEAD_DIM = 64
ROPE_THETA = 500000.0
NORM_EPS = 1e-6
IDX_HEADS = 4
IDX_DIM = 32
IDX_TOPK_MAX = 256
MOBA_BLOCK = 256
MOBA_TOPK = 3
N_BRANCH = 3
BRANCH_WIDTH = 512
NEG_BIG = -1e30
LOG2E = math.log2(math.e)
ROW_BLOCK = 64
INT_MIN = -2 ** 31
VMEM_LIMIT = 56 * 1024 * 1024

A_AQ, A_AK, A_AV, A_BQ, A_KV, A_IQ, A_IKW, A_COLS = 0, 512, 1024, 1536, 2048, 2176, 2304, 2432
B_START = 2340
B_CQ, B_CK, B_CV, B_G, B_COLS = 0, 512, 1024, 1536, 4608
Q_SCALE = HEAD_DIM ** -0.5 * LOG2E


def _nt_dot(a, b):
    return lax.dot_general(a, b, (((1,), (1,)), ((), ())), preferred_element_type=F32)


def _rms(x, g):
    return x * lax.rsqrt(jnp.mean(x * x, axis=-1, keepdims=True) + NORM_EPS) * g


def _cparams(sem):
    return pltpu.CompilerParams(dimension_semantics=sem, vmem_limit_bytes=VMEM_LIMIT)


def _resident(shape, index_map):
    return pl.BlockSpec(shape, index_map, pipeline_mode=pl.Buffered(1))


def _rope_table_kernel(pos_ref, tab_ref):
    pos = pos_ref[0].astype(F32)
    lane = lax.broadcasted_iota(jnp.int32, (1, LANES), 1)

    def tables(head_dim):
        rot = head_dim // 4
        half = rot // 2
        d = lane & (head_dim - 1)
        fi = d & (half - 1)
        inv = jnp.zeros((1, LANES), F32)
        for i in range(half):
            inv = jnp.where(fi == i, F32(ROPE_THETA ** (-(2.0 * i) / rot)), inv)
        ang = pos * inv
        cos = jnp.cos(ang)
        sin = jnp.sin(ang)
        c = jnp.where(d < rot, cos, 1.0)
        s = jnp.where(d < half, -sin, jnp.where(d < rot, sin, 0.0))
        return c, s

    c64, s64 = tables(HEAD_DIM)
    c32, s32 = tables(IDX_DIM)
    tab_ref[0, :, 0 * LANES:1 * LANES] = c64
    tab_ref[0, :, 1 * LANES:2 * LANES] = s64
    tab_ref[0, :, 2 * LANES:3 * LANES] = c32
    tab_ref[0, :, 3 * LANES:4 * LANES] = s32


def _rope_table(positions, tm):
    B, S = positions.shape
    return pl.pallas_call(
        _rope_table_kernel,
        grid=(B, S // tm),
        in_specs=[pl.BlockSpec((1, tm, 1), lambda b, m: (b, m, 0))],
        out_specs=pl.BlockSpec((1, tm, 4 * LANES), lambda b, m: (b, m, 0)),
        out_shape=jax.ShapeDtypeStruct((B, S, 4 * LANES), F32),
        compiler_params=_cparams(("parallel", "parallel")),
        name="rope_table",
    )(positions.reshape(B, S, 1))


def _inproj_kernel(x_ref, g_ref, tab_ref, wa_ref, wb_ref,
                   aq_ref, ak_ref, av_ref, bq_ref, k2_ref, v2_ref, iq_ref, ik4_ref, iw_ref,
                   cq_ref, ck_ref, cv_ref, gates_ref, kmean_ref):
    h = _rms(x_ref[0], g_ref[...]).astype(BF16)
    lane = lax.broadcasted_iota(jnp.int32, (1, LANES), 1)

    def proj(w_ref, off, width):
        return jnp.dot(h, w_ref[:, off:off + width], preferred_element_type=F32)

    def rope(y, head_dim):
        half = head_dim // 8
        t0 = 0 if head_dim == HEAD_DIM else 2 * LANES
        c = tab_ref[0, :, t0:t0 + LANES]
        s = tab_ref[0, :, t0 + LANES:t0 + 2 * LANES]
        first = (lane & (head_dim - 1)) < half
        blocks = []
        for j in range(y.shape[1] // LANES):
            yj = y[:, j * LANES:(j + 1) * LANES]
            up = pltpu.roll(yj, LANES - half, 1)
            dn = pltpu.roll(yj, half, 1)
            blocks.append(yj * c + jnp.where(first, up, dn) * s)
        return blocks

    def store(out_ref, blocks):
        for j, r in enumerate(blocks):
            out_ref[0, :, j * LANES:(j + 1) * LANES] = r.astype(out_ref.dtype)

    store(aq_ref, rope(proj(wa_ref, A_AQ, 512) * Q_SCALE, HEAD_DIM))
    store(ak_ref, rope(proj(wa_ref, A_AK, 512), HEAD_DIM))
    av_ref[0] = proj(wa_ref, A_AV, 512).astype(av_ref.dtype)
    store(bq_ref, rope(proj(wa_ref, A_BQ, 512) * Q_SCALE, HEAD_DIM))
    kv = proj(wa_ref, A_KV, LANES)
    vk = pltpu.roll(kv, HEAD_DIM, 1)
    lo = lane < HEAD_DIM
    store(k2_ref, rope(jnp.where(lo, kv, vk), HEAD_DIM))
    v2_ref[0] = jnp.where(lo, vk, kv).astype(v2_ref.dtype)
    store(iq_ref, rope(proj(wa_ref, A_IQ, LANES), IDX_DIM))
    ikw = proj(wa_ref, A_IKW, LANES)
    ik = rope(ikw, IDX_DIM)[0]
    ik4 = ik
    for rep in range(1, LANES // IDX_DIM):
        ik4 = jnp.where(lane < rep * IDX_DIM, ik4, pltpu.roll(ik, rep * IDX_DIM, 1))
    ik4_ref[0] = ik4.astype(ik4_ref.dtype)
    iw_ref[0] = jnp.where(lane < IDX_HEADS, pltpu.roll(ikw, LANES - IDX_DIM, 1), 0.0)
    store(cq_ref, rope(proj(wb_ref, B_CQ, 512) * Q_SCALE, HEAD_DIM))
    ck_blocks = rope(proj(wb_ref, B_CK, 512), HEAD_DIM)
    store(ck_ref, ck_blocks)
    cv_ref[0] = proj(wb_ref, B_CV, 512).astype(cv_ref.dtype)
    gates_ref[0] = proj(wb_ref, B_G, 3 * 1024).astype(gates_ref.dtype)

    tm = x_ref.shape[1]
    for blk in range(tm // MOBA_BLOCK):
        for j, r in enumerate(ck_blocks):
            part = r[blk * MOBA_BLOCK:(blk + 1) * MOBA_BLOCK]
            kmean_ref[0, 0, blk:blk + 1, j * LANES:(j + 1) * LANES] = (
                jnp.mean(part, axis=0, keepdims=True))


def _inproj(x, g, tab, w_a, w_b, layer, tm):
    B, S, D = x.shape
    nm = S // tm
    row = lambda width: pl.BlockSpec((1, tm, width), lambda b, m: (b, m, 0))
    shp = lambda width, dt: jax.ShapeDtypeStruct((B, S, width), dt)
    widths = [(512, BF16)] * 4 + [(LANES, BF16)] * 4 + [(LANES, F32)] + [(512, BF16)] * 3 + [(3072, BF16)]
    out_specs = [row(w) for w, _ in widths]
    out_shape = [shp(w, dt) for w, dt in widths]
    out_specs.append(pl.BlockSpec((1, 1, tm // MOBA_BLOCK, 512), lambda b, m: (b, m, 0, 0)))
    out_shape.append(jax.ShapeDtypeStruct((B, nm, tm // MOBA_BLOCK, 512), F32))
    return pl.pallas_call(
        _inproj_kernel,
        grid=(B, nm),
        in_specs=[row(D),
                  _resident((1, D), lambda b, m: (0, 0)),
                  row(4 * LANES),
                  _resident((None, D, A_COLS), lambda b, m: (layer, 0, 0)),
                  _resident((None, D, B_COLS), lambda b, m: (layer, 0, 0))],
        out_specs=out_specs,
        out_shape=out_shape,
        compiler_params=_cparams(("parallel", "parallel")),
        name="inproj",
    )(x, g.reshape(1, D), tab, w_a, w_b)


def _split_heads_rows(q):
    lo = lax.broadcasted_iota(jnp.int32, (1, LANES), 1) < HEAD_DIM
    zero = jnp.zeros_like(q)
    return jnp.concatenate([jnp.where(lo, q, zero), jnp.where(lo, zero, q)], axis=0)


def _online_update(s, v, m_ref, l_ref, acc_ref, bias=None, row0=0):
    rows_total = s.shape[0]
    nblk = s.shape[1] // LANES
    p_rows, alphas = [], []
    for r0 in range(0, rows_total, ROW_BLOCK):
        rows = slice(row0 + r0, row0 + r0 + ROW_BLOCK)
        cols = [s[r0:r0 + ROW_BLOCK, j * LANES:(j + 1) * LANES] for j in range(nblk)]
        if bias is not None:
            b0 = r0 % bias[0].shape[0]
            cols = [c + bj[b0:b0 + ROW_BLOCK] for c, bj in zip(cols, bias)]
        m_prev = m_ref[rows]
        m_new = jnp.maximum(
            m_prev, jnp.max(functools.reduce(jnp.maximum, cols), axis=1, keepdims=True))
        alpha = jnp.exp2(m_prev - m_new)
        ps = [jnp.exp2(c - m_new) for c in cols]
        l_ref[rows] = alpha * l_ref[rows] + functools.reduce(jnp.add, ps)
        m_ref[rows] = m_new
        p_rows.append(jnp.concatenate([pj.astype(BF16) for pj in ps], axis=1))
        alphas.append(alpha)
    p = jnp.concatenate(p_rows, axis=0)
    alpha = jnp.concatenate(alphas, axis=0)
    span = slice(row0, row0 + rows_total)
    acc_ref[span] = alpha * acc_ref[span] + jnp.dot(p, v, preferred_element_type=F32)


def _normalised(l_ref, acc_ref):
    return acc_ref[...] / jnp.sum(l_ref[...], axis=1, keepdims=True)


DIFF_SUB = 512
DIFF_BULK_KEYS = 1024


def _diff_attn_kernel(q_ref, k_ref, v_ref, lam_ref, g_ref, o_ref, m_ref, l_ref, acc_ref,
                      *, t, lam_init):
    qi = pl.program_id(2)
    sub = DIFF_SUB
    nsb = t // sub
    grp = 2 * sub
    q = q_ref[0]
    q2 = jnp.concatenate(
        [_split_heads_rows(q[j * sub:(j + 1) * sub]) for j in range(nsb)], axis=0)
    m_ref[...] = jnp.full(m_ref.shape, NEG_BIG, F32)
    l_ref[...] = jnp.zeros(l_ref.shape, F32)
    acc_ref[...] = jnp.zeros(acc_ref.shape, F32)

    def body(c, carry):
        off = pl.multiple_of(c * DIFF_BULK_KEYS, DIFF_BULK_KEYS)
        _online_update(_nt_dot(q2, k_ref[0, pl.ds(off, DIFF_BULK_KEYS), :]),
                       v_ref[0, pl.ds(off, DIFF_BULK_KEYS), :], m_ref, l_ref, acc_ref)
        return carry

    lax.fori_loop(0, qi * (t // DIFF_BULK_KEYS), body, 0)

    r = lax.broadcasted_iota(jnp.int32, (grp, 1), 0)
    causal = lax.broadcasted_iota(jnp.int32, (1, sub), 1) <= jnp.where(r >= sub, r - sub, r)
    for j in range(nsb):
        off = pl.multiple_of(qi * t + j * sub, sub)
        s = _nt_dot(q2[j * grp:], k_ref[0, pl.ds(off, sub), :])
        own = jnp.where(causal, s[:grp], NEG_BIG)
        s = own if j + 1 == nsb else jnp.concatenate([own, s[grp:]], axis=0)
        _online_update(s, v_ref[0, pl.ds(off, sub), :], m_ref, l_ref, acc_ref, row0=j * grp)

    lam_rows = lam_ref[...]
    e1 = jnp.exp(jnp.sum(lam_rows[0:1] * lam_rows[1:2], axis=1, keepdims=True))
    e2 = jnp.exp(jnp.sum(lam_rows[2:3] * lam_rows[3:4], axis=1, keepdims=True))
    lam = e1 - e2 + lam_init
    o = _normalised(l_ref, acc_ref)
    for j in range(nsb):
        out = o[j * grp:j * grp + sub] - lam * o[j * grp + sub:(j + 1) * grp]
        o_ref[0, j * sub:(j + 1) * sub] = (
            _rms(out, g_ref[...]) * (1.0 - lam_init)).astype(o_ref.dtype)


def _diff_attn(aq, ak, av, lam_rows, subln_g, lam_init, t):
    B, S, _ = aq.shape
    nh = aq.shape[2] // LANES
    kern = functools.partial(_diff_attn_kernel, t=t, lam_init=lam_init)
    return pl.pallas_call(
        kern,
        grid=(B, nh, S // t),
        in_specs=[pl.BlockSpec((1, t, LANES), lambda b, h, i: (b, i, h)),
                  pl.BlockSpec((1, S, LANES), lambda b, h, i: (b, 0, h)),
                  pl.BlockSpec((1, S, LANES), lambda b, h, i: (b, 0, h)),
                  pl.BlockSpec((4, HEAD_DIM), lambda b, h, i: (0, 0)),
                  pl.BlockSpec((1, LANES), lambda b, h, i: (0, 0))],
        out_specs=pl.BlockSpec((1, t, LANES), lambda b, h, i: (b, i, h)),
        out_shape=jax.ShapeDtypeStruct(aq.shape, BF16),
        scratch_shapes=[pltpu.VMEM((2 * t, LANES), F32), pltpu.VMEM((2 * t, LANES), F32),
                        pltpu.VMEM((2 * t, LANES), F32)],
        compiler_params=_cparams(("parallel", "parallel", "arbitrary")),
        name="diff_attn",
    )(aq, ak, av, lam_rows, subln_g.reshape(1, LANES))


BIT_GROUP = 256


def _bit_transpose32(words):
    w = list(words)
    for j, m in ((16, 0x0000FFFF), (8, 0x00FF00FF), (4, 0x0F0F0F0F), (2, 0x33333333), (1, 0x55555555)):
        for k in range(32):
            if k & j == 0:
                t = (w[k] ^ (w[k + j] >> j)) & m
                w[k] = w[k] ^ t
                w[k + j] = w[k + j] ^ (t << j)
    return w


def _dsa_kernel(q_ref, iq_ref, iw_ref, iqn_ref, iwn_ref, ik4_ref, k2_ref, v2_ref, o_ref,
                keys_ref, planes_ref, act_ref, jcut_ref, m_ref, l_ref, acc_ref,
                *, tq, tk, topk, idx_bits):
    qi = pl.program_id(1)
    nxt = jnp.minimum(qi + 1, pl.num_programs(1) - 1)
    slot = qi % 2
    chunks = lambda tile: (tile * tq + tq - 1) // tk + 1
    nkv = chunks(qi)
    lane = lax.broadcasted_iota(jnp.int32, (1, LANES), 1)
    row = lax.broadcasted_iota(jnp.int32, (tq, 1), 0)
    qpos = qi * tq + row
    lane_tk = lax.broadcasted_iota(jnp.int32, (1, tk), 1)
    nsub = tk // LANES
    seq = keys_ref.shape[2]

    def indexer_rows(iq_blk, iw_blk):
        iq = iq_blk[0]
        zero = jnp.zeros_like(iq)
        iq4 = jnp.concatenate(
            [jnp.where((lane >> (IDX_DIM.bit_length() - 1)) == hh, iq, zero)
             for hh in range(IDX_HEADS)], axis=0)
        iw = iw_blk[0]
        return iq4, [iw[:, hh:hh + 1] for hh in range(IDX_HEADS)]

    def score_chunk(c, tile, dst, iq4, iw_cols):
        off = pl.multiple_of(c * tk, tk)
        rel = jnp.maximum(_nt_dot(iq4, ik4_ref[0, pl.ds(off, tk), :]), 0.0)
        score = iw_cols[0] * rel[0:tq]
        for hh in range(1, IDX_HEADS):
            score = score + iw_cols[hh] * rel[hh * tq:(hh + 1) * tq]
        bits = lax.bitcast_convert_type(score, jnp.int32)
        key = bits ^ ((bits >> 31) & jnp.int32(0x7FFFFFFF))
        kpos = off + lane_tk
        key = jnp.where(score == 0.0, (seq - 1) - kpos, jnp.where(key > 0, key + seq, key))
        key = jnp.where(kpos <= tile * tq + row, key, jnp.int32(INT_MIN))
        keys_ref[dst, :, pl.ds(off, tk)] = key
        kts = [key[:, j * LANES:(j + 1) * LANES].T for j in range(nsub)]
        for gl in range(tk // BIT_GROUP):
            words = [kts[(gl * BIT_GROUP + 8 * g) // LANES][(8 * g) % LANES:(8 * g) % LANES + 8]
                     for g in range(32)]
            words = _bit_transpose32(words)
            words[0] = ~words[0]
            for b in range(32):
                planes_ref[c * (tk // BIT_GROUP) + gl, b] = words[b]

    @pl.when(qi == 0)
    def _():
        planes_ref[...] = jnp.zeros(planes_ref.shape, jnp.int32)
        iq4, iw_cols = indexer_rows(iq_ref, iw_ref)
        for c in range(chunks(0)):
            score_chunk(c, 0, 0, iq4, iw_cols)

    ngroups = planes_ref.shape[0]
    live = nkv * (tk // BIT_GROUP)
    for g in range(ngroups):
        act_ref[g] = jnp.broadcast_to(jnp.where(g < live, jnp.int32(-1), jnp.int32(0)), (8, tq))

    def sweep(i, keep, first=False):
        parts = [jnp.zeros((8, tq), jnp.int32) for _ in range(4)]
        for g in range(ngroups):
            act = act_ref[g]
            if not first:
                act = act & (planes_ref[g, i - 1] ^ keep)
                act_ref[g] = act
            parts[g % 4] = parts[g % 4] + lax.population_count(act & planes_ref[g, i])
        cnt = (parts[0] + parts[1]) + (parts[2] + parts[3])
        return jnp.sum(cnt.astype(F32), axis=0, keepdims=True)

    def decide(i, ones, thr_u, n_gt):
        take = n_gt + ones >= topk
        thr_u = thr_u | jnp.where(take, jnp.left_shift(jnp.int32(1), 31 - i), 0)
        return thr_u, jnp.where(take, n_gt, n_gt + ones), jnp.where(take, 0, -1).astype(jnp.int32)

    def pass_body(i, carry):
        thr_u, n_gt, keep = carry
        return decide(i, sweep(i, keep), thr_u, n_gt)

    start = decide(0, sweep(0, None, first=True),
                   jnp.zeros((1, tq), jnp.int32), jnp.zeros((1, tq), F32))
    thr_u, n_gt, keep = lax.fori_loop(1, 32, pass_body, start)
    n_eq = jnp.zeros((8, tq), jnp.int32)
    for g in range(ngroups):
        n_eq = n_eq + lax.population_count(act_ref[g] & (planes_ref[g, 31] ^ keep))
    thr = thr_u ^ jnp.int32(INT_MIN)
    cnt_ge = n_gt + jnp.sum(n_eq.astype(F32), axis=0, keepdims=True)
    need = topk - n_gt

    def as_rows(x):
        return jnp.broadcast_to(x, (LANES, tq)).T

    def key_block(off, j):
        return keys_ref[slot, :, pl.ds(pl.multiple_of(off + j * LANES, LANES), LANES)]

    thr_r = as_rows(thr)
    need_r = as_rows(need)
    jcut_ref[...] = jnp.full((tq, LANES), 2 ** 30, jnp.int32)

    @pl.when(jnp.max(cnt_ge) > topk)
    def _():
        def ties_before(cand):
            def body(c, cnt):
                off = pl.multiple_of(c * tk, tk)
                for j in range(nsub):
                    hit = (key_block(off, j) == thr_r) & (off + j * LANES + lane < cand)
                    cnt = jnp.where(hit, cnt + 1, cnt)
                return cnt
            cnt = lax.fori_loop(0, nkv, body, jnp.zeros((tq, LANES), jnp.int32))
            return jnp.sum(cnt.astype(F32), axis=1, keepdims=True)

        def idx_body(i, j):
            cand = j + jnp.left_shift(jnp.int32(1), idx_bits - 1 - i)
            return jnp.where(ties_before(cand) < need_r, cand, j)
        jcut_ref[...] = lax.fori_loop(0, idx_bits, idx_body, jnp.zeros((tq, LANES), jnp.int32))

    jcut_r = jcut_ref[...]

    q = q_ref[0]
    q8 = jnp.concatenate(
        [_split_heads_rows(q[:, p * LANES:(p + 1) * LANES]) for p in range(4)], axis=0)
    m_ref[...] = jnp.full(m_ref.shape, NEG_BIG, F32)
    l_ref[...] = jnp.zeros(l_ref.shape, F32)
    acc_ref[...] = jnp.zeros(acc_ref.shape, F32)
    iq4n, iwn_cols = indexer_rows(iqn_ref, iwn_ref)

    def prepare_next(c):
        score_chunk(c, nxt, 1 - slot, iq4n, iwn_cols)

    def attend(off, width):
        s = _nt_dot(q8, k2_ref[0, pl.ds(off, width), :])
        bias = []
        for j in range(width // LANES):
            keyj = key_block(off, j)
            kpos = off + j * LANES + lane
            sel = ((keyj > thr_r) | ((keyj == thr_r) & (kpos <= jcut_r))) & (kpos <= qpos)
            bias.append(jnp.where(sel, 0.0, NEG_BIG))
        _online_update(s, v2_ref[0, pl.ds(off, width), :], m_ref, l_ref, acc_ref, bias=bias)

    def attn_body(c, carry):
        attend(pl.multiple_of(c * 2 * tk, 2 * tk), 2 * tk)
        prepare_next(2 * c)
        prepare_next(2 * c + 1)
        return carry

    lax.fori_loop(0, nkv // 2, attn_body, 0)

    @pl.when(nkv % 2 == 1)
    def _():
        attend(pl.multiple_of((nkv - 1) * tk, tk), tk)
        prepare_next(nkv - 1)

    @pl.when(chunks(nxt) > nkv)
    def _():
        prepare_next(nkv)

    o = _normalised(l_ref, acc_ref)
    lo = lane < HEAD_DIM
    for p in range(4):
        o_ref[0, :, p * LANES:(p + 1) * LANES] = jnp.where(
            lo, o[(2 * p) * tq:(2 * p + 1) * tq], o[(2 * p + 1) * tq:(2 * p + 2) * tq]
        ).astype(o_ref.dtype)


def _dsa_attn(bq, iq, iw, ik4, k2, v2, tq, tk):
    B, S, C = bq.shape
    assert tk % tq == 0 and tk % BIT_GROUP == 0 and S % tk == 0
    topk = min(IDX_TOPK_MAX, S // 4)
    idx_bits = max(1, (S - 1).bit_length())
    kern = functools.partial(_dsa_kernel, tq=tq, tk=tk, topk=topk, idx_bits=idx_bits)
    nq = S // tq
    row = lambda width: pl.BlockSpec((1, tq, width), lambda b, i: (b, i, 0))
    nxt = lambda width: pl.BlockSpec((1, tq, width), lambda b, i: (b, jnp.minimum(i + 1, nq - 1), 0))
    full = _resident((1, S, LANES), lambda b, i: (b, 0, 0))
    return pl.pallas_call(
        kern,
        grid=(B, nq),
        in_specs=[row(C), row(LANES), row(LANES), nxt(LANES), nxt(LANES), full, full, full],
        out_specs=row(C),
        out_shape=jax.ShapeDtypeStruct(bq.shape, BF16),
        scratch_shapes=[pltpu.VMEM((2, tq, S), jnp.int32),
                        pltpu.VMEM((S // BIT_GROUP, 32, 8, tq), jnp.int32),
                        pltpu.VMEM((S // BIT_GROUP, 8, tq), jnp.int32),
                        pltpu.VMEM((tq, LANES), jnp.int32),
                        pltpu.VMEM((8 * tq, LANES), F32), pltpu.VMEM((8 * tq, LANES), F32),
                        pltpu.VMEM((8 * tq, LANES), F32)],
        compiler_params=_cparams(("parallel", "arbitrary")),
        name="dsa_attn",
    )(bq, iq, iw, iq, iw, ik4, k2, v2)


MOBA_TILE_BLOCKS = 8
MOBA_BULK_BLOCKS = 4


def _moba_kernel(q_ref, k_ref, v_ref, km_ref, o_ref, m_ref, l_ref, acc_ref, *, nb):
    nqb = MOBA_TILE_BLOCKS
    blk_rows = MOBA_BLOCK
    grp = 2 * blk_rows
    i = pl.program_id(2)
    first = nqb * i
    lane = lax.broadcasted_iota(jnp.int32, (1, LANES), 1)
    q = q_ref[0]
    q2 = jnp.concatenate(
        [_split_heads_rows(q[j * blk_rows:(j + 1) * blk_rows]) for j in range(nqb)], axis=0)

    km = jnp.concatenate([km_ref[0], jnp.zeros((LANES - nb, LANES), F32)], axis=0)
    km_hi = km.astype(BF16)
    r1 = km - km_hi.astype(F32)
    km_mid = r1.astype(BF16)
    km_lo = (r1 - km_mid.astype(F32)).astype(BF16)
    nb8 = -(-nb // 8) * 8
    gate_t = (_nt_dot(km_hi, q2) + _nt_dot(km_mid, q2) + _nt_dot(km_lo, q2))[:nb8]
    blk = lax.broadcasted_iota(jnp.int32, (nb8, 1), 0)
    blk_f = blk.astype(F32)
    col = lax.broadcasted_iota(jnp.int32, (1, nqb * grp), 1)
    past = blk < first + (col >> (grp.bit_length() - 1))
    g = jnp.where(past, gate_t, -jnp.inf)
    sel = jnp.zeros(g.shape, jnp.bool_)
    for _ in range(min(MOBA_TOPK, nb - 1)):
        mx = jnp.max(g, axis=0, keepdims=True)
        pick = blk_f == jnp.min(jnp.where(g == mx, blk_f, float(LANES)), axis=0, keepdims=True)
        sel = sel | pick
        g = jnp.where(pick, -jnp.inf, g)
    bias_t = jnp.concatenate([jnp.where(sel & past, 0.0, NEG_BIG),
                              jnp.full((LANES - nb8, nqb * grp), NEG_BIG, F32)], axis=0)
    qa = jnp.concatenate([q2, bias_t.T.astype(BF16)], axis=1)

    m_ref[...] = jnp.full(m_ref.shape, NEG_BIG, F32)
    l_ref[...] = jnp.zeros(l_ref.shape, F32)
    acc_ref[...] = jnp.zeros(acc_ref.shape, F32)

    def biased_logits(lhs, first_blk, nblk):
        n = nblk * blk_rows
        off = pl.multiple_of(first_blk * blk_rows, blk_rows)
        row_blk = lax.broadcasted_iota(jnp.int32, (n, 1), 0) >> (MOBA_BLOCK.bit_length() - 1)
        onehot = jnp.where(lane == first_blk + row_blk, 1.0, 0.0).astype(BF16)
        ka = jnp.concatenate([k_ref[0, pl.ds(off, n), :], onehot], axis=1)
        return _nt_dot(lhs, ka), v_ref[0, pl.ds(off, n), :]

    def body(c, carry):
        s, v = biased_logits(qa, MOBA_BULK_BLOCKS * c, MOBA_BULK_BLOCKS)
        _online_update(s, v, m_ref, l_ref, acc_ref)
        return carry

    lax.fori_loop(0, first // MOBA_BULK_BLOCKS, body, 0)

    r = lax.broadcasted_iota(jnp.int32, (grp, 1), 0)
    causal = (lax.broadcasted_iota(jnp.int32, (1, blk_rows), 1)
              <= jnp.where(r >= blk_rows, r - blk_rows, r))
    for j in range(nqb):
        off = pl.multiple_of((first + j) * blk_rows, blk_rows)
        s = jnp.where(causal, _nt_dot(q2[j * grp:(j + 1) * grp], k_ref[0, pl.ds(off, blk_rows), :]),
                      NEG_BIG)
        if j + 1 < nqb:
            s_later, _ = biased_logits(qa[(j + 1) * grp:], first + j, 1)
            s = jnp.concatenate([s, s_later], axis=0)
        _online_update(s, v_ref[0, pl.ds(off, blk_rows), :], m_ref, l_ref, acc_ref, row0=j * grp)

    o = _normalised(l_ref, acc_ref)
    lo = lane < HEAD_DIM
    for j in range(nqb):
        o_ref[0, j * blk_rows:(j + 1) * blk_rows] = jnp.where(
            lo, o[j * grp:j * grp + blk_rows], o[j * grp + blk_rows:(j + 1) * grp]).astype(o_ref.dtype)


def _moba_attn(cq, ck, cv, kmean):
    B, S, C = cq.shape
    nb = S // MOBA_BLOCK
    tq = MOBA_TILE_BLOCKS * MOBA_BLOCK
    return pl.pallas_call(
        functools.partial(_moba_kernel, nb=nb),
        grid=(B, C // LANES, S // tq),
        in_specs=[pl.BlockSpec((1, tq, LANES), lambda b, p, i: (b, i, p)),
                  pl.BlockSpec((1, S, LANES), lambda b, p, i: (b, 0, p)),
                  pl.BlockSpec((1, S, LANES), lambda b, p, i: (b, 0, p)),
                  pl.BlockSpec((1, nb, LANES), lambda b, p, i: (b, 0, p))],
        out_specs=pl.BlockSpec((1, tq, LANES), lambda b, p, i: (b, i, p)),
        out_shape=jax.ShapeDtypeStruct(cq.shape, BF16),
        scratch_shapes=[pltpu.VMEM((2 * tq, LANES), F32), pltpu.VMEM((2 * tq, LANES), F32),
                        pltpu.VMEM((2 * tq, LANES), F32)],
        compiler_params=_cparams(("parallel", "parallel", "arbitrary")),
        name="moba_attn",
    )(cq, ck, cv, kmean)


def _sigmoid(x):
    return 1.0 / (1.0 + jnp.exp(-x))


def _merge_kernel(ya_ref, yb_ref, yc_ref, gates_ref, x_ref, wb_ref, wo_ref, g_ref, o_ref):
    d = x_ref.shape[2]
    merged = None
    for n, y_ref in enumerate((ya_ref, yb_ref, yc_ref)):
        br = jnp.dot(y_ref[0], wb_ref[n], preferred_element_type=F32)
        term = _sigmoid(gates_ref[0, :, n * d:(n + 1) * d].astype(F32)) * br
        merged = term if merged is None else merged + term
    m = jnp.dot(merged.astype(BF16), wo_ref[...], preferred_element_type=F32)
    o_ref[0] = x_ref[0] + _rms(m, g_ref[...])


def _merge(ya, yb, yc, gates, x, wb, wo, layer, g, tm):
    B, S, D = x.shape
    row = lambda width: pl.BlockSpec((1, tm, width), lambda b, m: (b, m, 0))
    return pl.pallas_call(
        _merge_kernel,
        grid=(B, S // tm),
        in_specs=[row(BRANCH_WIDTH), row(BRANCH_WIDTH), row(BRANCH_WIDTH), row(N_BRANCH * D), row(D),
                  _resident((None,) + wb.shape[1:], lambda b, m: (layer, 0, 0, 0)),
                  _resident((None,) + wo.shape[1:], lambda b, m: (layer, 0, 0)),
                  _resident((1, D), lambda b, m: (0, 0))],
        out_specs=row(D),
        out_shape=jax.ShapeDtypeStruct(x.shape, F32),
        compiler_params=_cparams(("parallel", "parallel")),
        name="merge_out",
    )(ya, yb, yc, gates, x, wb, wo, g.reshape(1, D))


def _ffn_kernel(x_ref, g_in_ref, wi_ref, wo_ref, g_out_ref, o_ref, acc_ref, *, d_ff, tf):
    x = x_ref[0]
    h = _rms(x, g_in_ref[...]).astype(BF16)
    acc_ref[...] = jnp.zeros(acc_ref.shape, F32)

    def body(c, carry):
        off = pl.multiple_of(c * tf, tf)
        gt = jnp.dot(h, wi_ref[:, pl.ds(off, tf)], preferred_element_type=F32)
        up_off = pl.multiple_of(d_ff + off, LANES)
        up = jnp.dot(h, wi_ref[:, pl.ds(up_off, tf)], preferred_element_type=F32)
        act = (gt * _sigmoid(gt) * up).astype(BF16)
        acc_ref[...] += jnp.dot(act, wo_ref[pl.ds(off, tf), :], preferred_element_type=F32)
        return carry

    lax.fori_loop(0, d_ff // tf, body, 0)
    o_ref[0] = x + _rms(acc_ref[...], g_out_ref[...])


def _ffn(x, g_in, wi, wo, layer, g_out, tm, tf):
    B, S, D = x.shape
    d_ff = wo.shape[1]
    row = pl.BlockSpec((1, tm, D), lambda b, m: (b, m, 0))
    vec = _resident((1, D), lambda b, m: (0, 0))
    return pl.pallas_call(
        functools.partial(_ffn_kernel, d_ff=d_ff, tf=tf),
        grid=(B, S // tm),
        in_specs=[row, vec, _resident((None,) + wi.shape[1:], lambda b, m: (layer, 0, 0)),
                  _resident((None,) + wo.shape[1:], lambda b, m: (layer, 0, 0)), vec],
        out_specs=row,
        out_shape=jax.ShapeDtypeStruct(x.shape, F32),
        scratch_shapes=[pltpu.VMEM((tm, D), F32)],
        compiler_params=_cparams(("parallel", "parallel")),
        name="swiglu_ffn",
    )(x, g_in.reshape(1, D), wi, wo, g_out.reshape(1, D))


class _Tiles(NamedTuple):
    proj_rows: int
    dense_rows: int
    ffn_cols: int
    diff_tile: int
    dsa_queries: int
    dsa_keys: int


def _tiles(seq):
    return _Tiles(proj_rows=min(512, seq), dense_rows=min(1024, seq), ffn_cols=256,
                  diff_tile=min(2048, seq), dsa_queries=min(256, seq), dsa_keys=min(512, seq))


def kernel(x, positions, w_in, w_branch, w_out, lambda_q1, lambda_k1, lambda_q2, lambda_k2,
           subln_g, norm_g, w_ffn_in, w_ffn_out):
    B, S, D = x.shape
    depth = w_in.shape[0]
    tiles = _tiles(S)
    assert S % (MOBA_TILE_BLOCKS * MOBA_BLOCK) == 0 and S % tiles.diff_tile == 0
    tab = _rope_table(positions, tiles.proj_rows)
    assert w_in.shape[2] == B_START + B_COLS
    w_bf = w_in.astype(BF16)
    w_a, w_b = w_bf[..., :A_COLS], w_bf[..., B_START:]
    wb, wo = w_branch.astype(BF16), w_out.astype(BF16)
    wi_ffn, wo_ffn = w_ffn_in.astype(BF16), w_ffn_out.astype(BF16)
    for l in range(depth):
        lam_init = 0.8 - 0.6 * math.exp(-0.3 * l)
        (aq, ak, av, bq, k2, v2, iq, ik4, iw, cq, ck, cv, gates, kmean) = _inproj(
            x, norm_g[l, 0], tab, w_a, w_b, l, tiles.proj_rows)
        kmean = kmean.reshape(B, S // MOBA_BLOCK, 512)
        lam_rows = jnp.stack([lambda_q1[l], lambda_k1[l], lambda_q2[l], lambda_k2[l]])
        ya = _diff_attn(aq, ak, av, lam_rows, subln_g[l], lam_init, t=tiles.diff_tile)
        yb = _dsa_attn(bq, iq, iw, ik4, k2, v2, tq=tiles.dsa_queries, tk=tiles.dsa_keys)
        yc = _moba_attn(cq, ck, cv, kmean)
        x = _merge(ya, yb, yc, gates, x, wb, wo, l, norm_g[l, 1], tiles.dense_rows)
        x = _ffn(x, norm_g[l, 2], wi_ffn, wo_ffn, l, norm_g[l, 3], tiles.dense_rows,
                 tf=tiles.ffn_cols)
    return x
```

```python
import functools
import math
from typing import NamedTuple

import jax
import jax.numpy as jnp
from jax import lax
from jax.experimental import pallas as pl
from jax.experimental.pallas import tpu as pltpu

F32 = jnp.float32
BF16 = jnp.bfloat16

LANES = 128
HEAD_DIM = 64
ROPE_THETA = 500000.0
NORM_EPS = 1e-6
IDX_HEADS = 4
IDX_DIM = 32
IDX_TOPK_MAX = 256
MOBA_BLOCK = 256
MOBA_TOPK = 3
N_BRANCH = 3
BRANCH_WIDTH = 512
NEG_BIG = -1e30
LOG2E = math.log2(math.e)
ROW_BLOCK = 64
INT_MIN = -2 ** 31
VMEM_LIMIT = 56 * 1024 * 1024

A_AQ, A_AK, A_AV, A_BQ, A_KV, A_IQ, A_IKW, A_COLS = 0, 512, 1024, 1536, 2048, 2176, 2304, 2432
B_START = 2340
B_CQ, B_CK, B_CV, B_G, B_COLS = 0, 512, 1024, 1536, 4608
Q_SCALE = HEAD_DIM ** -0.5 * LOG2E


def _nt_dot(a, b):
    return lax.dot_general(a, b, (((1,), (1,)), ((), ())), preferred_element_type=F32)


def _rms(x, g):
    return x * lax.rsqrt(jnp.mean(x * x, axis=-1, keepdims=True) + NORM_EPS) * g


def _cparams(sem):
    return pltpu.CompilerParams(dimension_semantics=sem, vmem_limit_bytes=VMEM_LIMIT)


def _resident(shape, index_map):
    return pl.BlockSpec(shape, index_map, pipeline_mode=pl.Buffered(1))


def _rope_table_kernel(pos_ref, tab_ref):
    pos = pos_ref[0].astype(F32)
    lane = lax.broadcasted_iota(jnp.int32, (1, LANES), 1)

    def tables(head_dim):
        rot = head_dim // 4
        half = rot // 2
        d = lane & (head_dim - 1)
        fi = d & (half - 1)
        inv = jnp.zeros((1, LANES), F32)
        for i in range(half):
            inv = jnp.where(fi == i, F32(ROPE_THETA ** (-(2.0 * i) / rot)), inv)
        ang = pos * inv
        cos = jnp.cos(ang)
        sin = jnp.sin(ang)
        c = jnp.where(d < rot, cos, 1.0)
        s = jnp.where(d < half, -sin, jnp.where(d < rot, sin, 0.0))
        return c, s

    c64, s64 = tables(HEAD_DIM)
    c32, s32 = tables(IDX_DIM)
    tab_ref[0, :, 0 * LANES:1 * LANES] = c64
    tab_ref[0, :, 1 * LANES:2 * LANES] = s64
    tab_ref[0, :, 2 * LANES:3 * LANES] = c32
    tab_ref[0, :, 3 * LANES:4 * LANES] = s32


def _rope_table(positions, tm):
    B, S = positions.shape
    return pl.pallas_call(
        _rope_table_kernel,
        grid=(B, S // tm),
        in_specs=[pl.BlockSpec((1, tm, 1), lambda b, m: (b, m, 0))],
        out_specs=pl.BlockSpec((1, tm, 4 * LANES), lambda b, m: (b, m, 0)),
        out_shape=jax.ShapeDtypeStruct((B, S, 4 * LANES), F32),
        compiler_params=_cparams(("parallel", "parallel")),
        name="rope_table",
    )(positions.reshape(B, S, 1))


def _inproj_kernel(x_ref, g_ref, tab_ref, wa_ref, wb_ref,
                   aq_ref, ak_ref, av_ref, bq_ref, k2_ref, v2_ref, iq_ref, ik4_ref, iw_ref,
                   cq_ref, ck_ref, cv_ref, gates_ref, kmean_ref):
    h = _rms(x_ref[0], g_ref[...]).astype(BF16)
    lane = lax.broadcasted_iota(jnp.int32, (1, LANES), 1)

    def proj(w_ref, off, width):
        return jnp.dot(h, w_ref[:, off:off + width], preferred_element_type=F32)

    def rope(y, head_dim):
        half = head_dim // 8
        t0 = 0 if head_dim == HEAD_DIM else 2 * LANES
        c = tab_ref[0, :, t0:t0 + LANES]
        s = tab_ref[0, :, t0 + LANES:t0 + 2 * LANES]
        first = (lane & (head_dim - 1)) < half
        blocks = []
        for j in range(y.shape[1] // LANES):
            yj = y[:, j * LANES:(j + 1) * LANES]
            up = pltpu.roll(yj, LANES - half, 1)
            dn = pltpu.roll(yj, half, 1)
            blocks.append(yj * c + jnp.where(first, up, dn) * s)
        return blocks

    def store(out_ref, blocks):
        for j, r in enumerate(blocks):
            out_ref[0, :, j * LANES:(j + 1) * LANES] = r.astype(out_ref.dtype)

    store(aq_ref, rope(proj(wa_ref, A_AQ, 512) * Q_SCALE, HEAD_DIM))
    store(ak_ref, rope(proj(wa_ref, A_AK, 512), HEAD_DIM))
    av_ref[0] = proj(wa_ref, A_AV, 512).astype(av_ref.dtype)
    store(bq_ref, rope(proj(wa_ref, A_BQ, 512) * Q_SCALE, HEAD_DIM))
    kv = proj(wa_ref, A_KV, LANES)
    vk = pltpu.roll(kv, HEAD_DIM, 1)
    lo = lane < HEAD_DIM
    store(k2_ref, rope(jnp.where(lo, kv, vk), HEAD_DIM))
    v2_ref[0] = jnp.where(lo, vk, kv).astype(v2_ref.dtype)
    store(iq_ref, rope(proj(wa_ref, A_IQ, LANES), IDX_DIM))
    ikw = proj(wa_ref, A_IKW, LANES)
    ik = rope(ikw, IDX_DIM)[0]
    ik4 = ik
    for rep in range(1, LANES // IDX_DIM):
        ik4 = jnp.where(lane < rep * IDX_DIM, ik4, pltpu.roll(ik, rep * IDX_DIM, 1))
    ik4_ref[0] = ik4.astype(ik4_ref.dtype)
    iw_ref[0] = jnp.where(lane < IDX_HEADS, pltpu.roll(ikw, LANES - IDX_DIM, 1), 0.0)
    store(cq_ref, rope(proj(wb_ref, B_CQ, 512) * Q_SCALE, HEAD_DIM))
    ck_blocks = rope(proj(wb_ref, B_CK, 512), HEAD_DIM)
    store(ck_ref, ck_blocks)
    cv_ref[0] = proj(wb_ref, B_CV, 512).astype(cv_ref.dtype)
    gates_ref[0] = proj(wb_ref, B_G, 3 * 1024).astype(gates_ref.dtype)

    tm = x_ref.shape[1]
    for blk in range(tm // MOBA_BLOCK):
        for j, r in enumerate(ck_blocks):
            part = r[blk * MOBA_BLOCK:(blk + 1) * MOBA_BLOCK]
            kmean_ref[0, 0, blk:blk + 1, j * LANES:(j + 1) * LANES] = (
                jnp.mean(part, axis=0, keepdims=True))


def _inproj(x, g, tab, w_a, w_b, layer, tm):
    B, S, D = x.shape
    nm = S // tm
    row = lambda width: pl.BlockSpec((1, tm, width), lambda b, m: (b, m, 0))
    shp = lambda width, dt: jax.ShapeDtypeStruct((B, S, width), dt)
    widths = [(512, BF16)] * 4 + [(LANES, BF16)] * 4 + [(LANES, F32)] + [(512, BF16)] * 3 + [(3072, BF16)]
    out_specs = [row(w) for w, _ in widths]
    out_shape = [shp(w, dt) for w, dt in widths]
    out_specs.append(pl.BlockSpec((1, 1, tm // MOBA_BLOCK, 512), lambda b, m: (b, m, 0, 0)))
    out_shape.append(jax.ShapeDtypeStruct((B, nm, tm // MOBA_BLOCK, 512), F32))
    return pl.pallas_call(
        _inproj_kernel,
        grid=(B, nm),
        in_specs=[row(D),
                  _resident((1, D), lambda b, m: (0, 0)),
                  row(4 * LANES),
                  _resident((None, D, A_COLS), lambda b, m: (layer, 0, 0)),
                  _resident((None, D, B_COLS), lambda b, m: (layer, 0, 0))],
        out_specs=out_specs,
        out_shape=out_shape,
        compiler_params=_cparams(("parallel", "parallel")),
        name="inproj",
    )(x, g.reshape(1, D), tab, w_a, w_b)


def _split_heads_rows(q):
    lo = lax.broadcasted_iota(jnp.int32, (1, LANES), 1) < HEAD_DIM
    zero = jnp.zeros_like(q)
    return jnp.concatenate([jnp.where(lo, q, zero), jnp.where(lo, zero, q)], axis=0)


def _online_update(s, v, m_ref, l_ref, acc_ref, bias=None, row0=0):
    rows_total = s.shape[0]
    nblk = s.shape[1] // LANES
    p_rows, alphas = [], []
    for r0 in range(0, rows_total, ROW_BLOCK):
        rows = slice(row0 + r0, row0 + r0 + ROW_BLOCK)
        cols = [s[r0:r0 + ROW_BLOCK, j * LANES:(j + 1) * LANES] for j in range(nblk)]
        if bias is not None:
            b0 = r0 % bias[0].shape[0]
            cols = [c + bj[b0:b0 + ROW_BLOCK] for c, bj in zip(cols, bias)]
        m_prev = m_ref[rows]
        m_new = jnp.maximum(
            m_prev, jnp.max(functools.reduce(jnp.maximum, cols), axis=1, keepdims=True))
        alpha = jnp.exp2(m_prev - m_new)
        ps = [jnp.exp2(c - m_new) for c in cols]
        l_ref[rows] = alpha * l_ref[rows] + functools.reduce(jnp.add, ps)
        m_ref[rows] = m_new
        p_rows.append(jnp.concatenate([pj.astype(BF16) for pj in ps], axis=1))
        alphas.append(alpha)
    p = jnp.concatenate(p_rows, axis=0)
    alpha = jnp.concatenate(alphas, axis=0)
    span = slice(row0, row0 + rows_total)
    acc_ref[span] = alpha * acc_ref[span] + jnp.dot(p, v, preferred_element_type=F32)


def _normalised(l_ref, acc_ref):
    return acc_ref[...] / jnp.sum(l_ref[...], axis=1, keepdims=True)


DIFF_SUB = 512
DIFF_BULK_KEYS = 1024


def _diff_attn_kernel(q_ref, k_ref, v_ref, lam_ref, g_ref, o_ref, m_ref, l_ref, acc_ref,
                      *, t, lam_init):
    qi = pl.program_id(2)
    sub = DIFF_SUB
    nsb = t // sub
    grp = 2 * sub
    q = q_ref[0]
    q2 = jnp.concatenate(
        [_split_heads_rows(q[j * sub:(j + 1) * sub]) for j in range(nsb)], axis=0)
    m_ref[...] = jnp.full(m_ref.shape, NEG_BIG, F32)
    l_ref[...] = jnp.zeros(l_ref.shape, F32)
    acc_ref[...] = jnp.zeros(acc_ref.shape, F32)

    def body(c, carry):
        off = pl.multiple_of(c * DIFF_BULK_KEYS, DIFF_BULK_KEYS)
        _online_update(_nt_dot(q2, k_ref[0, pl.ds(off, DIFF_BULK_KEYS), :]),
                       v_ref[0, pl.ds(off, DIFF_BULK_KEYS), :], m_ref, l_ref, acc_ref)
        return carry

    lax.fori_loop(0, qi * (t // DIFF_BULK_KEYS), body, 0)

    r = lax.broadcasted_iota(jnp.int32, (grp, 1), 0)
    causal = lax.broadcasted_iota(jnp.int32, (1, sub), 1) <= jnp.where(r >= sub, r - sub, r)
    for j in range(nsb):
        off = pl.multiple_of(qi * t + j * sub, sub)
        s = _nt_dot(q2[j * grp:], k_ref[0, pl.ds(off, sub), :])
        own = jnp.where(causal, s[:grp], NEG_BIG)
        s = own if j + 1 == nsb else jnp.concatenate([own, s[grp:]], axis=0)
        _online_update(s, v_ref[0, pl.ds(off, sub), :], m_ref, l_ref, acc_ref, row0=j * grp)

    lam_rows = lam_ref[...]
    e1 = jnp.exp(jnp.sum(lam_rows[0:1] * lam_rows[1:2], axis=1, keepdims=True))
    e2 = jnp.exp(jnp.sum(lam_rows[2:3] * lam_rows[3:4], axis=1, keepdims=True))
    lam = e1 - e2 + lam_init
    o = _normalised(l_ref, acc_ref)
    for j in range(nsb):
        out = o[j * grp:j * grp + sub] - lam * o[j * grp + sub:(j + 1) * grp]
        o_ref[0, j * sub:(j + 1) * sub] = (
            _rms(out, g_ref[...]) * (1.0 - lam_init)).astype(o_ref.dtype)


def _diff_attn(aq, ak, av, lam_rows, subln_g, lam_init, t):
    B, S, _ = aq.shape
    nh = aq.shape[2] // LANES
    kern = functools.partial(_diff_attn_kernel, t=t, lam_init=lam_init)
    return pl.pallas_call(
        kern,
        grid=(B, nh, S // t),
        in_specs=[pl.BlockSpec((1, t, LANES), lambda b, h, i: (b, i, h)),
                  pl.BlockSpec((1, S, LANES), lambda b, h, i: (b, 0, h)),
                  pl.BlockSpec((1, S, LANES), lambda b, h, i: (b, 0, h)),
                  pl.BlockSpec((4, HEAD_DIM), lambda b, h, i: (0, 0)),
                  pl.BlockSpec((1, LANES), lambda b, h, i: (0, 0))],
        out_specs=pl.BlockSpec((1, t, LANES), lambda b, h, i: (b, i, h)),
        out_shape=jax.ShapeDtypeStruct(aq.shape, BF16),
        scratch_shapes=[pltpu.VMEM((2 * t, LANES), F32), pltpu.VMEM((2 * t, LANES), F32),
                        pltpu.VMEM((2 * t, LANES), F32)],
        compiler_params=_cparams(("parallel", "parallel", "arbitrary")),
        name="diff_attn",
    )(aq, ak, av, lam_rows, subln_g.reshape(1, LANES))


BIT_GROUP = 256


def _bit_transpose32(words):
    w = list(words)
    for j, m in ((16, 0x0000FFFF), (8, 0x00FF00FF), (4, 0x0F0F0F0F), (2, 0x33333333), (1, 0x55555555)):
        for k in range(32):
            if k & j == 0:
                t = (w[k] ^ (w[k + j] >> j)) & m
                w[k] = w[k] ^ t
                w[k + j] = w[k + j] ^ (t << j)
    return w


def _dsa_kernel(q_ref, iq_ref, iw_ref, iqn_ref, iwn_ref, ik4_ref, k2_ref, v2_ref, o_ref,
                keys_ref, planes_ref, act_ref, jcut_ref, m_ref, l_ref, acc_ref,
                *, tq, tk, topk, idx_bits):
    qi = pl.program_id(1)
    nxt = jnp.minimum(qi + 1, pl.num_programs(1) - 1)
    slot = qi % 2
    chunks = lambda tile: (tile * tq + tq - 1) // tk + 1
    nkv = chunks(qi)
    lane = lax.broadcasted_iota(jnp.int32, (1, LANES), 1)
    row = lax.broadcasted_iota(jnp.int32, (tq, 1), 0)
    qpos = qi * tq + row
    lane_tk = lax.broadcasted_iota(jnp.int32, (1, tk), 1)
    nsub = tk // LANES
    seq = keys_ref.shape[2]

    def indexer_rows(iq_blk, iw_blk):
        iq = iq_blk[0]
        zero = jnp.zeros_like(iq)
        iq4 = jnp.concatenate(
            [jnp.where((lane >> (IDX_DIM.bit_length() - 1)) == hh, iq, zero)
             for hh in range(IDX_HEADS)], axis=0)
        iw = iw_blk[0]
        return iq4, [iw[:, hh:hh + 1] for hh in range(IDX_HEADS)]

    def score_chunk(c, tile, dst, iq4, iw_cols):
        off = pl.multiple_of(c * tk, tk)
        rel = jnp.maximum(_nt_dot(iq4, ik4_ref[0, pl.ds(off, tk), :]), 0.0)
        score = iw_cols[0] * rel[0:tq]
        for hh in range(1, IDX_HEADS):
            score = score + iw_cols[hh] * rel[hh * tq:(hh + 1) * tq]
        bits = lax.bitcast_convert_type(score, jnp.int32)
        key = bits ^ ((bits >> 31) & jnp.int32(0x7FFFFFFF))
        kpos = off + lane_tk
        key = jnp.where(score == 0.0, (seq - 1) - kpos, jnp.where(key > 0, key + seq, key))
        key = jnp.where(kpos <= tile * tq + row, key, jnp.int32(INT_MIN))
        keys_ref[dst, :, pl.ds(off, tk)] = key
        kts = [key[:, j * LANES:(j + 1) * LANES].T for j in range(nsub)]
        for gl in range(tk // BIT_GROUP):
            words = [kts[(gl * BIT_GROUP + 8 * g) // LANES][(8 * g) % LANES:(8 * g) % LANES + 8]
                     for g in range(32)]
            words = _bit_transpose32(words)
            words[0] = ~words[0]
            for b in range(32):
                planes_ref[c * (tk // BIT_GROUP) + gl, b] = words[b]

    @pl.when(qi == 0)
    def _():
        planes_ref[...] = jnp.zeros(planes_ref.shape, jnp.int32)
        iq4, iw_cols = indexer_rows(iq_ref, iw_ref)
        for c in range(chunks(0)):
            score_chunk(c, 0, 0, iq4, iw_cols)

    ngroups = planes_ref.shape[0]
    live = nkv * (tk // BIT_GROUP)
    for g in range(ngroups):
        act_ref[g] = jnp.broadcast_to(jnp.where(g < live, jnp.int32(-1), jnp.int32(0)), (8, tq))

    def sweep(i, keep, first=False):
        parts = [jnp.zeros((8, tq), jnp.int32) for _ in range(4)]
        for g in range(ngroups):
            act = act_ref[g]
            if not first:
                act = act & (planes_ref[g, i - 1] ^ keep)
                act_ref[g] = act
            parts[g % 4] = parts[g % 4] + lax.population_count(act & planes_ref[g, i])
        cnt = (parts[0] + parts[1]) + (parts[2] + parts[3])
        return jnp.sum(cnt.astype(F32), axis=0, keepdims=True)

    def decide(i, ones, thr_u, n_gt):
        take = n_gt + ones >= topk
        thr_u = thr_u | jnp.where(take, jnp.left_shift(jnp.int32(1), 31 - i), 0)
        return thr_u, jnp.where(take, n_gt, n_gt + ones), jnp.where(take, 0, -1).astype(jnp.int32)

    def pass_body(i, carry):
        thr_u, n_gt, keep = carry
        return decide(i, sweep(i, keep), thr_u, n_gt)

    start = decide(0, sweep(0, None, first=True),
                   jnp.zeros((1, tq), jnp.int32), jnp.zeros((1, tq), F32))
    thr_u, n_gt, keep = lax.fori_loop(1, 32, pass_body, start)
    n_eq = jnp.zeros((8, tq), jnp.int32)
    for g in range(ngroups):
        n_eq = n_eq + lax.population_count(act_ref[g] & (planes_ref[g, 31] ^ keep))
    thr = thr_u ^ jnp.int32(INT_MIN)
    cnt_ge = n_gt + jnp.sum(n_eq.astype(F32), axis=0, keepdims=True)
    need = topk - n_gt

    def as_rows(x):
        return jnp.broadcast_to(x, (LANES, tq)).T

    def key_block(off, j):
        return keys_ref[slot, :, pl.ds(pl.multiple_of(off + j * LANES, LANES), LANES)]

    thr_r = as_rows(thr)
    need_r = as_rows(need)
    jcut_ref[...] = jnp.full((tq, LANES), 2 ** 30, jnp.int32)

    @pl.when(jnp.max(cnt_ge) > topk)
    def _():
        def ties_before(cand):
            def body(c, cnt):
                off = pl.multiple_of(c * tk, tk)
                for j in range(nsub):
                    hit = (key_block(off, j) == thr_r) & (off + j * LANES + lane < cand)
                    cnt = jnp.where(hit, cnt + 1, cnt)
                return cnt
            cnt = lax.fori_loop(0, nkv, body, jnp.zeros((tq, LANES), jnp.int32))
            return jnp.sum(cnt.astype(F32), axis=1, keepdims=True)

        def idx_body(i, j):
            cand = j + jnp.left_shift(jnp.int32(1), idx_bits - 1 - i)
            return jnp.where(ties_before(cand) < need_r, cand, j)
        jcut_ref[...] = lax.fori_loop(0, idx_bits, idx_body, jnp.zeros((tq, LANES), jnp.int32))

    jcut_r = jcut_ref[...]

    q = q_ref[0]
    q8 = jnp.concatenate(
        [_split_heads_rows(q[:, p * LANES:(p + 1) * LANES]) for p in range(4)], axis=0)
    m_ref[...] = jnp.full(m_ref.shape, NEG_BIG, F32)
    l_ref[...] = jnp.zeros(l_ref.shape, F32)
    acc_ref[...] = jnp.zeros(acc_ref.shape, F32)
    iq4n, iwn_cols = indexer_rows(iqn_ref, iwn_ref)

    def prepare_next(c):
        score_chunk(c, nxt, 1 - slot, iq4n, iwn_cols)

    def attend(off, width):
        s = _nt_dot(q8, k2_ref[0, pl.ds(off, width), :])
        bias = []
        for j in range(width // LANES):
            keyj = key_block(off, j)
            kpos = off + j * LANES + lane
            sel = ((keyj > thr_r) | ((keyj == thr_r) & (kpos <= jcut_r))) & (kpos <= qpos)
            bias.append(jnp.where(sel, 0.0, NEG_BIG))
        _online_update(s, v2_ref[0, pl.ds(off, width), :], m_ref, l_ref, acc_ref, bias=bias)

    def attn_body(c, carry):
        attend(pl.multiple_of(c * 2 * tk, 2 * tk), 2 * tk)
        prepare_next(2 * c)
        prepare_next(2 * c + 1)
        return carry

    lax.fori_loop(0, nkv // 2, attn_body, 0)

    @pl.when(nkv % 2 == 1)
    def _():
        attend(pl.multiple_of((nkv - 1) * tk, tk), tk)
        prepare_next(nkv - 1)

    @pl.when(chunks(nxt) > nkv)
    def _():
        prepare_next(nkv)

    o = _normalised(l_ref, acc_ref)
    lo = lane < HEAD_DIM
    for p in range(4):
        o_ref[0, :, p * LANES:(p + 1) * LANES] = jnp.where(
            lo, o[(2 * p) * tq:(2 * p + 1) * tq], o[(2 * p + 1) * tq:(2 * p + 2) * tq]
        ).astype(o_ref.dtype)


def _dsa_attn(bq, iq, iw, ik4, k2, v2, tq, tk):
    B, S, C = bq.shape
    assert tk % tq == 0 and tk % BIT_GROUP == 0 and S % tk == 0
    topk = min(IDX_TOPK_MAX, S // 4)
    idx_bits = max(1, (S - 1).bit_length())
    kern = functools.partial(_dsa_kernel, tq=tq, tk=tk, topk=topk, idx_bits=idx_bits)
    nq = S // tq
    row = lambda width: pl.BlockSpec((1, tq, width), lambda b, i: (b, i, 0))
    nxt = lambda width: pl.BlockSpec((1, tq, width), lambda b, i: (b, jnp.minimum(i + 1, nq - 1), 0))
    full = _resident((1, S, LANES), lambda b, i: (b, 0, 0))
    return pl.pallas_call(
        kern,
        grid=(B, nq),
        in_specs=[row(C), row(LANES), row(LANES), nxt(LANES), nxt(LANES), full, full, full],
        out_specs=row(C),
        out_shape=jax.ShapeDtypeStruct(bq.shape, BF16),
        scratch_shapes=[pltpu.VMEM((2, tq, S), jnp.int32),
                        pltpu.VMEM((S // BIT_GROUP, 32, 8, tq), jnp.int32),
                        pltpu.VMEM((S // BIT_GROUP, 8, tq), jnp.int32),
                        pltpu.VMEM((tq, LANES), jnp.int32),
                        pltpu.VMEM((8 * tq, LANES), F32), pltpu.VMEM((8 * tq, LANES), F32),
                        pltpu.VMEM((8 * tq, LANES), F32)],
        compiler_params=_cparams(("parallel", "arbitrary")),
        name="dsa_attn",
    )(bq, iq, iw, iq, iw, ik4, k2, v2)


MOBA_TILE_BLOCKS = 8
MOBA_BULK_BLOCKS = 4


def _moba_kernel(q_ref, k_ref, v_ref, km_ref, o_ref, m_ref, l_ref, acc_ref, *, nb):
    nqb = MOBA_TILE_BLOCKS
    blk_rows = MOBA_BLOCK
    grp = 2 * blk_rows
    i = pl.program_id(2)
    first = nqb * i
    lane = lax.broadcasted_iota(jnp.int32, (1, LANES), 1)
    q = q_ref[0]
    q2 = jnp.concatenate(
        [_split_heads_rows(q[j * blk_rows:(j + 1) * blk_rows]) for j in range(nqb)], axis=0)

    km = jnp.concatenate([km_ref[0], jnp.zeros((LANES - nb, LANES), F32)], axis=0)
    km_hi = km.astype(BF16)
    r1 = km - km_hi.astype(F32)
    km_mid = r1.astype(BF16)
    km_lo = (r1 - km_mid.astype(F32)).astype(BF16)
    nb8 = -(-nb // 8) * 8
    gate_t = (_nt_dot(km_hi, q2) + _nt_dot(km_mid, q2) + _nt_dot(km_lo, q2))[:nb8]
    blk = lax.broadcasted_iota(jnp.int32, (nb8, 1), 0)
    blk_f = blk.astype(F32)
    col = lax.broadcasted_iota(jnp.int32, (1, nqb * grp), 1)
    past = blk < first + (col >> (grp.bit_length() - 1))
    g = jnp.where(past, gate_t, -jnp.inf)
    sel = jnp.zeros(g.shape, jnp.bool_)
    for _ in range(min(MOBA_TOPK, nb - 1)):
        mx = jnp.max(g, axis=0, keepdims=True)
        pick = blk_f == jnp.min(jnp.where(g == mx, blk_f, float(LANES)), axis=0, keepdims=True)
        sel = sel | pick
        g = jnp.where(pick, -jnp.inf, g)
    bias_t = jnp.concatenate([jnp.where(sel & past, 0.0, NEG_BIG),
                              jnp.full((LANES - nb8, nqb * grp), NEG_BIG, F32)], axis=0)
    qa = jnp.concatenate([q2, bias_t.T.astype(BF16)], axis=1)

    m_ref[...] = jnp.full(m_ref.shape, NEG_BIG, F32)
    l_ref[...] = jnp.zeros(l_ref.shape, F32)
    acc_ref[...] = jnp.zeros(acc_ref.shape, F32)

    def biased_logits(lhs, first_blk, nblk):
        n = nblk * blk_rows
        off = pl.multiple_of(first_blk * blk_rows, blk_rows)
        row_blk = lax.broadcasted_iota(jnp.int32, (n, 1), 0) >> (MOBA_BLOCK.bit_length() - 1)
        onehot = jnp.where(lane == first_blk + row_blk, 1.0, 0.0).astype(BF16)
        ka = jnp.concatenate([k_ref[0, pl.ds(off, n), :], onehot], axis=1)
        return _nt_dot(lhs, ka), v_ref[0, pl.ds(off, n), :]

    def body(c, carry):
        s, v = biased_logits(qa, MOBA_BULK_BLOCKS * c, MOBA_BULK_BLOCKS)
        _online_update(s, v, m_ref, l_ref, acc_ref)
        return carry

    lax.fori_loop(0, first // MOBA_BULK_BLOCKS, body, 0)

    r = lax.broadcasted_iota(jnp.int32, (grp, 1), 0)
    causal = (lax.broadcasted_iota(jnp.int32, (1, blk_rows), 1)
              <= jnp.where(r >= blk_rows, r - blk_rows, r))
    for j in range(nqb):
        off = pl.multiple_of((first + j) * blk_rows, blk_rows)
        s = jnp.where(causal, _nt_dot(q2[j * grp:(j + 1) * grp], k_ref[0, pl.ds(off, blk_rows), :]),
                      NEG_BIG)
        if j + 1 < nqb:
            s_later, _ = biased_logits(qa[(j + 1) * grp:], first + j, 1)
            s = jnp.concatenate([s, s_later], axis=0)
        _online_update(s, v_ref[0, pl.ds(off, blk_rows), :], m_ref, l_ref, acc_ref, row0=j * grp)

    o = _normalised(l_ref, acc_ref)
    lo = lane < HEAD_DIM
    for j in range(nqb):
        o_ref[0, j * blk_rows:(j + 1) * blk_rows] = jnp.where(
            lo, o[j * grp:j * grp + blk_rows], o[j * grp + blk_rows:(j + 1) * grp]).astype(o_ref.dtype)


def _moba_attn(cq, ck, cv, kmean):
    B, S, C = cq.shape
    nb = S // MOBA_BLOCK
    tq = MOBA_TILE_BLOCKS * MOBA_BLOCK
    return pl.pallas_call(
        functools.partial(_moba_kernel, nb=nb),
        grid=(B, C // LANES, S // tq),
        in_specs=[pl.BlockSpec((1, tq, LANES), lambda b, p, i: (b, i, p)),
                  pl.BlockSpec((1, S, LANES), lambda b, p, i: (b, 0, p)),
                  pl.BlockSpec((1, S, LANES), lambda b, p, i: (b, 0, p)),
                  pl.BlockSpec((1, nb, LANES), lambda b, p, i: (b, 0, p))],
        out_specs=pl.BlockSpec((1, tq, LANES), lambda b, p, i: (b, i, p)),
        out_shape=jax.ShapeDtypeStruct(cq.shape, BF16),
        scratch_shapes=[pltpu.VMEM((2 * tq, LANES), F32), pltpu.VMEM((2 * tq, LANES), F32),
                        pltpu.VMEM((2 * tq, LANES), F32)],
        compiler_params=_cparams(("parallel", "parallel", "arbitrary")),
        name="moba_attn",
    )(cq, ck, cv, kmean)


def _sigmoid(x):
    return 1.0 / (1.0 + jnp.exp(-x))


def _merge_kernel(ya_ref, yb_ref, yc_ref, gates_ref, x_ref, wb_ref, wo_ref, g_ref, o_ref):
    d = x_ref.shape[2]
    merged = None
    for n, y_ref in enumerate((ya_ref, yb_ref, yc_ref)):
        br = jnp.dot(y_ref[0], wb_ref[n], preferred_element_type=F32)
        term = _sigmoid(gates_ref[0, :, n * d:(n + 1) * d].astype(F32)) * br
        merged = term if merged is None else merged + term
    m = jnp.dot(merged.astype(BF16), wo_ref[...], preferred_element_type=F32)
    o_ref[0] = x_ref[0] + _rms(m, g_ref[...])


def _merge(ya, yb, yc, gates, x, wb, wo, layer, g, tm):
    B, S, D = x.shape
    row = lambda width: pl.BlockSpec((1, tm, width), lambda b, m: (b, m, 0))
    return pl.pallas_call(
        _merge_kernel,
        grid=(B, S // tm),
        in_specs=[row(BRANCH_WIDTH), row(BRANCH_WIDTH), row(BRANCH_WIDTH), row(N_BRANCH * D), row(D),
                  _resident((None,) + wb.shape[1:], lambda b, m: (layer, 0, 0, 0)),
                  _resident((None,) + wo.shape[1:], lambda b, m: (layer, 0, 0)),
                  _resident((1, D), lambda b, m: (0, 0))],
        out_specs=row(D),
        out_shape=jax.ShapeDtypeStruct(x.shape, F32),
        compiler_params=_cparams(("parallel", "parallel")),
        name="merge_out",
    )(ya, yb, yc, gates, x, wb, wo, g.reshape(1, D))


def _ffn_kernel(x_ref, g_in_ref, wi_ref, wo_ref, g_out_ref, o_ref, acc_ref, *, d_ff, tf):
    x = x_ref[0]
    h = _rms(x, g_in_ref[...]).astype(BF16)
    acc_ref[...] = jnp.zeros(acc_ref.shape, F32)

    def body(c, carry):
        off = pl.multiple_of(c * tf, tf)
        gt = jnp.dot(h, wi_ref[:, pl.ds(off, tf)], preferred_element_type=F32)
        up_off = pl.multiple_of(d_ff + off, LANES)
        up = jnp.dot(h, wi_ref[:, pl.ds(up_off, tf)], preferred_element_type=F32)
        act = (gt * _sigmoid(gt) * up).astype(BF16)
        acc_ref[...] += jnp.dot(act, wo_ref[pl.ds(off, tf), :], preferred_element_type=F32)
        return carry

    lax.fori_loop(0, d_ff // tf, body, 0)
    o_ref[0] = x + _rms(acc_ref[...], g_out_ref[...])


def _ffn(x, g_in, wi, wo, layer, g_out, tm, tf):
    B, S, D = x.shape
    d_ff = wo.shape[1]
    row = pl.BlockSpec((1, tm, D), lambda b, m: (b, m, 0))
    vec = _resident((1, D), lambda b, m: (0, 0))
    return pl.pallas_call(
        functools.partial(_ffn_kernel, d_ff=d_ff, tf=tf),
        grid=(B, S // tm),
        in_specs=[row, vec, _resident((None,) + wi.shape[1:], lambda b, m: (layer, 0, 0)),
                  _resident((None,) + wo.shape[1:], lambda b, m: (layer, 0, 0)), vec],
        out_specs=row,
        out_shape=jax.ShapeDtypeStruct(x.shape, F32),
        scratch_shapes=[pltpu.VMEM((tm, D), F32)],
        compiler_params=_cparams(("parallel", "parallel")),
        name="swiglu_ffn",
    )(x, g_in.reshape(1, D), wi, wo, g_out.reshape(1, D))


class _Tiles(NamedTuple):
    proj_rows: int
    dense_rows: int
    ffn_cols: int
    diff_tile: int
    dsa_queries: int
    dsa_keys: int


def _tiles(seq):
    return _Tiles(proj_rows=min(512, seq), dense_rows=min(1024, seq), ffn_cols=256,
                  diff_tile=min(2048, seq), dsa_queries=min(256, seq), dsa_keys=min(512, seq))


def kernel(x, positions, w_in, w_branch, w_out, lambda_q1, lambda_k1, lambda_q2, lambda_k2,
           subln_g, norm_g, w_ffn_in, w_ffn_out):
    B, S, D = x.shape
    depth = w_in.shape[0]
    tiles = _tiles(S)
    assert S % (MOBA_TILE_BLOCKS * MOBA_BLOCK) == 0 and S % tiles.diff_tile == 0
    tab = _rope_table(positions, tiles.proj_rows)
    assert w_in.shape[2] == B_START + B_COLS
    w_a = w_in.astype(BF16)
    w_b = w_a[..., B_START:]
    wb, wo = w_branch.astype(BF16), w_out.astype(BF16)
    wi_ffn, wo_ffn = w_ffn_in.astype(BF16), w_ffn_out.astype(BF16)
    for l in range(depth):
        lam_init = 0.8 - 0.6 * math.exp(-0.3 * l)
        (aq, ak, av, bq, k2, v2, iq, ik4, iw, cq, ck, cv, gates, kmean) = _inproj(
            x, norm_g[l, 0], tab, w_a, w_b, l, tiles.proj_rows)
        kmean = kmean.reshape(B, S // MOBA_BLOCK, 512)
        lam_rows = jnp.stack([lambda_q1[l], lambda_k1[l], lambda_q2[l], lambda_k2[l]])
        ya = _diff_attn(aq, ak, av, lam_rows, subln_g[l], lam_init, t=tiles.diff_tile)
        yb = _dsa_attn(bq, iq, iw, ik4, k2, v2, tq=tiles.dsa_queries, tk=tiles.dsa_keys)
        yc = _moba_attn(cq, ck, cv, kmean)
        x = _merge(ya, yb, yc, gates, x, wb, wo, l, norm_g[l, 1], tiles.dense_rows)
        x = _ffn(x, norm_g[l, 2], wi_ffn, wo_ffn, l, norm_g[l, 3], tiles.dense_rows,
                 tf=tiles.ffn_cols)
    return x
```

```python
import functools
import math
from typing import NamedTuple

import jax
import jax.numpy as jnp
from jax import lax
from jax.experimental import pallas as pl
from jax.experimental.pallas import tpu as pltpu

F32 = jnp.float32
BF16 = jnp.bfloat16

LANES = 128
HEAD_DIM = 64
ROPE_THETA = 500000.0
NORM_EPS = 1e-6
IDX_HEADS = 4
IDX_DIM = 32
IDX_TOPK_MAX = 256
MOBA_BLOCK = 256
MOBA_TOPK = 3
N_BRANCH = 3
BRANCH_WIDTH = 512
NEG_BIG = -1e30
LOG2E = math.log2(math.e)
ROW_BLOCK = 64
INT_MIN = -2 ** 31
VMEM_LIMIT = 56 * 1024 * 1024

A_AQ, A_AK, A_AV, A_BQ, A_KV, A_IQ, A_IKW, A_COLS = 0, 512, 1024, 1536, 2048, 2176, 2304, 2432
B_START = 2340
B_CQ, B_CK, B_CV, B_G, B_COLS = 0, 512, 1024, 1536, 4608
Q_SCALE = HEAD_DIM ** -0.5 * LOG2E


def _nt_dot(a, b):
    return lax.dot_general(a, b, (((1,), (1,)), ((), ())), preferred_element_type=F32)


def _rms(x, g):
    return x * lax.rsqrt(jnp.mean(x * x, axis=-1, keepdims=True) + NORM_EPS) * g


def _cparams(sem):
    return pltpu.CompilerParams(dimension_semantics=sem, vmem_limit_bytes=VMEM_LIMIT)


def _resident(shape, index_map):
    return pl.BlockSpec(shape, index_map, pipeline_mode=pl.Buffered(1))


def _rope_table_kernel(pos_ref, tab_ref):
    pos = pos_ref[0].astype(F32)
    lane = lax.broadcasted_iota(jnp.int32, (1, LANES), 1)

    def tables(head_dim):
        rot = head_dim // 4
        half = rot // 2
        d = lane & (head_dim - 1)
        fi = d & (half - 1)
        inv = jnp.zeros((1, LANES), F32)
        for i in range(half):
            inv = jnp.where(fi == i, F32(ROPE_THETA ** (-(2.0 * i) / rot)), inv)
        ang = pos * inv
        cos = jnp.cos(ang)
        sin = jnp.sin(ang)
        c = jnp.where(d < rot, cos, 1.0)
        s = jnp.where(d < half, -sin, jnp.where(d < rot, sin, 0.0))
        return c, s

    c64, s64 = tables(HEAD_DIM)
    c32, s32 = tables(IDX_DIM)
    tab_ref[0, :, 0 * LANES:1 * LANES] = c64
    tab_ref[0, :, 1 * LANES:2 * LANES] = s64
    tab_ref[0, :, 2 * LANES:3 * LANES] = c32
    tab_ref[0, :, 3 * LANES:4 * LANES] = s32


def _rope_table(positions, tm):
    B, S = positions.shape
    return pl.pallas_call(
        _rope_table_kernel,
        grid=(B, S // tm),
        in_specs=[pl.BlockSpec((1, tm, 1), lambda b, m: (b, m, 0))],
        out_specs=pl.BlockSpec((1, tm, 4 * LANES), lambda b, m: (b, m, 0)),
        out_shape=jax.ShapeDtypeStruct((B, S, 4 * LANES), F32),
        compiler_params=_cparams(("parallel", "parallel")),
        name="rope_table",
    )(positions.reshape(B, S, 1))


def _inproj_kernel(x_ref, g_ref, tab_ref, wa_ref, wb_ref,
                   aq_ref, ak_ref, av_ref, bq_ref, k2_ref, v2_ref, iq_ref, ik4_ref, iw_ref,
                   cq_ref, ck_ref, cv_ref, gates_ref, kmean_ref):
    h = _rms(x_ref[0], g_ref[...]).astype(BF16)
    lane = lax.broadcasted_iota(jnp.int32, (1, LANES), 1)

    def proj(w_ref, off, width):
        return jnp.dot(h, w_ref[:, off:off + width], preferred_element_type=F32)

    def rope(y, head_dim):
        half = head_dim // 8
        t0 = 0 if head_dim == HEAD_DIM else 2 * LANES
        c = tab_ref[0, :, t0:t0 + LANES]
        s = tab_ref[0, :, t0 + LANES:t0 + 2 * LANES]
        first = (lane & (head_dim - 1)) < half
        blocks = []
        for j in range(y.shape[1] // LANES):
            yj = y[:, j * LANES:(j + 1) * LANES]
            up = pltpu.roll(yj, LANES - half, 1)
            dn = pltpu.roll(yj, half, 1)
            blocks.append(yj * c + jnp.where(first, up, dn) * s)
        return blocks

    def store(out_ref, blocks):
        for j, r in enumerate(blocks):
            out_ref[0, :, j * LANES:(j + 1) * LANES] = r.astype(out_ref.dtype)

    store(aq_ref, rope(proj(wa_ref, A_AQ, 512) * Q_SCALE, HEAD_DIM))
    store(ak_ref, rope(proj(wa_ref, A_AK, 512), HEAD_DIM))
    av_ref[0] = proj(wa_ref, A_AV, 512).astype(av_ref.dtype)
    store(bq_ref, rope(proj(wa_ref, A_BQ, 512) * Q_SCALE, HEAD_DIM))
    kv = proj(wa_ref, A_KV, LANES)
    vk = pltpu.roll(kv, HEAD_DIM, 1)
    lo = lane < HEAD_DIM
    store(k2_ref, rope(jnp.where(lo, kv, vk), HEAD_DIM))
    v2_ref[0] = jnp.where(lo, vk, kv).astype(v2_ref.dtype)
    store(iq_ref, rope(proj(wa_ref, A_IQ, LANES), IDX_DIM))
    ikw = proj(wa_ref, A_IKW, LANES)
    ik = rope(ikw, IDX_DIM)[0]
    ik4 = ik
    for rep in range(1, LANES // IDX_DIM):
        ik4 = jnp.where(lane < rep * IDX_DIM, ik4, pltpu.roll(ik, rep * IDX_DIM, 1))
    ik4_ref[0] = ik4.astype(ik4_ref.dtype)
    iw_ref[0] = jnp.where(lane < IDX_HEADS, pltpu.roll(ikw, LANES - IDX_DIM, 1), 0.0)
    store(cq_ref, rope(proj(wb_ref, B_CQ, 512) * Q_SCALE, HEAD_DIM))
    ck_blocks = rope(proj(wb_ref, B_CK, 512), HEAD_DIM)
    store(ck_ref, ck_blocks)
    cv_ref[0] = proj(wb_ref, B_CV, 512).astype(cv_ref.dtype)
    gates_ref[0] = proj(wb_ref, B_G, 3 * 1024).astype(gates_ref.dtype)

    tm = x_ref.shape[1]
    for blk in range(tm // MOBA_BLOCK):
        for j, r in enumerate(ck_blocks):
            part = r[blk * MOBA_BLOCK:(blk + 1) * MOBA_BLOCK]
            kmean_ref[0, 0, blk:blk + 1, j * LANES:(j + 1) * LANES] = (
                jnp.mean(part, axis=0, keepdims=True))


def _inproj(x, g, tab, w_a, w_b, layer, tm):
    B, S, D = x.shape
    nm = S // tm
    row = lambda width: pl.BlockSpec((1, tm, width), lambda b, m: (b, m, 0))
    shp = lambda width, dt: jax.ShapeDtypeStruct((B, S, width), dt)
    widths = [(512, BF16)] * 4 + [(LANES, BF16)] * 4 + [(LANES, F32)] + [(512, BF16)] * 3 + [(3072, BF16)]
    out_specs = [row(w) for w, _ in widths]
    out_shape = [shp(w, dt) for w, dt in widths]
    out_specs.append(pl.BlockSpec((1, 1, tm // MOBA_BLOCK, 512), lambda b, m: (b, m, 0, 0)))
    out_shape.append(jax.ShapeDtypeStruct((B, nm, tm // MOBA_BLOCK, 512), F32))
    return pl.pallas_call(
        _inproj_kernel,
        grid=(B, nm),
        in_specs=[row(D),
                  _resident((1, D), lambda b, m: (0, 0)),
                  row(4 * LANES),
                  _resident((None, D, A_COLS), lambda b, m: (layer, 0, 0)),
                  _resident((None, D, B_COLS), lambda b, m: (layer, 0, 0))],
        out_specs=out_specs,
        out_shape=out_shape,
        compiler_params=_cparams(("parallel", "parallel")),
        name="inproj",
    )(x, g.reshape(1, D), tab, w_a, w_b)


def _split_heads_rows(q):
    lo = lax.broadcasted_iota(jnp.int32, (1, LANES), 1) < HEAD_DIM
    zero = jnp.zeros_like(q)
    return jnp.concatenate([jnp.where(lo, q, zero), jnp.where(lo, zero, q)], axis=0)


def _online_update(s, v, m_ref, l_ref, acc_ref, bias=None, row0=0):
    rows_total = s.shape[0]
    nblk = s.shape[1] // LANES
    p_rows, alphas = [], []
    for r0 in range(0, rows_total, ROW_BLOCK):
        rows = slice(row0 + r0, row0 + r0 + ROW_BLOCK)
        cols = [s[r0:r0 + ROW_BLOCK, j * LANES:(j + 1) * LANES] for j in range(nblk)]
        if bias is not None:
            b0 = r0 % bias[0].shape[0]
            cols = [c + bj[b0:b0 + ROW_BLOCK] for c, bj in zip(cols, bias)]
        m_prev = m_ref[rows]
        m_new = jnp.maximum(
            m_prev, jnp.max(functools.reduce(jnp.maximum, cols), axis=1, keepdims=True))
        alpha = jnp.exp2(m_prev - m_new)
        ps = [jnp.exp2(c - m_new) for c in cols]
        l_ref[rows] = alpha * l_ref[rows] + functools.reduce(jnp.add, ps)
        m_ref[rows] = m_new
        p_rows.append(jnp.concatenate([pj.astype(BF16) for pj in ps], axis=1))
        alphas.append(alpha)
    p = jnp.concatenate(p_rows, axis=0)
    alpha = jnp.concatenate(alphas, axis=0)
    span = slice(row0, row0 + rows_total)
    acc_ref[span] = alpha * acc_ref[span] + jnp.dot(p, v, preferred_element_type=F32)


def _normalised(l_ref, acc_ref):
    return acc_ref[...] / jnp.sum(l_ref[...], axis=1, keepdims=True)


DIFF_SUB = 512
DIFF_BULK_KEYS = 1024


def _diff_attn_kernel(q_ref, k_ref, v_ref, lam_ref, g_ref, o_ref, m_ref, l_ref, acc_ref,
                      *, t, lam_init):
    qi = pl.program_id(2)
    sub = DIFF_SUB
    nsb = t // sub
    grp = 2 * sub
    q = q_ref[0]
    q2 = jnp.concatenate(
        [_split_heads_rows(q[j * sub:(j + 1) * sub]) for j in range(nsb)], axis=0)
    m_ref[...] = jnp.full(m_ref.shape, NEG_BIG, F32)
    l_ref[...] = jnp.zeros(l_ref.shape, F32)
    acc_ref[...] = jnp.zeros(acc_ref.shape, F32)

    def body(c, carry):
        off = pl.multiple_of(c * DIFF_BULK_KEYS, DIFF_BULK_KEYS)
        _online_update(_nt_dot(q2, k_ref[0, pl.ds(off, DIFF_BULK_KEYS), :]),
                       v_ref[0, pl.ds(off, DIFF_BULK_KEYS), :], m_ref, l_ref, acc_ref)
        return carry

    lax.fori_loop(0, qi * (t // DIFF_BULK_KEYS), body, 0)

    r = lax.broadcasted_iota(jnp.int32, (grp, 1), 0)
    causal = lax.broadcasted_iota(jnp.int32, (1, sub), 1) <= jnp.where(r >= sub, r - sub, r)
    for j in range(nsb):
        off = pl.multiple_of(qi * t + j * sub, sub)
        s = _nt_dot(q2[j * grp:], k_ref[0, pl.ds(off, sub), :])
        own = jnp.where(causal, s[:grp], NEG_BIG)
        s = own if j + 1 == nsb else jnp.concatenate([own, s[grp:]], axis=0)
        _online_update(s, v_ref[0, pl.ds(off, sub), :], m_ref, l_ref, acc_ref, row0=j * grp)

    lam_rows = lam_ref[...]
    e1 = jnp.exp(jnp.sum(lam_rows[0:1] * lam_rows[1:2], axis=1, keepdims=True))
    e2 = jnp.exp(jnp.sum(lam_rows[2:3] * lam_rows[3:4], axis=1, keepdims=True))
    lam = e1 - e2 + lam_init
    o = _normalised(l_ref, acc_ref)
    for j in range(nsb):
        out = o[j * grp:j * grp + sub] - lam * o[j * grp + sub:(j + 1) * grp]
        o_ref[0, j * sub:(j + 1) * sub] = (
            _rms(out, g_ref[...]) * (1.0 - lam_init)).astype(o_ref.dtype)


def _diff_attn(aq, ak, av, lam_rows, subln_g, lam_init, t):
    B, S, _ = aq.shape
    nh = aq.shape[2] // LANES
    kern = functools.partial(_diff_attn_kernel, t=t, lam_init=lam_init)
    return pl.pallas_call(
        kern,
        grid=(B, nh, S // t),
        in_specs=[pl.BlockSpec((1, t, LANES), lambda b, h, i: (b, i, h)),
                  pl.BlockSpec((1, S, LANES), lambda b, h, i: (b, 0, h)),
                  pl.BlockSpec((1, S, LANES), lambda b, h, i: (b, 0, h)),
                  pl.BlockSpec((4, HEAD_DIM), lambda b, h, i: (0, 0)),
                  pl.BlockSpec((1, LANES), lambda b, h, i: (0, 0))],
        out_specs=pl.BlockSpec((1, t, LANES), lambda b, h, i: (b, i, h)),
        out_shape=jax.ShapeDtypeStruct(aq.shape, BF16),
        scratch_shapes=[pltpu.VMEM((2 * t, LANES), F32), pltpu.VMEM((2 * t, LANES), F32),
                        pltpu.VMEM((2 * t, LANES), F32)],
        compiler_params=_cparams(("parallel", "parallel", "arbitrary")),
        name="diff_attn",
    )(aq, ak, av, lam_rows, subln_g.reshape(1, LANES))


BIT_GROUP = 256


def _bit_transpose32(words):
    w = list(words)
    for j, m in ((16, 0x0000FFFF), (8, 0x00FF00FF), (4, 0x0F0F0F0F), (2, 0x33333333), (1, 0x55555555)):
        for k in range(32):
            if k & j == 0:
                t = (w[k] ^ (w[k + j] >> j)) & m
                w[k] = w[k] ^ t
                w[k + j] = w[k + j] ^ (t << j)
    return w


def _dsa_kernel(q_ref, iq_ref, iw_ref, iqn_ref, iwn_ref, ik4_ref, k2_ref, v2_ref, o_ref,
                keys_ref, planes_ref, act_ref, jcut_ref, m_ref, l_ref, acc_ref,
                *, tq, tk, topk, idx_bits):
    qi = pl.program_id(1)
    nxt = jnp.minimum(qi + 1, pl.num_programs(1) - 1)
    slot = qi % 2
    chunks = lambda tile: (tile * tq + tq - 1) // tk + 1
    nkv = chunks(qi)
    lane = lax.broadcasted_iota(jnp.int32, (1, LANES), 1)
    row = lax.broadcasted_iota(jnp.int32, (tq, 1), 0)
    qpos = qi * tq + row
    lane_tk = lax.broadcasted_iota(jnp.int32, (1, tk), 1)
    nsub = tk // LANES
    seq = keys_ref.shape[2]

    def indexer_rows(iq_blk, iw_blk):
        iq = iq_blk[0]
        zero = jnp.zeros_like(iq)
        iq4 = jnp.concatenate(
            [jnp.where((lane >> (IDX_DIM.bit_length() - 1)) == hh, iq, zero)
             for hh in range(IDX_HEADS)], axis=0)
        iw = iw_blk[0]
        return iq4, [iw[:, hh:hh + 1] for hh in range(IDX_HEADS)]

    def score_chunk(c, tile, dst, iq4, iw_cols):
        off = pl.multiple_of(c * tk, tk)
        rel = jnp.maximum(_nt_dot(iq4, ik4_ref[0, pl.ds(off, tk), :]), 0.0)
        score = iw_cols[0] * rel[0:tq]
        for hh in range(1, IDX_HEADS):
            score = score + iw_cols[hh] * rel[hh * tq:(hh + 1) * tq]
        bits = lax.bitcast_convert_type(score, jnp.int32)
        key = bits ^ ((bits >> 31) & jnp.int32(0x7FFFFFFF))
        kpos = off + lane_tk
        key = jnp.where(score == 0.0, (seq - 1) - kpos, jnp.where(key > 0, key + seq, key))
        key = jnp.where(kpos <= tile * tq + row, key, jnp.int32(INT_MIN))
        keys_ref[dst, :, pl.ds(off, tk)] = key
        kts = [key[:, j * LANES:(j + 1) * LANES].T for j in range(nsub)]
        for gl in range(tk // BIT_GROUP):
            words = [kts[(gl * BIT_GROUP + 8 * g) // LANES][(8 * g) % LANES:(8 * g) % LANES + 8]
                     for g in range(32)]
            words = _bit_transpose32(words)
            words[0] = ~words[0]
            for b in range(32):
                planes_ref[c * (tk // BIT_GROUP) + gl, b] = words[b]

    @pl.when(qi == 0)
    def _():
        planes_ref[...] = jnp.zeros(planes_ref.shape, jnp.int32)
        iq4, iw_cols = indexer_rows(iq_ref, iw_ref)
        for c in range(chunks(0)):
            score_chunk(c, 0, 0, iq4, iw_cols)

    ngroups = planes_ref.shape[0]
    live = nkv * (tk // BIT_GROUP)
    for g in range(ngroups):
        act_ref[g] = jnp.broadcast_to(jnp.where(g < live, jnp.int32(-1), jnp.int32(0)), (8, tq))

    def sweep(i, keep, first=False):
        parts = [jnp.zeros((8, tq), jnp.int32) for _ in range(4)]
        for g in range(ngroups):
            act = act_ref[g]
            if not first:
                act = act & (planes_ref[g, i - 1] ^ keep)
                act_ref[g] = act
            parts[g % 4] = parts[g % 4] + lax.population_count(act & planes_ref[g, i])
        cnt = (parts[0] + parts[1]) + (parts[2] + parts[3])
        return jnp.sum(cnt.astype(F32), axis=0, keepdims=True)

    def decide(i, ones, thr_u, n_gt):
        take = n_gt + ones >= topk
        thr_u = thr_u | jnp.where(take, jnp.left_shift(jnp.int32(1), 31 - i), 0)
        return thr_u, jnp.where(take, n_gt, n_gt + ones), jnp.where(take, 0, -1).astype(jnp.int32)

    def pass_body(i, carry):
        thr_u, n_gt, keep = carry
        return decide(i, sweep(i, keep), thr_u, n_gt)

    start = decide(0, sweep(0, None, first=True),
                   jnp.zeros((1, tq), jnp.int32), jnp.zeros((1, tq), F32))
    thr_u, n_gt, keep = lax.fori_loop(1, 32, pass_body, start)
    n_eq = jnp.zeros((8, tq), jnp.int32)
    for g in range(ngroups):
        n_eq = n_eq + lax.population_count(act_ref[g] & (planes_ref[g, 31] ^ keep))
    thr = thr_u ^ jnp.int32(INT_MIN)
    cnt_ge = n_gt + jnp.sum(n_eq.astype(F32), axis=0, keepdims=True)
    need = topk - n_gt

    def as_rows(x):
        return jnp.broadcast_to(x, (LANES, tq)).T

    def key_block(off, j):
        return keys_ref[slot, :, pl.ds(pl.multiple_of(off + j * LANES, LANES), LANES)]

    thr_r = as_rows(thr)
    need_r = as_rows(need)
    jcut_ref[...] = jnp.full((tq, LANES), 2 ** 30, jnp.int32)

    @pl.when(jnp.max(cnt_ge) > topk)
    def _():
        def ties_before(cand):
            def body(c, cnt):
                off = pl.multiple_of(c * tk, tk)
                for j in range(nsub):
                    hit = (key_block(off, j) == thr_r) & (off + j * LANES + lane < cand)
                    cnt = jnp.where(hit, cnt + 1, cnt)
                return cnt
            cnt = lax.fori_loop(0, nkv, body, jnp.zeros((tq, LANES), jnp.int32))
            return jnp.sum(cnt.astype(F32), axis=1, keepdims=True)

        def idx_body(i, j):
            cand = j + jnp.left_shift(jnp.int32(1), idx_bits - 1 - i)
            return jnp.where(ties_before(cand) < need_r, cand, j)
        jcut_ref[...] = lax.fori_loop(0, idx_bits, idx_body, jnp.zeros((tq, LANES), jnp.int32))

    jcut_r = jcut_ref[...]

    q = q_ref[0]
    q8 = jnp.concatenate(
        [_split_heads_rows(q[:, p * LANES:(p + 1) * LANES]) for p in range(4)], axis=0)
    m_ref[...] = jnp.full(m_ref.shape, NEG_BIG, F32)
    l_ref[...] = jnp.zeros(l_ref.shape, F32)
    acc_ref[...] = jnp.zeros(acc_ref.shape, F32)
    iq4n, iwn_cols = indexer_rows(iqn_ref, iwn_ref)

    def prepare_next(c):
        score_chunk(c, nxt, 1 - slot, iq4n, iwn_cols)

    def attend(off, width):
        s = _nt_dot(q8, k2_ref[0, pl.ds(off, width), :])
        bias = []
        for j in range(width // LANES):
            keyj = key_block(off, j)
            kpos = off + j * LANES + lane
            sel = ((keyj > thr_r) | ((keyj == thr_r) & (kpos <= jcut_r))) & (kpos <= qpos)
            bias.append(jnp.where(sel, 0.0, NEG_BIG))
        _online_update(s, v2_ref[0, pl.ds(off, width), :], m_ref, l_ref, acc_ref, bias=bias)

    def attn_body(c, carry):
        attend(pl.multiple_of(c * 2 * tk, 2 * tk), 2 * tk)
        prepare_next(2 * c)
        prepare_next(2 * c + 1)
        return carry

    lax.fori_loop(0, nkv // 2, attn_body, 0)

    @pl.when(nkv % 2 == 1)
    def _():
        attend(pl.multiple_of((nkv - 1) * tk, tk), tk)
        prepare_next(nkv - 1)

    @pl.when(chunks(nxt) > nkv)
    def _():
        prepare_next(nkv)

    o = _normalised(l_ref, acc_ref)
    lo = lane < HEAD_DIM
    for p in range(4):
        o_ref[0, :, p * LANES:(p + 1) * LANES] = jnp.where(
            lo, o[(2 * p) * tq:(2 * p + 1) * tq], o[(2 * p + 1) * tq:(2 * p + 2) * tq]
        ).astype(o_ref.dtype)


def _dsa_attn(bq, iq, iw, ik4, k2, v2, tq, tk):
    B, S, C = bq.shape
    assert tk % tq == 0 and tk % BIT_GROUP == 0 and S % tk == 0
    topk = min(IDX_TOPK_MAX, S // 4)
    idx_bits = max(1, (S - 1).bit_length())
    kern = functools.partial(_dsa_kernel, tq=tq, tk=tk, topk=topk, idx_bits=idx_bits)
    nq = S // tq
    row = lambda width: pl.BlockSpec((1, tq, width), lambda b, i: (b, i, 0))
    nxt = lambda width: pl.BlockSpec((1, tq, width), lambda b, i: (b, jnp.minimum(i + 1, nq - 1), 0))
    full = _resident((1, S, LANES), lambda b, i: (b, 0, 0))
    return pl.pallas_call(
        kern,
        grid=(B, nq),
        in_specs=[row(C), row(LANES), row(LANES), nxt(LANES), nxt(LANES), full, full, full],
        out_specs=row(C),
        out_shape=jax.ShapeDtypeStruct(bq.shape, BF16),
        scratch_shapes=[pltpu.VMEM((2, tq, S), jnp.int32),
                        pltpu.VMEM((S // BIT_GROUP, 32, 8, tq), jnp.int32),
                        pltpu.VMEM((S // BIT_GROUP, 8, tq), jnp.int32),
                        pltpu.VMEM((tq, LANES), jnp.int32),
                        pltpu.VMEM((8 * tq, LANES), F32), pltpu.VMEM((8 * tq, LANES), F32),
                        pltpu.VMEM((8 * tq, LANES), F32)],
        compiler_params=_cparams(("parallel", "arbitrary")),
        name="dsa_attn",
    )(bq, iq, iw, iq, iw, ik4, k2, v2)


MOBA_TILE_BLOCKS = 8
MOBA_BULK_BLOCKS = 4


def _moba_kernel(q_ref, k_ref, v_ref, km_ref, o_ref, m_ref, l_ref, acc_ref, *, nb):
    nqb = MOBA_TILE_BLOCKS
    blk_rows = MOBA_BLOCK
    grp = 2 * blk_rows
    i = pl.program_id(2)
    first = nqb * i
    lane = lax.broadcasted_iota(jnp.int32, (1, LANES), 1)
    q = q_ref[0]
    q2 = jnp.concatenate(
        [_split_heads_rows(q[j * blk_rows:(j + 1) * blk_rows]) for j in range(nqb)], axis=0)

    km = jnp.concatenate([km_ref[0], jnp.zeros((LANES - nb, LANES), F32)], axis=0)
    km_hi = km.astype(BF16)
    r1 = km - km_hi.astype(F32)
    km_mid = r1.astype(BF16)
    km_lo = (r1 - km_mid.astype(F32)).astype(BF16)
    nb8 = -(-nb // 8) * 8
    gate_t = (_nt_dot(km_hi, q2) + _nt_dot(km_mid, q2) + _nt_dot(km_lo, q2))[:nb8]
    blk = lax.broadcasted_iota(jnp.int32, (nb8, 1), 0)
    blk_f = blk.astype(F32)
    col = lax.broadcasted_iota(jnp.int32, (1, nqb * grp), 1)
    past = blk < first + (col >> (grp.bit_length() - 1))
    g = jnp.where(past, gate_t, -jnp.inf)
    sel = jnp.zeros(g.shape, jnp.bool_)
    for _ in range(min(MOBA_TOPK, nb - 1)):
        mx = jnp.max(g, axis=0, keepdims=True)
        pick = blk_f == jnp.min(jnp.where(g == mx, blk_f, float(LANES)), axis=0, keepdims=True)
        sel = sel | pick
        g = jnp.where(pick, -jnp.inf, g)
    bias_t = jnp.concatenate([jnp.where(sel & past, 0.0, NEG_BIG),
                              jnp.full((LANES - nb8, nqb * grp), NEG_BIG, F32)], axis=0)
    qa = jnp.concatenate([q2, bias_t.T.astype(BF16)], axis=1)

    m_ref[...] = jnp.full(m_ref.shape, NEG_BIG, F32)
    l_ref[...] = jnp.zeros(l_ref.shape, F32)
    acc_ref[...] = jnp.zeros(acc_ref.shape, F32)

    def biased_logits(lhs, first_blk, nblk):
        n = nblk * blk_rows
        off = pl.multiple_of(first_blk * blk_rows, blk_rows)
        row_blk = lax.broadcasted_iota(jnp.int32, (n, 1), 0) >> (MOBA_BLOCK.bit_length() - 1)
        onehot = jnp.where(lane == first_blk + row_blk, 1.0, 0.0).astype(BF16)
        ka = jnp.concatenate([k_ref[0, pl.ds(off, n), :], onehot], axis=1)
        return _nt_dot(lhs, ka), v_ref[0, pl.ds(off, n), :]

    def body(c, carry):
        s, v = biased_logits(qa, MOBA_BULK_BLOCKS * c, MOBA_BULK_BLOCKS)
        _online_update(s, v, m_ref, l_ref, acc_ref)
        return carry

    lax.fori_loop(0, first // MOBA_BULK_BLOCKS, body, 0)

    r = lax.broadcasted_iota(jnp.int32, (grp, 1), 0)
    causal = (lax.broadcasted_iota(jnp.int32, (1, blk_rows), 1)
              <= jnp.where(r >= blk_rows, r - blk_rows, r))
    for j in range(nqb):
        off = pl.multiple_of((first + j) * blk_rows, blk_rows)
        s = jnp.where(causal, _nt_dot(q2[j * grp:(j + 1) * grp], k_ref[0, pl.ds(off, blk_rows), :]),
                      NEG_BIG)
        if j + 1 < nqb:
            s_later, _ = biased_logits(qa[(j + 1) * grp:], first + j, 1)
            s = jnp.concatenate([s, s_later], axis=0)
        _online_update(s, v_ref[0, pl.ds(off, blk_rows), :], m_ref, l_ref, acc_ref, row0=j * grp)

    o = _normalised(l_ref, acc_ref)
    lo = lane < HEAD_DIM
    for j in range(nqb):
        o_ref[0, j * blk_rows:(j + 1) * blk_rows] = jnp.where(
            lo, o[j * grp:j * grp + blk_rows], o[j * grp + blk_rows:(j + 1) * grp]).astype(o_ref.dtype)


def _moba_attn(cq, ck, cv, kmean):
    B, S, C = cq.shape
    nb = S // MOBA_BLOCK
    tq = MOBA_TILE_BLOCKS * MOBA_BLOCK
    return pl.pallas_call(
        functools.partial(_moba_kernel, nb=nb),
        grid=(B, C // LANES, S // tq),
        in_specs=[pl.BlockSpec((1, tq, LANES), lambda b, p, i: (b, i, p)),
                  pl.BlockSpec((1, S, LANES), lambda b, p, i: (b, 0, p)),
                  pl.BlockSpec((1, S, LANES), lambda b, p, i: (b, 0, p)),
                  pl.BlockSpec((1, nb, LANES), lambda b, p, i: (b, 0, p))],
        out_specs=pl.BlockSpec((1, tq, LANES), lambda b, p, i: (b, i, p)),
        out_shape=jax.ShapeDtypeStruct(cq.shape, BF16),
        scratch_shapes=[pltpu.VMEM((2 * tq, LANES), F32), pltpu.VMEM((2 * tq, LANES), F32),
                        pltpu.VMEM((2 * tq, LANES), F32)],
        compiler_params=_cparams(("parallel", "parallel", "arbitrary")),
        name="moba_attn",
    )(cq, ck, cv, kmean)


def _sigmoid(x):
    return 1.0 / (1.0 + jnp.exp(-x))


def _merge_kernel(ya_ref, yb_ref, yc_ref, gates_ref, x_ref, wb_ref, wo_ref, g_ref, o_ref):
    d = x_ref.shape[2]
    merged = None
    for n, y_ref in enumerate((ya_ref, yb_ref, yc_ref)):
        br = jnp.dot(y_ref[0], wb_ref[n], preferred_element_type=F32)
        term = _sigmoid(gates_ref[0, :, n * d:(n + 1) * d].astype(F32)) * br
        merged = term if merged is None else merged + term
    m = jnp.dot(merged.astype(BF16), wo_ref[...], preferred_element_type=F32)
    o_ref[0] = x_ref[0] + _rms(m, g_ref[...])


def _merge(ya, yb, yc, gates, x, wb, wo, layer, g, tm):
    B, S, D = x.shape
    row = lambda width: pl.BlockSpec((1, tm, width), lambda b, m: (b, m, 0))
    return pl.pallas_call(
        _merge_kernel,
        grid=(B, S // tm),
        in_specs=[row(BRANCH_WIDTH), row(BRANCH_WIDTH), row(BRANCH_WIDTH), row(N_BRANCH * D), row(D),
                  _resident((None,) + wb.shape[1:], lambda b, m: (layer, 0, 0, 0)),
                  _resident((None,) + wo.shape[1:], lambda b, m: (layer, 0, 0)),
                  _resident((1, D), lambda b, m: (0, 0))],
        out_specs=row(D),
        out_shape=jax.ShapeDtypeStruct(x.shape, F32),
        compiler_params=_cparams(("parallel", "parallel")),
        name="merge_out",
    )(ya, yb, yc, gates, x, wb, wo, g.reshape(1, D))


def _ffn_kernel(x_ref, g_in_ref, wi_ref, wo_ref, g_out_ref, o_ref, acc_ref, *, d_ff, tf):
    x = x_ref[0]
    h = _rms(x, g_in_ref[...]).astype(BF16)
    acc_ref[...] = jnp.zeros(acc_ref.shape, F32)

    def body(c, carry):
        off = pl.multiple_of(c * tf, tf)
        gt = jnp.dot(h, wi_ref[:, pl.ds(off, tf)], preferred_element_type=F32)
        up_off = pl.multiple_of(d_ff + off, LANES)
        up = jnp.dot(h, wi_ref[:, pl.ds(up_off, tf)], preferred_element_type=F32)
        act = (gt * _sigmoid(gt) * up).astype(BF16)
        acc_ref[...] += jnp.dot(act, wo_ref[pl.ds(off, tf), :], preferred_element_type=F32)
        return carry

    lax.fori_loop(0, d_ff // tf, body, 0)
    o_ref[0] = x + _rms(acc_ref[...], g_out_ref[...])


def _ffn(x, g_in, wi, wo, layer, g_out, tm, tf):
    B, S, D = x.shape
    d_ff = wo.shape[1]
    row = pl.BlockSpec((1, tm, D), lambda b, m: (b, m, 0))
    vec = _resident((1, D), lambda b, m: (0, 0))
    return pl.pallas_call(
        functools.partial(_ffn_kernel, d_ff=d_ff, tf=tf),
        grid=(B, S // tm),
        in_specs=[row, vec, _resident((None,) + wi.shape[1:], lambda b, m: (layer, 0, 0)),
                  _resident((None,) + wo.shape[1:], lambda b, m: (layer, 0, 0)), vec],
        out_specs=row,
        out_shape=jax.ShapeDtypeStruct(x.shape, F32),
        scratch_shapes=[pltpu.VMEM((tm, D), F32)],
        compiler_params=_cparams(("parallel", "parallel")),
        name="swiglu_ffn",
    )(x, g_in.reshape(1, D), wi, wo, g_out.reshape(1, D))


class _Tiles(NamedTuple):
    proj_rows: int
    dense_rows: int
    ffn_cols: int
    diff_tile: int
    dsa_queries: int
    dsa_keys: int


def _tiles(seq):
    return _Tiles(proj_rows=min(512, seq), dense_rows=min(1024, seq), ffn_cols=256,
                  diff_tile=min(2048, seq), dsa_queries=min(256, seq), dsa_keys=min(512, seq))


def kernel(x, positions, w_in, w_branch, w_out, lambda_q1, lambda_k1, lambda_q2, lambda_k2,
           subln_g, norm_g, w_ffn_in, w_ffn_out):
    B, S, D = x.shape
    depth = w_in.shape[0]
    tiles = _tiles(S)
    assert S % (MOBA_TILE_BLOCKS * MOBA_BLOCK) == 0 and S % tiles.diff_tile == 0
    tab = _rope_table(positions, tiles.proj_rows)
    assert w_in.shape[2] == B_START + B_COLS
    w_bf = w_in.astype(BF16)
    w_a, w_b = w_bf[..., :A_COLS], w_bf[..., B_START:]
    wb, wo = w_branch.astype(BF16), w_out.astype(BF16)
    wi_ffn, wo_ffn = w_ffn_in.astype(BF16), w_ffn_out.astype(BF16)
    for l in range(depth):
        lam_init = 0.8 - 0.6 * math.exp(-0.3 * l)
        (aq, ak, av, bq, k2, v2, iq, ik4, iw, cq, ck, cv, gates, kmean) = _inproj(
            x, norm_g[l, 0], tab, w_a, w_b, l, tiles.proj_rows)
        kmean = kmean.reshape(B, S // MOBA_BLOCK, 512)
        lam_rows = jnp.stack([lambda_q1[l], lambda_k1[l], lambda_q2[l], lambda_k2[l]])
        ya = _diff_attn(aq, ak, av, lam_rows, subln_g[l], lam_init, t=tiles.diff_tile)
        yb = _dsa_attn(bq, iq, iw, ik4, k2, v2, tq=tiles.dsa_queries, tk=tiles.dsa_keys)
        yc = _moba_attn(cq, ck, cv, kmean)
        x = _merge(ya, yb, yc, gates, x, wb, wo, l, norm_g[l, 1], tiles.dense_rows)
        x = _ffn(x, norm_g[l, 2], wi_ffn, wo_ffn, l, norm_g[l, 3], tiles.dense_rows,
                 tf=tiles.ffn_cols)
    return x
```

```python
import functools
import math
from typing import NamedTuple

import jax
import jax.numpy as jnp
from jax import lax
from jax.experimental import pallas as pl
from jax.experimental.pallas import tpu as pltpu

F32 = jnp.float32
BF16 = jnp.bfloat16

LANES = 128
HEAD_DIM = 64
ROPE_THETA = 500000.0
NORM_EPS = 1e-6
IDX_HEADS = 4
IDX_DIM = 32
IDX_TOPK_MAX = 256
MOBA_BLOCK = 256
MOBA_TOPK = 3
N_BRANCH = 3
BRANCH_WIDTH = 512
NEG_BIG = -1e30
LOG2E = math.log2(math.e)
ROW_BLOCK = 64
INT_MIN = -2 ** 31
VMEM_LIMIT = 56 * 1024 * 1024

A_AQ, A_AK, A_AV, A_BQ, A_KV, A_IQ, A_IKW, A_COLS = 0, 512, 1024, 1536, 2048, 2176, 2304, 2432
B_START = 2340
B_CQ, B_CK, B_CV, B_G, B_COLS = 0, 512, 1024, 1536, 4608
Q_SCALE = HEAD_DIM ** -0.5 * LOG2E


def _nt_dot(a, b):
    return lax.dot_general(a, b, (((1,), (1,)), ((), ())), preferred_element_type=F32)


def _rms(x, g):
    return x * lax.rsqrt(jnp.mean(x * x, axis=-1, keepdims=True) + NORM_EPS) * g


def _cparams(sem):
    return pltpu.CompilerParams(dimension_semantics=sem, vmem_limit_bytes=VMEM_LIMIT)


def _resident(shape, index_map):
    return pl.BlockSpec(shape, index_map, pipeline_mode=pl.Buffered(1))


def _rope_table_kernel(pos_ref, tab_ref):
    pos = pos_ref[0].astype(F32)
    lane = lax.broadcasted_iota(jnp.int32, (1, LANES), 1)

    def tables(head_dim):
        rot = head_dim // 4
        half = rot // 2
        d = lane & (head_dim - 1)
        fi = d & (half - 1)
        inv = jnp.zeros((1, LANES), F32)
        for i in range(half):
            inv = jnp.where(fi == i, F32(ROPE_THETA ** (-(2.0 * i) / rot)), inv)
        ang = pos * inv
        cos = jnp.cos(ang)
        sin = jnp.sin(ang)
        c = jnp.where(d < rot, cos, 1.0)
        s = jnp.where(d < half, -sin, jnp.where(d < rot, sin, 0.0))
        return c, s

    c64, s64 = tables(HEAD_DIM)
    c32, s32 = tables(IDX_DIM)
    tab_ref[0, :, 0 * LANES:1 * LANES] = c64
    tab_ref[0, :, 1 * LANES:2 * LANES] = s64
    tab_ref[0, :, 2 * LANES:3 * LANES] = c32
    tab_ref[0, :, 3 * LANES:4 * LANES] = s32


def _rope_table(positions, tm):
    B, S = positions.shape
    return pl.pallas_call(
        _rope_table_kernel,
        grid=(B, S // tm),
        in_specs=[pl.BlockSpec((1, tm, 1), lambda b, m: (b, m, 0))],
        out_specs=pl.BlockSpec((1, tm, 4 * LANES), lambda b, m: (b, m, 0)),
        out_shape=jax.ShapeDtypeStruct((B, S, 4 * LANES), F32),
        compiler_params=_cparams(("parallel", "parallel")),
        name="rope_table",
    )(positions.reshape(B, S, 1))


def _inproj_kernel(x_ref, g_ref, tab_ref, wa_ref, wb_ref,
                   aq_ref, ak_ref, av_ref, bq_ref, k2_ref, v2_ref, iq_ref, ik4_ref, iw_ref,
                   cq_ref, ck_ref, cv_ref, gates_ref, kmean_ref):
    h = _rms(x_ref[0], g_ref[...]).astype(BF16)
    lane = lax.broadcasted_iota(jnp.int32, (1, LANES), 1)

    def proj(w_ref, off, width):
        return jnp.dot(h, w_ref[:, off:off + width], preferred_element_type=F32)

    def rope(y, head_dim):
        half = head_dim // 8
        t0 = 0 if head_dim == HEAD_DIM else 2 * LANES
        c = tab_ref[0, :, t0:t0 + LANES]
        s = tab_ref[0, :, t0 + LANES:t0 + 2 * LANES]
        first = (lane & (head_dim - 1)) < half
        blocks = []
        for j in range(y.shape[1] // LANES):
            yj = y[:, j * LANES:(j + 1) * LANES]
            up = pltpu.roll(yj, LANES - half, 1)
            dn = pltpu.roll(yj, half, 1)
            blocks.append(yj * c + jnp.where(first, up, dn) * s)
        return blocks

    def store(out_ref, blocks):
        for j, r in enumerate(blocks):
            out_ref[0, :, j * LANES:(j + 1) * LANES] = r.astype(out_ref.dtype)

    store(aq_ref, rope(proj(wa_ref, A_AQ, 512) * Q_SCALE, HEAD_DIM))
    store(ak_ref, rope(proj(wa_ref, A_AK, 512), HEAD_DIM))
    av_ref[0] = proj(wa_ref, A_AV, 512).astype(av_ref.dtype)
    store(bq_ref, rope(proj(wa_ref, A_BQ, 512) * Q_SCALE, HEAD_DIM))
    kv = proj(wa_ref, A_KV, LANES)
    vk = pltpu.roll(kv, HEAD_DIM, 1)
    lo = lane < HEAD_DIM
    store(k2_ref, rope(jnp.where(lo, kv, vk), HEAD_DIM))
    v2_ref[0] = jnp.where(lo, vk, kv).astype(v2_ref.dtype)
    store(iq_ref, rope(proj(wa_ref, A_IQ, LANES), IDX_DIM))
    ikw = proj(wa_ref, A_IKW, LANES)
    ik = rope(ikw, IDX_DIM)[0]
    ik4 = ik
    for rep in range(1, LANES // IDX_DIM):
        ik4 = jnp.where(lane < rep * IDX_DIM, ik4, pltpu.roll(ik, rep * IDX_DIM, 1))
    ik4_ref[0] = ik4.astype(ik4_ref.dtype)
    iw_ref[0] = jnp.where(lane < IDX_HEADS, pltpu.roll(ikw, LANES - IDX_DIM, 1), 0.0)
    store(cq_ref, rope(proj(wb_ref, B_CQ, 512) * Q_SCALE, HEAD_DIM))
    ck_blocks = rope(proj(wb_ref, B_CK, 512), HEAD_DIM)
    store(ck_ref, ck_blocks)
    cv_ref[0] = proj(wb_ref, B_CV, 512).astype(cv_ref.dtype)
    gates_ref[0] = proj(wb_ref, B_G, 3 * 1024).astype(gates_ref.dtype)

    tm = x_ref.shape[1]
    for blk in range(tm // MOBA_BLOCK):
        for j, r in enumerate(ck_blocks):
            part = r[blk * MOBA_BLOCK:(blk + 1) * MOBA_BLOCK]
            kmean_ref[0, 0, blk:blk + 1, j * LANES:(j + 1) * LANES] = (
                jnp.mean(part, axis=0, keepdims=True))


def _inproj(x, g, tab, w_a, w_b, layer, tm):
    B, S, D = x.shape
    nm = S // tm
    row = lambda width: pl.BlockSpec((1, tm, width), lambda b, m: (b, m, 0))
    shp = lambda width, dt: jax.ShapeDtypeStruct((B, S, width), dt)
    widths = [(512, BF16)] * 4 + [(LANES, BF16)] * 4 + [(LANES, F32)] + [(512, BF16)] * 3 + [(3072, BF16)]
    out_specs = [row(w) for w, _ in widths]
    out_shape = [shp(w, dt) for w, dt in widths]
    out_specs.append(pl.BlockSpec((1, 1, tm // MOBA_BLOCK, 512), lambda b, m: (b, m, 0, 0)))
    out_shape.append(jax.ShapeDtypeStruct((B, nm, tm // MOBA_BLOCK, 512), F32))
    return pl.pallas_call(
        _inproj_kernel,
        grid=(B, nm),
        in_specs=[row(D),
                  _resident((1, D), lambda b, m: (0, 0)),
                  row(4 * LANES),
                  _resident((None, D, A_COLS), lambda b, m: (layer, 0, 0)),
                  _resident((None, D, B_COLS), lambda b, m: (layer, 0, 0))],
        out_specs=out_specs,
        out_shape=out_shape,
        compiler_params=_cparams(("parallel", "parallel")),
        name="inproj",
    )(x, g.reshape(1, D), tab, w_a, w_b)


def _split_heads_rows(q):
    lo = lax.broadcasted_iota(jnp.int32, (1, LANES), 1) < HEAD_DIM
    zero = jnp.zeros_like(q)
    return jnp.concatenate([jnp.where(lo, q, zero), jnp.where(lo, zero, q)], axis=0)


def _online_update(s, v, m_ref, l_ref, acc_ref, bias=None, row0=0):
    rows_total = s.shape[0]
    nblk = s.shape[1] // LANES
    p_rows, alphas = [], []
    for r0 in range(0, rows_total, ROW_BLOCK):
        rows = slice(row0 + r0, row0 + r0 + ROW_BLOCK)
        cols = [s[r0:r0 + ROW_BLOCK, j * LANES:(j + 1) * LANES] for j in range(nblk)]
        if bias is not None:
            b0 = r0 % bias[0].shape[0]
            cols = [c + bj[b0:b0 + ROW_BLOCK] for c, bj in zip(cols, bias)]
        m_prev = m_ref[rows]
        m_new = jnp.maximum(
            m_prev, jnp.max(functools.reduce(jnp.maximum, cols), axis=1, keepdims=True))
        alpha = jnp.exp2(m_prev - m_new)
        ps = [jnp.exp2(c - m_new) for c in cols]
        l_ref[rows] = alpha * l_ref[rows] + functools.reduce(jnp.add, ps)
        m_ref[rows] = m_new
        p_rows.append(jnp.concatenate([pj.astype(BF16) for pj in ps], axis=1))
        alphas.append(alpha)
    p = jnp.concatenate(p_rows, axis=0)
    alpha = jnp.concatenate(alphas, axis=0)
    span = slice(row0, row0 + rows_total)
    acc_ref[span] = alpha * acc_ref[span] + jnp.dot(p, v, preferred_element_type=F32)


def _normalised(l_ref, acc_ref):
    return acc_ref[...] / jnp.sum(l_ref[...], axis=1, keepdims=True)


DIFF_SUB = 256
DIFF_BULK_KEYS = 1024


def _diff_attn_kernel(q_ref, k_ref, v_ref, lam_ref, g_ref, o_ref, m_ref, l_ref, acc_ref,
                      *, t, lam_init):
    qi = pl.program_id(2)
    sub = DIFF_SUB
    nsb = t // sub
    grp = 2 * sub
    q = q_ref[0]
    q2 = jnp.concatenate(
        [_split_heads_rows(q[j * sub:(j + 1) * sub]) for j in range(nsb)], axis=0)
    m_ref[...] = jnp.full(m_ref.shape, NEG_BIG, F32)
    l_ref[...] = jnp.zeros(l_ref.shape, F32)
    acc_ref[...] = jnp.zeros(acc_ref.shape, F32)

    def body(c, carry):
        off = pl.multiple_of(c * DIFF_BULK_KEYS, DIFF_BULK_KEYS)
        _online_update(_nt_dot(q2, k_ref[0, pl.ds(off, DIFF_BULK_KEYS), :]),
                       v_ref[0, pl.ds(off, DIFF_BULK_KEYS), :], m_ref, l_ref, acc_ref)
        return carry

    lax.fori_loop(0, qi * (t // DIFF_BULK_KEYS), body, 0)

    r = lax.broadcasted_iota(jnp.int32, (grp, 1), 0)
    causal = lax.broadcasted_iota(jnp.int32, (1, sub), 1) <= jnp.where(r >= sub, r - sub, r)
    for j in range(nsb):
        off = pl.multiple_of(qi * t + j * sub, sub)
        s = _nt_dot(q2[j * grp:], k_ref[0, pl.ds(off, sub), :])
        own = jnp.where(causal, s[:grp], NEG_BIG)
        s = own if j + 1 == nsb else jnp.concatenate([own, s[grp:]], axis=0)
        _online_update(s, v_ref[0, pl.ds(off, sub), :], m_ref, l_ref, acc_ref, row0=j * grp)

    lam_rows = lam_ref[...]
    e1 = jnp.exp(jnp.sum(lam_rows[0:1] * lam_rows[1:2], axis=1, keepdims=True))
    e2 = jnp.exp(jnp.sum(lam_rows[2:3] * lam_rows[3:4], axis=1, keepdims=True))
    lam = e1 - e2 + lam_init
    o = _normalised(l_ref, acc_ref)
    for j in range(nsb):
        out = o[j * grp:j * grp + sub] - lam * o[j * grp + sub:(j + 1) * grp]
        o_ref[0, j * sub:(j + 1) * sub] = (
            _rms(out, g_ref[...]) * (1.0 - lam_init)).astype(o_ref.dtype)


def _diff_attn(aq, ak, av, lam_rows, subln_g, lam_init, t):
    B, S, _ = aq.shape
    nh = aq.shape[2] // LANES
    kern = functools.partial(_diff_attn_kernel, t=t, lam_init=lam_init)
    return pl.pallas_call(
        kern,
        grid=(B, nh, S // t),
        in_specs=[pl.BlockSpec((1, t, LANES), lambda b, h, i: (b, i, h)),
                  pl.BlockSpec((1, S, LANES), lambda b, h, i: (b, 0, h)),
                  pl.BlockSpec((1, S, LANES), lambda b, h, i: (b, 0, h)),
                  pl.BlockSpec((4, HEAD_DIM), lambda b, h, i: (0, 0)),
                  pl.BlockSpec((1, LANES), lambda b, h, i: (0, 0))],
        out_specs=pl.BlockSpec((1, t, LANES), lambda b, h, i: (b, i, h)),
        out_shape=jax.ShapeDtypeStruct(aq.shape, BF16),
        scratch_shapes=[pltpu.VMEM((2 * t, LANES), F32), pltpu.VMEM((2 * t, LANES), F32),
                        pltpu.VMEM((2 * t, LANES), F32)],
        compiler_params=_cparams(("parallel", "parallel", "arbitrary")),
        name="diff_attn",
    )(aq, ak, av, lam_rows, subln_g.reshape(1, LANES))


BIT_GROUP = 256


def _bit_transpose32(words):
    w = list(words)
    for j, m in ((16, 0x0000FFFF), (8, 0x00FF00FF), (4, 0x0F0F0F0F), (2, 0x33333333), (1, 0x55555555)):
        for k in range(32):
            if k & j == 0:
                t = (w[k] ^ (w[k + j] >> j)) & m
                w[k] = w[k] ^ t
                w[k + j] = w[k + j] ^ (t << j)
    return w


def _dsa_kernel(q_ref, iq_ref, iw_ref, iqn_ref, iwn_ref, ik4_ref, k2_ref, v2_ref, o_ref,
                keys_ref, planes_ref, act_ref, jcut_ref, m_ref, l_ref, acc_ref,
                *, tq, tk, topk, idx_bits):
    qi = pl.program_id(1)
    nxt = jnp.minimum(qi + 1, pl.num_programs(1) - 1)
    slot = qi % 2
    chunks = lambda tile: (tile * tq + tq - 1) // tk + 1
    nkv = chunks(qi)
    lane = lax.broadcasted_iota(jnp.int32, (1, LANES), 1)
    row = lax.broadcasted_iota(jnp.int32, (tq, 1), 0)
    qpos = qi * tq + row
    lane_tk = lax.broadcasted_iota(jnp.int32, (1, tk), 1)
    nsub = tk // LANES
    seq = keys_ref.shape[2]

    def indexer_rows(iq_blk, iw_blk):
        iq = iq_blk[0]
        zero = jnp.zeros_like(iq)
        iq4 = jnp.concatenate(
            [jnp.where((lane >> (IDX_DIM.bit_length() - 1)) == hh, iq, zero)
             for hh in range(IDX_HEADS)], axis=0)
        iw = iw_blk[0]
        return iq4, [iw[:, hh:hh + 1] for hh in range(IDX_HEADS)]

    def score_chunk(c, tile, dst, iq4, iw_cols):
        off = pl.multiple_of(c * tk, tk)
        rel = jnp.maximum(_nt_dot(iq4, ik4_ref[0, pl.ds(off, tk), :]), 0.0)
        score = iw_cols[0] * rel[0:tq]
        for hh in range(1, IDX_HEADS):
            score = score + iw_cols[hh] * rel[hh * tq:(hh + 1) * tq]
        bits = lax.bitcast_convert_type(score, jnp.int32)
        key = bits ^ ((bits >> 31) & jnp.int32(0x7FFFFFFF))
        kpos = off + lane_tk
        key = jnp.where(score == 0.0, (seq - 1) - kpos, jnp.where(key > 0, key + seq, key))
        key = jnp.where(kpos <= tile * tq + row, key, jnp.int32(INT_MIN))
        keys_ref[dst, :, pl.ds(off, tk)] = key
        kts = [key[:, j * LANES:(j + 1) * LANES].T for j in range(nsub)]
        for gl in range(tk // BIT_GROUP):
            words = [kts[(gl * BIT_GROUP + 8 * g) // LANES][(8 * g) % LANES:(8 * g) % LANES + 8]
                     for g in range(32)]
            words = _bit_transpose32(words)
            words[0] = ~words[0]
            for b in range(32):
                planes_ref[c * (tk // BIT_GROUP) + gl, b] = words[b]

    @pl.when(qi == 0)
    def _():
        planes_ref[...] = jnp.zeros(planes_ref.shape, jnp.int32)
        iq4, iw_cols = indexer_rows(iq_ref, iw_ref)
        for c in range(chunks(0)):
            score_chunk(c, 0, 0, iq4, iw_cols)

    ngroups = planes_ref.shape[0]
    live = nkv * (tk // BIT_GROUP)
    for g in range(ngroups):
        act_ref[g] = jnp.broadcast_to(jnp.where(g < live, jnp.int32(-1), jnp.int32(0)), (8, tq))

    def sweep(i, keep, first=False):
        parts = [jnp.zeros((8, tq), jnp.int32) for _ in range(4)]
        for g in range(ngroups):
            act = act_ref[g]
            if not first:
                act = act & (planes_ref[g, i - 1] ^ keep)
                act_ref[g] = act
            parts[g % 4] = parts[g % 4] + lax.population_count(act & planes_ref[g, i])
        cnt = (parts[0] + parts[1]) + (parts[2] + parts[3])
        return jnp.sum(cnt.astype(F32), axis=0, keepdims=True)

    def decide(i, ones, thr_u, n_gt):
        take = n_gt + ones >= topk
        thr_u = thr_u | jnp.where(take, jnp.left_shift(jnp.int32(1), 31 - i), 0)
        return thr_u, jnp.where(take, n_gt, n_gt + ones), jnp.where(take, 0, -1).astype(jnp.int32)

    def pass_body(i, carry):
        thr_u, n_gt, keep = carry
        return decide(i, sweep(i, keep), thr_u, n_gt)

    start = decide(0, sweep(0, None, first=True),
                   jnp.zeros((1, tq), jnp.int32), jnp.zeros((1, tq), F32))
    thr_u, n_gt, keep = lax.fori_loop(1, 32, pass_body, start)
    n_eq = jnp.zeros((8, tq), jnp.int32)
    for g in range(ngroups):
        n_eq = n_eq + lax.population_count(act_ref[g] & (planes_ref[g, 31] ^ keep))
    thr = thr_u ^ jnp.int32(INT_MIN)
    cnt_ge = n_gt + jnp.sum(n_eq.astype(F32), axis=0, keepdims=True)
    need = topk - n_gt

    def as_rows(x):
        return jnp.broadcast_to(x, (LANES, tq)).T

    def key_block(off, j):
        return keys_ref[slot, :, pl.ds(pl.multiple_of(off + j * LANES, LANES), LANES)]

    thr_r = as_rows(thr)
    need_r = as_rows(need)
    jcut_ref[...] = jnp.full((tq, LANES), 2 ** 30, jnp.int32)

    @pl.when(jnp.max(cnt_ge) > topk)
    def _():
        def ties_before(cand):
            def body(c, cnt):
                off = pl.multiple_of(c * tk, tk)
                for j in range(nsub):
                    hit = (key_block(off, j) == thr_r) & (off + j * LANES + lane < cand)
                    cnt = jnp.where(hit, cnt + 1, cnt)
                return cnt
            cnt = lax.fori_loop(0, nkv, body, jnp.zeros((tq, LANES), jnp.int32))
            return jnp.sum(cnt.astype(F32), axis=1, keepdims=True)

        def idx_body(i, j):
            cand = j + jnp.left_shift(jnp.int32(1), idx_bits - 1 - i)
            return jnp.where(ties_before(cand) < need_r, cand, j)
        jcut_ref[...] = lax.fori_loop(0, idx_bits, idx_body, jnp.zeros((tq, LANES), jnp.int32))

    jcut_r = jcut_ref[...]

    q = q_ref[0]
    q8 = jnp.concatenate(
        [_split_heads_rows(q[:, p * LANES:(p + 1) * LANES]) for p in range(4)], axis=0)
    m_ref[...] = jnp.full(m_ref.shape, NEG_BIG, F32)
    l_ref[...] = jnp.zeros(l_ref.shape, F32)
    acc_ref[...] = jnp.zeros(acc_ref.shape, F32)
    iq4n, iwn_cols = indexer_rows(iqn_ref, iwn_ref)

    def prepare_next(c):
        score_chunk(c, nxt, 1 - slot, iq4n, iwn_cols)

    def attend(off, width):
        s = _nt_dot(q8, k2_ref[0, pl.ds(off, width), :])
        bias = []
        for j in range(width // LANES):
            keyj = key_block(off, j)
            kpos = off + j * LANES + lane
            sel = ((keyj > thr_r) | ((keyj == thr_r) & (kpos <= jcut_r))) & (kpos <= qpos)
            bias.append(jnp.where(sel, 0.0, NEG_BIG))
        _online_update(s, v2_ref[0, pl.ds(off, width), :], m_ref, l_ref, acc_ref, bias=bias)

    def attn_body(c, carry):
        attend(pl.multiple_of(c * 2 * tk, 2 * tk), 2 * tk)
        prepare_next(2 * c)
        prepare_next(2 * c + 1)
        return carry

    lax.fori_loop(0, nkv // 2, attn_body, 0)

    @pl.when(nkv % 2 == 1)
    def _():
        attend(pl.multiple_of((nkv - 1) * tk, tk), tk)
        prepare_next(nkv - 1)

    @pl.when(chunks(nxt) > nkv)
    def _():
        prepare_next(nkv)

    o = _normalised(l_ref, acc_ref)
    lo = lane < HEAD_DIM
    for p in range(4):
        o_ref[0, :, p * LANES:(p + 1) * LANES] = jnp.where(
            lo, o[(2 * p) * tq:(2 * p + 1) * tq], o[(2 * p + 1) * tq:(2 * p + 2) * tq]
        ).astype(o_ref.dtype)


def _dsa_attn(bq, iq, iw, ik4, k2, v2, tq, tk):
    B, S, C = bq.shape
    assert tk % tq == 0 and tk % BIT_GROUP == 0 and S % tk == 0
    topk = min(IDX_TOPK_MAX, S // 4)
    idx_bits = max(1, (S - 1).bit_length())
    kern = functools.partial(_dsa_kernel, tq=tq, tk=tk, topk=topk, idx_bits=idx_bits)
    nq = S // tq
    row = lambda width: pl.BlockSpec((1, tq, width), lambda b, i: (b, i, 0))
    nxt = lambda width: pl.BlockSpec((1, tq, width), lambda b, i: (b, jnp.minimum(i + 1, nq - 1), 0))
    full = _resident((1, S, LANES), lambda b, i: (b, 0, 0))
    return pl.pallas_call(
        kern,
        grid=(B, nq),
        in_specs=[row(C), row(LANES), row(LANES), nxt(LANES), nxt(LANES), full, full, full],
        out_specs=row(C),
        out_shape=jax.ShapeDtypeStruct(bq.shape, BF16),
        scratch_shapes=[pltpu.VMEM((2, tq, S), jnp.int32),
                        pltpu.VMEM((S // BIT_GROUP, 32, 8, tq), jnp.int32),
                        pltpu.VMEM((S // BIT_GROUP, 8, tq), jnp.int32),
                        pltpu.VMEM((tq, LANES), jnp.int32),
                        pltpu.VMEM((8 * tq, LANES), F32), pltpu.VMEM((8 * tq, LANES), F32),
                        pltpu.VMEM((8 * tq, LANES), F32)],
        compiler_params=_cparams(("parallel", "arbitrary")),
        name="dsa_attn",
    )(bq, iq, iw, iq, iw, ik4, k2, v2)


MOBA_TILE_BLOCKS = 8
MOBA_BULK_BLOCKS = 4


def _moba_kernel(q_ref, k_ref, v_ref, km_ref, o_ref, m_ref, l_ref, acc_ref, *, nb):
    nqb = MOBA_TILE_BLOCKS
    blk_rows = MOBA_BLOCK
    grp = 2 * blk_rows
    i = pl.program_id(2)
    first = nqb * i
    lane = lax.broadcasted_iota(jnp.int32, (1, LANES), 1)
    q = q_ref[0]
    q2 = jnp.concatenate(
        [_split_heads_rows(q[j * blk_rows:(j + 1) * blk_rows]) for j in range(nqb)], axis=0)

    km = jnp.concatenate([km_ref[0], jnp.zeros((LANES - nb, LANES), F32)], axis=0)
    km_hi = km.astype(BF16)
    r1 = km - km_hi.astype(F32)
    km_mid = r1.astype(BF16)
    km_lo = (r1 - km_mid.astype(F32)).astype(BF16)
    nb8 = -(-nb // 8) * 8
    gate_t = (_nt_dot(km_hi, q2) + _nt_dot(km_mid, q2) + _nt_dot(km_lo, q2))[:nb8]
    blk = lax.broadcasted_iota(jnp.int32, (nb8, 1), 0)
    blk_f = blk.astype(F32)
    col = lax.broadcasted_iota(jnp.int32, (1, nqb * grp), 1)
    past = blk < first + (col >> (grp.bit_length() - 1))
    g = jnp.where(past, gate_t, -jnp.inf)
    sel = jnp.zeros(g.shape, jnp.bool_)
    for _ in range(min(MOBA_TOPK, nb - 1)):
        mx = jnp.max(g, axis=0, keepdims=True)
        pick = blk_f == jnp.min(jnp.where(g == mx, blk_f, float(LANES)), axis=0, keepdims=True)
        sel = sel | pick
        g = jnp.where(pick, -jnp.inf, g)
    bias_t = jnp.concatenate([jnp.where(sel & past, 0.0, NEG_BIG),
                              jnp.full((LANES - nb8, nqb * grp), NEG_BIG, F32)], axis=0)
    qa = jnp.concatenate([q2, bias_t.T.astype(BF16)], axis=1)

    m_ref[...] = jnp.full(m_ref.shape, NEG_BIG, F32)
    l_ref[...] = jnp.zeros(l_ref.shape, F32)
    acc_ref[...] = jnp.zeros(acc_ref.shape, F32)

    def biased_logits(lhs, first_blk, nblk):
        n = nblk * blk_rows
        off = pl.multiple_of(first_blk * blk_rows, blk_rows)
        row_blk = lax.broadcasted_iota(jnp.int32, (n, 1), 0) >> (MOBA_BLOCK.bit_length() - 1)
        onehot = jnp.where(lane == first_blk + row_blk, 1.0, 0.0).astype(BF16)
        ka = jnp.concatenate([k_ref[0, pl.ds(off, n), :], onehot], axis=1)
        return _nt_dot(lhs, ka), v_ref[0, pl.ds(off, n), :]

    def body(c, carry):
        s, v = biased_logits(qa, MOBA_BULK_BLOCKS * c, MOBA_BULK_BLOCKS)
        _online_update(s, v, m_ref, l_ref, acc_ref)
        return carry

    lax.fori_loop(0, first // MOBA_BULK_BLOCKS, body, 0)

    r = lax.broadcasted_iota(jnp.int32, (grp, 1), 0)
    causal = (lax.broadcasted_iota(jnp.int32, (1, blk_rows), 1)
              <= jnp.where(r >= blk_rows, r - blk_rows, r))
    for j in range(nqb):
        off = pl.multiple_of((first + j) * blk_rows, blk_rows)
        s = jnp.where(causal, _nt_dot(q2[j * grp:(j + 1) * grp], k_ref[0, pl.ds(off, blk_rows), :]),
                      NEG_BIG)
        if j + 1 < nqb:
            s_later, _ = biased_logits(qa[(j + 1) * grp:], first + j, 1)
            s = jnp.concatenate([s, s_later], axis=0)
        _online_update(s, v_ref[0, pl.ds(off, blk_rows), :], m_ref, l_ref, acc_ref, row0=j * grp)

    o = _normalised(l_ref, acc_ref)
    lo = lane < HEAD_DIM
    for j in range(nqb):
        o_ref[0, j * blk_rows:(j + 1) * blk_rows] = jnp.where(
            lo, o[j * grp:j * grp + blk_rows], o[j * grp + blk_rows:(j + 1) * grp]).astype(o_ref.dtype)


def _moba_attn(cq, ck, cv, kmean):
    B, S, C = cq.shape
    nb = S // MOBA_BLOCK
    tq = MOBA_TILE_BLOCKS * MOBA_BLOCK
    return pl.pallas_call(
        functools.partial(_moba_kernel, nb=nb),
        grid=(B, C // LANES, S // tq),
        in_specs=[pl.BlockSpec((1, tq, LANES), lambda b, p, i: (b, i, p)),
                  pl.BlockSpec((1, S, LANES), lambda b, p, i: (b, 0, p)),
                  pl.BlockSpec((1, S, LANES), lambda b, p, i: (b, 0, p)),
                  pl.BlockSpec((1, nb, LANES), lambda b, p, i: (b, 0, p))],
        out_specs=pl.BlockSpec((1, tq, LANES), lambda b, p, i: (b, i, p)),
        out_shape=jax.ShapeDtypeStruct(cq.shape, BF16),
        scratch_shapes=[pltpu.VMEM((2 * tq, LANES), F32), pltpu.VMEM((2 * tq, LANES), F32),
                        pltpu.VMEM((2 * tq, LANES), F32)],
        compiler_params=_cparams(("parallel", "parallel", "arbitrary")),
        name="moba_attn",
    )(cq, ck, cv, kmean)


def _sigmoid(x):
    return 1.0 / (1.0 + jnp.exp(-x))


def _merge_kernel(ya_ref, yb_ref, yc_ref, gates_ref, x_ref, wb_ref, wo_ref, g_ref, o_ref):
    d = x_ref.shape[2]
    merged = None
    for n, y_ref in enumerate((ya_ref, yb_ref, yc_ref)):
        br = jnp.dot(y_ref[0], wb_ref[n], preferred_element_type=F32)
        term = _sigmoid(gates_ref[0, :, n * d:(n + 1) * d].astype(F32)) * br
        merged = term if merged is None else merged + term
    m = jnp.dot(merged.astype(BF16), wo_ref[...], preferred_element_type=F32)
    o_ref[0] = x_ref[0] + _rms(m, g_ref[...])


def _merge(ya, yb, yc, gates, x, wb, wo, layer, g, tm):
    B, S, D = x.shape
    row = lambda width: pl.BlockSpec((1, tm, width), lambda b, m: (b, m, 0))
    return pl.pallas_call(
        _merge_kernel,
        grid=(B, S // tm),
        in_specs=[row(BRANCH_WIDTH), row(BRANCH_WIDTH), row(BRANCH_WIDTH), row(N_BRANCH * D), row(D),
                  _resident((None,) + wb.shape[1:], lambda b, m: (layer, 0, 0, 0)),
                  _resident((None,) + wo.shape[1:], lambda b, m: (layer, 0, 0)),
                  _resident((1, D), lambda b, m: (0, 0))],
        out_specs=row(D),
        out_shape=jax.ShapeDtypeStruct(x.shape, F32),
        compiler_params=_cparams(("parallel", "parallel")),
        name="merge_out",
    )(ya, yb, yc, gates, x, wb, wo, g.reshape(1, D))


def _ffn_kernel(x_ref, g_in_ref, wi_ref, wo_ref, g_out_ref, o_ref, acc_ref, *, d_ff, tf):
    x = x_ref[0]
    h = _rms(x, g_in_ref[...]).astype(BF16)
    acc_ref[...] = jnp.zeros(acc_ref.shape, F32)

    def body(c, carry):
        off = pl.multiple_of(c * tf, tf)
        gt = jnp.dot(h, wi_ref[:, pl.ds(off, tf)], preferred_element_type=F32)
        up_off = pl.multiple_of(d_ff + off, LANES)
        up = jnp.dot(h, wi_ref[:, pl.ds(up_off, tf)], preferred_element_type=F32)
        act = (gt * _sigmoid(gt) * up).astype(BF16)
        acc_ref[...] += jnp.dot(act, wo_ref[pl.ds(off, tf), :], preferred_element_type=F32)
        return carry

    lax.fori_loop(0, d_ff // tf, body, 0)
    o_ref[0] = x + _rms(acc_ref[...], g_out_ref[...])


def _ffn(x, g_in, wi, wo, layer, g_out, tm, tf):
    B, S, D = x.shape
    d_ff = wo.shape[1]
    row = pl.BlockSpec((1, tm, D), lambda b, m: (b, m, 0))
    vec = _resident((1, D), lambda b, m: (0, 0))
    return pl.pallas_call(
        functools.partial(_ffn_kernel, d_ff=d_ff, tf=tf),
        grid=(B, S // tm),
        in_specs=[row, vec, _resident((None,) + wi.shape[1:], lambda b, m: (layer, 0, 0)),
                  _resident((None,) + wo.shape[1:], lambda b, m: (layer, 0, 0)), vec],
        out_specs=row,
        out_shape=jax.ShapeDtypeStruct(x.shape, F32),
        scratch_shapes=[pltpu.VMEM((tm, D), F32)],
        compiler_params=_cparams(("parallel", "parallel")),
        name="swiglu_ffn",
    )(x, g_in.reshape(1, D), wi, wo, g_out.reshape(1, D))


class _Tiles(NamedTuple):
    proj_rows: int
    dense_rows: int
    ffn_cols: int
    diff_tile: int
    dsa_queries: int
    dsa_keys: int


def _tiles(seq):
    return _Tiles(proj_rows=min(512, seq), dense_rows=min(1024, seq), ffn_cols=256,
                  diff_tile=min(2048, seq), dsa_queries=min(256, seq), dsa_keys=min(512, seq))


def kernel(x, positions, w_in, w_branch, w_out, lambda_q1, lambda_k1, lambda_q2, lambda_k2,
           subln_g, norm_g, w_ffn_in, w_ffn_out):
    B, S, D = x.shape
    depth = w_in.shape[0]
    tiles = _tiles(S)
    assert S % (MOBA_TILE_BLOCKS * MOBA_BLOCK) == 0 and S % tiles.diff_tile == 0
    tab = _rope_table(positions, tiles.proj_rows)
    assert w_in.shape[2] == B_START + B_COLS
    w_bf = w_in.astype(BF16)
    w_a, w_b = w_bf[..., :A_COLS], w_bf[..., B_START:]
    wb, wo = w_branch.astype(BF16), w_out.astype(BF16)
    wi_ffn, wo_ffn = w_ffn_in.astype(BF16), w_ffn_out.astype(BF16)
    for l in range(depth):
        lam_init = 0.8 - 0.6 * math.exp(-0.3 * l)
        (aq, ak, av, bq, k2, v2, iq, ik4, iw, cq, ck, cv, gates, kmean) = _inproj(
            x, norm_g[l, 0], tab, w_a, w_b, l, tiles.proj_rows)
        kmean = kmean.reshape(B, S // MOBA_BLOCK, 512)
        lam_rows = jnp.stack([lambda_q1[l], lambda_k1[l], lambda_q2[l], lambda_k2[l]])
        ya = _diff_attn(aq, ak, av, lam_rows, subln_g[l], lam_init, t=tiles.diff_tile)
        yb = _dsa_attn(bq, iq, iw, ik4, k2, v2, tq=tiles.dsa_queries, tk=tiles.dsa_keys)
        yc = _moba_attn(cq, ck, cv, kmean)
        x = _merge(ya, yb, yc, gates, x, wb, wo, l, norm_g[l, 1], tiles.dense_rows)
        x = _ffn(x, norm_g[l, 2], wi_ffn, wo_ffn, l, norm_g[l, 3], tiles.dense_rows,
                 tf=tiles.ffn_cols)
    return x
```
